```python
import numpy as np
import jax
import jax.numpy as jnp
from jax import lax

D_MODEL = 1024
BATCH = 16
SEQ = 4096
DEPTH = 2

H_A = 4
DH_NOPE = 64
DH_ROPE = 32
D_C = 128
DH_V = 64
H_I = 4
D_I = 64
K_SEL_MAX = 256
ROPE_THETA = 10000.0
H_B = 4
DH_B = 128
CONV_W = 4
CHUNK = 64
F_BIAS_LO = 3.0
F_BIAS_HI = 6.0
H_C = 4
DH_C = 64
L_CMP = 32
D_STRIDE = 16
CMP_HIDDEN = 128
L_SEL = 64
N_TOP_MAX = 16
WINDOW = 512
Q_BLOCK = 128
D_FF = 2816
N_EXPERTS = 8
TOP_K_EXPERTS = 2
D_FF_EXPERT = 3584
N_DENSE = (DEPTH + 1) // 2
N_MOE = DEPTH // 2
N_MOD = 6
ALPHA = (2 * DEPTH) ** 0.25
BETA = (8 * DEPTH) ** -0.25
MIX_OUT = H_A * DH_V + H_B * DH_B + H_C * DH_C
SPLIT_SIZES = (H_A * DH_NOPE, H_A * DH_ROPE, D_C, DH_ROPE, H_I * D_I, D_I, H_I, H_B * DH_B, H_B * DH_B, H_B * DH_B, H_B, H_B, H_B * DH_B, H_C * DH_C, DH_C, DH_C, DH_C, DH_C, DH_C, DH_C, 3 * H_C)
P_IN = sum(SPLIT_SIZES)

kernel_name = 'hybrid_dsa_mlstm_nsa_moe_block'


def _layer_norm(x, g, b, eps=1e-5):
    xf = x.astype(jnp.float32)
    mu = jnp.mean(xf, -1, keepdims=True)
    var = jnp.mean(jnp.square(xf - mu), -1, keepdims=True)
    return ((xf - mu) * lax.rsqrt(var + eps) * g + b).astype(x.dtype)


def _rms_norm(x, g, eps=1e-6):
    xf = x.astype(jnp.float32)
    return (xf * lax.rsqrt(jnp.mean(xf * xf, -1, keepdims=True) + eps) * g).astype(x.dtype)


def _masked_softmax(s, mask):
    s = jnp.where(mask, s.astype(jnp.float32), -jnp.inf)
    m = jnp.max(s, axis=-1, keepdims=True)
    m = jnp.where(jnp.isfinite(m), m, 0.0)
    e = jnp.where(mask, jnp.exp(s - m), 0.0)
    return e / jnp.maximum(jnp.sum(e, axis=-1, keepdims=True), 1e-30)


def _rope(x, ang):
    x1, x2 = jnp.split(x.astype(jnp.float32), 2, axis=-1)
    cos, sin = jnp.cos(ang), jnp.sin(ang)
    return jnp.concatenate([x1 * cos - x2 * sin, x1 * sin + x2 * cos], -1).astype(x.dtype)


def _gather_rows(table, idx):
    return jax.vmap(lambda t, i: t[i])(table, idx)


def _to_blocks(a):
    return a.reshape(a.shape[0], -1, Q_BLOCK, *a.shape[2:]).swapaxes(0, 1)


def _from_blocks(a):
    a = a.swapaxes(0, 1)
    return a.reshape(a.shape[0], -1, *a.shape[3:])


def _block_starts(seq):
    return jnp.arange(seq // Q_BLOCK, dtype=jnp.int32) * Q_BLOCK


def _dsa_group(q_nope, q_rope, ckv, k_rope, iq, ik, iw, positions, w_uk, w_uv, kv_norm):
    bsz, seq = q_nope.shape[:2]
    k_sel = min(K_SEL_MAX, seq // 4)
    scale = (DH_NOPE + DH_ROPE) ** -0.5
    inv_freq = ROPE_THETA ** (-jnp.arange(0, DH_ROPE, 2, dtype=jnp.float32) / DH_ROPE)
    ang = positions.astype(jnp.float32)[..., None] * inv_freq
    q_rope = _rope(q_rope.reshape(bsz, seq, H_A, DH_ROPE), ang[:, :, None, :])
    k_rope = _rope(k_rope, ang)
    ckv = _rms_norm(ckv, kv_norm)
    q_abs = jnp.einsum('bshd,hdc->bshc', q_nope.reshape(bsz, seq, H_A, DH_NOPE), w_uk)
    iq = iq.reshape(bsz, seq, H_I, D_I)
    key_pos = jnp.arange(seq)

    def block(args):
        qa, qr, iqb, iwb, t0 = args
        q_pos = t0 + jnp.arange(Q_BLOCK)
        causal = key_pos[None, :] <= q_pos[:, None]
        logits = jnp.einsum('bthd,bsd->bths', iqb, ik)
        score = jnp.einsum('bths,bth->bts', jax.nn.relu(logits), iwb)
        score = jnp.where(causal[None], score, -jnp.inf)
        top_val, top_idx = lax.top_k(score, k_sel)
        valid = top_val > -jnp.inf
        c_sel = _gather_rows(ckv, top_idx)
        r_sel = _gather_rows(k_rope, top_idx)
        s = (jnp.einsum('bthc,btkc->bthk', qa, c_sel) + jnp.einsum('bthr,btkr->bthk', qr, r_sel)) * scale
        p = _masked_softmax(s, valid[:, :, None, :]).astype(c_sel.dtype)
        o_lat = jnp.einsum('bthk,btkc->bthc', p, c_sel)
        return jnp.einsum('bthc,hcd->bthd', o_lat, w_uv)

    out = lax.map(block, (_to_blocks(q_abs), _to_blocks(q_rope), _to_blocks(iq), _to_blocks(iw), _block_starts(seq)))
    return _from_blocks(out)


def _causal_conv(x, w, b):
    y = lax.conv_general_dilated(x, w[:, None, :].astype(x.dtype), (1,), [(CONV_W - 1, 0)], dimension_numbers=('NWC', 'WIO', 'NWC'), feature_group_count=x.shape[-1])
    return y + b


def _mlstm_chunk(state, inp):
    c_mat, n_vec, m = state
    q, k, v, ig, lf = inp
    b = jnp.cumsum(lf, axis=-1)
    tri = jnp.tril(jnp.ones((CHUNK, CHUNK), bool))
    log_d = jnp.where(tri, b[..., :, None] - b[..., None, :] + ig[..., None, :], -jnp.inf)
    log_inter = b + m[..., None]
    m_out = jnp.maximum(log_inter, jnp.max(log_d, -1))
    d = jnp.exp(log_d - m_out[..., None])
    w_inter = jnp.exp(log_inter - m_out)
    s = jnp.einsum('bhjd,bhsd->bhjs', q, k) * d
    num = w_inter[..., None] * jnp.einsum('bhvk,bhjk->bhjv', c_mat, q) + jnp.einsum('bhjs,bhsv->bhjv', s, v)
    den = w_inter * jnp.einsum('bhk,bhjk->bhj', n_vec, q) + jnp.sum(s, -1)
    h = num / jnp.maximum(jnp.abs(den), jnp.exp(-m_out))[..., None]
    b_last = b[..., -1]
    log_w = b_last[..., None] - b + ig
    m_new = jnp.maximum(b_last + m, jnp.max(log_w, -1))
    decay = jnp.exp(b_last + m - m_new)
    w = jnp.exp(log_w - m_new[..., None])
    c_mat = decay[..., None, None] * c_mat + jnp.einsum('bhs,bhsv,bhsk->bhvk', w, v, k)
    n_vec = decay[..., None] * n_vec + jnp.einsum('bhs,bhsk->bhk', w, k)
    return (c_mat, n_vec, m_new), h


def _mlstm_group(q, k, v, i_pre, f_pre, conv_w, conv_b, gate_b):
    bsz, seq = q.shape[:2]
    n_chunk = seq // CHUNK
    f32 = jnp.float32
    qk = jax.nn.silu(_causal_conv(jnp.concatenate([q, k], -1), conv_w, conv_b))
    q_c, k_c = jnp.split(qk, 2, axis=-1)

    def heads(a, scale=1.0):
        a = a.astype(f32).reshape(bsz, n_chunk, CHUNK, H_B, DH_B) * scale
        return a.transpose(1, 0, 3, 2, 4)

    def per_chunk(a):
        return a.reshape(bsz, n_chunk, CHUNK, H_B).transpose(1, 0, 3, 2)

    ig = per_chunk(i_pre.astype(f32) + gate_b[0])
    lf = per_chunk(jax.nn.log_sigmoid(f_pre.astype(f32) + gate_b[1]))
    init = (jnp.zeros((bsz, H_B, DH_B, DH_B), f32), jnp.zeros((bsz, H_B, DH_B), f32), jnp.zeros((bsz, H_B), f32))
    _, h = lax.scan(_mlstm_chunk, init, (heads(q_c), heads(k_c, DH_B ** -0.5), heads(v), ig, lf))
    return h.transpose(1, 0, 3, 2, 4).reshape(bsz, seq, H_B, DH_B).astype(v.dtype)


def _nsa_group(q, k_cmp, v_cmp, k_slc, v_slc, k_win, v_win, g_pre, cmp_pos, cmp_w1, cmp_w2):
    bsz, seq = q.shape[:2]
    scale = DH_C ** -0.5
    q = q.reshape(bsz, seq, H_C, DH_C)
    n_cmp = (seq - L_CMP) // D_STRIDE + 1
    n_sel = seq // L_SEL
    n_top = min(N_TOP_MAX, n_sel)
    cmp_start = np.arange(n_cmp) * D_STRIDE
    cmp_idx = cmp_start[:, None] + np.arange(L_CMP)[None, :]
    sel_start = np.arange(n_sel) * L_SEL
    cover = jnp.asarray(((cmp_start[:, None] < sel_start[None, :] + L_SEL) & (cmp_start[:, None] + L_CMP > sel_start[None, :])).astype(np.float32))
    cmp_end = jnp.asarray(cmp_start + L_CMP - 1)

    def compress(a, j):
        blocks = (a[:, cmp_idx] + cmp_pos[j]).reshape(bsz, n_cmp, L_CMP * DH_C)
        return jax.nn.silu(blocks @ cmp_w1[j]) @ cmp_w2[j]

    kc, vc = compress(k_cmp, 0), compress(v_cmp, 1)
    ks_blk = k_slc.reshape(bsz, n_sel, L_SEL, DH_C)
    vs_blk = v_slc.reshape(bsz, n_sel, L_SEL, DH_C)
    pad = ((0, 0), (WINDOW, 0), (0, 0))
    kw_pad, vw_pad = jnp.pad(k_win, pad), jnp.pad(v_win, pad)
    gates = jax.nn.sigmoid(g_pre.reshape(bsz, seq, 3, H_C))
    blk = jnp.arange(n_sel)
    in_blk = jnp.arange(L_SEL)
    win_off = jnp.arange(WINDOW + Q_BLOCK) - WINDOW

    def block(args):
        qb, gb, t0 = args
        q_pos = t0 + jnp.arange(Q_BLOCK)
        cmask = cmp_end[None, :] <= q_pos[:, None]
        p_c = _masked_softmax(jnp.einsum('bthd,bnd->bthn', qb, kc) * scale, cmask[None, :, None, :])
        o_cmp = jnp.einsum('bthn,bnd->bthd', p_c.astype(vc.dtype), vc)
        imp = jnp.einsum('bthn,nj->btj', p_c, cover)
        cur = q_pos // L_SEL
        admissible = blk[None, :] <= cur[:, None]
        forced = (blk[None, :] == 0) | (blk[None, :] == cur[:, None]) | (blk[None, :] == cur[:, None] - 1)
        imp = jnp.where((admissible & forced)[None], jnp.inf, jnp.where(admissible[None], imp, -jnp.inf))
        top_val, top_idx = lax.top_k(imp, n_top)
        tok_pos = top_idx[..., None] * L_SEL + in_blk
        smask = ((top_val > -jnp.inf)[..., None] & (tok_pos <= q_pos[None, :, None, None])).reshape(bsz, Q_BLOCK, n_top * L_SEL)
        ks = _gather_rows(ks_blk, top_idx).reshape(bsz, Q_BLOCK, n_top * L_SEL, DH_C)
        vs = _gather_rows(vs_blk, top_idx).reshape(bsz, Q_BLOCK, n_top * L_SEL, DH_C)
        p_s = _masked_softmax(jnp.einsum('bthd,btkd->bthk', qb, ks) * scale, smask[:, :, None, :])
        o_slc = jnp.einsum('bthk,btkd->bthd', p_s.astype(vs.dtype), vs)
        kw = lax.dynamic_slice_in_dim(kw_pad, t0, WINDOW + Q_BLOCK, axis=1)
        vw = lax.dynamic_slice_in_dim(vw_pad, t0, WINDOW + Q_BLOCK, axis=1)
        k_pos = t0 + win_off
        wmask = (k_pos[None, :] >= 0) & (k_pos[None, :] <= q_pos[:, None]) & (q_pos[:, None] - k_pos[None, :] < WINDOW)
        p_w = _masked_softmax(jnp.einsum('bthd,bkd->bthk', qb, kw) * scale, wmask[None, :, None, :])
        o_win = jnp.einsum('bthk,bkd->bthd', p_w.astype(vw.dtype), vw)
        return gb[:, :, 0, :, None] * o_cmp + gb[:, :, 1, :, None] * o_slc + gb[:, :, 2, :, None] * o_win

    out = lax.map(block, (_to_blocks(q), _to_blocks(gates), _block_starts(seq)))
    return _from_blocks(out)


def _hybrid_mixer(u, positions, w_in, w_uk, w_uv, kv_norm, conv_w, conv_b, gate_b, cmp_pos, cmp_w1, cmp_w2, grp_norm, w_out):
    bsz, seq = u.shape[:2]
    cuts = np.cumsum(SPLIT_SIZES)[:-1].tolist()
    (qn, qr, ckv, kr, iq, ik, iw, mq, mk, mv, mi, mf, mo, nq, nkc, nvc, nks, nvs, nkw, nvw, ng) = jnp.split(u @ w_in, cuts, axis=-1)
    o_a = _dsa_group(qn, qr, ckv, kr, iq, ik, iw, positions, w_uk, w_uv, kv_norm)
    h_b = _mlstm_group(mq, mk, mv, mi, mf, conv_w, conv_b, gate_b)
    o_c = _nsa_group(nq, nkc, nvc, nks, nvs, nkw, nvw, ng, cmp_pos, cmp_w1, cmp_w2)
    g_a, g_b, g_c = jnp.split(grp_norm, [H_A * DH_V, H_A * DH_V + H_B * DH_B])
    y_a = _rms_norm(o_a, g_a.reshape(H_A, DH_V)).reshape(bsz, seq, -1)
    y_b = _rms_norm(h_b, g_b.reshape(H_B, DH_B)).reshape(bsz, seq, -1) * jax.nn.sigmoid(mo)
    y_c = _rms_norm(o_c, g_c.reshape(H_C, DH_C)).reshape(bsz, seq, -1)
    return jnp.concatenate([y_a, y_b, y_c], -1) @ w_out


def _swiglu(u, w_gate, w_up, w_down):
    return (jax.nn.silu(u @ w_gate) * (u @ w_up)) @ w_down


def _moe_swiglu(u, router, w_gate, w_up, w_down):
    bsz, seq, d = u.shape
    t = u.reshape(-1, d)
    logits = (t @ router).astype(jnp.float32)
    top_val, top_idx = lax.top_k(logits, TOP_K_EXPERTS)
    weights = jax.nn.softmax(top_val, axis=-1)
    combine = jnp.sum(jax.nn.one_hot(top_idx, N_EXPERTS, dtype=jnp.float32) * weights[..., None], axis=1).astype(t.dtype)
    y = jnp.zeros_like(t)
    for e in range(N_EXPERTS):
        y = y + combine[:, e:e + 1] * _swiglu(t, w_gate[e], w_up[e], w_down[e])
    return y.reshape(bsz, seq, d)


def setup_inputs(seed: int = 0) -> dict:
    key = jax.random.key(seed)
    keys = iter(jax.random.split(key, 40))
    f32 = jnp.float32

    def nrm(shape, scale):
        return jax.random.normal(next(keys), shape, f32) * scale

    def gain(shape):
        return 1.0 + nrm(shape, 0.02)

    x = nrm((BATCH, SEQ, D_MODEL), 1.0)
    c = nrm((BATCH, D_MODEL), 1.0)
    positions = jnp.arange(SEQ, dtype=jnp.int32)[None, :] + jax.random.randint(next(keys), (BATCH, 1), 0, SEQ, dtype=jnp.int32)
    f_bias = jnp.linspace(F_BIAS_LO, F_BIAS_HI, H_B, dtype=f32)
    gate_b = jnp.stack([nrm((DEPTH, H_B), 0.1), f_bias + nrm((DEPTH, H_B), 0.1)], axis=1)
    return {
        'x': x,
        'c': c,
        'positions': positions,
        'w_mod': nrm((D_MODEL, DEPTH * N_MOD * D_MODEL), 0.1 * D_MODEL ** -0.5),
        'b_mod': nrm((DEPTH * N_MOD * D_MODEL,), 0.02),
        'w_in': nrm((DEPTH, D_MODEL, P_IN), D_MODEL ** -0.5),
        'dsa_w_uk': nrm((DEPTH, H_A, DH_NOPE, D_C), D_C ** -0.5),
        'dsa_w_uv': nrm((DEPTH, H_A, D_C, DH_V), D_C ** -0.5),
        'dsa_kv_norm': gain((DEPTH, D_C)),
        'mlstm_conv_w': nrm((DEPTH, CONV_W, 2 * H_B * DH_B), CONV_W ** -0.5),
        'mlstm_conv_b': nrm((DEPTH, 2 * H_B * DH_B), 0.02),
        'mlstm_gate_b': gate_b,
        'nsa_cmp_pos': nrm((DEPTH, 2, L_CMP, DH_C), 0.02),
        'nsa_cmp_w1': nrm((DEPTH, 2, L_CMP * DH_C, CMP_HIDDEN), (L_CMP * DH_C) ** -0.5),
        'nsa_cmp_w2': nrm((DEPTH, 2, CMP_HIDDEN, DH_C), CMP_HIDDEN ** -0.5),
        'grp_norm': gain((DEPTH, MIX_OUT)),
        'w_out': nrm((DEPTH, MIX_OUT, D_MODEL), BETA * MIX_OUT ** -0.5),
        'ln_g': gain((DEPTH, 2, D_MODEL)),
        'ln_b': nrm((DEPTH, 2, D_MODEL), 0.02),
        'ffn_w_gate': nrm((N_DENSE, D_MODEL, D_FF), D_MODEL ** -0.5),
        'ffn_w_up': nrm((N_DENSE, D_MODEL, D_FF), D_MODEL ** -0.5),
        'ffn_w_down': nrm((N_DENSE, D_FF, D_MODEL), BETA * D_FF ** -0.5),
        'moe_router': nrm((N_MOE, D_MODEL, N_EXPERTS), D_MODEL ** -0.5),
        'moe_w_gate': nrm((N_MOE, N_EXPERTS, D_MODEL, D_FF_EXPERT), D_MODEL ** -0.5),
        'moe_w_up': nrm((N_MOE, N_EXPERTS, D_MODEL, D_FF_EXPERT), D_MODEL ** -0.5),
        'moe_w_down': nrm((N_MOE, N_EXPERTS, D_FF_EXPERT, D_MODEL), BETA * D_FF_EXPERT ** -0.5),
    }


def reference(x, c, positions, w_mod, b_mod, w_in, dsa_w_uk, dsa_w_uv, dsa_kv_norm, mlstm_conv_w, mlstm_conv_b, mlstm_gate_b, nsa_cmp_pos, nsa_cmp_w1, nsa_cmp_w2, grp_norm, w_out, ln_g, ln_b, ffn_w_gate, ffn_w_up, ffn_w_down, moe_router, moe_w_gate, moe_w_up, moe_w_down):
    bsz = x.shape[0]
    mod = (jax.nn.silu(c) @ w_mod + b_mod).reshape(bsz, DEPTH, N_MOD, D_MODEL)
    for l in range(DEPTH):
        sh1, sc1, g1, sh2, sc2, g2 = [mod[:, l, j, None, :] for j in range(N_MOD)]
        u = x * (1.0 + sc1) + sh1
        y = _hybrid_mixer(u, positions, w_in[l], dsa_w_uk[l], dsa_w_uv[l], dsa_kv_norm[l], mlstm_conv_w[l], mlstm_conv_b[l], mlstm_gate_b[l], nsa_cmp_pos[l], nsa_cmp_w1[l], nsa_cmp_w2[l], grp_norm[l], w_out[l])
        x = _layer_norm(ALPHA * x + (1.0 + g1) * y, ln_g[l, 0], ln_b[l, 0])
        u = x * (1.0 + sc2) + sh2
        if l % 2 == 0:
            y = _swiglu(u, ffn_w_gate[l // 2], ffn_w_up[l // 2], ffn_w_down[l // 2])
        else:
            y = _moe_swiglu(u, moe_router[l // 2], moe_w_gate[l // 2], moe_w_up[l // 2], moe_w_down[l // 2])
        x = _layer_norm(ALPHA * x + (1.0 + g2) * y, ln_g[l, 1], ln_b[l, 1])
    return x
```

```python
import functools

import numpy as np
import jax
import jax.numpy as jnp
from jax import lax
from jax.experimental import pallas as pl
from jax.experimental.pallas import tpu as pltpu

F32 = jnp.float32
BF16 = jnp.bfloat16

D_MODEL = 1024
DEPTH = 2
H_A, DH_NOPE, DH_ROPE, D_C, DH_V, H_I, D_I = 4, 64, 32, 128, 64, 4, 64
K_SEL_MAX = 256
ROPE_THETA = 10000.0
H_B, DH_B, CONV_W, CHUNK = 4, 128, 4, 64
H_C, DH_C, L_CMP, D_STRIDE, CMP_HIDDEN, L_SEL, N_TOP_MAX, WINDOW, Q_BLOCK = 4, 64, 32, 16, 128, 64, 16, 512, 128
D_FF = 2816
N_EXPERTS = 8
D_FF_EXPERT = 3584
N_MOD = 6
ALPHA = (2 * DEPTH) ** 0.25
SPLIT_SIZES = (H_A * DH_NOPE, H_A * DH_ROPE, D_C, DH_ROPE, H_I * D_I, D_I, H_I, H_B * DH_B, H_B * DH_B, H_B * DH_B, H_B, H_B, H_B * DH_B, H_C * DH_C, DH_C, DH_C, DH_C, DH_C, DH_C, DH_C, 3 * H_C)
N_GROUP_A = sum(SPLIT_SIZES[:7])
N_GROUP_B = sum(SPLIT_SIZES[7:13])
N_GROUP_C = sum(SPLIT_SIZES[13:])

LANE = 128
VMEM_LIMIT = 56 * 1024 * 1024


def _round_up(n, m):
    return (n + m - 1) // m * m


def _params(*sem):
    return pltpu.CompilerParams(dimension_semantics=sem, vmem_limit_bytes=VMEM_LIMIT)


def _layer_norm_rows(z, g, b):
    mu = jnp.mean(z, -1, keepdims=True)
    zc = z - mu
    var = jnp.mean(zc * zc, -1, keepdims=True)
    return zc * lax.rsqrt(var + 1e-5) * g + b


def _mod_kernel(c_ref, w_ref, b_ref, o_ref):
    c = c_ref[...]
    a = c * jax.nn.sigmoid(c)
    o_ref[...] = jnp.dot(a, w_ref[...], preferred_element_type=F32, precision=lax.Precision.HIGHEST) + b_ref[...]


def _modulation(c, w_mod, b_mod):
    bsz, d = c.shape
    n = w_mod.shape[1]
    tn = 1024
    return pl.pallas_call(
        _mod_kernel,
        grid=(n // tn,),
        in_specs=[pl.BlockSpec((bsz, d), lambda j: (0, 0)),
                  pl.BlockSpec((d, tn), lambda j: (0, j)),
                  pl.BlockSpec((1, tn), lambda j: (0, j))],
        out_specs=pl.BlockSpec((bsz, tn), lambda j: (0, j)),
        out_shape=jax.ShapeDtypeStruct((bsz, n), F32),
        compiler_params=_params("arbitrary"),
        name="adaln_mod",
    )(c, w_mod, b_mod.reshape(1, n))


def _inproj_kernel(x_ref, sc_ref, sh_ref, wa_ref, wb_ref, wc_ref, oa_ref, ob_ref, oc_ref):
    u = (x_ref[...] * (1.0 + sc_ref[...]) + sh_ref[...]).astype(BF16)
    oa_ref[...] = jnp.dot(u, wa_ref[...], preferred_element_type=F32)
    ob_ref[...] = jnp.dot(u, wb_ref[...], preferred_element_type=F32)
    oc_ref[...] = jnp.dot(u, wc_ref[...], preferred_element_type=F32)


def _input_projection(x, sc, sh, wa, wb, wc):
    bsz, seq, d = x.shape
    tm = 512
    na, nb, nc = wa.shape[1], wb.shape[1], wc.shape[1]
    row = lambda n: pl.BlockSpec((None, tm, n), lambda b, i: (b, i, 0))
    vec = pl.BlockSpec((None, 1, d), lambda b, i: (b, 0, 0))
    full = lambda n: pl.BlockSpec((d, n), lambda b, i: (0, 0))
    return pl.pallas_call(
        _inproj_kernel,
        grid=(bsz, seq // tm),
        in_specs=[row(d), vec, vec, full(na), full(nb), full(nc)],
        out_specs=[row(na), row(nb), row(nc)],
        out_shape=[jax.ShapeDtypeStruct((bsz, seq, n), F32) for n in (na, nb, nc)],
        compiler_params=_params("arbitrary", "arbitrary"),
        name="input_projection",
    )(x, sc, sh, wa, wb, wc)


def _outproj_kernel(ya_ref, yb_ref, yc_ref, x_ref, g_ref, wa_ref, wb_ref, wc_ref, lng_ref, lnb_ref, o_ref):
    y = jnp.dot(ya_ref[...], wa_ref[...], preferred_element_type=F32)
    y += jnp.dot(yb_ref[...], wb_ref[...], preferred_element_type=F32)
    y += jnp.dot(yc_ref[...], wc_ref[...], preferred_element_type=F32)
    z = ALPHA * x_ref[...] + (1.0 + g_ref[...]) * y
    o_ref[...] = _layer_norm_rows(z, lng_ref[...], lnb_ref[...])


def _output_projection(ya, yb, yc, x, g, w_out, ln_g, ln_b):
    bsz, seq, d = x.shape
    tm = 512
    na, nb, nc = ya.shape[-1], yb.shape[-1], yc.shape[-1]
    wa, wb, wc = w_out[:na], w_out[na:na + nb], w_out[na + nb:]
    row = lambda n: pl.BlockSpec((None, tm, n), lambda b, i: (b, i, 0))
    vec = pl.BlockSpec((None, 1, d), lambda b, i: (b, 0, 0))
    full = lambda r, c: pl.BlockSpec((r, c), lambda b, i: (0, 0))
    return pl.pallas_call(
        _outproj_kernel,
        grid=(bsz, seq // tm),
        in_specs=[row(na), row(nb), row(nc), row(d), vec, full(na, d), full(nb, d), full(nc, d), full(1, d), full(1, d)],
        out_specs=row(d),
        out_shape=jax.ShapeDtypeStruct((bsz, seq, d), F32),
        compiler_params=_params("arbitrary", "arbitrary"),
        name="output_projection_ln",
    )(ya, yb, yc, x, g, wa, wb, wc, ln_g.reshape(1, d), ln_b.reshape(1, d))


def _ffn_kernel(x_ref, sc_ref, sh_ref, g_ref, wg_ref, wu_ref, wd_ref, lng_ref, lnb_ref, o_ref, u_scr, acc_scr):
    f = pl.program_id(2)

    @pl.when(f == 0)
    def _():
        u_scr[...] = (x_ref[...] * (1.0 + sc_ref[...]) + sh_ref[...]).astype(BF16)
        acc_scr[...] = jnp.zeros_like(acc_scr)

    u = u_scr[...]
    a = jnp.dot(u, wg_ref[...], preferred_element_type=F32)
    b = jnp.dot(u, wu_ref[...], preferred_element_type=F32)
    h = (a * jax.nn.sigmoid(a) * b).astype(BF16)
    acc_scr[...] += jnp.dot(h, wd_ref[...], preferred_element_type=F32)

    @pl.when(f == pl.num_programs(2) - 1)
    def _():
        z = ALPHA * x_ref[...] + (1.0 + g_ref[...]) * acc_scr[...]
        o_ref[...] = _layer_norm_rows(z, lng_ref[...], lnb_ref[...])


def _dense_ffn(x, sc, sh, g, wg, wu, wd, ln_g, ln_b):
    bsz, seq, d = x.shape
    ff = wg.shape[1]
    tm, tf = 1024, 256
    row = pl.BlockSpec((None, tm, d), lambda b, i, f: (b, i, 0))
    vec = pl.BlockSpec((None, 1, d), lambda b, i, f: (b, 0, 0))
    one = pl.BlockSpec((1, d), lambda b, i, f: (0, 0))
    return pl.pallas_call(
        _ffn_kernel,
        grid=(bsz, seq // tm, ff // tf),
        in_specs=[row, vec, vec, vec,
                  pl.BlockSpec((d, tf), lambda b, i, f: (0, f)),
                  pl.BlockSpec((d, tf), lambda b, i, f: (0, f)),
                  pl.BlockSpec((tf, d), lambda b, i, f: (f, 0)),
                  one, one],
        out_specs=row,
        out_shape=jax.ShapeDtypeStruct((bsz, seq, d), F32),
        scratch_shapes=[pltpu.VMEM((tm, d), BF16), pltpu.VMEM((tm, d), F32)],
        compiler_params=_params("arbitrary", "arbitrary", "arbitrary"),
        name="dense_swiglu_ln",
    )(x, sc, sh, g, wg, wu, wd, ln_g.reshape(1, d), ln_b.reshape(1, d))


def _router_kernel(x_ref, sc_ref, sh_ref, r_ref, o_ref):
    u = x_ref[...] * (1.0 + sc_ref[...]) + sh_ref[...]
    logits = jnp.dot(u, r_ref[...], preferred_element_type=F32, precision=lax.Precision.HIGHEST)
    lane = lax.broadcasted_iota(jnp.int32, logits.shape, 1)
    neg = -jnp.inf
    l1 = jnp.where(lane < N_EXPERTS, logits, neg)
    m1 = jnp.max(l1, -1, keepdims=True)
    i1 = jnp.min(jnp.where(l1 == m1, lane, LANE), -1, keepdims=True)
    l2 = jnp.where(lane == i1, neg, l1)
    m2 = jnp.max(l2, -1, keepdims=True)
    i2 = jnp.min(jnp.where(l2 == m2, lane, LANE), -1, keepdims=True)
    e2 = jnp.exp(m2 - m1)
    w1 = 1.0 / (1.0 + e2)
    w2 = e2 / (1.0 + e2)
    o_ref[...] = jnp.where(lane == i1, w1, jnp.where(lane == i2, w2, 0.0))


def _moe_router(x, sc, sh, router):
    bsz, seq, d = x.shape
    tm = 512
    r = jnp.pad(router, ((0, 0), (0, LANE - router.shape[1])))
    row = lambda n: pl.BlockSpec((None, tm, n), lambda b, i: (b, i, 0))
    vec = pl.BlockSpec((None, 1, d), lambda b, i: (b, 0, 0))
    return pl.pallas_call(
        _router_kernel,
        grid=(bsz, seq // tm),
        in_specs=[row(d), vec, vec, pl.BlockSpec((d, LANE), lambda b, i: (0, 0))],
        out_specs=row(LANE),
        out_shape=jax.ShapeDtypeStruct((bsz, seq, LANE), F32),
        compiler_params=_params("arbitrary", "arbitrary"),
        name="moe_router",
    )(x, sc, sh, r)


def _moe_kernel(x_ref, sc_ref, sh_ref, g_ref, cmb_ref, wg_ref, wu_ref, wd_ref, lng_ref, lnb_ref, o_ref, u_scr, acc_scr):
    e = pl.program_id(2)
    f = pl.program_id(3)

    @pl.when((e == 0) & (f == 0))
    def _():
        u_scr[...] = (x_ref[...] * (1.0 + sc_ref[...]) + sh_ref[...]).astype(BF16)
        acc_scr[...] = jnp.zeros_like(acc_scr)

    cmb = cmb_ref[...]
    lane = lax.broadcasted_iota(jnp.int32, cmb.shape, 1)
    ce = jnp.sum(jnp.where(lane == e, cmb, 0.0), axis=1, keepdims=True)
    u = u_scr[...]
    a = jnp.dot(u, wg_ref[...], preferred_element_type=F32)
    b = jnp.dot(u, wu_ref[...], preferred_element_type=F32)
    h = (a * jax.nn.sigmoid(a) * b).astype(BF16)
    acc_scr[...] += ce * jnp.dot(h, wd_ref[...], preferred_element_type=F32)

    @pl.when((e == pl.num_programs(2) - 1) & (f == pl.num_programs(3) - 1))
    def _():
        z = ALPHA * x_ref[...] + (1.0 + g_ref[...]) * acc_scr[...]
        o_ref[...] = _layer_norm_rows(z, lng_ref[...], lnb_ref[...])


def _moe_ffn(x, sc, sh, g, combine, wg, wu, wd, ln_g, ln_b):
    bsz, seq, d = x.shape
    n_e, _, ff = wg.shape
    tm, tf = 1024, 512
    row = lambda n: pl.BlockSpec((None, tm, n), lambda b, i, e, f: (b, i, 0))
    vec = pl.BlockSpec((None, 1, d), lambda b, i, e, f: (b, 0, 0))
    one = pl.BlockSpec((1, d), lambda b, i, e, f: (0, 0))
    return pl.pallas_call(
        _moe_kernel,
        grid=(bsz, seq // tm, n_e, ff // tf),
        in_specs=[row(d), vec, vec, vec, row(LANE),
                  pl.BlockSpec((None, d, tf), lambda b, i, e, f: (e, 0, f)),
                  pl.BlockSpec((None, d, tf), lambda b, i, e, f: (e, 0, f)),
                  pl.BlockSpec((None, tf, d), lambda b, i, e, f: (e, f, 0)),
                  one, one],
        out_specs=row(d),
        out_shape=jax.ShapeDtypeStruct((bsz, seq, d), F32),
        scratch_shapes=[pltpu.VMEM((tm, d), BF16), pltpu.VMEM((tm, d), F32)],
        compiler_params=_params("arbitrary", "arbitrary", "arbitrary", "arbitrary"),
        name="moe_swiglu_ln",
    )(x, sc, sh, g, combine, wg, wu, wd, ln_g.reshape(1, d), ln_b.reshape(1, d))


def _rms_norm(x, g, eps=1e-6):
    xf = x.astype(F32)
    return (xf * lax.rsqrt(jnp.mean(xf * xf, -1, keepdims=True) + eps) * g).astype(x.dtype)


def _masked_softmax(s, mask):
    s = jnp.where(mask, s.astype(F32), -jnp.inf)
    m = jnp.max(s, axis=-1, keepdims=True)
    m = jnp.where(jnp.isfinite(m), m, 0.0)
    e = jnp.where(mask, jnp.exp(s - m), 0.0)
    return e / jnp.maximum(jnp.sum(e, axis=-1, keepdims=True), 1e-30)


def _rope(x, ang):
    x1, x2 = jnp.split(x.astype(F32), 2, axis=-1)
    cos, sin = jnp.cos(ang), jnp.sin(ang)
    return jnp.concatenate([x1 * cos - x2 * sin, x1 * sin + x2 * cos], -1).astype(x.dtype)


def _gather_rows(table, idx):
    return jax.vmap(lambda t, i: t[i])(table, idx)


def _to_blocks(a):
    return a.reshape(a.shape[0], -1, Q_BLOCK, *a.shape[2:]).swapaxes(0, 1)


def _from_blocks(a):
    a = a.swapaxes(0, 1)
    return a.reshape(a.shape[0], -1, *a.shape[3:])


def _block_starts(seq):
    return jnp.arange(seq // Q_BLOCK, dtype=jnp.int32) * Q_BLOCK


def _dsa_group(q_nope, q_rope, ckv, k_rope, iq, ik, iw, positions, w_uk, w_uv, kv_norm):
    bsz, seq = q_nope.shape[:2]
    k_sel = min(K_SEL_MAX, seq // 4)
    scale = (DH_NOPE + DH_ROPE) ** -0.5
    inv_freq = ROPE_THETA ** (-jnp.arange(0, DH_ROPE, 2, dtype=F32) / DH_ROPE)
    ang = positions.astype(F32)[..., None] * inv_freq
    q_rope = _rope(q_rope.reshape(bsz, seq, H_A, DH_ROPE), ang[:, :, None, :])
    k_rope = _rope(k_rope, ang)
    ckv = _rms_norm(ckv, kv_norm)
    q_abs = jnp.einsum('bshd,hdc->bshc', q_nope.reshape(bsz, seq, H_A, DH_NOPE), w_uk)
    iq = iq.reshape(bsz, seq, H_I, D_I)
    key_pos = jnp.arange(seq)

    def block(args):
        qa, qr, iqb, iwb, t0 = args
        q_pos = t0 + jnp.arange(Q_BLOCK)
        causal = key_pos[None, :] <= q_pos[:, None]
        logits = jnp.einsum('bthd,bsd->bths', iqb, ik)
        score = jnp.einsum('bths,bth->bts', jax.nn.relu(logits), iwb)
        score = jnp.where(causal[None], score, -jnp.inf)
        top_val, top_idx = lax.top_k(score, k_sel)
        valid = top_val > -jnp.inf
        c_sel = _gather_rows(ckv, top_idx)
        r_sel = _gather_rows(k_rope, top_idx)
        s = (jnp.einsum('bthc,btkc->bthk', qa, c_sel) + jnp.einsum('bthr,btkr->bthk', qr, r_sel)) * scale
        p = _masked_softmax(s, valid[:, :, None, :]).astype(c_sel.dtype)
        o_lat = jnp.einsum('bthk,btkc->bthc', p, c_sel)
        return jnp.einsum('bthc,hcd->bthd', o_lat, w_uv)

    out = lax.map(block, (_to_blocks(q_abs), _to_blocks(q_rope), _to_blocks(iq), _to_blocks(iw), _block_starts(seq)))
    return _from_blocks(out)


def _causal_conv(x, w, b):
    y = lax.conv_general_dilated(x, w[:, None, :].astype(x.dtype), (1,), [(CONV_W - 1, 0)], dimension_numbers=('NWC', 'WIO', 'NWC'), feature_group_count=x.shape[-1])
    return y + b


def _mlstm_chunk(state, inp):
    c_mat, n_vec, m = state
    q, k, v, ig, lf = inp
    b = jnp.cumsum(lf, axis=-1)
    tri = jnp.tril(jnp.ones((CHUNK, CHUNK), bool))
    log_d = jnp.where(tri, b[..., :, None] - b[..., None, :] + ig[..., None, :], -jnp.inf)
    log_inter = b + m[..., None]
    m_out = jnp.maximum(log_inter, jnp.max(log_d, -1))
    d = jnp.exp(log_d - m_out[..., None])
    w_inter = jnp.exp(log_inter - m_out)
    s = jnp.einsum('bhjd,bhsd->bhjs', q, k) * d
    num = w_inter[..., None] * jnp.einsum('bhvk,bhjk->bhjv', c_mat, q) + jnp.einsum('bhjs,bhsv->bhjv', s, v)
    den = w_inter * jnp.einsum('bhk,bhjk->bhj', n_vec, q) + jnp.sum(s, -1)
    h = num / jnp.maximum(jnp.abs(den), jnp.exp(-m_out))[..., None]
    b_last = b[..., -1]
    log_w = b_last[..., None] - b + ig
    m_new = jnp.maximum(b_last + m, jnp.max(log_w, -1))
    decay = jnp.exp(b_last + m - m_new)
    w = jnp.exp(log_w - m_new[..., None])
    c_mat = decay[..., None, None] * c_mat + jnp.einsum('bhs,bhsv,bhsk->bhvk', w, v, k)
    n_vec = decay[..., None] * n_vec + jnp.einsum('bhs,bhsk->bhk', w, k)
    return (c_mat, n_vec, m_new), h


def _mlstm_group(q, k, v, i_pre, f_pre, conv_w, conv_b, gate_b):
    bsz, seq = q.shape[:2]
    n_chunk = seq // CHUNK
    qk = jax.nn.silu(_causal_conv(jnp.concatenate([q, k], -1), conv_w, conv_b))
    q_c, k_c = jnp.split(qk, 2, axis=-1)

    def heads(a, scale=1.0):
        a = a.astype(F32).reshape(bsz, n_chunk, CHUNK, H_B, DH_B) * scale
        return a.transpose(1, 0, 3, 2, 4)

    def per_chunk(a):
        return a.reshape(bsz, n_chunk, CHUNK, H_B).transpose(1, 0, 3, 2)

    ig = per_chunk(i_pre.astype(F32) + gate_b[0])
    lf = per_chunk(jax.nn.log_sigmoid(f_pre.astype(F32) + gate_b[1]))
    init = (jnp.zeros((bsz, H_B, DH_B, DH_B), F32), jnp.zeros((bsz, H_B, DH_B), F32), jnp.zeros((bsz, H_B), F32))
    _, h = lax.scan(_mlstm_chunk, init, (heads(q_c), heads(k_c, DH_B ** -0.5), heads(v), ig, lf))
    return h.transpose(1, 0, 3, 2, 4).reshape(bsz, seq, H_B, DH_B).astype(v.dtype)


def _nsa_group(q, k_cmp, v_cmp, k_slc, v_slc, k_win, v_win, g_pre, cmp_pos, cmp_w1, cmp_w2):
    bsz, seq = q.shape[:2]
    scale = DH_C ** -0.5
    q = q.reshape(bsz, seq, H_C, DH_C)
    n_cmp = (seq - L_CMP) // D_STRIDE + 1
    n_sel = seq // L_SEL
    n_top = min(N_TOP_MAX, n_sel)
    cmp_start = np.arange(n_cmp) * D_STRIDE
    cmp_idx = cmp_start[:, None] + np.arange(L_CMP)[None, :]
    sel_start = np.arange(n_sel) * L_SEL
    cover = jnp.asarray(((cmp_start[:, None] < sel_start[None, :] + L_SEL) & (cmp_start[:, None] + L_CMP > sel_start[None, :])).astype(np.float32))
    cmp_end = jnp.asarray(cmp_start + L_CMP - 1)

    def compress(a, j):
        blocks = (a[:, cmp_idx] + cmp_pos[j]).reshape(bsz, n_cmp, L_CMP * DH_C)
        return jax.nn.silu(blocks @ cmp_w1[j]) @ cmp_w2[j]

    kc, vc = compress(k_cmp, 0), compress(v_cmp, 1)
    ks_blk = k_slc.reshape(bsz, n_sel, L_SEL, DH_C)
    vs_blk = v_slc.reshape(bsz, n_sel, L_SEL, DH_C)
    pad = ((0, 0), (WINDOW, 0), (0, 0))
    kw_pad, vw_pad = jnp.pad(k_win, pad), jnp.pad(v_win, pad)
    gates = jax.nn.sigmoid(g_pre.reshape(bsz, seq, 3, H_C))
    blk = jnp.arange(n_sel)
    in_blk = jnp.arange(L_SEL)
    win_off = jnp.arange(WINDOW + Q_BLOCK) - WINDOW

    def block(args):
        qb, gb, t0 = args
        q_pos = t0 + jnp.arange(Q_BLOCK)
        cmask = cmp_end[None, :] <= q_pos[:, None]
        p_c = _masked_softmax(jnp.einsum('bthd,bnd->bthn', qb, kc) * scale, cmask[None, :, None, :])
        o_cmp = jnp.einsum('bthn,bnd->bthd', p_c.astype(vc.dtype), vc)
        imp = jnp.einsum('bthn,nj->btj', p_c, cover)
        cur = q_pos // L_SEL
        admissible = blk[None, :] <= cur[:, None]
        forced = (blk[None, :] == 0) | (blk[None, :] == cur[:, None]) | (blk[None, :] == cur[:, None] - 1)
        imp = jnp.where((admissible & forced)[None], jnp.inf, jnp.where(admissible[None], imp, -jnp.inf))
        top_val, top_idx = lax.top_k(imp, n_top)
        tok_pos = top_idx[..., None] * L_SEL + in_blk
        smask = ((top_val > -jnp.inf)[..., None] & (tok_pos <= q_pos[None, :, None, None])).reshape(bsz, Q_BLOCK, n_top * L_SEL)
        ks = _gather_rows(ks_blk, top_idx).reshape(bsz, Q_BLOCK, n_top * L_SEL, DH_C)
        vs = _gather_rows(vs_blk, top_idx).reshape(bsz, Q_BLOCK, n_top * L_SEL, DH_C)
        p_s = _masked_softmax(jnp.einsum('bthd,btkd->bthk', qb, ks) * scale, smask[:, :, None, :])
        o_slc = jnp.einsum('bthk,btkd->bthd', p_s.astype(vs.dtype), vs)
        kw = lax.dynamic_slice_in_dim(kw_pad, t0, WINDOW + Q_BLOCK, axis=1)
        vw = lax.dynamic_slice_in_dim(vw_pad, t0, WINDOW + Q_BLOCK, axis=1)
        k_pos = t0 + win_off
        wmask = (k_pos[None, :] >= 0) & (k_pos[None, :] <= q_pos[:, None]) & (q_pos[:, None] - k_pos[None, :] < WINDOW)
        p_w = _masked_softmax(jnp.einsum('bthd,bkd->bthk', qb, kw) * scale, wmask[None, :, None, :])
        o_win = jnp.einsum('bthk,bkd->bthd', p_w.astype(vw.dtype), vw)
        return gb[:, :, 0, :, None] * o_cmp + gb[:, :, 1, :, None] * o_slc + gb[:, :, 2, :, None] * o_win

    out = lax.map(block, (_to_blocks(q), _to_blocks(gates), _block_starts(seq)))
    return _from_blocks(out)


def _split(a, sizes):
    cuts = np.cumsum(sizes)[:-1].tolist()
    return jnp.split(a[..., :sum(sizes)], cuts, axis=-1)


def _mixers(pa, pb, pc, positions, w_uk, w_uv, kv_norm, conv_w, conv_b, gate_b, cmp_pos, cmp_w1, cmp_w2, grp_norm):
    bsz, seq = pa.shape[:2]
    qn, qr, ckv, kr, iq, ik, iw = _split(pa, SPLIT_SIZES[:7])
    mq, mk, mv, mi, mf, mo = _split(pb, SPLIT_SIZES[7:13])
    nq, nkc, nvc, nks, nvs, nkw, nvw, ng = _split(pc, SPLIT_SIZES[13:])
    o_a = _dsa_group(qn, qr, ckv, kr, iq, ik, iw, positions, w_uk, w_uv, kv_norm)
    h_b = _mlstm_group(mq, mk, mv, mi, mf, conv_w, conv_b, gate_b)
    o_c = _nsa_group(nq, nkc, nvc, nks, nvs, nkw, nvw, ng, cmp_pos, cmp_w1, cmp_w2)
    g_a, g_b, g_c = jnp.split(grp_norm, [H_A * DH_V, H_A * DH_V + H_B * DH_B])
    y_a = _rms_norm(o_a, g_a.reshape(H_A, DH_V)).reshape(bsz, seq, -1)
    y_b = _rms_norm(h_b, g_b.reshape(H_B, DH_B)).reshape(bsz, seq, -1) * jax.nn.sigmoid(mo)
    y_c = _rms_norm(o_c, g_c.reshape(H_C, DH_C)).reshape(bsz, seq, -1)
    return y_a.astype(BF16), y_b.astype(BF16), y_c.astype(BF16)


def _pad_cols(w, n):
    return jnp.pad(w, ((0, 0), (0, n - w.shape[1])))


def kernel(x, c, positions, w_mod, b_mod, w_in, dsa_w_uk, dsa_w_uv, dsa_kv_norm, mlstm_conv_w, mlstm_conv_b, mlstm_gate_b, nsa_cmp_pos, nsa_cmp_w1, nsa_cmp_w2, grp_norm, w_out, ln_g, ln_b, ffn_w_gate, ffn_w_up, ffn_w_down, moe_router, moe_w_gate, moe_w_up, moe_w_down):
    bsz = x.shape[0]
    mod = _modulation(c, w_mod, b_mod).reshape(bsz, DEPTH, N_MOD, 1, D_MODEL)
    for l in range(DEPTH):
        sh1, sc1, g1, sh2, sc2, g2 = [mod[:, l, j] for j in range(N_MOD)]
        w = w_in[l].astype(BF16)
        wa = _pad_cols(w[:, :N_GROUP_A], _round_up(N_GROUP_A, LANE))
        wb = _pad_cols(w[:, N_GROUP_A:N_GROUP_A + N_GROUP_B], _round_up(N_GROUP_B, LANE))
        wc = _pad_cols(w[:, N_GROUP_A + N_GROUP_B:], _round_up(N_GROUP_C, LANE))
        pa, pb, pc = _input_projection(x, sc1, sh1, wa, wb, wc)
        ya, yb, yc = _mixers(pa, pb, pc, positions, dsa_w_uk[l], dsa_w_uv[l], dsa_kv_norm[l], mlstm_conv_w[l], mlstm_conv_b[l], mlstm_gate_b[l], nsa_cmp_pos[l], nsa_cmp_w1[l], nsa_cmp_w2[l], grp_norm[l])
        x = _output_projection(ya, yb, yc, x, g1, w_out[l].astype(BF16), ln_g[l, 0], ln_b[l, 0])
        if l % 2 == 0:
            k = l // 2
            x = _dense_ffn(x, sc2, sh2, g2, ffn_w_gate[k].astype(BF16), ffn_w_up[k].astype(BF16), ffn_w_down[k].astype(BF16), ln_g[l, 1], ln_b[l, 1])
        else:
            k = l // 2
            combine = _moe_router(x, sc2, sh2, moe_router[k])
            x = _moe_ffn(x, sc2, sh2, g2, combine, moe_w_gate[k].astype(BF16), moe_w_up[k].astype(BF16), moe_w_down[k].astype(BF16), ln_g[l, 1], ln_b[l, 1])
    return x
```

```python
import functools

import numpy as np
import jax
import jax.numpy as jnp
from jax import lax
from jax.experimental import pallas as pl
from jax.experimental.pallas import tpu as pltpu

F32 = jnp.float32
BF16 = jnp.bfloat16

D_MODEL = 1024
DEPTH = 2
H_A, DH_NOPE, DH_ROPE, D_C, DH_V, H_I, D_I = 4, 64, 32, 128, 64, 4, 64
K_SEL_MAX = 256
ROPE_THETA = 10000.0
H_B, DH_B, CONV_W, CHUNK = 4, 128, 4, 64
H_C, DH_C, L_CMP, D_STRIDE, CMP_HIDDEN, L_SEL, N_TOP_MAX, WINDOW, Q_BLOCK = 4, 64, 32, 16, 128, 64, 16, 512, 128
D_FF = 2816
N_EXPERTS = 8
D_FF_EXPERT = 3584
N_MOD = 6
ALPHA = (2 * DEPTH) ** 0.25
SPLIT_SIZES = (H_A * DH_NOPE, H_A * DH_ROPE, D_C, DH_ROPE, H_I * D_I, D_I, H_I, H_B * DH_B, H_B * DH_B, H_B * DH_B, H_B, H_B, H_B * DH_B, H_C * DH_C, DH_C, DH_C, DH_C, DH_C, DH_C, DH_C, 3 * H_C)
N_GROUP_A = sum(SPLIT_SIZES[:7])
N_GROUP_B = sum(SPLIT_SIZES[7:13])
N_GROUP_C = sum(SPLIT_SIZES[13:])

LANE = 128
VMEM_LIMIT = 56 * 1024 * 1024


def _round_up(n, m):
    return (n + m - 1) // m * m


def _params(*sem):
    return pltpu.CompilerParams(dimension_semantics=sem, vmem_limit_bytes=VMEM_LIMIT)


def _layer_norm_rows(z, g, b):
    mu = jnp.mean(z, -1, keepdims=True)
    zc = z - mu
    var = jnp.mean(zc * zc, -1, keepdims=True)
    return zc * lax.rsqrt(var + 1e-5) * g + b


def _mod_kernel(c_ref, w_ref, b_ref, o_ref):
    c = c_ref[...]
    a = c * jax.nn.sigmoid(c)
    o_ref[...] = jnp.dot(a, w_ref[...], preferred_element_type=F32, precision=lax.Precision.HIGHEST) + b_ref[...]


def _modulation(c, w_mod, b_mod):
    bsz, d = c.shape
    n = w_mod.shape[1]
    tn = 1024
    return pl.pallas_call(
        _mod_kernel,
        grid=(n // tn,),
        in_specs=[pl.BlockSpec((bsz, d), lambda j: (0, 0)),
                  pl.BlockSpec((d, tn), lambda j: (0, j)),
                  pl.BlockSpec((1, tn), lambda j: (0, j))],
        out_specs=pl.BlockSpec((bsz, tn), lambda j: (0, j)),
        out_shape=jax.ShapeDtypeStruct((bsz, n), F32),
        compiler_params=_params("arbitrary"),
        name="adaln_mod",
    )(c, w_mod, b_mod.reshape(1, n))


NSA_Q0 = 0
NSA_CMP0 = H_C * LANE
NSA_KS0 = NSA_CMP0 + LANE
NSA_VS0 = NSA_KS0 + LANE
NSA_KW0 = NSA_VS0 + LANE
NSA_VW0 = NSA_KW0 + LANE
NSA_G0 = NSA_VW0 + LANE
NSA_COLS = NSA_G0 + LANE


def _nsa_weight_layout(w):
    d = w.shape[0]
    nq, nkc, nvc, nks, nvs, nkw, nvw, ng = jnp.split(w, np.cumsum(SPLIT_SIZES[13:])[:-1].tolist(), axis=1)
    z = lambda n: jnp.zeros((d, n), w.dtype)
    half = LANE - DH_C
    cols = []
    for h in range(H_C):
        cols += [nq[:, h * DH_C:(h + 1) * DH_C], z(half)]
    cols += [nkc, nvc]
    for t in (nks, nvs, nkw, nvw):
        cols += [t, z(half)]
    cols += [ng, z(LANE - 3 * H_C)]
    return jnp.concatenate(cols, axis=1)


def _inproj_kernel(x_ref, sc_ref, sh_ref, wa_ref, wb_ref, wc_ref, oa_ref, ob_ref,
                   nq_ref, ncmp_ref, nks_ref, nvs_ref, nkw_ref, nvw_ref, ng_ref):
    u = (x_ref[...] * (1.0 + sc_ref[...]) + sh_ref[...]).astype(BF16)
    oa_ref[...] = jnp.dot(u, wa_ref[...], preferred_element_type=F32)
    ob_ref[...] = jnp.dot(u, wb_ref[...], preferred_element_type=F32)
    oc = jnp.dot(u, wc_ref[...], preferred_element_type=F32)
    nq_ref[...] = oc[:, NSA_Q0:NSA_Q0 + H_C * LANE].astype(BF16)
    ncmp_ref[...] = oc[:, NSA_CMP0:NSA_CMP0 + LANE]
    nks_ref[...] = oc[:, NSA_KS0:NSA_KS0 + LANE].astype(BF16)
    nvs_ref[...] = oc[:, NSA_VS0:NSA_VS0 + LANE].astype(BF16)
    nkw_ref[...] = oc[:, NSA_KW0:NSA_KW0 + LANE].astype(BF16)
    nvw_ref[...] = oc[:, NSA_VW0:NSA_VW0 + LANE].astype(BF16)
    ng_ref[...] = oc[:, NSA_G0:NSA_G0 + LANE]


def _input_projection(x, sc, sh, wa, wb, wc):
    bsz, seq, d = x.shape
    tm = 512
    na, nb, nc = wa.shape[1], wb.shape[1], wc.shape[1]
    row = lambda n: pl.BlockSpec((None, tm, n), lambda b, i: (b, i, 0))
    vec = pl.BlockSpec((None, 1, d), lambda b, i: (b, 0, 0))
    full = lambda n: pl.BlockSpec((d, n), lambda b, i: (0, 0))
    outs = [(na, F32), (nb, F32), (H_C * LANE, BF16), (LANE, F32), (LANE, BF16), (LANE, BF16), (LANE, BF16), (LANE, BF16), (LANE, F32)]
    return pl.pallas_call(
        _inproj_kernel,
        grid=(bsz, seq // tm),
        in_specs=[row(d), vec, vec, full(na), full(nb), full(nc)],
        out_specs=[row(n) for n, _ in outs],
        out_shape=[jax.ShapeDtypeStruct((bsz, seq, n), dt) for n, dt in outs],
        compiler_params=_params("arbitrary", "arbitrary"),
        name="input_projection",
    )(x, sc, sh, wa, wb, wc)


DSA_QN0 = 0
DSA_QR0 = H_A * LANE
DSA_CKV0 = DSA_QR0 + LANE
DSA_IQ0 = DSA_CKV0 + LANE
DSA_G0 = DSA_IQ0 + H_I * LANE
DSA_COLS = DSA_G0 + LANE
DSA_KR_LANE = D_I
DSA_IW_LANE = D_I + DH_ROPE
DSA_TQ = 128
DSA_KC = 512
INT_MIN = -2 ** 31
KEY_NEG_INF = int(np.int32(np.float32(-np.inf).view(np.int32)) ^ np.int32(0x7FFFFFFF))


def _dsa_weight_layout(w):
    d = w.shape[0]
    qn, qr, ckv, kr, iq, ik, iw = jnp.split(w, np.cumsum(SPLIT_SIZES[:7])[:-1].tolist(), axis=1)
    z = lambda n: jnp.zeros((d, n), w.dtype)
    cols = []
    for h in range(H_A):
        cols += [qn[:, h * DH_NOPE:(h + 1) * DH_NOPE], z(LANE - DH_NOPE)]
    cols += [qr, ckv]
    for h in range(H_I):
        cols += [iq[:, h * D_I:(h + 1) * D_I], z(LANE - D_I)]
    cols += [ik, kr, iw, z(LANE - D_I - DH_ROPE - H_I)]
    return jnp.concatenate(cols, axis=1)


def _rope_table_kernel(pos_ref, freq_ref, cos_ref, sin_ref):
    ang = pos_ref[...].astype(F32) * freq_ref[...]
    lane = lax.broadcasted_iota(jnp.int32, (1, LANE), 1)
    first = (lane % DH_ROPE) < DH_ROPE // 2
    cos_ref[...] = jnp.cos(ang)
    sin_ref[...] = jnp.where(first, -jnp.sin(ang), jnp.sin(ang))


def _rope_table(positions):
    bsz, seq = positions.shape
    tm = 512
    inv_freq = ROPE_THETA ** (-jnp.arange(0, DH_ROPE, 2, dtype=F32) / DH_ROPE)
    freq = jnp.tile(inv_freq, LANE // (DH_ROPE // 2)).reshape(1, LANE)
    out = pl.BlockSpec((None, tm, LANE), lambda b, i: (b, i, 0))
    return pl.pallas_call(
        _rope_table_kernel,
        grid=(bsz, seq // tm),
        in_specs=[pl.BlockSpec((None, tm, 1), lambda b, i: (b, i, 0)), pl.BlockSpec((1, LANE), lambda b, i: (0, 0))],
        out_specs=[out, out],
        out_shape=[jax.ShapeDtypeStruct((bsz, seq, LANE), F32)] * 2,
        compiler_params=_params("arbitrary", "arbitrary"),
        name="rope_table",
    )(positions.reshape(bsz, seq, 1), freq)


def _dsa_prep_kernel(pa_ref, cos_ref, sin_ref, wuk_ref, kvn_ref, qc_ref, iq_ref, iw_ref, kc_ref, ik_ref):
    scale = (DH_NOPE + DH_ROPE) ** -0.5
    cos, sin = cos_ref[...], sin_ref[...]
    lane = lax.broadcasted_iota(jnp.int32, (1, LANE), 1)
    first = (lane % DH_ROPE) < DH_ROPE // 2
    rope_lanes = (lane >= DSA_KR_LANE) & (lane < DSA_KR_LANE + DH_ROPE)

    def rope(v):
        partner = jnp.where(first, pltpu.roll(v, LANE - DH_ROPE // 2, 1), pltpu.roll(v, DH_ROPE // 2, 1))
        return v * cos + partner * sin

    g = pa_ref[:, DSA_G0:DSA_G0 + LANE]
    ckv = pa_ref[:, DSA_CKV0:DSA_CKV0 + LANE]
    ckv_n = ckv * lax.rsqrt(jnp.mean(ckv * ckv, -1, keepdims=True) + 1e-6) * kvn_ref[...]
    kc_ref[:, 0:LANE] = ckv_n.astype(BF16)
    kc_ref[:, LANE:2 * LANE] = jnp.where(rope_lanes, rope(g), 0.0).astype(BF16)
    ik_ref[...] = jnp.where(lane < D_I, g, 0.0).astype(BF16)
    iw_ref[...] = g
    iq_ref[...] = pa_ref[:, DSA_IQ0:DSA_IQ0 + H_I * LANE].astype(BF16)
    qr = rope(pa_ref[:, DSA_QR0:DSA_QR0 + LANE]) * scale
    for h in range(H_A):
        qn = pa_ref[:, DSA_QN0 + h * LANE:DSA_QN0 + (h + 1) * LANE].astype(BF16)
        q_abs = jnp.dot(qn, wuk_ref[h], preferred_element_type=F32) * scale
        shift = (DSA_KR_LANE - DH_ROPE * h) % LANE
        qr_h = pltpu.roll(qr, shift, 1) if shift else qr
        qc_ref[:, 2 * h * LANE:(2 * h + 1) * LANE] = q_abs.astype(BF16)
        qc_ref[:, (2 * h + 1) * LANE:(2 * h + 2) * LANE] = jnp.where(rope_lanes, qr_h, 0.0).astype(BF16)


def _dsa_prep(pa, cos, sin, w_uk, kv_norm):
    bsz, seq, _ = pa.shape
    tm = 512
    wuk = jnp.pad(w_uk, ((0, 0), (0, LANE - DH_NOPE), (0, 0))).astype(BF16)
    row = lambda n: pl.BlockSpec((None, tm, n), lambda b, i: (b, i, 0))
    outs = [(2 * H_A * LANE, BF16), (H_I * LANE, BF16), (LANE, F32), (2 * LANE, BF16), (LANE, BF16)]
    return pl.pallas_call(
        _dsa_prep_kernel,
        grid=(bsz, seq // tm),
        in_specs=[row(DSA_COLS), row(LANE), row(LANE),
                  pl.BlockSpec((H_A, LANE, D_C), lambda b, i: (0, 0, 0)), pl.BlockSpec((1, D_C), lambda b, i: (0, 0))],
        out_specs=[row(n) for n, _ in outs],
        out_shape=[jax.ShapeDtypeStruct((bsz, seq, n), dt) for n, dt in outs],
        compiler_params=_params("arbitrary", "arbitrary"),
        name="dsa_prep",
    )(pa, cos, sin, wuk, kv_norm.reshape(1, D_C))


def _dsa_kernel(k_sel, qc_ref, iq_ref, iw_ref, kc_ref, ik_ref, tri_ref, wuv_ref, gn_ref, o_ref, key_scr):
    tq, kcs = DSA_TQ, DSA_KC
    t0 = pl.program_id(1) * tq
    n_chunk = (t0 + tq + kcs - 1) // kcs
    nt = (((1,), (1,)), ((), ()))
    qpos = t0 + lax.broadcasted_iota(jnp.int32, (tq, 1), 0)
    iw = iw_ref[...]

    def score_chunk(c, carry):
        k0 = pl.multiple_of(c * kcs, kcs)
        ikc = ik_ref[pl.ds(k0, kcs), :]
        sc = jnp.zeros((tq, kcs), F32)
        for h in range(H_I):
            lg = lax.dot_general(iq_ref[:, h * LANE:(h + 1) * LANE], ikc, nt, preferred_element_type=F32)
            sc = sc + jnp.maximum(lg, 0.0) * iw[:, DSA_IW_LANE + h:DSA_IW_LANE + h + 1]
        sc = jnp.where(sc == 0.0, 0.0, sc)
        kpos = k0 + lax.broadcasted_iota(jnp.int32, (1, kcs), 1)
        sc = jnp.where(kpos <= qpos, sc, -jnp.inf)
        bits = pltpu.bitcast(sc, jnp.int32)
        key_scr[:, pl.ds(k0, kcs)] = jnp.where(bits < 0, bits ^ 0x7FFFFFFF, bits)
        return carry

    lax.fori_loop(0, n_chunk, score_chunk, 0)

    def count(pred):
        def body(c, acc):
            k0 = pl.multiple_of(c * kcs, kcs)
            one = jnp.where(pred(key_scr[:, pl.ds(k0, kcs)]), 1.0, 0.0)
            part = one[:, 0:LANE]
            for j in range(1, kcs // LANE):
                part = part + one[:, j * LANE:(j + 1) * LANE]
            return acc + part
        acc = lax.fori_loop(0, n_chunk, body, jnp.zeros((tq, LANE), F32))
        return jnp.sum(acc, -1, keepdims=True)

    thr = jnp.where(count(lambda k: k >= 0) >= k_sel, jnp.int32(0), jnp.int32(INT_MIN))

    def bit_step(i, thr):
        cand = thr | jnp.left_shift(jnp.int32(1), 30 - i)
        return jnp.where(count(lambda k: k >= cand) >= k_sel, cand, thr)

    thr = lax.fori_loop(0, 31, bit_step, thr)
    room = k_sel - count(lambda k: k > thr)

    qall = jnp.concatenate([qc_ref[:, 2 * h * LANE:(2 * h + 2) * LANE] for h in range(H_A)], axis=0)

    def attn_chunk(c, carry):
        m, l, acc, seen = carry
        k0 = pl.multiple_of(c * kcs, kcs)
        key = key_scr[:, pl.ds(k0, kcs)]
        tie = jnp.where(key == thr, 1.0, 0.0)
        prefix = jnp.dot(tie.astype(BF16), tri_ref[...], preferred_element_type=F32)
        take = (key > thr) | ((tie > 0.5) & (seen + prefix <= room))
        ok = jnp.where(take & (key > KEY_NEG_INF), 1.0, 0.0)
        seen = seen + jnp.sum(tie, -1, keepdims=True)
        ok4 = jnp.concatenate([ok] * H_A, axis=0) > 0.5
        kv = kc_ref[pl.ds(k0, kcs), :]
        s = lax.dot_general(qall, kv, nt, preferred_element_type=F32)
        s = jnp.where(ok4, s, NEG)
        m_new = jnp.maximum(m, jnp.max(s, -1, keepdims=True))
        alpha = jnp.exp(m - m_new)
        p = jnp.where(ok4, jnp.exp(s - m_new), 0.0)
        l = alpha * l + jnp.sum(p, -1, keepdims=True)
        acc = alpha * acc + jnp.dot(p.astype(BF16), kv[:, 0:D_C], preferred_element_type=F32)
        return m_new, l, acc, seen

    init = (jnp.full((H_A * tq, 1), NEG, F32), jnp.zeros((H_A * tq, 1), F32), jnp.zeros((H_A * tq, D_C), F32), jnp.zeros((tq, 1), F32))
    _, l, acc, _ = lax.fori_loop(0, n_chunk, attn_chunk, init)
    o_lat = (acc / jnp.maximum(l, 1e-30)).astype(BF16)
    for h in range(H_A):
        o = jnp.dot(o_lat[h * tq:(h + 1) * tq], wuv_ref[h], preferred_element_type=F32)
        ms = jnp.sum(o * o, -1, keepdims=True) * (1.0 / DH_V)
        o_ref[:, h * LANE:(h + 1) * LANE] = (o * lax.rsqrt(ms + 1e-6) * gn_ref[h:h + 1, :]).astype(o_ref.dtype)


def _dsa_attention(qc, iq, iw, kc, ik, w_uv, g_a):
    bsz, seq, _ = qc.shape
    k_sel = float(min(K_SEL_MAX, seq // 4))
    tq, kcs = DSA_TQ, DSA_KC
    tri = jnp.asarray(np.triu(np.ones((kcs, kcs), np.float32)), BF16)
    wuv = jnp.pad(w_uv, ((0, 0), (0, 0), (0, LANE - DH_V))).astype(BF16)
    gn = jnp.pad(g_a.reshape(H_A, DH_V), ((0, 0), (0, LANE - DH_V)))
    row = lambda n: pl.BlockSpec((None, tq, n), lambda b, i: (b, i, 0))
    per_b = lambda n: pl.BlockSpec((None, seq, n), lambda b, i: (b, 0, 0))
    return pl.pallas_call(
        functools.partial(_dsa_kernel, k_sel),
        grid=(bsz, seq // tq),
        in_specs=[row(2 * H_A * LANE), row(H_I * LANE), row(LANE), per_b(2 * LANE), per_b(LANE),
                  pl.BlockSpec((kcs, kcs), lambda b, i: (0, 0)), pl.BlockSpec((H_A, D_C, LANE), lambda b, i: (0, 0, 0)),
                  pl.BlockSpec((H_A, LANE), lambda b, i: (0, 0))],
        out_specs=row(H_A * LANE),
        out_shape=jax.ShapeDtypeStruct((bsz, seq, H_A * LANE), BF16),
        scratch_shapes=[pltpu.VMEM((tq, seq), jnp.int32)],
        compiler_params=_params("arbitrary", "arbitrary"),
        name="dsa_attention",
    )(qc, iq, iw, kc, ik, tri, wuv, gn)


def _nsa_compress_kernel(a_ref, pos_ref, w1t_ref, w1b_ref, w2k_ref, w2v_ref, kc_ref, vc_ref):
    a = a_ref[...]
    top = jnp.dot((a + pos_ref[0:1, :]).astype(BF16), w1t_ref[...], preferred_element_type=F32)
    bot = jnp.dot((a + pos_ref[1:2, :]).astype(BF16), w1b_ref[...], preferred_element_type=F32)
    n = a.shape[0]
    pre = top + jnp.concatenate([bot[1:], bot[:1]], axis=0)
    h = (pre * jax.nn.sigmoid(pre)).astype(BF16)
    hid = w2k_ref.shape[0]
    kc_ref[...] = jnp.dot(h[:, :hid], w2k_ref[...], preferred_element_type=F32).astype(BF16)
    vc_ref[...] = jnp.dot(h[:, hid:], w2v_ref[...], preferred_element_type=F32).astype(BF16)


def _nsa_compress(ncmp, cmp_pos, cmp_w1, cmp_w2):
    bsz, seq, _ = ncmp.shape
    n_grp = seq // D_STRIDE
    per = L_CMP // D_STRIDE
    width = D_STRIDE * LANE
    a = ncmp.reshape(bsz, n_grp, width)
    w1 = cmp_w1.reshape(2, per, D_STRIDE, DH_C, CMP_HIDDEN)
    zer = jnp.zeros((D_STRIDE, DH_C, CMP_HIDDEN), cmp_w1.dtype)

    def expand(p):
        wk = jnp.concatenate([w1[0, p], zer], axis=1)
        wv = jnp.concatenate([zer, w1[1, p]], axis=1)
        return jnp.concatenate([wk, wv], axis=2).reshape(width, 2 * CMP_HIDDEN).astype(BF16)

    pos = cmp_pos.reshape(2, per, D_STRIDE, DH_C)
    pos = jnp.concatenate([pos[0], pos[1]], axis=-1).reshape(per, width)
    pad_out = ((0, 0), (0, LANE - DH_C))
    w2k = jnp.pad(cmp_w2[0], pad_out).astype(BF16)
    w2v = jnp.pad(cmp_w2[1], pad_out).astype(BF16)
    full = lambda r, c: pl.BlockSpec((r, c), lambda b: (0, 0))
    out = pl.BlockSpec((None, n_grp, LANE), lambda b: (b, 0, 0))
    return pl.pallas_call(
        _nsa_compress_kernel,
        grid=(bsz,),
        in_specs=[pl.BlockSpec((None, n_grp, width), lambda b: (b, 0, 0)), full(per, width),
                  full(width, 2 * CMP_HIDDEN), full(width, 2 * CMP_HIDDEN), full(CMP_HIDDEN, LANE), full(CMP_HIDDEN, LANE)],
        out_specs=[out, out],
        out_shape=[jax.ShapeDtypeStruct((bsz, n_grp, LANE), BF16)] * 2,
        compiler_params=_params("arbitrary"),
        name="nsa_compress",
    )(a, pos, expand(0), expand(1), w2k, w2v)


NSA_TQ = 128
NSA_KC = 512
NEG = -1e30


def _softmax_rows(s, mask):
    s = jnp.where(mask, s, NEG)
    m = jnp.max(s, -1, keepdims=True)
    e = jnp.where(mask, jnp.exp(s - m), 0.0)
    return e / jnp.maximum(jnp.sum(e, -1, keepdims=True), 1e-30)


def _nsa_kernel(q_ref, g_ref, kc_ref, vc_ref, ks_ref, vs_ref, kw_ref, vw_ref, cover_ref, expand_ref, gn_ref, o_ref):
    tq, kc_sz = NSA_TQ, NSA_KC
    t0 = pl.program_id(1) * tq
    nt = (((1,), (1,)), ((), ()))
    q = q_ref[...] * (DH_C ** -0.5)
    qa = jnp.concatenate([q[:, h * LANE:(h + 1) * LANE] for h in range(H_C)], axis=0)
    rows = lax.broadcasted_iota(jnp.int32, (H_C * tq, 1), 0)
    qpos4 = t0 + (rows & (tq - 1))

    n_grp = kc_ref.shape[0]
    s_c = lax.dot_general(qa, kc_ref[...], nt, preferred_element_type=F32)
    n_idx = lax.broadcasted_iota(jnp.int32, (1, n_grp), 1)
    cmask = (n_idx * D_STRIDE + (L_CMP - 1) <= qpos4) & (n_idx < n_grp - 1)
    p_c = _softmax_rows(s_c, cmask)
    o_cmp = jnp.dot(p_c.astype(BF16), vc_ref[...], preferred_element_type=F32)

    p_sum = p_c[0:tq]
    for h in range(1, H_C):
        p_sum = p_sum + p_c[h * tq:(h + 1) * tq]
    hi = p_sum.astype(BF16)
    lo = (p_sum - hi.astype(F32)).astype(BF16)
    cov = cover_ref[...]
    imp_t = lax.dot_general(cov, hi, nt, preferred_element_type=F32) + lax.dot_general(cov, lo, nt, preferred_element_type=F32)
    n_sel = cov.shape[0]
    jrow = lax.broadcasted_iota(jnp.int32, (n_sel, tq), 0)
    cur = (t0 + lax.broadcasted_iota(jnp.int32, (n_sel, tq), 1)) // L_SEL
    adm = jrow <= cur
    forced = (jrow == 0) | (jrow == cur) | (jrow == cur - 1)
    val = jnp.where(adm & forced, jnp.inf, jnp.where(adm, imp_t, -jnp.inf))
    rank = jnp.zeros((n_sel, tq), F32)
    for jp in range(n_sel):
        r = val[jp:jp + 1, :]
        ahead = (r > val) | ((r == val) & (jrow > jp))
        rank = rank + jnp.where(ahead, 1.0, 0.0)
    sel_t = jnp.where((rank < min(N_TOP_MAX, n_sel)) & (val > -jnp.inf), 1.0, 0.0)
    sel_t = jnp.concatenate([sel_t, jnp.zeros((LANE - n_sel, tq), F32)], axis=0) if n_sel < LANE else sel_t
    sel = sel_t.T.astype(BF16)

    qpos = t0 + lax.broadcasted_iota(jnp.int32, (tq, 1), 0)

    def chunk(c, carry):
        m, l, acc = carry
        k0 = pl.multiple_of(c * kc_sz, kc_sz)
        s = lax.dot_general(qa, ks_ref[pl.ds(k0, kc_sz), :], nt, preferred_element_type=F32)
        picked = jnp.dot(sel, expand_ref[:, pl.ds(k0, kc_sz)], preferred_element_type=F32)
        kpos = k0 + lax.broadcasted_iota(jnp.int32, (1, kc_sz), 1)
        ok = jnp.where(kpos <= qpos, picked, 0.0)
        ok4 = jnp.concatenate([ok] * H_C, axis=0) > 0.5
        s = jnp.where(ok4, s, NEG)
        m_new = jnp.maximum(m, jnp.max(s, -1, keepdims=True))
        alpha = jnp.exp(m - m_new)
        p = jnp.where(ok4, jnp.exp(s - m_new), 0.0)
        l = alpha * l + jnp.sum(p, -1, keepdims=True)
        acc = alpha * acc + jnp.dot(p.astype(BF16), vs_ref[pl.ds(k0, kc_sz), :], preferred_element_type=F32)
        return m_new, l, acc

    n_chunk = (t0 + tq + kc_sz - 1) // kc_sz
    init = (jnp.full((H_C * tq, 1), NEG, F32), jnp.zeros((H_C * tq, 1), F32), jnp.zeros((H_C * tq, LANE), F32))
    _, l_s, acc_s = lax.fori_loop(0, n_chunk, chunk, init)
    o_slc = acc_s / jnp.maximum(l_s, 1e-30)

    span = WINDOW + tq
    w0 = pl.multiple_of(jnp.maximum(t0 - WINDOW, 0), tq)
    s_w = lax.dot_general(qa, kw_ref[pl.ds(w0, span), :], nt, preferred_element_type=F32)
    kpos_w = w0 + lax.broadcasted_iota(jnp.int32, (1, span), 1)
    wmask = (kpos_w <= qpos4) & (qpos4 - kpos_w < WINDOW)
    p_w = _softmax_rows(s_w, wmask)
    o_win = jnp.dot(p_w.astype(BF16), vw_ref[pl.ds(w0, span), :], preferred_element_type=F32)

    gates = jax.nn.sigmoid(g_ref[...])
    for h in range(H_C):
        sl = slice(h * tq, (h + 1) * tq)
        o = (gates[:, h:h + 1] * o_cmp[sl] + gates[:, H_C + h:H_C + h + 1] * o_slc[sl]
             + gates[:, 2 * H_C + h:2 * H_C + h + 1] * o_win[sl])
        ms = jnp.sum(o * o, -1, keepdims=True) * (1.0 / DH_C)
        o_ref[:, h * LANE:(h + 1) * LANE] = (o * lax.rsqrt(ms + 1e-6) * gn_ref[h:h + 1, :]).astype(o_ref.dtype)


def _nsa_attention(nq, ng, kc, vc, nks, nvs, nkw, nvw, g_c):
    bsz, seq, _ = nq.shape
    n_grp = kc.shape[1]
    n_sel = seq // L_SEL
    grp_start = np.arange(n_grp) * D_STRIDE
    sel_start = np.arange(n_sel) * L_SEL
    cover_t = ((grp_start[None, :] < sel_start[:, None] + L_SEL) & (grp_start[None, :] + L_CMP > sel_start[:, None]))
    cover_t = jnp.asarray(cover_t.astype(np.float32), BF16)
    expand = (np.arange(seq)[None, :] // L_SEL == np.arange(LANE)[:, None]).astype(np.float32)
    expand = jnp.asarray(expand, BF16)
    gn = jnp.pad(g_c.reshape(H_C, DH_C), ((0, 0), (0, LANE - DH_C)))
    tq = NSA_TQ
    row = lambda n: pl.BlockSpec((None, tq, n), lambda b, i: (b, i, 0))
    per_b = lambda r: pl.BlockSpec((None, r, LANE), lambda b, i: (b, 0, 0))
    full = lambda r, c: pl.BlockSpec((r, c), lambda b, i: (0, 0))
    return pl.pallas_call(
        _nsa_kernel,
        grid=(bsz, seq // tq),
        in_specs=[row(H_C * LANE), row(LANE), per_b(n_grp), per_b(n_grp), per_b(seq), per_b(seq), per_b(seq), per_b(seq),
                  full(n_sel, n_grp), full(LANE, seq), full(H_C, LANE)],
        out_specs=row(H_C * LANE),
        out_shape=jax.ShapeDtypeStruct((bsz, seq, H_C * LANE), BF16),
        compiler_params=_params("arbitrary", "arbitrary"),
        name="nsa_attention",
    )(nq, ng, kc, vc, nks, nvs, nkw, nvw, cover_t, expand, gn)


def _outproj_kernel(ya_ref, yb_ref, yc_ref, x_ref, g_ref, wa_ref, wb_ref, wc_ref, lng_ref, lnb_ref, o_ref):
    y = jnp.dot(ya_ref[...], wa_ref[...], preferred_element_type=F32)
    y += jnp.dot(yb_ref[...], wb_ref[...], preferred_element_type=F32)
    y += jnp.dot(yc_ref[...], wc_ref[...], preferred_element_type=F32)
    z = ALPHA * x_ref[...] + (1.0 + g_ref[...]) * y
    o_ref[...] = _layer_norm_rows(z, lng_ref[...], lnb_ref[...])


def _pad_head_rows(w, n_head, dh):
    d = w.shape[1]
    return jnp.pad(w.reshape(n_head, dh, d), ((0, 0), (0, LANE - dh), (0, 0))).reshape(n_head * LANE, d)


def _output_projection(ya, yb, yc, x, g, wa, wb, wc, ln_g, ln_b):
    bsz, seq, d = x.shape
    tm = 512
    na, nb, nc = ya.shape[-1], yb.shape[-1], yc.shape[-1]
    row = lambda n: pl.BlockSpec((None, tm, n), lambda b, i: (b, i, 0))
    vec = pl.BlockSpec((None, 1, d), lambda b, i: (b, 0, 0))
    full = lambda r, c: pl.BlockSpec((r, c), lambda b, i: (0, 0))
    return pl.pallas_call(
        _outproj_kernel,
        grid=(bsz, seq // tm),
        in_specs=[row(na), row(nb), row(nc), row(d), vec, full(na, d), full(nb, d), full(nc, d), full(1, d), full(1, d)],
        out_specs=row(d),
        out_shape=jax.ShapeDtypeStruct((bsz, seq, d), F32),
        compiler_params=_params("arbitrary", "arbitrary"),
        name="output_projection_ln",
    )(ya, yb, yc, x, g, wa, wb, wc, ln_g.reshape(1, d), ln_b.reshape(1, d))


def _ffn_kernel(x_ref, sc_ref, sh_ref, g_ref, wg_ref, wu_ref, wd_ref, lng_ref, lnb_ref, o_ref, u_scr, acc_scr):
    f = pl.program_id(2)

    @pl.when(f == 0)
    def _():
        u_scr[...] = (x_ref[...] * (1.0 + sc_ref[...]) + sh_ref[...]).astype(BF16)
        acc_scr[...] = jnp.zeros_like(acc_scr)

    u = u_scr[...]
    a = jnp.dot(u, wg_ref[...], preferred_element_type=F32)
    b = jnp.dot(u, wu_ref[...], preferred_element_type=F32)
    h = (a * jax.nn.sigmoid(a) * b).astype(BF16)
    acc_scr[...] += jnp.dot(h, wd_ref[...], preferred_element_type=F32)

    @pl.when(f == pl.num_programs(2) - 1)
    def _():
        z = ALPHA * x_ref[...] + (1.0 + g_ref[...]) * acc_scr[...]
        o_ref[...] = _layer_norm_rows(z, lng_ref[...], lnb_ref[...])


def _dense_ffn(x, sc, sh, g, wg, wu, wd, ln_g, ln_b):
    bsz, seq, d = x.shape
    ff = wg.shape[1]
    tm, tf = 1024, 256
    row = pl.BlockSpec((None, tm, d), lambda b, i, f: (b, i, 0))
    vec = pl.BlockSpec((None, 1, d), lambda b, i, f: (b, 0, 0))
    one = pl.BlockSpec((1, d), lambda b, i, f: (0, 0))
    return pl.pallas_call(
        _ffn_kernel,
        grid=(bsz, seq // tm, ff // tf),
        in_specs=[row, vec, vec, vec,
                  pl.BlockSpec((d, tf), lambda b, i, f: (0, f)),
                  pl.BlockSpec((d, tf), lambda b, i, f: (0, f)),
                  pl.BlockSpec((tf, d), lambda b, i, f: (f, 0)),
                  one, one],
        out_specs=row,
        out_shape=jax.ShapeDtypeStruct((bsz, seq, d), F32),
        scratch_shapes=[pltpu.VMEM((tm, d), BF16), pltpu.VMEM((tm, d), F32)],
        compiler_params=_params("arbitrary", "arbitrary", "arbitrary"),
        name="dense_swiglu_ln",
    )(x, sc, sh, g, wg, wu, wd, ln_g.reshape(1, d), ln_b.reshape(1, d))


def _router_kernel(x_ref, sc_ref, sh_ref, r_ref, o_ref):
    u = x_ref[...] * (1.0 + sc_ref[...]) + sh_ref[...]
    logits = jnp.dot(u, r_ref[...], preferred_element_type=F32, precision=lax.Precision.HIGHEST)
    lane = lax.broadcasted_iota(jnp.int32, logits.shape, 1)
    neg = -jnp.inf
    l1 = jnp.where(lane < N_EXPERTS, logits, neg)
    m1 = jnp.max(l1, -1, keepdims=True)
    i1 = jnp.min(jnp.where(l1 == m1, lane, LANE), -1, keepdims=True)
    l2 = jnp.where(lane == i1, neg, l1)
    m2 = jnp.max(l2, -1, keepdims=True)
    i2 = jnp.min(jnp.where(l2 == m2, lane, LANE), -1, keepdims=True)
    e2 = jnp.exp(m2 - m1)
    w1 = 1.0 / (1.0 + e2)
    w2 = e2 / (1.0 + e2)
    o_ref[...] = jnp.where(lane == i1, w1, jnp.where(lane == i2, w2, 0.0))


def _moe_router(x, sc, sh, router):
    bsz, seq, d = x.shape
    tm = 512
    r = jnp.pad(router, ((0, 0), (0, LANE - router.shape[1])))
    row = lambda n: pl.BlockSpec((None, tm, n), lambda b, i: (b, i, 0))
    vec = pl.BlockSpec((None, 1, d), lambda b, i: (b, 0, 0))
    return pl.pallas_call(
        _router_kernel,
        grid=(bsz, seq // tm),
        in_specs=[row(d), vec, vec, pl.BlockSpec((d, LANE), lambda b, i: (0, 0))],
        out_specs=row(LANE),
        out_shape=jax.ShapeDtypeStruct((bsz, seq, LANE), F32),
        compiler_params=_params("arbitrary", "arbitrary"),
        name="moe_router",
    )(x, sc, sh, r)


def _moe_kernel(x_ref, sc_ref, sh_ref, g_ref, cmb_ref, wg_ref, wu_ref, wd_ref, lng_ref, lnb_ref, o_ref, u_scr, acc_scr):
    e = pl.program_id(2)
    f = pl.program_id(3)

    @pl.when((e == 0) & (f == 0))
    def _():
        u_scr[...] = (x_ref[...] * (1.0 + sc_ref[...]) + sh_ref[...]).astype(BF16)
        acc_scr[...] = jnp.zeros_like(acc_scr)

    cmb = cmb_ref[...]
    lane = lax.broadcasted_iota(jnp.int32, cmb.shape, 1)
    ce = jnp.sum(jnp.where(lane == e, cmb, 0.0), axis=1, keepdims=True)
    u = u_scr[...]
    a = jnp.dot(u, wg_ref[...], preferred_element_type=F32)
    b = jnp.dot(u, wu_ref[...], preferred_element_type=F32)
    h = (a * jax.nn.sigmoid(a) * b).astype(BF16)
    acc_scr[...] += ce * jnp.dot(h, wd_ref[...], preferred_element_type=F32)

    @pl.when((e == pl.num_programs(2) - 1) & (f == pl.num_programs(3) - 1))
    def _():
        z = ALPHA * x_ref[...] + (1.0 + g_ref[...]) * acc_scr[...]
        o_ref[...] = _layer_norm_rows(z, lng_ref[...], lnb_ref[...])


def _moe_ffn(x, sc, sh, g, combine, wg, wu, wd, ln_g, ln_b):
    bsz, seq, d = x.shape
    n_e, _, ff = wg.shape
    tm, tf = 1024, 512
    row = lambda n: pl.BlockSpec((None, tm, n), lambda b, i, e, f: (b, i, 0))
    vec = pl.BlockSpec((None, 1, d), lambda b, i, e, f: (b, 0, 0))
    one = pl.BlockSpec((1, d), lambda b, i, e, f: (0, 0))
    return pl.pallas_call(
        _moe_kernel,
        grid=(bsz, seq // tm, n_e, ff // tf),
        in_specs=[row(d), vec, vec, vec, row(LANE),
                  pl.BlockSpec((None, d, tf), lambda b, i, e, f: (e, 0, f)),
                  pl.BlockSpec((None, d, tf), lambda b, i, e, f: (e, 0, f)),
                  pl.BlockSpec((None, tf, d), lambda b, i, e, f: (e, f, 0)),
                  one, one],
        out_specs=row(d),
        out_shape=jax.ShapeDtypeStruct((bsz, seq, d), F32),
        scratch_shapes=[pltpu.VMEM((tm, d), BF16), pltpu.VMEM((tm, d), F32)],
        compiler_params=_params("arbitrary", "arbitrary", "arbitrary", "arbitrary"),
        name="moe_swiglu_ln",
    )(x, sc, sh, g, combine, wg, wu, wd, ln_g.reshape(1, d), ln_b.reshape(1, d))


def _rms_norm(x, g, eps=1e-6):
    xf = x.astype(F32)
    return (xf * lax.rsqrt(jnp.mean(xf * xf, -1, keepdims=True) + eps) * g).astype(x.dtype)


def _causal_conv(x, w, b):
    y = lax.conv_general_dilated(x, w[:, None, :].astype(x.dtype), (1,), [(CONV_W - 1, 0)], dimension_numbers=('NWC', 'WIO', 'NWC'), feature_group_count=x.shape[-1])
    return y + b


def _mlstm_chunk(state, inp):
    c_mat, n_vec, m = state
    q, k, v, ig, lf = inp
    b = jnp.cumsum(lf, axis=-1)
    tri = jnp.tril(jnp.ones((CHUNK, CHUNK), bool))
    log_d = jnp.where(tri, b[..., :, None] - b[..., None, :] + ig[..., None, :], -jnp.inf)
    log_inter = b + m[..., None]
    m_out = jnp.maximum(log_inter, jnp.max(log_d, -1))
    d = jnp.exp(log_d - m_out[..., None])
    w_inter = jnp.exp(log_inter - m_out)
    s = jnp.einsum('bhjd,bhsd->bhjs', q, k) * d
    num = w_inter[..., None] * jnp.einsum('bhvk,bhjk->bhjv', c_mat, q) + jnp.einsum('bhjs,bhsv->bhjv', s, v)
    den = w_inter * jnp.einsum('bhk,bhjk->bhj', n_vec, q) + jnp.sum(s, -1)
    h = num / jnp.maximum(jnp.abs(den), jnp.exp(-m_out))[..., None]
    b_last = b[..., -1]
    log_w = b_last[..., None] - b + ig
    m_new = jnp.maximum(b_last + m, jnp.max(log_w, -1))
    decay = jnp.exp(b_last + m - m_new)
    w = jnp.exp(log_w - m_new[..., None])
    c_mat = decay[..., None, None] * c_mat + jnp.einsum('bhs,bhsv,bhsk->bhvk', w, v, k)
    n_vec = decay[..., None] * n_vec + jnp.einsum('bhs,bhsk->bhk', w, k)
    return (c_mat, n_vec, m_new), h


def _mlstm_group(q, k, v, i_pre, f_pre, conv_w, conv_b, gate_b):
    bsz, seq = q.shape[:2]
    n_chunk = seq // CHUNK
    qk = jax.nn.silu(_causal_conv(jnp.concatenate([q, k], -1), conv_w, conv_b))
    q_c, k_c = jnp.split(qk, 2, axis=-1)

    def heads(a, scale=1.0):
        a = a.astype(F32).reshape(bsz, n_chunk, CHUNK, H_B, DH_B) * scale
        return a.transpose(1, 0, 3, 2, 4)

    def per_chunk(a):
        return a.reshape(bsz, n_chunk, CHUNK, H_B).transpose(1, 0, 3, 2)

    ig = per_chunk(i_pre.astype(F32) + gate_b[0])
    lf = per_chunk(jax.nn.log_sigmoid(f_pre.astype(F32) + gate_b[1]))
    init = (jnp.zeros((bsz, H_B, DH_B, DH_B), F32), jnp.zeros((bsz, H_B, DH_B), F32), jnp.zeros((bsz, H_B), F32))
    _, h = lax.scan(_mlstm_chunk, init, (heads(q_c), heads(k_c, DH_B ** -0.5), heads(v), ig, lf))
    return h.transpose(1, 0, 3, 2, 4).reshape(bsz, seq, H_B, DH_B).astype(v.dtype)


def _split(a, sizes):
    cuts = np.cumsum(sizes)[:-1].tolist()
    return jnp.split(a[..., :sum(sizes)], cuts, axis=-1)


def _mixers(pa, pb, nsa, rope, w_uk, w_uv, kv_norm, conv_w, conv_b, gate_b, cmp_pos, cmp_w1, cmp_w2, grp_norm):
    bsz, seq = pa.shape[:2]
    mq, mk, mv, mi, mf, mo = _split(pb, SPLIT_SIZES[7:13])
    g_a, g_b, g_c = jnp.split(grp_norm, [H_A * DH_V, H_A * DH_V + H_B * DH_B])
    qc, iq, iw, kc_a, ik = _dsa_prep(pa, rope[0], rope[1], w_uk, kv_norm)
    y_a = _dsa_attention(qc, iq, iw, kc_a, ik, w_uv, g_a)
    h_b = _mlstm_group(mq, mk, mv, mi, mf, conv_w, conv_b, gate_b)
    nq, ncmp, nks, nvs, nkw, nvw, ng = nsa
    kc, vc = _nsa_compress(ncmp, cmp_pos, cmp_w1, cmp_w2)
    y_c = _nsa_attention(nq, ng, kc, vc, nks, nvs, nkw, nvw, g_c)
    y_b = _rms_norm(h_b, g_b.reshape(H_B, DH_B)).reshape(bsz, seq, -1) * jax.nn.sigmoid(mo)
    return y_a, y_b.astype(BF16), y_c


def _pad_cols(w, n):
    return jnp.pad(w, ((0, 0), (0, n - w.shape[1])))


def kernel(x, c, positions, w_mod, b_mod, w_in, dsa_w_uk, dsa_w_uv, dsa_kv_norm, mlstm_conv_w, mlstm_conv_b, mlstm_gate_b, nsa_cmp_pos, nsa_cmp_w1, nsa_cmp_w2, grp_norm, w_out, ln_g, ln_b, ffn_w_gate, ffn_w_up, ffn_w_down, moe_router, moe_w_gate, moe_w_up, moe_w_down):
    bsz = x.shape[0]
    mod = _modulation(c, w_mod, b_mod).reshape(bsz, DEPTH, N_MOD, 1, D_MODEL)
    rope = _rope_table(positions)
    for l in range(DEPTH):
        sh1, sc1, g1, sh2, sc2, g2 = [mod[:, l, j] for j in range(N_MOD)]
        w = w_in[l].astype(BF16)
        wa = _dsa_weight_layout(w[:, :N_GROUP_A])
        wb = _pad_cols(w[:, N_GROUP_A:N_GROUP_A + N_GROUP_B], _round_up(N_GROUP_B, LANE))
        wc = _nsa_weight_layout(w[:, N_GROUP_A + N_GROUP_B:])
        pa, pb, *nsa = _input_projection(x, sc1, sh1, wa, wb, wc)
        ya, yb, yc = _mixers(pa, pb, nsa, rope, dsa_w_uk[l], dsa_w_uv[l], dsa_kv_norm[l], mlstm_conv_w[l], mlstm_conv_b[l], mlstm_gate_b[l], nsa_cmp_pos[l], nsa_cmp_w1[l], nsa_cmp_w2[l], grp_norm[l])
        wo = w_out[l].astype(BF16)
        n_a, n_b = H_A * DH_V, H_B * DH_B
        x = _output_projection(ya, yb, yc, x, g1, _pad_head_rows(wo[:n_a], H_A, DH_V), wo[n_a:n_a + n_b], _pad_head_rows(wo[n_a + n_b:], H_C, DH_C), ln_g[l, 0], ln_b[l, 0])
        if l % 2 == 0:
            k = l // 2
            x = _dense_ffn(x, sc2, sh2, g2, ffn_w_gate[k].astype(BF16), ffn_w_up[k].astype(BF16), ffn_w_down[k].astype(BF16), ln_g[l, 1], ln_b[l, 1])
        else:
            k = l // 2
            combine = _moe_router(x, sc2, sh2, moe_router[k])
            x = _moe_ffn(x, sc2, sh2, g2, combine, moe_w_gate[k].astype(BF16), moe_w_up[k].astype(BF16), moe_w_down[k].astype(BF16), ln_g[l, 1], ln_b[l, 1])
    return x
```

```python
import functools

import numpy as np
import jax
import jax.numpy as jnp
from jax import lax
from jax.experimental import pallas as pl
from jax.experimental.pallas import tpu as pltpu

F32 = jnp.float32
BF16 = jnp.bfloat16

D_MODEL = 1024
DEPTH = 2
H_A, DH_NOPE, DH_ROPE, D_C, DH_V, H_I, D_I = 4, 64, 32, 128, 64, 4, 64
K_SEL_MAX = 256
ROPE_THETA = 10000.0
H_B, DH_B, CONV_W, CHUNK = 4, 128, 4, 64
H_C, DH_C, L_CMP, D_STRIDE, CMP_HIDDEN, L_SEL, N_TOP_MAX, WINDOW, Q_BLOCK = 4, 64, 32, 16, 128, 64, 16, 512, 128
D_FF = 2816
N_EXPERTS = 8
D_FF_EXPERT = 3584
N_MOD = 6
ALPHA = (2 * DEPTH) ** 0.25
SPLIT_SIZES = (H_A * DH_NOPE, H_A * DH_ROPE, D_C, DH_ROPE, H_I * D_I, D_I, H_I, H_B * DH_B, H_B * DH_B, H_B * DH_B, H_B, H_B, H_B * DH_B, H_C * DH_C, DH_C, DH_C, DH_C, DH_C, DH_C, DH_C, 3 * H_C)
N_GROUP_A = sum(SPLIT_SIZES[:7])
N_GROUP_B = sum(SPLIT_SIZES[7:13])
N_GROUP_C = sum(SPLIT_SIZES[13:])

LANE = 128
VMEM_LIMIT = 56 * 1024 * 1024


def _round_up(n, m):
    return (n + m - 1) // m * m


def _params(*sem):
    return pltpu.CompilerParams(dimension_semantics=sem, vmem_limit_bytes=VMEM_LIMIT)


def _layer_norm_rows(z, g, b):
    mu = jnp.mean(z, -1, keepdims=True)
    zc = z - mu
    var = jnp.mean(zc * zc, -1, keepdims=True)
    return zc * lax.rsqrt(var + 1e-5) * g + b


def _mod_kernel(c_ref, w_ref, b_ref, o_ref):
    c = c_ref[...]
    a = c * jax.nn.sigmoid(c)
    o_ref[...] = jnp.dot(a, w_ref[...], preferred_element_type=F32, precision=lax.Precision.HIGHEST) + b_ref[...]


def _modulation(c, w_mod, b_mod):
    bsz, d = c.shape
    n = w_mod.shape[1]
    tn = 1024
    return pl.pallas_call(
        _mod_kernel,
        grid=(n // tn,),
        in_specs=[pl.BlockSpec((bsz, d), lambda j: (0, 0)),
                  pl.BlockSpec((d, tn), lambda j: (0, j)),
                  pl.BlockSpec((1, tn), lambda j: (0, j))],
        out_specs=pl.BlockSpec((bsz, tn), lambda j: (0, j)),
        out_shape=jax.ShapeDtypeStruct((bsz, n), F32),
        compiler_params=_params("arbitrary"),
        name="adaln_mod",
    )(c, w_mod, b_mod.reshape(1, n))


NSA_Q0 = 0
NSA_CMP0 = H_C * LANE
NSA_KS0 = NSA_CMP0 + LANE
NSA_VS0 = NSA_KS0 + LANE
NSA_KW0 = NSA_VS0 + LANE
NSA_VW0 = NSA_KW0 + LANE
NSA_G0 = NSA_VW0 + LANE
NSA_COLS = NSA_G0 + LANE


def _nsa_weight_layout(w):
    d = w.shape[0]
    nq, nkc, nvc, nks, nvs, nkw, nvw, ng = jnp.split(w, np.cumsum(SPLIT_SIZES[13:])[:-1].tolist(), axis=1)
    z = lambda n: jnp.zeros((d, n), w.dtype)
    half = LANE - DH_C
    cols = []
    for h in range(H_C):
        cols += [nq[:, h * DH_C:(h + 1) * DH_C], z(half)]
    cols += [nkc, nvc]
    for t in (nks, nvs, nkw, nvw):
        cols += [t, z(half)]
    cols += [ng, z(LANE - 3 * H_C)]
    return jnp.concatenate(cols, axis=1)


def _mlstm_weight_layout(w):
    mq, mk, mv, mi, mf, mo = jnp.split(w, np.cumsum(SPLIT_SIZES[7:13])[:-1].tolist(), axis=1)
    return jnp.concatenate([mq, mk, mv, mo, mi, mf, jnp.zeros((w.shape[0], LANE - 2 * H_B), w.dtype)], axis=1)


ML_D = H_B * DH_B


def _inproj_kernel(x_ref, sc_ref, sh_ref, wa_ref, wb_ref, wc_ref, oa_ref, mqk_ref, mv_ref, mo_ref, mg_ref,
                   nq_ref, ncmp_ref, nks_ref, nvs_ref, nkw_ref, nvw_ref, ng_ref):
    u = (x_ref[...] * (1.0 + sc_ref[...]) + sh_ref[...]).astype(BF16)
    oa_ref[...] = jnp.dot(u, wa_ref[...], preferred_element_type=F32)
    ob = jnp.dot(u, wb_ref[...], preferred_element_type=F32)
    mqk_ref[...] = ob[:, 0:2 * ML_D]
    mv_ref[...] = ob[:, 2 * ML_D:3 * ML_D].astype(BF16)
    mo_ref[...] = ob[:, 3 * ML_D:4 * ML_D]
    mg_ref[...] = ob[:, 4 * ML_D:4 * ML_D + LANE]
    oc = jnp.dot(u, wc_ref[...], preferred_element_type=F32)
    nq_ref[...] = oc[:, NSA_Q0:NSA_Q0 + H_C * LANE].astype(BF16)
    ncmp_ref[...] = oc[:, NSA_CMP0:NSA_CMP0 + LANE]
    nks_ref[...] = oc[:, NSA_KS0:NSA_KS0 + LANE].astype(BF16)
    nvs_ref[...] = oc[:, NSA_VS0:NSA_VS0 + LANE].astype(BF16)
    nkw_ref[...] = oc[:, NSA_KW0:NSA_KW0 + LANE].astype(BF16)
    nvw_ref[...] = oc[:, NSA_VW0:NSA_VW0 + LANE].astype(BF16)
    ng_ref[...] = oc[:, NSA_G0:NSA_G0 + LANE]


def _input_projection(x, sc, sh, wa, wb, wc):
    bsz, seq, d = x.shape
    tm = 512
    na, nb, nc = wa.shape[1], wb.shape[1], wc.shape[1]
    row = lambda n: pl.BlockSpec((None, tm, n), lambda b, i: (b, i, 0))
    vec = pl.BlockSpec((None, 1, d), lambda b, i: (b, 0, 0))
    full = lambda n: pl.BlockSpec((d, n), lambda b, i: (0, 0))
    outs = [(na, F32), (2 * ML_D, F32), (ML_D, BF16), (ML_D, F32), (LANE, F32),
            (H_C * LANE, BF16), (LANE, F32), (LANE, BF16), (LANE, BF16), (LANE, BF16), (LANE, BF16), (LANE, F32)]
    return pl.pallas_call(
        _inproj_kernel,
        grid=(bsz, seq // tm),
        in_specs=[row(d), vec, vec, full(na), full(nb), full(nc)],
        out_specs=[row(n) for n, _ in outs],
        out_shape=[jax.ShapeDtypeStruct((bsz, seq, n), dt) for n, dt in outs],
        compiler_params=_params("arbitrary", "arbitrary"),
        name="input_projection",
    )(x, sc, sh, wa, wb, wc)


DSA_QN0 = 0
DSA_QR0 = H_A * LANE
DSA_CKV0 = DSA_QR0 + LANE
DSA_IQ0 = DSA_CKV0 + LANE
DSA_G0 = DSA_IQ0 + H_I * LANE
DSA_COLS = DSA_G0 + LANE
DSA_KR_LANE = D_I
DSA_IW_LANE = D_I + DH_ROPE
DSA_TQ = 128
DSA_KC = 512
INT_MIN = -2 ** 31
KEY_NEG_INF = int(np.int32(np.float32(-np.inf).view(np.int32)) ^ np.int32(0x7FFFFFFF))


def _dsa_weight_layout(w):
    d = w.shape[0]
    qn, qr, ckv, kr, iq, ik, iw = jnp.split(w, np.cumsum(SPLIT_SIZES[:7])[:-1].tolist(), axis=1)
    z = lambda n: jnp.zeros((d, n), w.dtype)
    cols = []
    for h in range(H_A):
        cols += [qn[:, h * DH_NOPE:(h + 1) * DH_NOPE], z(LANE - DH_NOPE)]
    cols += [qr, ckv]
    for h in range(H_I):
        cols += [iq[:, h * D_I:(h + 1) * D_I], z(LANE - D_I)]
    cols += [ik, kr, iw, z(LANE - D_I - DH_ROPE - H_I)]
    return jnp.concatenate(cols, axis=1)


def _rope_table_kernel(pos_ref, freq_ref, cos_ref, sin_ref):
    ang = pos_ref[...].astype(F32) * freq_ref[...]
    lane = lax.broadcasted_iota(jnp.int32, (1, LANE), 1)
    first = (lane % DH_ROPE) < DH_ROPE // 2
    cos_ref[...] = jnp.cos(ang)
    sin_ref[...] = jnp.where(first, -jnp.sin(ang), jnp.sin(ang))


def _rope_table(positions):
    bsz, seq = positions.shape
    tm = 512
    inv_freq = ROPE_THETA ** (-jnp.arange(0, DH_ROPE, 2, dtype=F32) / DH_ROPE)
    freq = jnp.tile(inv_freq, LANE // (DH_ROPE // 2)).reshape(1, LANE)
    out = pl.BlockSpec((None, tm, LANE), lambda b, i: (b, i, 0))
    return pl.pallas_call(
        _rope_table_kernel,
        grid=(bsz, seq // tm),
        in_specs=[pl.BlockSpec((None, tm, 1), lambda b, i: (b, i, 0)), pl.BlockSpec((1, LANE), lambda b, i: (0, 0))],
        out_specs=[out, out],
        out_shape=[jax.ShapeDtypeStruct((bsz, seq, LANE), F32)] * 2,
        compiler_params=_params("arbitrary", "arbitrary"),
        name="rope_table",
    )(positions.reshape(bsz, seq, 1), freq)


def _dsa_prep_kernel(pa_ref, cos_ref, sin_ref, wuk_ref, kvn_ref, qc_ref, iq_ref, iw_ref, kc_ref, ik_ref):
    scale = (DH_NOPE + DH_ROPE) ** -0.5
    cos, sin = cos_ref[...], sin_ref[...]
    lane = lax.broadcasted_iota(jnp.int32, (1, LANE), 1)
    first = (lane % DH_ROPE) < DH_ROPE // 2
    rope_lanes = (lane >= DSA_KR_LANE) & (lane < DSA_KR_LANE + DH_ROPE)

    def rope(v):
        partner = jnp.where(first, pltpu.roll(v, LANE - DH_ROPE // 2, 1), pltpu.roll(v, DH_ROPE // 2, 1))
        return v * cos + partner * sin

    g = pa_ref[:, DSA_G0:DSA_G0 + LANE]
    ckv = pa_ref[:, DSA_CKV0:DSA_CKV0 + LANE]
    ckv_n = ckv * lax.rsqrt(jnp.mean(ckv * ckv, -1, keepdims=True) + 1e-6) * kvn_ref[...]
    kc_ref[:, 0:LANE] = ckv_n.astype(BF16)
    kc_ref[:, LANE:2 * LANE] = jnp.where(rope_lanes, rope(g), 0.0).astype(BF16)
    ik_ref[...] = jnp.where(lane < D_I, g, 0.0).astype(BF16)
    iw_ref[...] = g
    iq_ref[...] = pa_ref[:, DSA_IQ0:DSA_IQ0 + H_I * LANE].astype(BF16)
    qr = rope(pa_ref[:, DSA_QR0:DSA_QR0 + LANE]) * scale
    for h in range(H_A):
        qn = pa_ref[:, DSA_QN0 + h * LANE:DSA_QN0 + (h + 1) * LANE].astype(BF16)
        q_abs = jnp.dot(qn, wuk_ref[h], preferred_element_type=F32) * scale
        shift = (DSA_KR_LANE - DH_ROPE * h) % LANE
        qr_h = pltpu.roll(qr, shift, 1) if shift else qr
        qc_ref[:, 2 * h * LANE:(2 * h + 1) * LANE] = q_abs.astype(BF16)
        qc_ref[:, (2 * h + 1) * LANE:(2 * h + 2) * LANE] = jnp.where(rope_lanes, qr_h, 0.0).astype(BF16)


def _dsa_prep(pa, cos, sin, w_uk, kv_norm):
    bsz, seq, _ = pa.shape
    tm = 512
    wuk = jnp.pad(w_uk, ((0, 0), (0, LANE - DH_NOPE), (0, 0))).astype(BF16)
    row = lambda n: pl.BlockSpec((None, tm, n), lambda b, i: (b, i, 0))
    outs = [(2 * H_A * LANE, BF16), (H_I * LANE, BF16), (LANE, F32), (2 * LANE, BF16), (LANE, BF16)]
    return pl.pallas_call(
        _dsa_prep_kernel,
        grid=(bsz, seq // tm),
        in_specs=[row(DSA_COLS), row(LANE), row(LANE),
                  pl.BlockSpec((H_A, LANE, D_C), lambda b, i: (0, 0, 0)), pl.BlockSpec((1, D_C), lambda b, i: (0, 0))],
        out_specs=[row(n) for n, _ in outs],
        out_shape=[jax.ShapeDtypeStruct((bsz, seq, n), dt) for n, dt in outs],
        compiler_params=_params("arbitrary", "arbitrary"),
        name="dsa_prep",
    )(pa, cos, sin, wuk, kv_norm.reshape(1, D_C))


def _dsa_kernel(k_sel, qc_ref, iq_ref, iw_ref, kc_ref, ik_ref, tri_ref, wuv_ref, gn_ref, o_ref, key_scr):
    tq, kcs = DSA_TQ, DSA_KC
    t0 = pl.program_id(1) * tq
    n_chunk = (t0 + tq + kcs - 1) // kcs
    nt = (((1,), (1,)), ((), ()))
    qpos = t0 + lax.broadcasted_iota(jnp.int32, (tq, 1), 0)
    iw = iw_ref[...]

    def score_chunk(c, carry):
        k0 = pl.multiple_of(c * kcs, kcs)
        ikc = ik_ref[pl.ds(k0, kcs), :]
        sc = jnp.zeros((tq, kcs), F32)
        for h in range(H_I):
            lg = lax.dot_general(iq_ref[:, h * LANE:(h + 1) * LANE], ikc, nt, preferred_element_type=F32)
            sc = sc + jnp.maximum(lg, 0.0) * iw[:, DSA_IW_LANE + h:DSA_IW_LANE + h + 1]
        sc = jnp.where(sc == 0.0, 0.0, sc)
        kpos = k0 + lax.broadcasted_iota(jnp.int32, (1, kcs), 1)
        sc = jnp.where(kpos <= qpos, sc, -jnp.inf)
        bits = pltpu.bitcast(sc, jnp.int32)
        key_scr[:, pl.ds(k0, kcs)] = jnp.where(bits < 0, bits ^ 0x7FFFFFFF, bits)
        return carry

    lax.fori_loop(0, n_chunk, score_chunk, 0)

    def count(pred):
        def body(c, acc):
            k0 = pl.multiple_of(c * kcs, kcs)
            one = jnp.where(pred(key_scr[:, pl.ds(k0, kcs)]), 1.0, 0.0)
            part = one[:, 0:LANE]
            for j in range(1, kcs // LANE):
                part = part + one[:, j * LANE:(j + 1) * LANE]
            return acc + part
        acc = lax.fori_loop(0, n_chunk, body, jnp.zeros((tq, LANE), F32))
        return jnp.sum(acc, -1, keepdims=True)

    thr = jnp.where(count(lambda k: k >= 0) >= k_sel, jnp.int32(0), jnp.int32(INT_MIN))

    def bit_step(i, thr):
        cand = thr | jnp.left_shift(jnp.int32(1), 30 - i)
        return jnp.where(count(lambda k: k >= cand) >= k_sel, cand, thr)

    thr = lax.fori_loop(0, 31, bit_step, thr)
    room = k_sel - count(lambda k: k > thr)

    qall = jnp.concatenate([qc_ref[:, 2 * h * LANE:(2 * h + 2) * LANE] for h in range(H_A)], axis=0)

    def attn_chunk(c, carry):
        m, l, acc, seen = carry
        k0 = pl.multiple_of(c * kcs, kcs)
        key = key_scr[:, pl.ds(k0, kcs)]
        tie = jnp.where(key == thr, 1.0, 0.0)
        prefix = jnp.dot(tie.astype(BF16), tri_ref[...], preferred_element_type=F32)
        take = (key > thr) | ((tie > 0.5) & (seen + prefix <= room))
        ok = jnp.where(take & (key > KEY_NEG_INF), 1.0, 0.0)
        seen = seen + jnp.sum(tie, -1, keepdims=True)
        ok4 = jnp.concatenate([ok] * H_A, axis=0) > 0.5
        kv = kc_ref[pl.ds(k0, kcs), :]
        s = lax.dot_general(qall, kv, nt, preferred_element_type=F32)
        s = jnp.where(ok4, s, NEG)
        m_new = jnp.maximum(m, jnp.max(s, -1, keepdims=True))
        alpha = jnp.exp(m - m_new)
        p = jnp.where(ok4, jnp.exp(s - m_new), 0.0)
        l = alpha * l + jnp.sum(p, -1, keepdims=True)
        acc = alpha * acc + jnp.dot(p.astype(BF16), kv[:, 0:D_C], preferred_element_type=F32)
        return m_new, l, acc, seen

    init = (jnp.full((H_A * tq, 1), NEG, F32), jnp.zeros((H_A * tq, 1), F32), jnp.zeros((H_A * tq, D_C), F32), jnp.zeros((tq, 1), F32))
    _, l, acc, _ = lax.fori_loop(0, n_chunk, attn_chunk, init)
    o_lat = (acc / jnp.maximum(l, 1e-30)).astype(BF16)
    for h in range(H_A):
        o = jnp.dot(o_lat[h * tq:(h + 1) * tq], wuv_ref[h], preferred_element_type=F32)
        ms = jnp.sum(o * o, -1, keepdims=True) * (1.0 / DH_V)
        o_ref[:, h * LANE:(h + 1) * LANE] = (o * lax.rsqrt(ms + 1e-6) * gn_ref[h:h + 1, :]).astype(o_ref.dtype)


def _dsa_attention(qc, iq, iw, kc, ik, w_uv, g_a):
    bsz, seq, _ = qc.shape
    k_sel = float(min(K_SEL_MAX, seq // 4))
    tq, kcs = DSA_TQ, DSA_KC
    tri = jnp.asarray(np.triu(np.ones((kcs, kcs), np.float32)), BF16)
    wuv = jnp.pad(w_uv, ((0, 0), (0, 0), (0, LANE - DH_V))).astype(BF16)
    gn = jnp.pad(g_a.reshape(H_A, DH_V), ((0, 0), (0, LANE - DH_V)))
    row = lambda n: pl.BlockSpec((None, tq, n), lambda b, i: (b, i, 0))
    per_b = lambda n: pl.BlockSpec((None, seq, n), lambda b, i: (b, 0, 0))
    return pl.pallas_call(
        functools.partial(_dsa_kernel, k_sel),
        grid=(bsz, seq // tq),
        in_specs=[row(2 * H_A * LANE), row(H_I * LANE), row(LANE), per_b(2 * LANE), per_b(LANE),
                  pl.BlockSpec((kcs, kcs), lambda b, i: (0, 0)), pl.BlockSpec((H_A, D_C, LANE), lambda b, i: (0, 0, 0)),
                  pl.BlockSpec((H_A, LANE), lambda b, i: (0, 0))],
        out_specs=row(H_A * LANE),
        out_shape=jax.ShapeDtypeStruct((bsz, seq, H_A * LANE), BF16),
        scratch_shapes=[pltpu.VMEM((tq, seq), jnp.int32)],
        compiler_params=_params("arbitrary", "arbitrary"),
        name="dsa_attention",
    )(qc, iq, iw, kc, ik, tri, wuv, gn)


def _nsa_compress_kernel(a_ref, pos_ref, w1t_ref, w1b_ref, w2k_ref, w2v_ref, kc_ref, vc_ref):
    a = a_ref[...]
    top = jnp.dot((a + pos_ref[0:1, :]).astype(BF16), w1t_ref[...], preferred_element_type=F32)
    bot = jnp.dot((a + pos_ref[1:2, :]).astype(BF16), w1b_ref[...], preferred_element_type=F32)
    n = a.shape[0]
    pre = top + jnp.concatenate([bot[1:], bot[:1]], axis=0)
    h = (pre * jax.nn.sigmoid(pre)).astype(BF16)
    hid = w2k_ref.shape[0]
    kc_ref[...] = jnp.dot(h[:, :hid], w2k_ref[...], preferred_element_type=F32).astype(BF16)
    vc_ref[...] = jnp.dot(h[:, hid:], w2v_ref[...], preferred_element_type=F32).astype(BF16)


def _nsa_compress(ncmp, cmp_pos, cmp_w1, cmp_w2):
    bsz, seq, _ = ncmp.shape
    n_grp = seq // D_STRIDE
    per = L_CMP // D_STRIDE
    width = D_STRIDE * LANE
    a = ncmp.reshape(bsz, n_grp, width)
    w1 = cmp_w1.reshape(2, per, D_STRIDE, DH_C, CMP_HIDDEN)
    zer = jnp.zeros((D_STRIDE, DH_C, CMP_HIDDEN), cmp_w1.dtype)

    def expand(p):
        wk = jnp.concatenate([w1[0, p], zer], axis=1)
        wv = jnp.concatenate([zer, w1[1, p]], axis=1)
        return jnp.concatenate([wk, wv], axis=2).reshape(width, 2 * CMP_HIDDEN).astype(BF16)

    pos = cmp_pos.reshape(2, per, D_STRIDE, DH_C)
    pos = jnp.concatenate([pos[0], pos[1]], axis=-1).reshape(per, width)
    pad_out = ((0, 0), (0, LANE - DH_C))
    w2k = jnp.pad(cmp_w2[0], pad_out).astype(BF16)
    w2v = jnp.pad(cmp_w2[1], pad_out).astype(BF16)
    full = lambda r, c: pl.BlockSpec((r, c), lambda b: (0, 0))
    out = pl.BlockSpec((None, n_grp, LANE), lambda b: (b, 0, 0))
    return pl.pallas_call(
        _nsa_compress_kernel,
        grid=(bsz,),
        in_specs=[pl.BlockSpec((None, n_grp, width), lambda b: (b, 0, 0)), full(per, width),
                  full(width, 2 * CMP_HIDDEN), full(width, 2 * CMP_HIDDEN), full(CMP_HIDDEN, LANE), full(CMP_HIDDEN, LANE)],
        out_specs=[out, out],
        out_shape=[jax.ShapeDtypeStruct((bsz, n_grp, LANE), BF16)] * 2,
        compiler_params=_params("arbitrary"),
        name="nsa_compress",
    )(a, pos, expand(0), expand(1), w2k, w2v)


NSA_TQ = 128
NSA_KC = 512
NEG = -1e30


def _softmax_rows(s, mask):
    s = jnp.where(mask, s, NEG)
    m = jnp.max(s, -1, keepdims=True)
    e = jnp.where(mask, jnp.exp(s - m), 0.0)
    return e / jnp.maximum(jnp.sum(e, -1, keepdims=True), 1e-30)


def _nsa_kernel(q_ref, g_ref, kc_ref, vc_ref, ks_ref, vs_ref, kw_ref, vw_ref, cover_ref, expand_ref, gn_ref, o_ref):
    tq, kc_sz = NSA_TQ, NSA_KC
    t0 = pl.program_id(1) * tq
    nt = (((1,), (1,)), ((), ()))
    q = q_ref[...] * (DH_C ** -0.5)
    qa = jnp.concatenate([q[:, h * LANE:(h + 1) * LANE] for h in range(H_C)], axis=0)
    rows = lax.broadcasted_iota(jnp.int32, (H_C * tq, 1), 0)
    qpos4 = t0 + (rows & (tq - 1))

    n_grp = kc_ref.shape[0]
    s_c = lax.dot_general(qa, kc_ref[...], nt, preferred_element_type=F32)
    n_idx = lax.broadcasted_iota(jnp.int32, (1, n_grp), 1)
    cmask = (n_idx * D_STRIDE + (L_CMP - 1) <= qpos4) & (n_idx < n_grp - 1)
    p_c = _softmax_rows(s_c, cmask)
    o_cmp = jnp.dot(p_c.astype(BF16), vc_ref[...], preferred_element_type=F32)

    p_sum = p_c[0:tq]
    for h in range(1, H_C):
        p_sum = p_sum + p_c[h * tq:(h + 1) * tq]
    hi = p_sum.astype(BF16)
    lo = (p_sum - hi.astype(F32)).astype(BF16)
    cov = cover_ref[...]
    imp_t = lax.dot_general(cov, hi, nt, preferred_element_type=F32) + lax.dot_general(cov, lo, nt, preferred_element_type=F32)
    n_sel = cov.shape[0]
    jrow = lax.broadcasted_iota(jnp.int32, (n_sel, tq), 0)
    cur = (t0 + lax.broadcasted_iota(jnp.int32, (n_sel, tq), 1)) // L_SEL
    adm = jrow <= cur
    forced = (jrow == 0) | (jrow == cur) | (jrow == cur - 1)
    val = jnp.where(adm & forced, jnp.inf, jnp.where(adm, imp_t, -jnp.inf))
    rank = jnp.zeros((n_sel, tq), F32)
    for jp in range(n_sel):
        r = val[jp:jp + 1, :]
        ahead = (r > val) | ((r == val) & (jrow > jp))
        rank = rank + jnp.where(ahead, 1.0, 0.0)
    sel_t = jnp.where((rank < min(N_TOP_MAX, n_sel)) & (val > -jnp.inf), 1.0, 0.0)
    sel_t = jnp.concatenate([sel_t, jnp.zeros((LANE - n_sel, tq), F32)], axis=0) if n_sel < LANE else sel_t
    sel = sel_t.T.astype(BF16)

    qpos = t0 + lax.broadcasted_iota(jnp.int32, (tq, 1), 0)

    def chunk(c, carry):
        m, l, acc = carry
        k0 = pl.multiple_of(c * kc_sz, kc_sz)
        s = lax.dot_general(qa, ks_ref[pl.ds(k0, kc_sz), :], nt, preferred_element_type=F32)
        picked = jnp.dot(sel, expand_ref[:, pl.ds(k0, kc_sz)], preferred_element_type=F32)
        kpos = k0 + lax.broadcasted_iota(jnp.int32, (1, kc_sz), 1)
        ok = jnp.where(kpos <= qpos, picked, 0.0)
        ok4 = jnp.concatenate([ok] * H_C, axis=0) > 0.5
        s = jnp.where(ok4, s, NEG)
        m_new = jnp.maximum(m, jnp.max(s, -1, keepdims=True))
        alpha = jnp.exp(m - m_new)
        p = jnp.where(ok4, jnp.exp(s - m_new), 0.0)
        l = alpha * l + jnp.sum(p, -1, keepdims=True)
        acc = alpha * acc + jnp.dot(p.astype(BF16), vs_ref[pl.ds(k0, kc_sz), :], preferred_element_type=F32)
        return m_new, l, acc

    n_chunk = (t0 + tq + kc_sz - 1) // kc_sz
    init = (jnp.full((H_C * tq, 1), NEG, F32), jnp.zeros((H_C * tq, 1), F32), jnp.zeros((H_C * tq, LANE), F32))
    _, l_s, acc_s = lax.fori_loop(0, n_chunk, chunk, init)
    o_slc = acc_s / jnp.maximum(l_s, 1e-30)

    span = WINDOW + tq
    w0 = pl.multiple_of(jnp.maximum(t0 - WINDOW, 0), tq)
    s_w = lax.dot_general(qa, kw_ref[pl.ds(w0, span), :], nt, preferred_element_type=F32)
    kpos_w = w0 + lax.broadcasted_iota(jnp.int32, (1, span), 1)
    wmask = (kpos_w <= qpos4) & (qpos4 - kpos_w < WINDOW)
    p_w = _softmax_rows(s_w, wmask)
    o_win = jnp.dot(p_w.astype(BF16), vw_ref[pl.ds(w0, span), :], preferred_element_type=F32)

    gates = jax.nn.sigmoid(g_ref[...])
    for h in range(H_C):
        sl = slice(h * tq, (h + 1) * tq)
        o = (gates[:, h:h + 1] * o_cmp[sl] + gates[:, H_C + h:H_C + h + 1] * o_slc[sl]
             + gates[:, 2 * H_C + h:2 * H_C + h + 1] * o_win[sl])
        ms = jnp.sum(o * o, -1, keepdims=True) * (1.0 / DH_C)
        o_ref[:, h * LANE:(h + 1) * LANE] = (o * lax.rsqrt(ms + 1e-6) * gn_ref[h:h + 1, :]).astype(o_ref.dtype)


def _nsa_attention(nq, ng, kc, vc, nks, nvs, nkw, nvw, g_c):
    bsz, seq, _ = nq.shape
    n_grp = kc.shape[1]
    n_sel = seq // L_SEL
    grp_start = np.arange(n_grp) * D_STRIDE
    sel_start = np.arange(n_sel) * L_SEL
    cover_t = ((grp_start[None, :] < sel_start[:, None] + L_SEL) & (grp_start[None, :] + L_CMP > sel_start[:, None]))
    cover_t = jnp.asarray(cover_t.astype(np.float32), BF16)
    expand = (np.arange(seq)[None, :] // L_SEL == np.arange(LANE)[:, None]).astype(np.float32)
    expand = jnp.asarray(expand, BF16)
    gn = jnp.pad(g_c.reshape(H_C, DH_C), ((0, 0), (0, LANE - DH_C)))
    tq = NSA_TQ
    row = lambda n: pl.BlockSpec((None, tq, n), lambda b, i: (b, i, 0))
    per_b = lambda r: pl.BlockSpec((None, r, LANE), lambda b, i: (b, 0, 0))
    full = lambda r, c: pl.BlockSpec((r, c), lambda b, i: (0, 0))
    return pl.pallas_call(
        _nsa_kernel,
        grid=(bsz, seq // tq),
        in_specs=[row(H_C * LANE), row(LANE), per_b(n_grp), per_b(n_grp), per_b(seq), per_b(seq), per_b(seq), per_b(seq),
                  full(n_sel, n_grp), full(LANE, seq), full(H_C, LANE)],
        out_specs=row(H_C * LANE),
        out_shape=jax.ShapeDtypeStruct((bsz, seq, H_C * LANE), BF16),
        compiler_params=_params("arbitrary", "arbitrary"),
        name="nsa_attention",
    )(nq, ng, kc, vc, nks, nvs, nkw, nvw, cover_t, expand, gn)


def _outproj_kernel(ya_ref, yb_ref, yc_ref, x_ref, g_ref, wa_ref, wb_ref, wc_ref, lng_ref, lnb_ref, o_ref):
    y = jnp.dot(ya_ref[...], wa_ref[...], preferred_element_type=F32)
    y += jnp.dot(yb_ref[...], wb_ref[...], preferred_element_type=F32)
    y += jnp.dot(yc_ref[...], wc_ref[...], preferred_element_type=F32)
    z = ALPHA * x_ref[...] + (1.0 + g_ref[...]) * y
    o_ref[...] = _layer_norm_rows(z, lng_ref[...], lnb_ref[...])


def _pad_head_rows(w, n_head, dh):
    d = w.shape[1]
    return jnp.pad(w.reshape(n_head, dh, d), ((0, 0), (0, LANE - dh), (0, 0))).reshape(n_head * LANE, d)


def _output_projection(ya, yb, yc, x, g, wa, wb, wc, ln_g, ln_b):
    bsz, seq, d = x.shape
    tm = 512
    na, nb, nc = ya.shape[-1], yb.shape[-1], yc.shape[-1]
    row = lambda n: pl.BlockSpec((None, tm, n), lambda b, i: (b, i, 0))
    vec = pl.BlockSpec((None, 1, d), lambda b, i: (b, 0, 0))
    full = lambda r, c: pl.BlockSpec((r, c), lambda b, i: (0, 0))
    return pl.pallas_call(
        _outproj_kernel,
        grid=(bsz, seq // tm),
        in_specs=[row(na), row(nb), row(nc), row(d), vec, full(na, d), full(nb, d), full(nc, d), full(1, d), full(1, d)],
        out_specs=row(d),
        out_shape=jax.ShapeDtypeStruct((bsz, seq, d), F32),
        compiler_params=_params("arbitrary", "arbitrary"),
        name="output_projection_ln",
    )(ya, yb, yc, x, g, wa, wb, wc, ln_g.reshape(1, d), ln_b.reshape(1, d))


def _ffn_kernel(x_ref, sc_ref, sh_ref, g_ref, wg_ref, wu_ref, wd_ref, lng_ref, lnb_ref, o_ref, u_scr, acc_scr):
    f = pl.program_id(2)

    @pl.when(f == 0)
    def _():
        u_scr[...] = (x_ref[...] * (1.0 + sc_ref[...]) + sh_ref[...]).astype(BF16)
        acc_scr[...] = jnp.zeros_like(acc_scr)

    u = u_scr[...]
    a = jnp.dot(u, wg_ref[...], preferred_element_type=F32)
    b = jnp.dot(u, wu_ref[...], preferred_element_type=F32)
    h = (a * jax.nn.sigmoid(a) * b).astype(BF16)
    acc_scr[...] += jnp.dot(h, wd_ref[...], preferred_element_type=F32)

    @pl.when(f == pl.num_programs(2) - 1)
    def _():
        z = ALPHA * x_ref[...] + (1.0 + g_ref[...]) * acc_scr[...]
        o_ref[...] = _layer_norm_rows(z, lng_ref[...], lnb_ref[...])


def _dense_ffn(x, sc, sh, g, wg, wu, wd, ln_g, ln_b):
    bsz, seq, d = x.shape
    ff = wg.shape[1]
    tm, tf = 1024, 256
    row = pl.BlockSpec((None, tm, d), lambda b, i, f: (b, i, 0))
    vec = pl.BlockSpec((None, 1, d), lambda b, i, f: (b, 0, 0))
    one = pl.BlockSpec((1, d), lambda b, i, f: (0, 0))
    return pl.pallas_call(
        _ffn_kernel,
        grid=(bsz, seq // tm, ff // tf),
        in_specs=[row, vec, vec, vec,
                  pl.BlockSpec((d, tf), lambda b, i, f: (0, f)),
                  pl.BlockSpec((d, tf), lambda b, i, f: (0, f)),
                  pl.BlockSpec((tf, d), lambda b, i, f: (f, 0)),
                  one, one],
        out_specs=row,
        out_shape=jax.ShapeDtypeStruct((bsz, seq, d), F32),
        scratch_shapes=[pltpu.VMEM((tm, d), BF16), pltpu.VMEM((tm, d), F32)],
        compiler_params=_params("arbitrary", "arbitrary", "arbitrary"),
        name="dense_swiglu_ln",
    )(x, sc, sh, g, wg, wu, wd, ln_g.reshape(1, d), ln_b.reshape(1, d))


def _router_kernel(x_ref, sc_ref, sh_ref, r_ref, o_ref):
    u = x_ref[...] * (1.0 + sc_ref[...]) + sh_ref[...]
    logits = jnp.dot(u, r_ref[...], preferred_element_type=F32, precision=lax.Precision.HIGHEST)
    lane = lax.broadcasted_iota(jnp.int32, logits.shape, 1)
    neg = -jnp.inf
    l1 = jnp.where(lane < N_EXPERTS, logits, neg)
    m1 = jnp.max(l1, -1, keepdims=True)
    i1 = jnp.min(jnp.where(l1 == m1, lane, LANE), -1, keepdims=True)
    l2 = jnp.where(lane == i1, neg, l1)
    m2 = jnp.max(l2, -1, keepdims=True)
    i2 = jnp.min(jnp.where(l2 == m2, lane, LANE), -1, keepdims=True)
    e2 = jnp.exp(m2 - m1)
    w1 = 1.0 / (1.0 + e2)
    w2 = e2 / (1.0 + e2)
    o_ref[...] = jnp.where(lane == i1, w1, jnp.where(lane == i2, w2, 0.0))


def _moe_router(x, sc, sh, router):
    bsz, seq, d = x.shape
    tm = 512
    r = jnp.pad(router, ((0, 0), (0, LANE - router.shape[1])))
    row = lambda n: pl.BlockSpec((None, tm, n), lambda b, i: (b, i, 0))
    vec = pl.BlockSpec((None, 1, d), lambda b, i: (b, 0, 0))
    return pl.pallas_call(
        _router_kernel,
        grid=(bsz, seq // tm),
        in_specs=[row(d), vec, vec, pl.BlockSpec((d, LANE), lambda b, i: (0, 0))],
        out_specs=row(LANE),
        out_shape=jax.ShapeDtypeStruct((bsz, seq, LANE), F32),
        compiler_params=_params("arbitrary", "arbitrary"),
        name="moe_router",
    )(x, sc, sh, r)


def _moe_kernel(x_ref, sc_ref, sh_ref, g_ref, cmb_ref, wg_ref, wu_ref, wd_ref, lng_ref, lnb_ref, o_ref, u_scr, acc_scr):
    e = pl.program_id(2)
    f = pl.program_id(3)

    @pl.when((e == 0) & (f == 0))
    def _():
        u_scr[...] = (x_ref[...] * (1.0 + sc_ref[...]) + sh_ref[...]).astype(BF16)
        acc_scr[...] = jnp.zeros_like(acc_scr)

    cmb = cmb_ref[...]
    lane = lax.broadcasted_iota(jnp.int32, cmb.shape, 1)
    ce = jnp.sum(jnp.where(lane == e, cmb, 0.0), axis=1, keepdims=True)
    u = u_scr[...]
    a = jnp.dot(u, wg_ref[...], preferred_element_type=F32)
    b = jnp.dot(u, wu_ref[...], preferred_element_type=F32)
    h = (a * jax.nn.sigmoid(a) * b).astype(BF16)
    acc_scr[...] += ce * jnp.dot(h, wd_ref[...], preferred_element_type=F32)

    @pl.when((e == pl.num_programs(2) - 1) & (f == pl.num_programs(3) - 1))
    def _():
        z = ALPHA * x_ref[...] + (1.0 + g_ref[...]) * acc_scr[...]
        o_ref[...] = _layer_norm_rows(z, lng_ref[...], lnb_ref[...])


def _moe_ffn(x, sc, sh, g, combine, wg, wu, wd, ln_g, ln_b):
    bsz, seq, d = x.shape
    n_e, _, ff = wg.shape
    tm, tf = 1024, 512
    row = lambda n: pl.BlockSpec((None, tm, n), lambda b, i, e, f: (b, i, 0))
    vec = pl.BlockSpec((None, 1, d), lambda b, i, e, f: (b, 0, 0))
    one = pl.BlockSpec((1, d), lambda b, i, e, f: (0, 0))
    return pl.pallas_call(
        _moe_kernel,
        grid=(bsz, seq // tm, n_e, ff // tf),
        in_specs=[row(d), vec, vec, vec, row(LANE),
                  pl.BlockSpec((None, d, tf), lambda b, i, e, f: (e, 0, f)),
                  pl.BlockSpec((None, d, tf), lambda b, i, e, f: (e, 0, f)),
                  pl.BlockSpec((None, tf, d), lambda b, i, e, f: (e, f, 0)),
                  one, one],
        out_specs=row(d),
        out_shape=jax.ShapeDtypeStruct((bsz, seq, d), F32),
        scratch_shapes=[pltpu.VMEM((tm, d), BF16), pltpu.VMEM((tm, d), F32)],
        compiler_params=_params("arbitrary", "arbitrary", "arbitrary", "arbitrary"),
        name="moe_swiglu_ln",
    )(x, sc, sh, g, combine, wg, wu, wd, ln_g.reshape(1, d), ln_b.reshape(1, d))


ML_TT = 256
ML_SUB = 128


def _mlstm_kernel(qk_ref, tail_ref, v_ref, og_ref, g_ref, cw_ref, cb_ref, gb_ref, gn_ref, y_ref,
                  ct_scr, n_scr, m_scr, q_scr, k_scr):
    tt = ML_TT
    step = pl.program_id(1)

    @pl.when(step == 0)
    def _():
        ct_scr[...] = jnp.zeros_like(ct_scr)
        n_scr[...] = jnp.zeros_like(n_scr)
        m_scr[...] = jnp.zeros_like(m_scr)

    x = qk_ref[...]
    tail = jnp.where(step == 0, 0.0, tail_ref[...])
    row8 = lax.broadcasted_iota(jnp.int32, (8, 1), 0)
    pre = x * cw_ref[CONV_W - 1:CONV_W, :] + cb_ref[...]
    for s in range(1, CONV_W):
        rolled = pltpu.roll(x, s, 0)
        head = jnp.where(row8 < s, pltpu.roll(tail, s, 0), rolled[0:8])
        pre = pre + jnp.concatenate([head, rolled[8:]], axis=0) * cw_ref[CONV_W - 1 - s:CONV_W - s, :]
    act = pre * jax.nn.sigmoid(pre)
    q_scr[...] = act[:, 0:ML_D].astype(BF16)
    k_scr[...] = (act[:, ML_D:2 * ML_D] * (DH_B ** -0.5)).astype(BF16)

    lane = lax.broadcasted_iota(jnp.int32, (1, LANE), 1)
    tok = lane % CHUNK
    jj = lax.broadcasted_iota(jnp.int32, (CHUNK, CHUNK), 0)
    ss = lax.broadcasted_iota(jnp.int32, (CHUNK, CHUNK), 1)
    nt = (((1,), (1,)), ((), ()))
    tn = (((0,), (0,)), ((), ()))
    for sub in range(tt // ML_SUB):
        r0 = sub * ML_SUB
        gp = g_ref[r0:r0 + ML_SUB, :] + gb_ref[...]
        lsig = jnp.minimum(gp, 0.0) - jnp.log(1.0 + jnp.exp(-jnp.abs(gp)))
        col = jnp.where(lane < H_B, gp, lsig)
        rowl = col.T
        b = rowl[0:8]
        for sft in (1, 2, 4, 8, 16, 32):
            b = b + jnp.where(tok >= sft, pltpu.roll(b, sft, 1), 0.0)
        bcol = jnp.concatenate([b, jnp.zeros((LANE - 8, ML_SUB), F32)], axis=0).T
        for ci in range(ML_SUB // CHUNK):
            c0 = ci * CHUNK
            rows = slice(r0 + c0, r0 + c0 + CHUNK)
            for h in range(H_B):
                hs = slice(h * DH_B, (h + 1) * DH_B)
                b_col = bcol[c0:c0 + CHUNK, H_B + h:H_B + h + 1]
                ig_col = col[c0:c0 + CHUNK, h:h + 1]
                b_row = b[H_B + h:H_B + h + 1, c0:c0 + CHUNK]
                ig_row = rowl[h:h + 1, c0:c0 + CHUNK]
                m_old = m_scr[h:h + 1, 0:1]
                log_d = jnp.where(jj >= ss, b_col - b_row + ig_row, NEG)
                log_inter = b_col + m_old
                m_out = jnp.maximum(log_inter, jnp.max(log_d, -1, keepdims=True))
                d = jnp.exp(log_d - m_out)
                w_inter = jnp.exp(log_inter - m_out)
                qh, kh, vh = q_scr[rows, hs], k_scr[rows, hs], v_ref[rows, hs]
                s = lax.dot_general(qh, kh, nt, preferred_element_type=F32) * d
                ct = ct_scr[h]
                num = w_inter * jnp.dot(qh, ct.astype(BF16), preferred_element_type=F32) + jnp.dot(s.astype(BF16), vh, preferred_element_type=F32)
                n_row = n_scr[h:h + 1, :]
                den = w_inter * jnp.sum(qh.astype(F32) * n_row, -1, keepdims=True) + jnp.sum(s, -1, keepdims=True)
                hid = num / jnp.maximum(jnp.abs(den), jnp.exp(-m_out))
                b_last = b_row[:, CHUNK - 1:CHUNK]
                m_new = jnp.maximum(b_last + m_old, jnp.max(b_last - b_row + ig_row, -1, keepdims=True))
                decay = jnp.exp(b_last + m_old - m_new)
                w_col = jnp.exp(b_last - b_col + ig_col - m_new)
                wv = (w_col * vh.astype(F32)).astype(BF16)
                ct_scr[h] = decay * ct + lax.dot_general(kh, wv, tn, preferred_element_type=F32)
                n_scr[h:h + 1, :] = decay * n_row + jnp.sum(w_col * kh.astype(F32), axis=0, keepdims=True)
                m_scr[h:h + 1, :] = jnp.broadcast_to(m_new, (1, LANE))
                ms = jnp.mean(hid * hid, -1, keepdims=True)
                y = hid * lax.rsqrt(ms + 1e-6) * gn_ref[:, hs] * jax.nn.sigmoid(og_ref[rows, hs])
                y_ref[rows, hs] = y.astype(y_ref.dtype)


def _mlstm(mqk, mv, mo, mg, conv_w, conv_b, gate_b, g_b):
    bsz, seq, _ = mqk.shape
    tt = ML_TT
    gb = jnp.pad(gate_b.reshape(1, 2 * H_B), ((0, 0), (0, LANE - 2 * H_B)))
    row = lambda n: pl.BlockSpec((None, tt, n), lambda b, i: (b, i, 0))
    full = lambda r, c: pl.BlockSpec((r, c), lambda b, i: (0, 0))
    tail = pl.BlockSpec((None, 8, 2 * ML_D), lambda b, i: (b, jnp.maximum(i * (tt // 8) - 1, 0), 0))
    return pl.pallas_call(
        _mlstm_kernel,
        grid=(bsz, seq // tt),
        in_specs=[row(2 * ML_D), tail, row(ML_D), row(ML_D), row(LANE),
                  full(CONV_W, 2 * ML_D), full(1, 2 * ML_D), full(1, LANE), full(1, ML_D)],
        out_specs=row(ML_D),
        out_shape=jax.ShapeDtypeStruct((bsz, seq, ML_D), BF16),
        scratch_shapes=[pltpu.VMEM((H_B, DH_B, DH_B), F32), pltpu.VMEM((8, DH_B), F32), pltpu.VMEM((8, LANE), F32),
                        pltpu.VMEM((tt, ML_D), BF16), pltpu.VMEM((tt, ML_D), BF16)],
        compiler_params=_params("arbitrary", "arbitrary"),
        name="mlstm_scan",
    )(mqk, mqk, mv, mo, mg, conv_w, conv_b.reshape(1, 2 * ML_D), gb, g_b.reshape(1, ML_D))


def _mixers(pa, mls, nsa, rope, w_uk, w_uv, kv_norm, conv_w, conv_b, gate_b, cmp_pos, cmp_w1, cmp_w2, grp_norm):
    g_a, g_b, g_c = jnp.split(grp_norm, [H_A * DH_V, H_A * DH_V + H_B * DH_B])
    qc, iq, iw, kc_a, ik = _dsa_prep(pa, rope[0], rope[1], w_uk, kv_norm)
    y_a = _dsa_attention(qc, iq, iw, kc_a, ik, w_uv, g_a)
    y_b = _mlstm(*mls, conv_w, conv_b, gate_b, g_b)
    nq, ncmp, nks, nvs, nkw, nvw, ng = nsa
    kc, vc = _nsa_compress(ncmp, cmp_pos, cmp_w1, cmp_w2)
    y_c = _nsa_attention(nq, ng, kc, vc, nks, nvs, nkw, nvw, g_c)
    return y_a, y_b, y_c


def _pad_cols(w, n):
    return jnp.pad(w, ((0, 0), (0, n - w.shape[1])))


def kernel(x, c, positions, w_mod, b_mod, w_in, dsa_w_uk, dsa_w_uv, dsa_kv_norm, mlstm_conv_w, mlstm_conv_b, mlstm_gate_b, nsa_cmp_pos, nsa_cmp_w1, nsa_cmp_w2, grp_norm, w_out, ln_g, ln_b, ffn_w_gate, ffn_w_up, ffn_w_down, moe_router, moe_w_gate, moe_w_up, moe_w_down):
    bsz = x.shape[0]
    mod = _modulation(c, w_mod, b_mod).reshape(bsz, DEPTH, N_MOD, 1, D_MODEL)
    rope = _rope_table(positions)
    for l in range(DEPTH):
        sh1, sc1, g1, sh2, sc2, g2 = [mod[:, l, j] for j in range(N_MOD)]
        w = w_in[l].astype(BF16)
        wa = _dsa_weight_layout(w[:, :N_GROUP_A])
        wb = _mlstm_weight_layout(w[:, N_GROUP_A:N_GROUP_A + N_GROUP_B])
        wc = _nsa_weight_layout(w[:, N_GROUP_A + N_GROUP_B:])
        pa, mqk, mv, mo, mg, *nsa = _input_projection(x, sc1, sh1, wa, wb, wc)
        ya, yb, yc = _mixers(pa, (mqk, mv, mo, mg), nsa, rope, dsa_w_uk[l], dsa_w_uv[l], dsa_kv_norm[l], mlstm_conv_w[l], mlstm_conv_b[l], mlstm_gate_b[l], nsa_cmp_pos[l], nsa_cmp_w1[l], nsa_cmp_w2[l], grp_norm[l])
        wo = w_out[l].astype(BF16)
        n_a, n_b = H_A * DH_V, H_B * DH_B
        x = _output_projection(ya, yb, yc, x, g1, _pad_head_rows(wo[:n_a], H_A, DH_V), wo[n_a:n_a + n_b], _pad_head_rows(wo[n_a + n_b:], H_C, DH_C), ln_g[l, 0], ln_b[l, 0])
        if l % 2 == 0:
            k = l // 2
            x = _dense_ffn(x, sc2, sh2, g2, ffn_w_gate[k].astype(BF16), ffn_w_up[k].astype(BF16), ffn_w_down[k].astype(BF16), ln_g[l, 1], ln_b[l, 1])
        else:
            k = l // 2
            combine = _moe_router(x, sc2, sh2, moe_router[k])
            x = _moe_ffn(x, sc2, sh2, g2, combine, moe_w_gate[k].astype(BF16), moe_w_up[k].astype(BF16), moe_w_down[k].astype(BF16), ln_g[l, 1], ln_b[l, 1])
    return x
```

```python
import functools

import numpy as np
import jax
import jax.numpy as jnp
from jax import lax
from jax.experimental import pallas as pl
from jax.experimental.pallas import tpu as pltpu

F32 = jnp.float32
BF16 = jnp.bfloat16

D_MODEL = 1024
DEPTH = 2
H_A, DH_NOPE, DH_ROPE, D_C, DH_V, H_I, D_I = 4, 64, 32, 128, 64, 4, 64
K_SEL_MAX = 256
ROPE_THETA = 10000.0
H_B, DH_B, CONV_W, CHUNK = 4, 128, 4, 64
H_C, DH_C, L_CMP, D_STRIDE, CMP_HIDDEN, L_SEL, N_TOP_MAX, WINDOW, Q_BLOCK = 4, 64, 32, 16, 128, 64, 16, 512, 128
D_FF = 2816
N_EXPERTS = 8
D_FF_EXPERT = 3584
N_MOD = 6
ALPHA = (2 * DEPTH) ** 0.25
SPLIT_SIZES = (H_A * DH_NOPE, H_A * DH_ROPE, D_C, DH_ROPE, H_I * D_I, D_I, H_I, H_B * DH_B, H_B * DH_B, H_B * DH_B, H_B, H_B, H_B * DH_B, H_C * DH_C, DH_C, DH_C, DH_C, DH_C, DH_C, DH_C, 3 * H_C)
N_GROUP_A = sum(SPLIT_SIZES[:7])
N_GROUP_B = sum(SPLIT_SIZES[7:13])
N_GROUP_C = sum(SPLIT_SIZES[13:])

LANE = 128
VMEM_LIMIT = 56 * 1024 * 1024


def _round_up(n, m):
    return (n + m - 1) // m * m


def _params(*sem):
    return pltpu.CompilerParams(dimension_semantics=sem, vmem_limit_bytes=VMEM_LIMIT)


FOLD_ROWS = 64


def _fold_rows(x, op):
    parts = [x[i:i + FOLD_ROWS] for i in range(0, x.shape[0], FOLD_ROWS)]
    while len(parts) > 1:
        parts = [op(parts[i], parts[i + 1]) if i + 1 < len(parts) else parts[i] for i in range(0, len(parts), 2)]
    return parts[0]


def _layer_norm_rows(z, g, b):
    mu = jnp.mean(z, -1, keepdims=True)
    zc = z - mu
    var = jnp.mean(zc * zc, -1, keepdims=True)
    return zc * lax.rsqrt(var + 1e-5) * g + b


def _mod_kernel(c_ref, w_ref, b_ref, o_ref):
    c = c_ref[...]
    a = c * jax.nn.sigmoid(c)
    o_ref[...] = jnp.dot(a, w_ref[...], preferred_element_type=F32, precision=lax.Precision.HIGHEST) + b_ref[...]


def _modulation(c, w_mod, b_mod):
    bsz, d = c.shape
    n = w_mod.shape[1]
    tn = 1024
    return pl.pallas_call(
        _mod_kernel,
        grid=(n // tn,),
        in_specs=[pl.BlockSpec((bsz, d), lambda j: (0, 0)),
                  pl.BlockSpec((d, tn), lambda j: (0, j)),
                  pl.BlockSpec((1, tn), lambda j: (0, j))],
        out_specs=pl.BlockSpec((bsz, tn), lambda j: (0, j)),
        out_shape=jax.ShapeDtypeStruct((bsz, n), F32),
        compiler_params=_params("arbitrary"),
        name="adaln_mod",
    )(c, w_mod, b_mod.reshape(1, n))


NSA_Q0 = 0
NSA_CMP0 = H_C * LANE
NSA_KS0 = NSA_CMP0 + LANE
NSA_KW0 = NSA_KS0 + LANE
NSA_G0 = NSA_KW0 + LANE
NSA_COLS = NSA_G0 + LANE


def _nsa_weight_layout(w):
    d = w.shape[0]
    nq, nkc, nvc, nks, nvs, nkw, nvw, ng = jnp.split(w, np.cumsum(SPLIT_SIZES[13:])[:-1].tolist(), axis=1)
    z = lambda n: jnp.zeros((d, n), w.dtype)
    half = LANE - DH_C
    cols = []
    for h in range(H_C):
        cols += [nq[:, h * DH_C:(h + 1) * DH_C], z(half)]
    cols += [nkc, nvc, nks, z(half), nkw, z(half), ng, z(LANE - 3 * H_C)]
    values_t = jnp.concatenate([nvs, z(half), nvw, z(half)], axis=1).T
    return jnp.concatenate(cols, axis=1), values_t


def _mlstm_weight_layout(w):
    mq, mk, mv, mi, mf, mo = jnp.split(w, np.cumsum(SPLIT_SIZES[7:13])[:-1].tolist(), axis=1)
    return jnp.concatenate([mq, mk, mv, mo, mi, mf, jnp.zeros((w.shape[0], LANE - 2 * H_B), w.dtype)], axis=1)


ML_D = H_B * DH_B


def _inproj_kernel(x_ref, sc_ref, sh_ref, wa_ref, wb_ref, wc_ref, wvt_ref, oa_ref, mqk_ref, mv_ref, mo_ref, mg_ref,
                   nq_ref, ncmp_ref, nks_ref, nvs_ref, nkw_ref, nvw_ref, ng_ref):
    u = (x_ref[...] * (1.0 + sc_ref[...]) + sh_ref[...]).astype(BF16)
    oa_ref[...] = jnp.dot(u, wa_ref[...], preferred_element_type=F32)
    ob = jnp.dot(u, wb_ref[...], preferred_element_type=F32)
    mqk_ref[...] = ob[:, 0:2 * ML_D]
    mv_ref[...] = ob[:, 2 * ML_D:3 * ML_D].astype(BF16)
    mo_ref[...] = ob[:, 3 * ML_D:4 * ML_D]
    mg_ref[...] = ob[:, 4 * ML_D:4 * ML_D + LANE]
    oc = jnp.dot(u, wc_ref[...], preferred_element_type=F32)
    nq_ref[...] = oc[:, NSA_Q0:NSA_Q0 + H_C * LANE].astype(BF16)
    ncmp_ref[...] = oc[:, NSA_CMP0:NSA_CMP0 + LANE]
    nks_ref[...] = oc[:, NSA_KS0:NSA_KS0 + LANE].astype(BF16)
    nkw_ref[...] = oc[:, NSA_KW0:NSA_KW0 + LANE].astype(BF16)
    ng_ref[...] = oc[:, NSA_G0:NSA_G0 + LANE]
    vt = lax.dot_general(wvt_ref[...], u, (((1,), (1,)), ((), ())), preferred_element_type=F32)
    nvs_ref[...] = vt[0:LANE].astype(BF16)
    nvw_ref[...] = vt[LANE:2 * LANE].astype(BF16)


def _input_projection(x, sc, sh, wa, wb, wc, wvt):
    bsz, seq, d = x.shape
    tm = 512
    na, nb, nc = wa.shape[1], wb.shape[1], wc.shape[1]
    row = lambda n: pl.BlockSpec((None, tm, n), lambda b, i: (b, i, 0))
    col = pl.BlockSpec((None, LANE, tm), lambda b, i: (b, 0, i))
    vec = pl.BlockSpec((None, 1, d), lambda b, i: (b, 0, 0))
    full = lambda n: pl.BlockSpec((d, n), lambda b, i: (0, 0))
    tok = lambda n, dt: (row(n), jax.ShapeDtypeStruct((bsz, seq, n), dt))
    feat = (col, jax.ShapeDtypeStruct((bsz, LANE, seq), BF16))
    outs = [tok(na, F32), tok(2 * ML_D, F32), tok(ML_D, BF16), tok(ML_D, F32), tok(LANE, F32),
            tok(H_C * LANE, BF16), tok(LANE, F32), tok(LANE, BF16), feat, tok(LANE, BF16), feat, tok(LANE, F32)]
    return pl.pallas_call(
        _inproj_kernel,
        grid=(bsz, seq // tm),
        in_specs=[row(d), vec, vec, full(na), full(nb), full(nc), pl.BlockSpec((2 * LANE, d), lambda b, i: (0, 0))],
        out_specs=[spec for spec, _ in outs],
        out_shape=[shape for _, shape in outs],
        compiler_params=_params("arbitrary", "arbitrary"),
        name="input_projection",
    )(x, sc, sh, wa, wb, wc, wvt)


DSA_QN0 = 0
DSA_QR0 = H_A * LANE
DSA_CKV0 = DSA_QR0 + LANE
DSA_IQ0 = DSA_CKV0 + LANE
DSA_G0 = DSA_IQ0 + H_I * LANE
DSA_COLS = DSA_G0 + LANE
DSA_KR_LANE = D_I
DSA_IW_LANE = D_I + DH_ROPE
DSA_TQ = 128
DSA_KC = 512
INT_MIN = -2 ** 31
KEY_NEG_INF = int(np.int32(np.float32(-np.inf).view(np.int32)) ^ np.int32(0x7FFFFFFF))


def _dsa_weight_layout(w):
    d = w.shape[0]
    qn, qr, ckv, kr, iq, ik, iw = jnp.split(w, np.cumsum(SPLIT_SIZES[:7])[:-1].tolist(), axis=1)
    z = lambda n: jnp.zeros((d, n), w.dtype)
    cols = []
    for h in range(H_A):
        cols += [qn[:, h * DH_NOPE:(h + 1) * DH_NOPE], z(LANE - DH_NOPE)]
    cols += [qr, ckv]
    for h in range(H_I):
        cols += [iq[:, h * D_I:(h + 1) * D_I], z(LANE - D_I)]
    cols += [ik, kr, iw, z(LANE - D_I - DH_ROPE - H_I)]
    return jnp.concatenate(cols, axis=1)


def _rope_table_kernel(pos_ref, freq_ref, cos_ref, sin_ref):
    ang = pos_ref[...].astype(F32) * freq_ref[...]
    lane = lax.broadcasted_iota(jnp.int32, (1, LANE), 1)
    first = (lane % DH_ROPE) < DH_ROPE // 2
    cos_ref[...] = jnp.cos(ang)
    sin_ref[...] = jnp.where(first, -jnp.sin(ang), jnp.sin(ang))


def _rope_table(positions):
    bsz, seq = positions.shape
    tm = 512
    inv_freq = ROPE_THETA ** (-jnp.arange(0, DH_ROPE, 2, dtype=F32) / DH_ROPE)
    freq = jnp.tile(inv_freq, LANE // (DH_ROPE // 2)).reshape(1, LANE)
    out = pl.BlockSpec((None, tm, LANE), lambda b, i: (b, i, 0))
    return pl.pallas_call(
        _rope_table_kernel,
        grid=(bsz, seq // tm),
        in_specs=[pl.BlockSpec((None, tm, 1), lambda b, i: (b, i, 0)), pl.BlockSpec((1, LANE), lambda b, i: (0, 0))],
        out_specs=[out, out],
        out_shape=[jax.ShapeDtypeStruct((bsz, seq, LANE), F32)] * 2,
        compiler_params=_params("arbitrary", "arbitrary"),
        name="rope_table",
    )(positions.reshape(bsz, seq, 1), freq)


def _dsa_prep_kernel(pa_ref, cos_ref, sin_ref, wuk_ref, kvn_ref, qc_ref, iq_ref, iw_ref, kc_ref, ik_ref, ct_ref):
    scale = (DH_NOPE + DH_ROPE) ** -0.5
    cos, sin = cos_ref[...], sin_ref[...]
    lane = lax.broadcasted_iota(jnp.int32, (1, LANE), 1)
    first = (lane % DH_ROPE) < DH_ROPE // 2
    rope_lanes = (lane >= DSA_KR_LANE) & (lane < DSA_KR_LANE + DH_ROPE)

    def rope(v):
        partner = jnp.where(first, pltpu.roll(v, LANE - DH_ROPE // 2, 1), pltpu.roll(v, DH_ROPE // 2, 1))
        return v * cos + partner * sin

    g = pa_ref[:, DSA_G0:DSA_G0 + LANE]
    ckv = pa_ref[:, DSA_CKV0:DSA_CKV0 + LANE]
    ckv_n = ckv * lax.rsqrt(jnp.mean(ckv * ckv, -1, keepdims=True) + 1e-6) * kvn_ref[...]
    kc_ref[:, 0:LANE] = ckv_n.astype(BF16)
    ct_ref[...] = ckv_n.T.astype(BF16)
    kc_ref[:, LANE:2 * LANE] = jnp.where(rope_lanes, rope(g), 0.0).astype(BF16)
    ik_ref[...] = jnp.where(lane < D_I, g, 0.0).astype(BF16)
    iw_ref[...] = g
    iq_ref[...] = pa_ref[:, DSA_IQ0:DSA_IQ0 + H_I * LANE].astype(BF16)
    qr = rope(pa_ref[:, DSA_QR0:DSA_QR0 + LANE]) * scale
    for h in range(H_A):
        qn = pa_ref[:, DSA_QN0 + h * LANE:DSA_QN0 + (h + 1) * LANE].astype(BF16)
        q_abs = jnp.dot(qn, wuk_ref[h], preferred_element_type=F32) * scale
        shift = (DSA_KR_LANE - DH_ROPE * h) % LANE
        qr_h = pltpu.roll(qr, shift, 1) if shift else qr
        qc_ref[:, 2 * h * LANE:(2 * h + 1) * LANE] = q_abs.astype(BF16)
        qc_ref[:, (2 * h + 1) * LANE:(2 * h + 2) * LANE] = jnp.where(rope_lanes, qr_h, 0.0).astype(BF16)


def _dsa_prep(pa, cos, sin, w_uk, kv_norm):
    bsz, seq, _ = pa.shape
    tm = 512
    wuk = jnp.pad(w_uk, ((0, 0), (0, LANE - DH_NOPE), (0, 0))).astype(BF16)
    row = lambda n: pl.BlockSpec((None, tm, n), lambda b, i: (b, i, 0))
    outs = [(2 * H_A * LANE, BF16), (H_I * LANE, BF16), (LANE, F32), (2 * LANE, BF16), (LANE, BF16)]
    return pl.pallas_call(
        _dsa_prep_kernel,
        grid=(bsz, seq // tm),
        in_specs=[row(DSA_COLS), row(LANE), row(LANE),
                  pl.BlockSpec((H_A, LANE, D_C), lambda b, i: (0, 0, 0)), pl.BlockSpec((1, D_C), lambda b, i: (0, 0))],
        out_specs=[row(n) for n, _ in outs] + [pl.BlockSpec((None, D_C, tm), lambda b, i: (b, 0, i))],
        out_shape=[jax.ShapeDtypeStruct((bsz, seq, n), dt) for n, dt in outs] + [jax.ShapeDtypeStruct((bsz, D_C, seq), BF16)],
        compiler_params=_params("arbitrary", "arbitrary"),
        name="dsa_prep",
    )(pa, cos, sin, wuk, kv_norm.reshape(1, D_C))


def _dsa_kernel(k_sel, qc_ref, iq_ref, iw_ref, kc_ref, ct_ref, ik_ref, tri_ref, wuv_ref, gn_ref, o_ref, key_scr, acc_scr):
    tq, kcs = DSA_TQ, DSA_KC
    t0 = pl.program_id(1) * tq
    n_chunk = (t0 + tq + kcs - 1) // kcs
    nt = (((1,), (1,)), ((), ()))
    qpos = t0 + lax.broadcasted_iota(jnp.int32, (1, tq), 1)
    iw_t = iw_ref[...].T

    def score_chunk(c, carry):
        k0 = pl.multiple_of(c * kcs, kcs)
        ikc = ik_ref[pl.ds(k0, kcs), :]
        sc = jnp.zeros((kcs, tq), F32)
        for h in range(H_I):
            lg = lax.dot_general(ikc, iq_ref[:, h * LANE:(h + 1) * LANE], nt, preferred_element_type=F32)
            sc = sc + jnp.maximum(lg, 0.0) * iw_t[DSA_IW_LANE + h:DSA_IW_LANE + h + 1, :]
        sc = jnp.where(sc == 0.0, 0.0, sc)
        kpos = k0 + lax.broadcasted_iota(jnp.int32, (kcs, 1), 0)
        sc = jnp.where(kpos <= qpos, sc, -jnp.inf)
        bits = pltpu.bitcast(sc, jnp.int32)
        key_scr[pl.ds(k0, kcs), :] = jnp.where(bits < 0, bits ^ 0x7FFFFFFF, bits)
        return carry

    lax.fori_loop(0, n_chunk, score_chunk, 0)

    def count(pred):
        def body(c, acc):
            k0 = pl.multiple_of(c * kcs, kcs)
            one = jnp.where(pred(key_scr[pl.ds(k0, kcs), :]), 1.0, 0.0)
            return acc + _fold_rows(one, jnp.add)
        acc = lax.fori_loop(0, n_chunk, body, jnp.zeros((FOLD_ROWS, tq), F32))
        return jnp.sum(acc, axis=0, keepdims=True)

    thr = jnp.where(count(lambda k: k >= 0) >= k_sel, jnp.int32(0), jnp.int32(INT_MIN))

    def bit_step(i, thr):
        cand = thr | jnp.left_shift(jnp.int32(1), 30 - i)
        return jnp.where(count(lambda k: k >= cand) >= k_sel, cand, thr)

    thr = lax.fori_loop(0, 31, bit_step, thr)
    room = k_sel - count(lambda k: k > thr)

    qall = jnp.concatenate([qc_ref[:, 2 * h * LANE:(2 * h + 2) * LANE] for h in range(H_A)], axis=0)

    room = jnp.where(thr > KEY_NEG_INF, room, 0.0)
    acc_scr[...] = jnp.zeros_like(acc_scr)

    def attn_chunk(c, carry):
        m, l, seen = carry
        k0 = pl.multiple_of(c * kcs, kcs)
        key = key_scr[pl.ds(k0, kcs), :]
        tie = jnp.where(key == thr, 1.0, 0.0)
        prefix = jnp.dot(tri_ref[...], tie.astype(BF16), preferred_element_type=F32)
        tie_bias = jnp.where(seen + prefix <= room, jnp.where(key == thr, 0.0, NEG), NEG)
        bias = jnp.where(key > thr, 0.0, tie_bias)
        seen = seen + jnp.sum(_fold_rows(tie, jnp.add), axis=0, keepdims=True)
        s = lax.dot_general(kc_ref[pl.ds(k0, kcs), :], qall, nt, preferred_element_type=F32)
        m_new, l_new, ps = [], [], []
        for h in range(H_A):
            hs = slice(h * tq, (h + 1) * tq)
            sh = s[:, hs] + bias
            mh = jnp.maximum(m[:, hs], jnp.max(_fold_rows(sh, jnp.maximum), axis=0, keepdims=True))
            ph = jnp.exp(sh - mh)
            l_new.append(jnp.exp(m[:, hs] - mh) * l[:, hs] + jnp.sum(_fold_rows(ph, jnp.add), axis=0, keepdims=True))
            m_new.append(mh)
            ps.append(ph.astype(BF16))
        m_new = jnp.concatenate(m_new, axis=1)
        acc_scr[...] = jnp.exp(m - m_new) * acc_scr[...] + jnp.dot(ct_ref[:, pl.ds(k0, kcs)], jnp.concatenate(ps, axis=1), preferred_element_type=F32)
        return m_new, jnp.concatenate(l_new, axis=1), seen

    init = (jnp.full((1, H_A * tq), M_FLOOR, F32), jnp.zeros((1, H_A * tq), F32), jnp.zeros((1, tq), F32))
    _, l, _ = lax.fori_loop(0, n_chunk, attn_chunk, init)
    o_lat = (acc_scr[...] / jnp.maximum(l, 1e-30)).astype(BF16)
    for h in range(H_A):
        o = lax.dot_general(o_lat[:, h * tq:(h + 1) * tq], wuv_ref[h], (((0,), (0,)), ((), ())),
                            preferred_element_type=F32)
        ms = jnp.sum(o * o, -1, keepdims=True) * (1.0 / DH_V)
        o_ref[:, h * LANE:(h + 1) * LANE] = (o * lax.rsqrt(ms + 1e-6) * gn_ref[h:h + 1, :]).astype(o_ref.dtype)


def _dsa_attention(qc, iq, iw, kc, ct, ik, w_uv, g_a):
    bsz, seq, _ = qc.shape
    k_sel = float(min(K_SEL_MAX, seq // 4))
    tq, kcs = DSA_TQ, DSA_KC
    tri = jnp.asarray(np.tril(np.ones((kcs, kcs), np.float32)), BF16)
    wuv = jnp.pad(w_uv, ((0, 0), (0, 0), (0, LANE - DH_V))).astype(BF16)
    gn = jnp.pad(g_a.reshape(H_A, DH_V), ((0, 0), (0, LANE - DH_V)))
    row = lambda n: pl.BlockSpec((None, tq, n), lambda b, i: (b, i, 0))
    per_b = lambda n: pl.BlockSpec((None, seq, n), lambda b, i: (b, 0, 0))
    return pl.pallas_call(
        functools.partial(_dsa_kernel, k_sel),
        grid=(bsz, seq // tq),
        in_specs=[row(2 * H_A * LANE), row(H_I * LANE), row(LANE), per_b(2 * LANE),
                  pl.BlockSpec((None, D_C, seq), lambda b, i: (b, 0, 0)), per_b(LANE),
                  pl.BlockSpec((kcs, kcs), lambda b, i: (0, 0)), pl.BlockSpec((H_A, D_C, LANE), lambda b, i: (0, 0, 0)),
                  pl.BlockSpec((H_A, LANE), lambda b, i: (0, 0))],
        out_specs=row(H_A * LANE),
        out_shape=jax.ShapeDtypeStruct((bsz, seq, H_A * LANE), BF16),
        scratch_shapes=[pltpu.VMEM((seq, tq), jnp.int32), pltpu.VMEM((D_C, H_A * tq), F32)],
        compiler_params=_params("arbitrary", "arbitrary"),
        name="dsa_attention",
    )(qc, iq, iw, kc, ct, ik, tri, wuv, gn)


def _nsa_compress_kernel(a_ref, pos_ref, w1t_ref, w1b_ref, w2k_ref, w2v_ref, kc_ref, vc_ref):
    a = a_ref[...]
    top = jnp.dot((a + pos_ref[0:1, :]).astype(BF16), w1t_ref[...], preferred_element_type=F32)
    bot = jnp.dot((a + pos_ref[1:2, :]).astype(BF16), w1b_ref[...], preferred_element_type=F32)
    n = a.shape[0]
    pre = top + jnp.concatenate([bot[1:], bot[:1]], axis=0)
    h = (pre * jax.nn.sigmoid(pre)).astype(BF16)
    hid = w2k_ref.shape[0]
    kc_ref[...] = jnp.dot(h[:, :hid], w2k_ref[...], preferred_element_type=F32).astype(BF16)
    vc_ref[...] = lax.dot_general(w2v_ref[...], h[:, hid:], (((1,), (1,)), ((), ())), preferred_element_type=F32).astype(BF16)


def _nsa_compress(ncmp, cmp_pos, cmp_w1, cmp_w2):
    bsz, seq, _ = ncmp.shape
    n_grp = seq // D_STRIDE
    per = L_CMP // D_STRIDE
    width = D_STRIDE * LANE
    a = ncmp.reshape(bsz, n_grp, width)
    w1 = cmp_w1.reshape(2, per, D_STRIDE, DH_C, CMP_HIDDEN)
    zer = jnp.zeros((D_STRIDE, DH_C, CMP_HIDDEN), cmp_w1.dtype)

    def expand(p):
        wk = jnp.concatenate([w1[0, p], zer], axis=1)
        wv = jnp.concatenate([zer, w1[1, p]], axis=1)
        return jnp.concatenate([wk, wv], axis=2).reshape(width, 2 * CMP_HIDDEN).astype(BF16)

    pos = cmp_pos.reshape(2, per, D_STRIDE, DH_C)
    pos = jnp.concatenate([pos[0], pos[1]], axis=-1).reshape(per, width)
    pad_out = ((0, 0), (0, LANE - DH_C))
    w2k = jnp.pad(cmp_w2[0], pad_out).astype(BF16)
    w2v = jnp.pad(cmp_w2[1], pad_out).astype(BF16).T
    full = lambda r, c: pl.BlockSpec((r, c), lambda b: (0, 0))
    return pl.pallas_call(
        _nsa_compress_kernel,
        grid=(bsz,),
        in_specs=[pl.BlockSpec((None, n_grp, width), lambda b: (b, 0, 0)), full(per, width),
                  full(width, 2 * CMP_HIDDEN), full(width, 2 * CMP_HIDDEN), full(CMP_HIDDEN, LANE), full(LANE, CMP_HIDDEN)],
        out_specs=[pl.BlockSpec((None, n_grp, LANE), lambda b: (b, 0, 0)), pl.BlockSpec((None, LANE, n_grp), lambda b: (b, 0, 0))],
        out_shape=[jax.ShapeDtypeStruct((bsz, n_grp, LANE), BF16), jax.ShapeDtypeStruct((bsz, LANE, n_grp), BF16)],
        compiler_params=_params("arbitrary"),
        name="nsa_compress",
    )(a, pos, expand(0), expand(1), w2k, w2v)


NSA_TQ = 128
NSA_KC = 512
NEG = -1e30
M_FLOOR = -1e20


def _softmax_cols(s, bias, n_head, tq):
    out = []
    for h in range(n_head):
        sh = s[:, h * tq:(h + 1) * tq] + bias
        m = jnp.maximum(jnp.max(_fold_rows(sh, jnp.maximum), axis=0, keepdims=True), M_FLOOR)
        e = jnp.exp(sh - m)
        den = jnp.sum(_fold_rows(e, jnp.add), axis=0, keepdims=True)
        out.append(e * (1.0 / jnp.maximum(den, 1e-30)))
    return out


def _nsa_kernel(q_ref, g_ref, kc_ref, vct_ref, ks_ref, vst_ref, kw_ref, vwt_ref, cover_ref, expand_ref, gn_ref, o_ref, acc_scr):
    tq, kc_sz = NSA_TQ, NSA_KC
    t0 = pl.program_id(1) * tq
    nt = (((1,), (1,)), ((), ()))
    q = q_ref[...] * (DH_C ** -0.5)
    qa = jnp.concatenate([q[:, h * LANE:(h + 1) * LANE] for h in range(H_C)], axis=0)
    qpos = t0 + lax.broadcasted_iota(jnp.int32, (1, tq), 1)

    n_grp = kc_ref.shape[0]
    s_c = lax.dot_general(kc_ref[...], qa, nt, preferred_element_type=F32)
    n_idx = lax.broadcasted_iota(jnp.int32, (n_grp, 1), 0)
    visible = jnp.where(n_idx < n_grp - 1, n_idx * D_STRIDE + (L_CMP - 1), 2 ** 30)
    p_c = _softmax_cols(s_c, jnp.where(visible <= qpos, 0.0, NEG), H_C, tq)
    o_cmp = jnp.dot(vct_ref[...], jnp.concatenate([p.astype(BF16) for p in p_c], axis=1), preferred_element_type=F32)

    p_sum = p_c[0]
    for h in range(1, H_C):
        p_sum = p_sum + p_c[h]
    hi = p_sum.astype(BF16)
    lo = (p_sum - hi.astype(F32)).astype(BF16)
    cov = cover_ref[...]
    imp_t = jnp.dot(cov, hi, preferred_element_type=F32) + jnp.dot(cov, lo, preferred_element_type=F32)
    n_sel = cov.shape[0]
    jrow = lax.broadcasted_iota(jnp.int32, (n_sel, tq), 0)
    cur = (t0 + lax.broadcasted_iota(jnp.int32, (n_sel, tq), 1)) // L_SEL
    adm = jrow <= cur
    forced = (jrow == 0) | (jrow == cur) | (jrow == cur - 1)
    val = jnp.where(adm & forced, jnp.inf, jnp.where(adm, imp_t, -jnp.inf))
    rank = jnp.zeros((n_sel, tq), F32)
    for jp in range(n_sel):
        r = val[jp:jp + 1, :]
        rank = rank + jnp.where(r == val, jnp.where(jrow > jp, 1.0, 0.0), jnp.where(r > val, 1.0, 0.0))
    sel_t = jnp.where(rank < min(N_TOP_MAX, n_sel), jnp.where(val > -jnp.inf, 1.0, 0.0), 0.0)
    if n_sel < LANE:
        sel_t = jnp.concatenate([sel_t, jnp.zeros((LANE - n_sel, tq), F32)], axis=0)
    sel_t = sel_t.astype(BF16)

    acc_scr[...] = jnp.zeros_like(acc_scr)

    def chunk(c, carry):
        m, l = carry
        k0 = pl.multiple_of(c * kc_sz, kc_sz)
        s = lax.dot_general(ks_ref[pl.ds(k0, kc_sz), :], qa, nt, preferred_element_type=F32)
        picked = jnp.dot(expand_ref[pl.ds(k0, kc_sz), :], sel_t, preferred_element_type=F32)
        kpos = k0 + lax.broadcasted_iota(jnp.int32, (kc_sz, 1), 0)
        bias = jnp.where(kpos <= qpos, jnp.where(picked > 0.5, 0.0, NEG), NEG)
        m_new, l_new, ps = [], [], []
        for h in range(H_C):
            hs = slice(h * tq, (h + 1) * tq)
            sh = s[:, hs] + bias
            mh = jnp.maximum(m[:, hs], jnp.max(_fold_rows(sh, jnp.maximum), axis=0, keepdims=True))
            ph = jnp.exp(sh - mh)
            l_new.append(jnp.exp(m[:, hs] - mh) * l[:, hs] + jnp.sum(_fold_rows(ph, jnp.add), axis=0, keepdims=True))
            m_new.append(mh)
            ps.append(ph.astype(BF16))
        m_new = jnp.concatenate(m_new, axis=1)
        acc_scr[...] = jnp.exp(m - m_new) * acc_scr[...] + jnp.dot(vst_ref[:, pl.ds(k0, kc_sz)], jnp.concatenate(ps, axis=1), preferred_element_type=F32)
        return m_new, jnp.concatenate(l_new, axis=1)

    n_chunk = (t0 + tq + kc_sz - 1) // kc_sz
    init = (jnp.full((1, H_C * tq), M_FLOOR, F32), jnp.zeros((1, H_C * tq), F32))
    _, l_s = lax.fori_loop(0, n_chunk, chunk, init)
    o_slc = acc_scr[...] * (1.0 / jnp.maximum(l_s, 1e-30))

    span = WINDOW + tq
    w0 = pl.multiple_of(jnp.maximum(t0 - WINDOW, 0), tq)
    s_w = lax.dot_general(kw_ref[pl.ds(w0, span), :], qa, nt, preferred_element_type=F32)
    kpos_w = w0 + lax.broadcasted_iota(jnp.int32, (span, 1), 0)
    wbias = jnp.where(kpos_w <= qpos, jnp.where(qpos - kpos_w < WINDOW, 0.0, NEG), NEG)
    p_w = _softmax_cols(s_w, wbias, H_C, tq)
    o_win = jnp.dot(vwt_ref[:, pl.ds(w0, span)], jnp.concatenate([p.astype(BF16) for p in p_w], axis=1), preferred_element_type=F32)

    gates_t = jax.nn.sigmoid(g_ref[...]).T
    for h in range(H_C):
        hs = slice(h * tq, (h + 1) * tq)
        o_t = (gates_t[h:h + 1] * o_cmp[:, hs] + gates_t[H_C + h:H_C + h + 1] * o_slc[:, hs]
               + gates_t[2 * H_C + h:2 * H_C + h + 1] * o_win[:, hs])
        o = o_t.T
        ms = jnp.sum(o * o, -1, keepdims=True) * (1.0 / DH_C)
        o_ref[:, h * LANE:(h + 1) * LANE] = (o * lax.rsqrt(ms + 1e-6) * gn_ref[h:h + 1, :]).astype(o_ref.dtype)


def _nsa_attention(nq, ng, kc, vc_t, nks, nvs_t, nkw, nvw_t, g_c):
    bsz, seq, _ = nq.shape
    n_grp = kc.shape[1]
    n_sel = seq // L_SEL
    grp_start = np.arange(n_grp) * D_STRIDE
    sel_start = np.arange(n_sel) * L_SEL
    cover_t = ((grp_start[None, :] < sel_start[:, None] + L_SEL) & (grp_start[None, :] + L_CMP > sel_start[:, None]))
    cover_t = jnp.asarray(cover_t.astype(np.float32), BF16)
    expand = (np.arange(seq)[:, None] // L_SEL == np.arange(LANE)[None, :]).astype(np.float32)
    expand = jnp.asarray(expand, BF16)
    gn = jnp.pad(g_c.reshape(H_C, DH_C), ((0, 0), (0, LANE - DH_C)))
    tq = NSA_TQ
    row = lambda n: pl.BlockSpec((None, tq, n), lambda b, i: (b, i, 0))
    per_b = lambda r, c: pl.BlockSpec((None, r, c), lambda b, i: (b, 0, 0))
    full = lambda r, c: pl.BlockSpec((r, c), lambda b, i: (0, 0))
    return pl.pallas_call(
        _nsa_kernel,
        grid=(bsz, seq // tq),
        in_specs=[row(H_C * LANE), row(LANE), per_b(n_grp, LANE), per_b(LANE, n_grp), per_b(seq, LANE), per_b(LANE, seq),
                  per_b(seq, LANE), per_b(LANE, seq), full(n_sel, n_grp), full(seq, LANE), full(H_C, LANE)],
        out_specs=row(H_C * LANE),
        out_shape=jax.ShapeDtypeStruct((bsz, seq, H_C * LANE), BF16),
        scratch_shapes=[pltpu.VMEM((LANE, H_C * tq), F32)],
        compiler_params=_params("arbitrary", "arbitrary"),
        name="nsa_attention",
    )(nq, ng, kc, vc_t, nks, nvs_t, nkw, nvw_t, cover_t, expand, gn)


def _outproj_kernel(ya_ref, yb_ref, yc_ref, x_ref, g_ref, wa_ref, wb_ref, wc_ref, lng_ref, lnb_ref, o_ref):
    y = jnp.dot(ya_ref[...], wa_ref[...], preferred_element_type=F32)
    y += jnp.dot(yb_ref[...], wb_ref[...], preferred_element_type=F32)
    y += jnp.dot(yc_ref[...], wc_ref[...], preferred_element_type=F32)
    z = ALPHA * x_ref[...] + (1.0 + g_ref[...]) * y
    o_ref[...] = _layer_norm_rows(z, lng_ref[...], lnb_ref[...])


def _pad_head_rows(w, n_head, dh):
    d = w.shape[1]
    return jnp.pad(w.reshape(n_head, dh, d), ((0, 0), (0, LANE - dh), (0, 0))).reshape(n_head * LANE, d)


def _output_projection(ya, yb, yc, x, g, wa, wb, wc, ln_g, ln_b):
    bsz, seq, d = x.shape
    tm = 512
    na, nb, nc = ya.shape[-1], yb.shape[-1], yc.shape[-1]
    row = lambda n: pl.BlockSpec((None, tm, n), lambda b, i: (b, i, 0))
    vec = pl.BlockSpec((None, 1, d), lambda b, i: (b, 0, 0))
    full = lambda r, c: pl.BlockSpec((r, c), lambda b, i: (0, 0))
    return pl.pallas_call(
        _outproj_kernel,
        grid=(bsz, seq // tm),
        in_specs=[row(na), row(nb), row(nc), row(d), vec, full(na, d), full(nb, d), full(nc, d), full(1, d), full(1, d)],
        out_specs=row(d),
        out_shape=jax.ShapeDtypeStruct((bsz, seq, d), F32),
        compiler_params=_params("arbitrary", "arbitrary"),
        name="output_projection_ln",
    )(ya, yb, yc, x, g, wa, wb, wc, ln_g.reshape(1, d), ln_b.reshape(1, d))


def _ffn_kernel(x_ref, sc_ref, sh_ref, g_ref, wg_ref, wu_ref, wd_ref, lng_ref, lnb_ref, o_ref, u_scr, acc_scr):
    f = pl.program_id(2)

    @pl.when(f == 0)
    def _():
        u_scr[...] = (x_ref[...] * (1.0 + sc_ref[...]) + sh_ref[...]).astype(BF16)
        acc_scr[...] = jnp.zeros_like(acc_scr)

    u = u_scr[...]
    a = jnp.dot(u, wg_ref[...], preferred_element_type=F32)
    b = jnp.dot(u, wu_ref[...], preferred_element_type=F32)
    h = (a * jax.nn.sigmoid(a) * b).astype(BF16)
    acc_scr[...] += jnp.dot(h, wd_ref[...], preferred_element_type=F32)

    @pl.when(f == pl.num_programs(2) - 1)
    def _():
        z = ALPHA * x_ref[...] + (1.0 + g_ref[...]) * acc_scr[...]
        o_ref[...] = _layer_norm_rows(z, lng_ref[...], lnb_ref[...])


def _dense_ffn(x, sc, sh, g, wg, wu, wd, ln_g, ln_b):
    bsz, seq, d = x.shape
    ff = wg.shape[1]
    tm, tf = 1024, 256
    row = pl.BlockSpec((None, tm, d), lambda b, i, f: (b, i, 0))
    vec = pl.BlockSpec((None, 1, d), lambda b, i, f: (b, 0, 0))
    one = pl.BlockSpec((1, d), lambda b, i, f: (0, 0))
    return pl.pallas_call(
        _ffn_kernel,
        grid=(bsz, seq // tm, ff // tf),
        in_specs=[row, vec, vec, vec,
                  pl.BlockSpec((d, tf), lambda b, i, f: (0, f)),
                  pl.BlockSpec((d, tf), lambda b, i, f: (0, f)),
                  pl.BlockSpec((tf, d), lambda b, i, f: (f, 0)),
                  one, one],
        out_specs=row,
        out_shape=jax.ShapeDtypeStruct((bsz, seq, d), F32),
        scratch_shapes=[pltpu.VMEM((tm, d), BF16), pltpu.VMEM((tm, d), F32)],
        compiler_params=_params("arbitrary", "arbitrary", "arbitrary"),
        name="dense_swiglu_ln",
    )(x, sc, sh, g, wg, wu, wd, ln_g.reshape(1, d), ln_b.reshape(1, d))


def _router_kernel(x_ref, sc_ref, sh_ref, r_ref, lo_ref, up_ref, gate_ref, slot_ref, slot_t_ref, cnt_ref):
    u = x_ref[...] * (1.0 + sc_ref[...]) + sh_ref[...]
    logits = jnp.dot(u, r_ref[...], preferred_element_type=F32, precision=lax.Precision.HIGHEST)
    lane = lax.broadcasted_iota(jnp.int32, logits.shape, 1)
    neg = -jnp.inf
    l1 = jnp.where(lane < N_EXPERTS, logits, neg)
    m1 = jnp.max(l1, -1, keepdims=True)
    i1 = jnp.min(jnp.where(l1 == m1, lane, LANE), -1, keepdims=True)
    l2 = jnp.where(lane == i1, neg, l1)
    m2 = jnp.max(l2, -1, keepdims=True)
    i2 = jnp.min(jnp.where(l2 == m2, lane, LANE), -1, keepdims=True)
    e2 = jnp.exp(m2 - m1)
    w1 = 1.0 / (1.0 + e2)
    w2 = e2 / (1.0 + e2)
    gate_ref[...] = jnp.where(lane == i1, w1, jnp.where(lane == i2, w2, 0.0))
    routed = jnp.where((lane == i1) | (lane == i2), 1.0, 0.0)
    before = jnp.dot(lo_ref[...], routed.astype(BF16), preferred_element_type=F32)
    slot_ref[...] = jnp.where(routed > 0.5, before, -1.0)
    routed_t = routed.T
    before_t = jnp.dot(routed_t.astype(BF16), up_ref[...], preferred_element_type=F32)
    slot_t_ref[...] = jnp.where(routed_t > 0.5, before_t, -1.0)
    cnt_ref[...] = jnp.broadcast_to(jnp.sum(routed, axis=0, keepdims=True), cnt_ref.shape).astype(jnp.int32)


MOE_TM = 1024
MOE_ROWS = 320
MOE_TF = 896


def _moe_router(x, sc, sh, router):
    bsz, seq, d = x.shape
    tm = MOE_TM
    r = jnp.pad(router, ((0, 0), (0, LANE - router.shape[1])))
    upper = np.triu(np.ones((tm, tm), np.float32), 1)
    up, lo = jnp.asarray(upper, BF16), jnp.asarray(upper.T, BF16)
    row = lambda n: pl.BlockSpec((None, tm, n), lambda b, i: (b, i, 0))
    vec = pl.BlockSpec((None, 1, d), lambda b, i: (b, 0, 0))
    full = lambda a, c: pl.BlockSpec((a, c), lambda b, i: (0, 0))
    n_tile = seq // tm
    return pl.pallas_call(
        _router_kernel,
        grid=(bsz, n_tile),
        in_specs=[row(d), vec, vec, full(d, LANE), full(tm, tm), full(tm, tm)],
        out_specs=[row(LANE), row(LANE), pl.BlockSpec((None, None, LANE, tm), lambda b, i: (b, i, 0, 0)),
                   pl.BlockSpec((None, None, 8, LANE), lambda b, i: (b, i, 0, 0))],
        out_shape=[jax.ShapeDtypeStruct((bsz, seq, LANE), F32), jax.ShapeDtypeStruct((bsz, seq, LANE), F32),
                   jax.ShapeDtypeStruct((bsz, n_tile, LANE, tm), F32), jax.ShapeDtypeStruct((bsz, n_tile, 8, LANE), jnp.int32)],
        compiler_params=_params("arbitrary", "arbitrary"),
        name="moe_router",
    )(x, sc, sh, r, lo, up)


def _moe_kernel(cnt_ref, x_ref, sc_ref, sh_ref, g_ref, gate_ref, slot_ref, slot_t_ref, wg_ref, wu_ref, wd_ref, lng_ref, lnb_ref,
                o_ref, u_scr, xg_scr, acc_scr):
    tm, rows = MOE_TM, MOE_ROWS
    e = pl.program_id(2)
    f = pl.program_id(3)
    tile = pl.program_id(0) * pl.num_programs(1) + pl.program_id(1)
    n_pass = (cnt_ref[tile * N_EXPERTS + e] + rows - 1) // rows

    @pl.when((e == 0) & (f == 0))
    def _():
        u_scr[...] = (x_ref[...] * (1.0 + sc_ref[...]) + sh_ref[...]).astype(BF16)
        o_ref[...] = jnp.zeros_like(o_ref)

    @pl.when(f == 0)
    def _():
        slot_row = slot_t_ref[pl.ds(e, 1), :]

        def gather(p, carry):
            want = p * rows + lax.broadcasted_iota(jnp.int32, (rows, 1), 0)
            pick = jnp.where(slot_row == want.astype(F32), 1.0, 0.0).astype(BF16)
            xg_scr[p] = jnp.dot(pick, u_scr[...], preferred_element_type=F32).astype(BF16)
            acc_scr[p] = jnp.zeros((rows, x_ref.shape[-1]), F32)
            return carry

        lax.fori_loop(0, n_pass, gather, 0)

    def ffn(p, carry):
        xg = xg_scr[p]
        a = jnp.dot(xg, wg_ref[...], preferred_element_type=F32)
        b = jnp.dot(xg, wu_ref[...], preferred_element_type=F32)
        h = (a * jax.nn.sigmoid(a) * b).astype(BF16)
        acc_scr[p] += jnp.dot(h, wd_ref[...], preferred_element_type=F32)
        return carry

    lax.fori_loop(0, n_pass, ffn, 0)

    @pl.when(f == pl.num_programs(3) - 1)
    def _():
        lane = lax.broadcasted_iota(jnp.int32, (tm, LANE), 1)
        slot_col = jnp.sum(jnp.where(lane == e, slot_ref[...], 0.0), axis=1, keepdims=True)
        gate_col = jnp.sum(jnp.where(lane == e, gate_ref[...], 0.0), axis=1, keepdims=True)

        def scatter(p, carry):
            want = p * rows + lax.broadcasted_iota(jnp.int32, (1, rows), 1)
            put = jnp.where(slot_col == want.astype(F32), 1.0, 0.0).astype(BF16)
            y = acc_scr[p]
            hi = y.astype(BF16)
            lo = (y - hi.astype(F32)).astype(BF16)
            back = jnp.dot(put, hi, preferred_element_type=F32) + jnp.dot(put, lo, preferred_element_type=F32)
            o_ref[...] += gate_col * back
            return carry

        lax.fori_loop(0, n_pass, scatter, 0)

    @pl.when((e == pl.num_programs(2) - 1) & (f == pl.num_programs(3) - 1))
    def _():
        z = ALPHA * x_ref[...] + (1.0 + g_ref[...]) * o_ref[...]
        o_ref[...] = _layer_norm_rows(z, lng_ref[...], lnb_ref[...])


def _moe_ffn(x, sc, sh, g, routing, wg, wu, wd, ln_g, ln_b):
    gate, slot, slot_t, cnt = routing
    bsz, seq, d = x.shape
    n_e, _, ff = wg.shape
    tm, tf, rows = MOE_TM, MOE_TF, MOE_ROWS
    max_pass = -(-tm // rows)
    counts = cnt[:, :, 0, :n_e].reshape(-1)
    row = lambda n: pl.BlockSpec((None, tm, n), lambda b, i, e, f, c: (b, i, 0))
    vec = pl.BlockSpec((None, 1, d), lambda b, i, e, f, c: (b, 0, 0))
    one = pl.BlockSpec((1, d), lambda b, i, e, f, c: (0, 0))
    grid_spec = pltpu.PrefetchScalarGridSpec(
        num_scalar_prefetch=1,
        grid=(bsz, seq // tm, n_e, ff // tf),
        in_specs=[row(d), vec, vec, vec, row(LANE), row(LANE),
                  pl.BlockSpec((None, None, LANE, tm), lambda b, i, e, f, c: (b, i, 0, 0)),
                  pl.BlockSpec((None, d, tf), lambda b, i, e, f, c: (e, 0, f)),
                  pl.BlockSpec((None, d, tf), lambda b, i, e, f, c: (e, 0, f)),
                  pl.BlockSpec((None, tf, d), lambda b, i, e, f, c: (e, f, 0)),
                  one, one],
        out_specs=row(d),
        scratch_shapes=[pltpu.VMEM((tm, d), BF16), pltpu.VMEM((max_pass, rows, d), BF16), pltpu.VMEM((max_pass, rows, d), F32)],
    )
    return pl.pallas_call(
        _moe_kernel,
        grid_spec=grid_spec,
        out_shape=jax.ShapeDtypeStruct((bsz, seq, d), F32),
        compiler_params=_params("arbitrary", "arbitrary", "arbitrary", "arbitrary"),
        name="moe_swiglu_ln",
    )(counts, x, sc, sh, g, gate, slot, slot_t, wg, wu, wd, ln_g.reshape(1, d), ln_b.reshape(1, d))


ML_TT = 256
ML_SUB = 128


def _mlstm_kernel(qk_ref, tail_ref, v_ref, og_ref, g_ref, cw_ref, cb_ref, gb_ref, gn_ref, y_ref,
                  ct_scr, n_scr, m_scr, q_scr, k_scr):
    tt = ML_TT
    step = pl.program_id(1)

    @pl.when(step == 0)
    def _():
        ct_scr[...] = jnp.zeros_like(ct_scr)
        n_scr[...] = jnp.zeros_like(n_scr)
        m_scr[...] = jnp.zeros_like(m_scr)

    x = qk_ref[...]
    tail = jnp.where(step == 0, 0.0, tail_ref[...])
    row8 = lax.broadcasted_iota(jnp.int32, (8, 1), 0)
    pre = x * cw_ref[CONV_W - 1:CONV_W, :] + cb_ref[...]
    for s in range(1, CONV_W):
        rolled = pltpu.roll(x, s, 0)
        head = jnp.where(row8 < s, pltpu.roll(tail, s, 0), rolled[0:8])
        pre = pre + jnp.concatenate([head, rolled[8:]], axis=0) * cw_ref[CONV_W - 1 - s:CONV_W - s, :]
    act = pre * jax.nn.sigmoid(pre)
    q_scr[...] = act[:, 0:ML_D].astype(BF16)
    k_scr[...] = (act[:, ML_D:2 * ML_D] * (DH_B ** -0.5)).astype(BF16)

    lane = lax.broadcasted_iota(jnp.int32, (1, LANE), 1)
    tok = lane % CHUNK
    jj = lax.broadcasted_iota(jnp.int32, (CHUNK, CHUNK), 0)
    ss = lax.broadcasted_iota(jnp.int32, (CHUNK, CHUNK), 1)
    nt = (((1,), (1,)), ((), ()))
    tn = (((0,), (0,)), ((), ()))
    for sub in range(tt // ML_SUB):
        r0 = sub * ML_SUB
        gp = g_ref[r0:r0 + ML_SUB, :] + gb_ref[...]
        lsig = jnp.minimum(gp, 0.0) - jnp.log(1.0 + jnp.exp(-jnp.abs(gp)))
        col = jnp.where(lane < H_B, gp, lsig)
        rowl = col.T
        b = rowl[0:8]
        for sft in (1, 2, 4, 8, 16, 32):
            b = b + jnp.where(tok >= sft, pltpu.roll(b, sft, 1), 0.0)
        bcol = jnp.concatenate([b, jnp.zeros((LANE - 8, ML_SUB), F32)], axis=0).T
        for ci in range(ML_SUB // CHUNK):
            c0 = ci * CHUNK
            rows = slice(r0 + c0, r0 + c0 + CHUNK)
            for h in range(H_B):
                hs = slice(h * DH_B, (h + 1) * DH_B)
                b_col = bcol[c0:c0 + CHUNK, H_B + h:H_B + h + 1]
                ig_col = col[c0:c0 + CHUNK, h:h + 1]
                b_row = b[H_B + h:H_B + h + 1, c0:c0 + CHUNK]
                ig_row = rowl[h:h + 1, c0:c0 + CHUNK]
                m_old = m_scr[h:h + 1, 0:1]
                log_d = jnp.where(jj >= ss, b_col - b_row + ig_row, NEG)
                log_inter = b_col + m_old
                m_out = jnp.maximum(log_inter, jnp.max(log_d, -1, keepdims=True))
                d = jnp.exp(log_d - m_out)
                w_inter = jnp.exp(log_inter - m_out)
                qh, kh, vh = q_scr[rows, hs], k_scr[rows, hs], v_ref[rows, hs]
                s = lax.dot_general(qh, kh, nt, preferred_element_type=F32) * d
                ct = ct_scr[h]
                num = w_inter * jnp.dot(qh, ct.astype(BF16), preferred_element_type=F32) + jnp.dot(s.astype(BF16), vh, preferred_element_type=F32)
                n_row = n_scr[h:h + 1, :]
                den = w_inter * jnp.sum(qh.astype(F32) * n_row, -1, keepdims=True) + jnp.sum(s, -1, keepdims=True)
                hid = num / jnp.maximum(jnp.abs(den), jnp.exp(-m_out))
                b_last = b_row[:, CHUNK - 1:CHUNK]
                m_new = jnp.maximum(b_last + m_old, jnp.max(b_last - b_row + ig_row, -1, keepdims=True))
                decay = jnp.exp(b_last + m_old - m_new)
                w_col = jnp.exp(b_last - b_col + ig_col - m_new)
                wv = (w_col * vh.astype(F32)).astype(BF16)
                ct_scr[h] = decay * ct + lax.dot_general(kh, wv, tn, preferred_element_type=F32)
                n_scr[h:h + 1, :] = decay * n_row + jnp.sum(w_col * kh.astype(F32), axis=0, keepdims=True)
                m_scr[h:h + 1, :] = jnp.broadcast_to(m_new, (1, LANE))
                ms = jnp.mean(hid * hid, -1, keepdims=True)
                y = hid * lax.rsqrt(ms + 1e-6) * gn_ref[:, hs] * jax.nn.sigmoid(og_ref[rows, hs])
                y_ref[rows, hs] = y.astype(y_ref.dtype)


def _mlstm(mqk, mv, mo, mg, conv_w, conv_b, gate_b, g_b):
    bsz, seq, _ = mqk.shape
    tt = ML_TT
    gb = jnp.pad(gate_b.reshape(1, 2 * H_B), ((0, 0), (0, LANE - 2 * H_B)))
    row = lambda n: pl.BlockSpec((None, tt, n), lambda b, i: (b, i, 0))
    full = lambda r, c: pl.BlockSpec((r, c), lambda b, i: (0, 0))
    tail = pl.BlockSpec((None, 8, 2 * ML_D), lambda b, i: (b, jnp.maximum(i * (tt // 8) - 1, 0), 0))
    return pl.pallas_call(
        _mlstm_kernel,
        grid=(bsz, seq // tt),
        in_specs=[row(2 * ML_D), tail, row(ML_D), row(ML_D), row(LANE),
                  full(CONV_W, 2 * ML_D), full(1, 2 * ML_D), full(1, LANE), full(1, ML_D)],
        out_specs=row(ML_D),
        out_shape=jax.ShapeDtypeStruct((bsz, seq, ML_D), BF16),
        scratch_shapes=[pltpu.VMEM((H_B, DH_B, DH_B), F32), pltpu.VMEM((8, DH_B), F32), pltpu.VMEM((8, LANE), F32),
                        pltpu.VMEM((tt, ML_D), BF16), pltpu.VMEM((tt, ML_D), BF16)],
        compiler_params=_params("arbitrary", "arbitrary"),
        name="mlstm_scan",
    )(mqk, mqk, mv, mo, mg, conv_w, conv_b.reshape(1, 2 * ML_D), gb, g_b.reshape(1, ML_D))


def _mixers(pa, mls, nsa, rope, w_uk, w_uv, kv_norm, conv_w, conv_b, gate_b, cmp_pos, cmp_w1, cmp_w2, grp_norm):
    g_a, g_b, g_c = jnp.split(grp_norm, [H_A * DH_V, H_A * DH_V + H_B * DH_B])
    qc, iq, iw, kc_a, ik, ct_a = _dsa_prep(pa, rope[0], rope[1], w_uk, kv_norm)
    y_a = _dsa_attention(qc, iq, iw, kc_a, ct_a, ik, w_uv, g_a)
    y_b = _mlstm(*mls, conv_w, conv_b, gate_b, g_b)
    nq, ncmp, nks, nvs, nkw, nvw, ng = nsa
    kc, vc = _nsa_compress(ncmp, cmp_pos, cmp_w1, cmp_w2)
    y_c = _nsa_attention(nq, ng, kc, vc, nks, nvs, nkw, nvw, g_c)
    return y_a, y_b, y_c


def _pad_cols(w, n):
    return jnp.pad(w, ((0, 0), (0, n - w.shape[1])))


def kernel(x, c, positions, w_mod, b_mod, w_in, dsa_w_uk, dsa_w_uv, dsa_kv_norm, mlstm_conv_w, mlstm_conv_b, mlstm_gate_b, nsa_cmp_pos, nsa_cmp_w1, nsa_cmp_w2, grp_norm, w_out, ln_g, ln_b, ffn_w_gate, ffn_w_up, ffn_w_down, moe_router, moe_w_gate, moe_w_up, moe_w_down):
    bsz = x.shape[0]
    mod = _modulation(c, w_mod, b_mod).reshape(bsz, DEPTH, N_MOD, 1, D_MODEL)
    rope = _rope_table(positions)
    for l in range(DEPTH):
        sh1, sc1, g1, sh2, sc2, g2 = [mod[:, l, j] for j in range(N_MOD)]
        w = w_in[l].astype(BF16)
        wa = _dsa_weight_layout(w[:, :N_GROUP_A])
        wb = _mlstm_weight_layout(w[:, N_GROUP_A:N_GROUP_A + N_GROUP_B])
        wc, wvt = _nsa_weight_layout(w[:, N_GROUP_A + N_GROUP_B:])
        pa, mqk, mv, mo, mg, *nsa = _input_projection(x, sc1, sh1, wa, wb, wc, wvt)
        ya, yb, yc = _mixers(pa, (mqk, mv, mo, mg), nsa, rope, dsa_w_uk[l], dsa_w_uv[l], dsa_kv_norm[l], mlstm_conv_w[l], mlstm_conv_b[l], mlstm_gate_b[l], nsa_cmp_pos[l], nsa_cmp_w1[l], nsa_cmp_w2[l], grp_norm[l])
        wo = w_out[l].astype(BF16)
        n_a, n_b = H_A * DH_V, H_B * DH_B
        x = _output_projection(ya, yb, yc, x, g1, _pad_head_rows(wo[:n_a], H_A, DH_V), wo[n_a:n_a + n_b], _pad_head_rows(wo[n_a + n_b:], H_C, DH_C), ln_g[l, 0], ln_b[l, 0])
        if l % 2 == 0:
            k = l // 2
            x = _dense_ffn(x, sc2, sh2, g2, ffn_w_gate[k].astype(BF16), ffn_w_up[k].astype(BF16), ffn_w_down[k].astype(BF16), ln_g[l, 1], ln_b[l, 1])
        else:
            k = l // 2
            routing = _moe_router(x, sc2, sh2, moe_router[k])
            x = _moe_ffn(x, sc2, sh2, g2, routing,moe_w_gate[k].astype(BF16), moe_w_up[k].astype(BF16), moe_w_down[k].astype(BF16), ln_g[l, 1], ln_b[l, 1])
    return x
```

```python
import functools

import numpy as np
import jax
import jax.numpy as jnp
from jax import lax
from jax.experimental import pallas as pl
from jax.experimental.pallas import tpu as pltpu

F32 = jnp.float32
BF16 = jnp.bfloat16

D_MODEL = 1024
DEPTH = 2
H_A, DH_NOPE, DH_ROPE, D_C, DH_V, H_I, D_I = 4, 64, 32, 128, 64, 4, 64
K_SEL_MAX = 256
ROPE_THETA = 10000.0
H_B, DH_B, CONV_W, CHUNK = 4, 128, 4, 64
H_C, DH_C, L_CMP, D_STRIDE, CMP_HIDDEN, L_SEL, N_TOP_MAX, WINDOW, Q_BLOCK = 4, 64, 32, 16, 128, 64, 16, 512, 128
D_FF = 2816
N_EXPERTS = 8
D_FF_EXPERT = 3584
N_MOD = 6
ALPHA = (2 * DEPTH) ** 0.25
SPLIT_SIZES = (H_A * DH_NOPE, H_A * DH_ROPE, D_C, DH_ROPE, H_I * D_I, D_I, H_I, H_B * DH_B, H_B * DH_B, H_B * DH_B, H_B, H_B, H_B * DH_B, H_C * DH_C, DH_C, DH_C, DH_C, DH_C, DH_C, DH_C, 3 * H_C)
N_GROUP_A = sum(SPLIT_SIZES[:7])
N_GROUP_B = sum(SPLIT_SIZES[7:13])
N_GROUP_C = sum(SPLIT_SIZES[13:])

LANE = 128
VMEM_LIMIT = 56 * 1024 * 1024


def _round_up(n, m):
    return (n + m - 1) // m * m


def _params(*sem):
    return pltpu.CompilerParams(dimension_semantics=sem, vmem_limit_bytes=VMEM_LIMIT)


FOLD_ROWS = 64


def _fold_rows(x, op):
    parts = [x[i:i + FOLD_ROWS] for i in range(0, x.shape[0], FOLD_ROWS)]
    while len(parts) > 1:
        parts = [op(parts[i], parts[i + 1]) if i + 1 < len(parts) else parts[i] for i in range(0, len(parts), 2)]
    return parts[0]


def _attend_block(s_buf, b_buf, v_t, m, l, acc_scr, n_head, tq):
    bias = b_buf[...]
    m_new, l_new, ps = [], [], []
    for h in range(n_head):
        hs = slice(h * tq, (h + 1) * tq)
        sh = s_buf[:, hs] + bias
        mh = jnp.maximum(m[:, hs], jnp.max(_fold_rows(sh, jnp.maximum), axis=0, keepdims=True))
        ph = jnp.exp(sh - mh)
        l_new.append(jnp.exp(m[:, hs] - mh) * l[:, hs] + jnp.sum(_fold_rows(ph, jnp.add), axis=0, keepdims=True))
        m_new.append(mh)
        ps.append(ph.astype(BF16))
    m_new = jnp.concatenate(m_new, axis=1)
    acc_scr[...] = jnp.exp(m - m_new) * acc_scr[...] + jnp.dot(v_t, jnp.concatenate(ps, axis=1), preferred_element_type=F32)
    return m_new, jnp.concatenate(l_new, axis=1)


def _layer_norm_rows(z, g, b):
    mu = jnp.mean(z, -1, keepdims=True)
    zc = z - mu
    var = jnp.mean(zc * zc, -1, keepdims=True)
    return zc * lax.rsqrt(var + 1e-5) * g + b


def _mod_kernel(c_ref, w_ref, b_ref, o_ref):
    c = c_ref[...]
    a = c * jax.nn.sigmoid(c)
    o_ref[...] = jnp.dot(a, w_ref[...], preferred_element_type=F32, precision=lax.Precision.HIGHEST) + b_ref[...]


def _modulation(c, w_mod, b_mod):
    bsz, d = c.shape
    n = w_mod.shape[1]
    tn = 1024
    return pl.pallas_call(
        _mod_kernel,
        grid=(n // tn,),
        in_specs=[pl.BlockSpec((bsz, d), lambda j: (0, 0)),
                  pl.BlockSpec((d, tn), lambda j: (0, j)),
                  pl.BlockSpec((1, tn), lambda j: (0, j))],
        out_specs=pl.BlockSpec((bsz, tn), lambda j: (0, j)),
        out_shape=jax.ShapeDtypeStruct((bsz, n), F32),
        compiler_params=_params("arbitrary"),
        name="adaln_mod",
    )(c, w_mod, b_mod.reshape(1, n))


NSA_Q0 = 0
NSA_CMP0 = H_C * LANE
NSA_KS0 = NSA_CMP0 + LANE
NSA_KW0 = NSA_KS0 + LANE
NSA_G0 = NSA_KW0 + LANE
NSA_COLS = NSA_G0 + LANE


def _nsa_weight_layout(w):
    d = w.shape[0]
    nq, nkc, nvc, nks, nvs, nkw, nvw, ng = jnp.split(w, np.cumsum(SPLIT_SIZES[13:])[:-1].tolist(), axis=1)
    z = lambda n: jnp.zeros((d, n), w.dtype)
    half = LANE - DH_C
    cols = []
    for h in range(H_C):
        cols += [nq[:, h * DH_C:(h + 1) * DH_C], z(half)]
    cols += [nkc, nvc, nks, z(half), nkw, z(half), ng, z(LANE - 3 * H_C)]
    values_t = jnp.concatenate([nvs, z(half), nvw, z(half)], axis=1).T
    return jnp.concatenate(cols, axis=1), values_t


def _mlstm_weight_layout(w):
    mq, mk, mv, mi, mf, mo = jnp.split(w, np.cumsum(SPLIT_SIZES[7:13])[:-1].tolist(), axis=1)
    return jnp.concatenate([mq, mk, mv, mo, mi, mf, jnp.zeros((w.shape[0], LANE - 2 * H_B), w.dtype)], axis=1)


ML_D = H_B * DH_B


def _inproj_kernel(x_ref, sc_ref, sh_ref, wa_ref, wb_ref, wc_ref, wvt_ref, oa_ref, mqk_ref, mv_ref, mo_ref, mg_ref,
                   nq_ref, ncmp_ref, nks_ref, nvs_ref, nkw_ref, nvw_ref, ng_ref):
    u = (x_ref[...] * (1.0 + sc_ref[...]) + sh_ref[...]).astype(BF16)
    oa_ref[...] = jnp.dot(u, wa_ref[...], preferred_element_type=F32)
    ob = jnp.dot(u, wb_ref[...], preferred_element_type=F32)
    mqk_ref[...] = ob[:, 0:2 * ML_D]
    mv_ref[...] = ob[:, 2 * ML_D:3 * ML_D].astype(BF16)
    mo_ref[...] = ob[:, 3 * ML_D:4 * ML_D]
    mg_ref[...] = ob[:, 4 * ML_D:4 * ML_D + LANE]
    oc = jnp.dot(u, wc_ref[...], preferred_element_type=F32)
    nq_ref[...] = oc[:, NSA_Q0:NSA_Q0 + H_C * LANE].astype(BF16)
    ncmp_ref[...] = oc[:, NSA_CMP0:NSA_CMP0 + LANE]
    nks_ref[...] = oc[:, NSA_KS0:NSA_KS0 + LANE].astype(BF16)
    nkw_ref[...] = oc[:, NSA_KW0:NSA_KW0 + LANE].astype(BF16)
    ng_ref[...] = oc[:, NSA_G0:NSA_G0 + LANE]
    vt = lax.dot_general(wvt_ref[...], u, (((1,), (1,)), ((), ())), preferred_element_type=F32)
    nvs_ref[...] = vt[0:LANE].astype(BF16)
    nvw_ref[...] = vt[LANE:2 * LANE].astype(BF16)


def _input_projection(x, sc, sh, wa, wb, wc, wvt):
    bsz, seq, d = x.shape
    tm = 512
    na, nb, nc = wa.shape[1], wb.shape[1], wc.shape[1]
    row = lambda n: pl.BlockSpec((None, tm, n), lambda b, i: (b, i, 0))
    col = pl.BlockSpec((None, LANE, tm), lambda b, i: (b, 0, i))
    vec = pl.BlockSpec((None, 1, d), lambda b, i: (b, 0, 0))
    full = lambda n: pl.BlockSpec((d, n), lambda b, i: (0, 0))
    tok = lambda n, dt: (row(n), jax.ShapeDtypeStruct((bsz, seq, n), dt))
    feat = (col, jax.ShapeDtypeStruct((bsz, LANE, seq), BF16))
    outs = [tok(na, F32), tok(2 * ML_D, F32), tok(ML_D, BF16), tok(ML_D, F32), tok(LANE, F32),
            tok(H_C * LANE, BF16), tok(LANE, F32), tok(LANE, BF16), feat, tok(LANE, BF16), feat, tok(LANE, F32)]
    return pl.pallas_call(
        _inproj_kernel,
        grid=(bsz, seq // tm),
        in_specs=[row(d), vec, vec, full(na), full(nb), full(nc), pl.BlockSpec((2 * LANE, d), lambda b, i: (0, 0))],
        out_specs=[spec for spec, _ in outs],
        out_shape=[shape for _, shape in outs],
        compiler_params=_params("arbitrary", "arbitrary"),
        name="input_projection",
    )(x, sc, sh, wa, wb, wc, wvt)


DSA_QN0 = 0
DSA_QR0 = H_A * LANE
DSA_CKV0 = DSA_QR0 + LANE
DSA_IQ0 = DSA_CKV0 + LANE
DSA_G0 = DSA_IQ0 + H_I * LANE
DSA_COLS = DSA_G0 + LANE
DSA_KR_LANE = D_I
DSA_IW_LANE = D_I + DH_ROPE
DSA_TQ = 128
DSA_KC = 512
INT_MIN = -2 ** 31
KEY_NEG_INF = int(np.int32(np.float32(-np.inf).view(np.int32)) ^ np.int32(0x7FFFFFFF))


def _dsa_weight_layout(w):
    d = w.shape[0]
    qn, qr, ckv, kr, iq, ik, iw = jnp.split(w, np.cumsum(SPLIT_SIZES[:7])[:-1].tolist(), axis=1)
    z = lambda n: jnp.zeros((d, n), w.dtype)
    cols = []
    for h in range(H_A):
        cols += [qn[:, h * DH_NOPE:(h + 1) * DH_NOPE], z(LANE - DH_NOPE)]
    cols += [qr, ckv]
    for h in range(H_I):
        cols += [iq[:, h * D_I:(h + 1) * D_I], z(LANE - D_I)]
    cols += [ik, kr, iw, z(LANE - D_I - DH_ROPE - H_I)]
    return jnp.concatenate(cols, axis=1)


def _rope_table_kernel(pos_ref, freq_ref, cos_ref, sin_ref):
    ang = pos_ref[...].astype(F32) * freq_ref[...]
    lane = lax.broadcasted_iota(jnp.int32, (1, LANE), 1)
    first = (lane % DH_ROPE) < DH_ROPE // 2
    cos_ref[...] = jnp.cos(ang)
    sin_ref[...] = jnp.where(first, -jnp.sin(ang), jnp.sin(ang))


def _rope_table(positions):
    bsz, seq = positions.shape
    tm = 512
    inv_freq = ROPE_THETA ** (-jnp.arange(0, DH_ROPE, 2, dtype=F32) / DH_ROPE)
    freq = jnp.tile(inv_freq, LANE // (DH_ROPE // 2)).reshape(1, LANE)
    out = pl.BlockSpec((None, tm, LANE), lambda b, i: (b, i, 0))
    return pl.pallas_call(
        _rope_table_kernel,
        grid=(bsz, seq // tm),
        in_specs=[pl.BlockSpec((None, tm, 1), lambda b, i: (b, i, 0)), pl.BlockSpec((1, LANE), lambda b, i: (0, 0))],
        out_specs=[out, out],
        out_shape=[jax.ShapeDtypeStruct((bsz, seq, LANE), F32)] * 2,
        compiler_params=_params("arbitrary", "arbitrary"),
        name="rope_table",
    )(positions.reshape(bsz, seq, 1), freq)


def _dsa_prep_kernel(pa_ref, cos_ref, sin_ref, wuk_ref, kvn_ref, qc_ref, iq_ref, iw_ref, kc_ref, ik_ref, ct_ref):
    scale = (DH_NOPE + DH_ROPE) ** -0.5
    cos, sin = cos_ref[...], sin_ref[...]
    lane = lax.broadcasted_iota(jnp.int32, (1, LANE), 1)
    first = (lane % DH_ROPE) < DH_ROPE // 2
    rope_lanes = (lane >= DSA_KR_LANE) & (lane < DSA_KR_LANE + DH_ROPE)

    def rope(v):
        partner = jnp.where(first, pltpu.roll(v, LANE - DH_ROPE // 2, 1), pltpu.roll(v, DH_ROPE // 2, 1))
        return v * cos + partner * sin

    g = pa_ref[:, DSA_G0:DSA_G0 + LANE]
    ckv = pa_ref[:, DSA_CKV0:DSA_CKV0 + LANE]
    ckv_n = ckv * lax.rsqrt(jnp.mean(ckv * ckv, -1, keepdims=True) + 1e-6) * kvn_ref[...]
    kc_ref[:, 0:LANE] = ckv_n.astype(BF16)
    ct_ref[...] = ckv_n.T.astype(BF16)
    kc_ref[:, LANE:2 * LANE] = jnp.where(rope_lanes, rope(g), 0.0).astype(BF16)
    ik_ref[...] = jnp.where(lane < D_I, g, 0.0).astype(BF16)
    iw_ref[...] = g
    iq_ref[...] = pa_ref[:, DSA_IQ0:DSA_IQ0 + H_I * LANE].astype(BF16)
    qr = rope(pa_ref[:, DSA_QR0:DSA_QR0 + LANE]) * scale
    for h in range(H_A):
        qn = pa_ref[:, DSA_QN0 + h * LANE:DSA_QN0 + (h + 1) * LANE].astype(BF16)
        q_abs = jnp.dot(qn, wuk_ref[h], preferred_element_type=F32) * scale
        shift = (DSA_KR_LANE - DH_ROPE * h) % LANE
        qr_h = pltpu.roll(qr, shift, 1) if shift else qr
        qc_ref[:, 2 * h * LANE:(2 * h + 1) * LANE] = q_abs.astype(BF16)
        qc_ref[:, (2 * h + 1) * LANE:(2 * h + 2) * LANE] = jnp.where(rope_lanes, qr_h, 0.0).astype(BF16)


def _dsa_prep(pa, cos, sin, w_uk, kv_norm):
    bsz, seq, _ = pa.shape
    tm = 512
    wuk = jnp.pad(w_uk, ((0, 0), (0, LANE - DH_NOPE), (0, 0))).astype(BF16)
    row = lambda n: pl.BlockSpec((None, tm, n), lambda b, i: (b, i, 0))
    outs = [(2 * H_A * LANE, BF16), (H_I * LANE, BF16), (LANE, F32), (2 * LANE, BF16), (LANE, BF16)]
    return pl.pallas_call(
        _dsa_prep_kernel,
        grid=(bsz, seq // tm),
        in_specs=[row(DSA_COLS), row(LANE), row(LANE),
                  pl.BlockSpec((H_A, LANE, D_C), lambda b, i: (0, 0, 0)), pl.BlockSpec((1, D_C), lambda b, i: (0, 0))],
        out_specs=[row(n) for n, _ in outs] + [pl.BlockSpec((None, D_C, tm), lambda b, i: (b, 0, i))],
        out_shape=[jax.ShapeDtypeStruct((bsz, seq, n), dt) for n, dt in outs] + [jax.ShapeDtypeStruct((bsz, D_C, seq), BF16)],
        compiler_params=_params("arbitrary", "arbitrary"),
        name="dsa_prep",
    )(pa, cos, sin, wuk, kv_norm.reshape(1, D_C))


def _dsa_kernel(k_sel, qc_ref, iq_ref, iw_ref, kc_ref, ct_ref, ik_ref, tri_ref, wuv_ref, gn_ref, o_ref,
                key_scr, acc_scr, s0_scr, s1_scr, b0_scr, b1_scr):
    tq, kcs = DSA_TQ, DSA_KC
    t0 = pl.program_id(1) * tq
    n_chunk = (t0 + tq + kcs - 1) // kcs
    nt = (((1,), (1,)), ((), ()))
    qpos = t0 + lax.broadcasted_iota(jnp.int32, (1, tq), 1)
    iw_t = iw_ref[...].T

    def score_chunk(c, carry):
        k0 = pl.multiple_of(c * kcs, kcs)
        ikc = ik_ref[pl.ds(k0, kcs), :]
        sc = jnp.zeros((kcs, tq), F32)
        for h in range(H_I):
            lg = lax.dot_general(ikc, iq_ref[:, h * LANE:(h + 1) * LANE], nt, preferred_element_type=F32)
            sc = sc + jnp.maximum(lg, 0.0) * iw_t[DSA_IW_LANE + h:DSA_IW_LANE + h + 1, :]
        sc = jnp.where(sc == 0.0, 0.0, sc)
        kpos = k0 + lax.broadcasted_iota(jnp.int32, (kcs, 1), 0)
        sc = jnp.where(kpos <= qpos, sc, -jnp.inf)
        bits = pltpu.bitcast(sc, jnp.int32)
        key_scr[pl.ds(k0, kcs), :] = jnp.where(bits < 0, bits ^ 0x7FFFFFFF, bits)
        return carry

    lax.fori_loop(0, n_chunk, score_chunk, 0)

    def count(pred):
        def body(c, acc):
            k0 = pl.multiple_of(c * kcs, kcs)
            one = jnp.where(pred(key_scr[pl.ds(k0, kcs), :]), 1.0, 0.0)
            return acc + _fold_rows(one, jnp.add)
        acc = lax.fori_loop(0, n_chunk, body, jnp.zeros((FOLD_ROWS, tq), F32))
        return jnp.sum(acc, axis=0, keepdims=True)

    thr = jnp.where(count(lambda k: k >= 0) >= k_sel, jnp.int32(0), jnp.int32(INT_MIN))

    def bit_step(i, thr):
        cand = thr | jnp.left_shift(jnp.int32(1), 30 - i)
        return jnp.where(count(lambda k: k >= cand) >= k_sel, cand, thr)

    thr = lax.fori_loop(0, 31, bit_step, thr)
    room = k_sel - count(lambda k: k > thr)

    qall = jnp.concatenate([qc_ref[:, 2 * h * LANE:(2 * h + 2) * LANE] for h in range(H_A)], axis=0)

    room = jnp.where(thr > KEY_NEG_INF, room, 0.0)
    acc_scr[...] = jnp.zeros_like(acc_scr)

    half = kcs // 2

    def prepare(k0, seen, s_buf, b_buf):
        key = key_scr[pl.ds(k0, half), :]
        tie = jnp.where(key == thr, 1.0, 0.0)
        prefix = jnp.dot(tri_ref[...], tie.astype(BF16), preferred_element_type=F32)
        tie_bias = jnp.where(seen + prefix <= room, jnp.where(key == thr, 0.0, NEG), NEG)
        b_buf[...] = jnp.where(key > thr, 0.0, tie_bias)
        s_buf[...] = lax.dot_general(kc_ref[pl.ds(k0, half), :], qall, nt, preferred_element_type=F32)
        return seen + jnp.sum(_fold_rows(tie, jnp.add), axis=0, keepdims=True)

    def attn_chunk(c, carry):
        m, l, seen = carry
        k0 = pl.multiple_of(c * kcs, kcs)
        k1 = pl.multiple_of(k0 + half, half)
        seen = prepare(k1, seen, s1_scr, b1_scr)
        m, l = _attend_block(s0_scr, b0_scr, ct_ref[:, pl.ds(k0, half)], m, l, acc_scr, H_A, tq)
        k2 = pl.multiple_of(jnp.minimum(c + 1, n_chunk - 1) * kcs, kcs)
        seen = prepare(k2, seen, s0_scr, b0_scr)
        m, l = _attend_block(s1_scr, b1_scr, ct_ref[:, pl.ds(k1, half)], m, l, acc_scr, H_A, tq)
        return m, l, seen

    seen0 = prepare(0, jnp.zeros((1, tq), F32), s0_scr, b0_scr)
    init = (jnp.full((1, H_A * tq), M_FLOOR, F32), jnp.zeros((1, H_A * tq), F32), seen0)
    _, l, _ = lax.fori_loop(0, n_chunk, attn_chunk, init)
    o_lat = (acc_scr[...] / jnp.maximum(l, 1e-30)).astype(BF16)
    for h in range(H_A):
        o = lax.dot_general(o_lat[:, h * tq:(h + 1) * tq], wuv_ref[h], (((0,), (0,)), ((), ())),
                            preferred_element_type=F32)
        ms = jnp.sum(o * o, -1, keepdims=True) * (1.0 / DH_V)
        o_ref[:, h * LANE:(h + 1) * LANE] = (o * lax.rsqrt(ms + 1e-6) * gn_ref[h:h + 1, :]).astype(o_ref.dtype)


def _dsa_attention(qc, iq, iw, kc, ct, ik, w_uv, g_a):
    bsz, seq, _ = qc.shape
    k_sel = float(min(K_SEL_MAX, seq // 4))
    tq, kcs = DSA_TQ, DSA_KC
    half = kcs // 2
    tri = jnp.asarray(np.tril(np.ones((half, half), np.float32)), BF16)
    wuv = jnp.pad(w_uv, ((0, 0), (0, 0), (0, LANE - DH_V))).astype(BF16)
    gn = jnp.pad(g_a.reshape(H_A, DH_V), ((0, 0), (0, LANE - DH_V)))
    row = lambda n: pl.BlockSpec((None, tq, n), lambda b, i: (b, i, 0))
    per_b = lambda n: pl.BlockSpec((None, seq, n), lambda b, i: (b, 0, 0))
    return pl.pallas_call(
        functools.partial(_dsa_kernel, k_sel),
        grid=(bsz, seq // tq),
        in_specs=[row(2 * H_A * LANE), row(H_I * LANE), row(LANE), per_b(2 * LANE),
                  pl.BlockSpec((None, D_C, seq), lambda b, i: (b, 0, 0)), per_b(LANE),
                  pl.BlockSpec((half, half), lambda b, i: (0, 0)), pl.BlockSpec((H_A, D_C, LANE), lambda b, i: (0, 0, 0)),
                  pl.BlockSpec((H_A, LANE), lambda b, i: (0, 0))],
        out_specs=row(H_A * LANE),
        out_shape=jax.ShapeDtypeStruct((bsz, seq, H_A * LANE), BF16),
        scratch_shapes=[pltpu.VMEM((seq, tq), jnp.int32), pltpu.VMEM((D_C, H_A * tq), F32),
                        pltpu.VMEM((half, H_A * tq), F32), pltpu.VMEM((half, H_A * tq), F32),
                        pltpu.VMEM((half, tq), F32), pltpu.VMEM((half, tq), F32)],
        compiler_params=_params("arbitrary", "arbitrary"),
        name="dsa_attention",
    )(qc, iq, iw, kc, ct, ik, tri, wuv, gn)


def _nsa_compress_kernel(a_ref, pos_ref, w1t_ref, w1b_ref, w2k_ref, w2v_ref, kc_ref, vc_ref):
    a = a_ref[...]
    top = jnp.dot((a + pos_ref[0:1, :]).astype(BF16), w1t_ref[...], preferred_element_type=F32)
    bot = jnp.dot((a + pos_ref[1:2, :]).astype(BF16), w1b_ref[...], preferred_element_type=F32)
    n = a.shape[0]
    pre = top + jnp.concatenate([bot[1:], bot[:1]], axis=0)
    h = (pre * jax.nn.sigmoid(pre)).astype(BF16)
    hid = w2k_ref.shape[0]
    kc_ref[...] = jnp.dot(h[:, :hid], w2k_ref[...], preferred_element_type=F32).astype(BF16)
    vc_ref[...] = lax.dot_general(w2v_ref[...], h[:, hid:], (((1,), (1,)), ((), ())), preferred_element_type=F32).astype(BF16)


def _nsa_compress(ncmp, cmp_pos, cmp_w1, cmp_w2):
    bsz, seq, _ = ncmp.shape
    n_grp = seq // D_STRIDE
    per = L_CMP // D_STRIDE
    width = D_STRIDE * LANE
    a = ncmp.reshape(bsz, n_grp, width)
    w1 = cmp_w1.reshape(2, per, D_STRIDE, DH_C, CMP_HIDDEN)
    zer = jnp.zeros((D_STRIDE, DH_C, CMP_HIDDEN), cmp_w1.dtype)

    def expand(p):
        wk = jnp.concatenate([w1[0, p], zer], axis=1)
        wv = jnp.concatenate([zer, w1[1, p]], axis=1)
        return jnp.concatenate([wk, wv], axis=2).reshape(width, 2 * CMP_HIDDEN).astype(BF16)

    pos = cmp_pos.reshape(2, per, D_STRIDE, DH_C)
    pos = jnp.concatenate([pos[0], pos[1]], axis=-1).reshape(per, width)
    pad_out = ((0, 0), (0, LANE - DH_C))
    w2k = jnp.pad(cmp_w2[0], pad_out).astype(BF16)
    w2v = jnp.pad(cmp_w2[1], pad_out).astype(BF16).T
    full = lambda r, c: pl.BlockSpec((r, c), lambda b: (0, 0))
    return pl.pallas_call(
        _nsa_compress_kernel,
        grid=(bsz,),
        in_specs=[pl.BlockSpec((None, n_grp, width), lambda b: (b, 0, 0)), full(per, width),
                  full(width, 2 * CMP_HIDDEN), full(width, 2 * CMP_HIDDEN), full(CMP_HIDDEN, LANE), full(LANE, CMP_HIDDEN)],
        out_specs=[pl.BlockSpec((None, n_grp, LANE), lambda b: (b, 0, 0)), pl.BlockSpec((None, LANE, n_grp), lambda b: (b, 0, 0))],
        out_shape=[jax.ShapeDtypeStruct((bsz, n_grp, LANE), BF16), jax.ShapeDtypeStruct((bsz, LANE, n_grp), BF16)],
        compiler_params=_params("arbitrary"),
        name="nsa_compress",
    )(a, pos, expand(0), expand(1), w2k, w2v)


NSA_TQ = 128
NSA_KC = 512
NEG = -1e30
M_FLOOR = -1e20


def _softmax_cols(s, bias, n_head, tq):
    out = []
    for h in range(n_head):
        sh = s[:, h * tq:(h + 1) * tq] + bias
        m = jnp.maximum(jnp.max(_fold_rows(sh, jnp.maximum), axis=0, keepdims=True), M_FLOOR)
        e = jnp.exp(sh - m)
        den = jnp.sum(_fold_rows(e, jnp.add), axis=0, keepdims=True)
        out.append(e * (1.0 / jnp.maximum(den, 1e-30)))
    return out


def _nsa_kernel(q_ref, g_ref, kc_ref, vct_ref, ks_ref, vst_ref, kw_ref, vwt_ref, cover_ref, expand_ref, gn_ref, o_ref,
                acc_scr, s0_scr, s1_scr, b0_scr, b1_scr):
    tq, kc_sz = NSA_TQ, NSA_KC
    t0 = pl.program_id(1) * tq
    nt = (((1,), (1,)), ((), ()))
    q = q_ref[...] * (DH_C ** -0.5)
    qa = jnp.concatenate([q[:, h * LANE:(h + 1) * LANE] for h in range(H_C)], axis=0)
    qpos = t0 + lax.broadcasted_iota(jnp.int32, (1, tq), 1)

    n_grp = kc_ref.shape[0]
    s_c = lax.dot_general(kc_ref[...], qa, nt, preferred_element_type=F32)
    n_idx = lax.broadcasted_iota(jnp.int32, (n_grp, 1), 0)
    visible = jnp.where(n_idx < n_grp - 1, n_idx * D_STRIDE + (L_CMP - 1), 2 ** 30)
    p_c = _softmax_cols(s_c, jnp.where(visible <= qpos, 0.0, NEG), H_C, tq)
    o_cmp = jnp.dot(vct_ref[...], jnp.concatenate([p.astype(BF16) for p in p_c], axis=1), preferred_element_type=F32)

    p_sum = p_c[0]
    for h in range(1, H_C):
        p_sum = p_sum + p_c[h]
    hi = p_sum.astype(BF16)
    lo = (p_sum - hi.astype(F32)).astype(BF16)
    cov = cover_ref[...]
    imp_t = jnp.dot(cov, hi, preferred_element_type=F32) + jnp.dot(cov, lo, preferred_element_type=F32)
    n_sel = cov.shape[0]
    jrow = lax.broadcasted_iota(jnp.int32, (n_sel, tq), 0)
    cur = (t0 + lax.broadcasted_iota(jnp.int32, (n_sel, tq), 1)) // L_SEL
    adm = jrow <= cur
    forced = (jrow == 0) | (jrow == cur) | (jrow == cur - 1)
    val = jnp.where(adm & forced, jnp.inf, jnp.where(adm, imp_t, -jnp.inf))
    rank = jnp.zeros((n_sel, tq), F32)
    for jp in range(n_sel):
        r = val[jp:jp + 1, :]
        rank = rank + jnp.where(r == val, jnp.where(jrow > jp, 1.0, 0.0), jnp.where(r > val, 1.0, 0.0))
    sel_t = jnp.where(rank < min(N_TOP_MAX, n_sel), jnp.where(val > -jnp.inf, 1.0, 0.0), 0.0)
    if n_sel < LANE:
        sel_t = jnp.concatenate([sel_t, jnp.zeros((LANE - n_sel, tq), F32)], axis=0)
    sel_t = sel_t.astype(BF16)

    acc_scr[...] = jnp.zeros_like(acc_scr)

    half = kc_sz // 2
    n_chunk = (t0 + tq + kc_sz - 1) // kc_sz

    def prepare(k0, s_buf, b_buf):
        s_buf[...] = lax.dot_general(ks_ref[pl.ds(k0, half), :], qa, nt, preferred_element_type=F32)
        picked = jnp.dot(expand_ref[pl.ds(k0, half), :], sel_t, preferred_element_type=F32)
        kpos = k0 + lax.broadcasted_iota(jnp.int32, (half, 1), 0)
        b_buf[...] = jnp.where(kpos <= qpos, jnp.where(picked > 0.5, 0.0, NEG), NEG)

    def chunk(c, carry):
        m, l = carry
        k0 = pl.multiple_of(c * kc_sz, kc_sz)
        k1 = pl.multiple_of(k0 + half, half)
        prepare(k1, s1_scr, b1_scr)
        m, l = _attend_block(s0_scr, b0_scr, vst_ref[:, pl.ds(k0, half)], m, l, acc_scr, H_C, tq)
        prepare(pl.multiple_of(jnp.minimum(c + 1, n_chunk - 1) * kc_sz, kc_sz), s0_scr, b0_scr)
        return _attend_block(s1_scr, b1_scr, vst_ref[:, pl.ds(k1, half)], m, l, acc_scr, H_C, tq)

    prepare(0, s0_scr, b0_scr)
    init = (jnp.full((1, H_C * tq), M_FLOOR, F32), jnp.zeros((1, H_C * tq), F32))
    _, l_s = lax.fori_loop(0, n_chunk, chunk, init)
    o_slc = acc_scr[...] * (1.0 / jnp.maximum(l_s, 1e-30))

    span = WINDOW + tq
    w0 = pl.multiple_of(jnp.maximum(t0 - WINDOW, 0), tq)
    s_w = lax.dot_general(kw_ref[pl.ds(w0, span), :], qa, nt, preferred_element_type=F32)
    kpos_w = w0 + lax.broadcasted_iota(jnp.int32, (span, 1), 0)
    wbias = jnp.where(kpos_w <= qpos, jnp.where(qpos - kpos_w < WINDOW, 0.0, NEG), NEG)
    p_w = _softmax_cols(s_w, wbias, H_C, tq)
    o_win = jnp.dot(vwt_ref[:, pl.ds(w0, span)], jnp.concatenate([p.astype(BF16) for p in p_w], axis=1), preferred_element_type=F32)

    gates_t = jax.nn.sigmoid(g_ref[...]).T
    for h in range(H_C):
        hs = slice(h * tq, (h + 1) * tq)
        o_t = (gates_t[h:h + 1] * o_cmp[:, hs] + gates_t[H_C + h:H_C + h + 1] * o_slc[:, hs]
               + gates_t[2 * H_C + h:2 * H_C + h + 1] * o_win[:, hs])
        o = o_t.T
        ms = jnp.sum(o * o, -1, keepdims=True) * (1.0 / DH_C)
        o_ref[:, h * LANE:(h + 1) * LANE] = (o * lax.rsqrt(ms + 1e-6) * gn_ref[h:h + 1, :]).astype(o_ref.dtype)


def _nsa_attention(nq, ng, kc, vc_t, nks, nvs_t, nkw, nvw_t, g_c):
    bsz, seq, _ = nq.shape
    n_grp = kc.shape[1]
    n_sel = seq // L_SEL
    grp_start = np.arange(n_grp) * D_STRIDE
    sel_start = np.arange(n_sel) * L_SEL
    cover_t = ((grp_start[None, :] < sel_start[:, None] + L_SEL) & (grp_start[None, :] + L_CMP > sel_start[:, None]))
    cover_t = jnp.asarray(cover_t.astype(np.float32), BF16)
    expand = (np.arange(seq)[:, None] // L_SEL == np.arange(LANE)[None, :]).astype(np.float32)
    expand = jnp.asarray(expand, BF16)
    gn = jnp.pad(g_c.reshape(H_C, DH_C), ((0, 0), (0, LANE - DH_C)))
    tq = NSA_TQ
    row = lambda n: pl.BlockSpec((None, tq, n), lambda b, i: (b, i, 0))
    per_b = lambda r, c: pl.BlockSpec((None, r, c), lambda b, i: (b, 0, 0))
    full = lambda r, c: pl.BlockSpec((r, c), lambda b, i: (0, 0))
    return pl.pallas_call(
        _nsa_kernel,
        grid=(bsz, seq // tq),
        in_specs=[row(H_C * LANE), row(LANE), per_b(n_grp, LANE), per_b(LANE, n_grp), per_b(seq, LANE), per_b(LANE, seq),
                  per_b(seq, LANE), per_b(LANE, seq), full(n_sel, n_grp), full(seq, LANE), full(H_C, LANE)],
        out_specs=row(H_C * LANE),
        out_shape=jax.ShapeDtypeStruct((bsz, seq, H_C * LANE), BF16),
        scratch_shapes=[pltpu.VMEM((LANE, H_C * tq), F32),
                        pltpu.VMEM((NSA_KC // 2, H_C * tq), F32), pltpu.VMEM((NSA_KC // 2, H_C * tq), F32),
                        pltpu.VMEM((NSA_KC // 2, tq), F32), pltpu.VMEM((NSA_KC // 2, tq), F32)],
        compiler_params=_params("arbitrary", "arbitrary"),
        name="nsa_attention",
    )(nq, ng, kc, vc_t, nks, nvs_t, nkw, nvw_t, cover_t, expand, gn)


def _outproj_kernel(ya_ref, yb_ref, yc_ref, x_ref, g_ref, wa_ref, wb_ref, wc_ref, lng_ref, lnb_ref, o_ref):
    y = jnp.dot(ya_ref[...], wa_ref[...], preferred_element_type=F32)
    y += jnp.dot(yb_ref[...], wb_ref[...], preferred_element_type=F32)
    y += jnp.dot(yc_ref[...], wc_ref[...], preferred_element_type=F32)
    z = ALPHA * x_ref[...] + (1.0 + g_ref[...]) * y
    o_ref[...] = _layer_norm_rows(z, lng_ref[...], lnb_ref[...])


def _pad_head_rows(w, n_head, dh):
    d = w.shape[1]
    return jnp.pad(w.reshape(n_head, dh, d), ((0, 0), (0, LANE - dh), (0, 0))).reshape(n_head * LANE, d)


def _output_projection(ya, yb, yc, x, g, wa, wb, wc, ln_g, ln_b):
    bsz, seq, d = x.shape
    tm = 512
    na, nb, nc = ya.shape[-1], yb.shape[-1], yc.shape[-1]
    row = lambda n: pl.BlockSpec((None, tm, n), lambda b, i: (b, i, 0))
    vec = pl.BlockSpec((None, 1, d), lambda b, i: (b, 0, 0))
    full = lambda r, c: pl.BlockSpec((r, c), lambda b, i: (0, 0))
    return pl.pallas_call(
        _outproj_kernel,
        grid=(bsz, seq // tm),
        in_specs=[row(na), row(nb), row(nc), row(d), vec, full(na, d), full(nb, d), full(nc, d), full(1, d), full(1, d)],
        out_specs=row(d),
        out_shape=jax.ShapeDtypeStruct((bsz, seq, d), F32),
        compiler_params=_params("arbitrary", "arbitrary"),
        name="output_projection_ln",
    )(ya, yb, yc, x, g, wa, wb, wc, ln_g.reshape(1, d), ln_b.reshape(1, d))


def _tile_gate_up(wg, wu, tf):
    *lead, d, ff = wg.shape
    split = lambda w: jnp.moveaxis(w.reshape(*lead, d, ff // tf, tf), -2, -3)
    return jnp.concatenate([split(wg), split(wu)], axis=-1)


def _swiglu_chunk(u, wgu_ref, wd_ref):
    tf = wd_ref.shape[0]
    ab = jnp.dot(u, wgu_ref[...], preferred_element_type=F32)
    a, b = ab[:, :tf], ab[:, tf:]
    return jnp.dot((a * jax.nn.sigmoid(a) * b).astype(BF16), wd_ref[...], preferred_element_type=F32)


def _ffn_kernel(x_ref, sc_ref, sh_ref, g_ref, wgu_ref, wd_ref, lng_ref, lnb_ref, o_ref, u_scr, acc_scr):
    f = pl.program_id(2)

    @pl.when(f == 0)
    def _():
        u_scr[...] = (x_ref[...] * (1.0 + sc_ref[...]) + sh_ref[...]).astype(BF16)
        acc_scr[...] = jnp.zeros_like(acc_scr)

    acc_scr[...] += _swiglu_chunk(u_scr[...], wgu_ref, wd_ref)

    @pl.when(f == pl.num_programs(2) - 1)
    def _():
        z = ALPHA * x_ref[...] + (1.0 + g_ref[...]) * acc_scr[...]
        o_ref[...] = _layer_norm_rows(z, lng_ref[...], lnb_ref[...])


def _dense_ffn(x, sc, sh, g, wg, wu, wd, ln_g, ln_b):
    bsz, seq, d = x.shape
    ff = wg.shape[1]
    tm, tf = 1024, 256
    row = pl.BlockSpec((None, tm, d), lambda b, i, f: (b, i, 0))
    vec = pl.BlockSpec((None, 1, d), lambda b, i, f: (b, 0, 0))
    one = pl.BlockSpec((1, d), lambda b, i, f: (0, 0))
    return pl.pallas_call(
        _ffn_kernel,
        grid=(bsz, seq // tm, ff // tf),
        in_specs=[row, vec, vec, vec,
                  pl.BlockSpec((None, d, 2 * tf), lambda b, i, f: (f, 0, 0)),
                  pl.BlockSpec((tf, d), lambda b, i, f: (f, 0)),
                  one, one],
        out_specs=row,
        out_shape=jax.ShapeDtypeStruct((bsz, seq, d), F32),
        scratch_shapes=[pltpu.VMEM((tm, d), BF16), pltpu.VMEM((tm, d), F32)],
        compiler_params=_params("arbitrary", "arbitrary", "arbitrary"),
        name="dense_swiglu_ln",
    )(x, sc, sh, g, _tile_gate_up(wg, wu, tf), wd, ln_g.reshape(1, d), ln_b.reshape(1, d))


def _router_kernel(x_ref, sc_ref, sh_ref, r_ref, lo_ref, up_ref, gate_ref, slot_ref, slot_t_ref, cnt_ref):
    u = x_ref[...] * (1.0 + sc_ref[...]) + sh_ref[...]
    logits = jnp.dot(u, r_ref[...], preferred_element_type=F32, precision=lax.Precision.HIGHEST)
    lane = lax.broadcasted_iota(jnp.int32, logits.shape, 1)
    neg = -jnp.inf
    l1 = jnp.where(lane < N_EXPERTS, logits, neg)
    m1 = jnp.max(l1, -1, keepdims=True)
    i1 = jnp.min(jnp.where(l1 == m1, lane, LANE), -1, keepdims=True)
    l2 = jnp.where(lane == i1, neg, l1)
    m2 = jnp.max(l2, -1, keepdims=True)
    i2 = jnp.min(jnp.where(l2 == m2, lane, LANE), -1, keepdims=True)
    e2 = jnp.exp(m2 - m1)
    w1 = 1.0 / (1.0 + e2)
    w2 = e2 / (1.0 + e2)
    gate_ref[...] = jnp.where(lane == i1, w1, jnp.where(lane == i2, w2, 0.0))
    routed = jnp.where((lane == i1) | (lane == i2), 1.0, 0.0)
    before = jnp.dot(lo_ref[...], routed.astype(BF16), preferred_element_type=F32)
    slot_ref[...] = jnp.where(routed > 0.5, before, -1.0)
    routed_t = routed.T
    before_t = jnp.dot(routed_t.astype(BF16), up_ref[...], preferred_element_type=F32)
    slot_t_ref[...] = jnp.where(routed_t > 0.5, before_t, -1.0)
    cnt_ref[...] = jnp.broadcast_to(jnp.sum(routed, axis=0, keepdims=True), cnt_ref.shape).astype(jnp.int32)


MOE_TM = 1024
MOE_ROWS = 320
MOE_TF = 896


def _moe_router(x, sc, sh, router):
    bsz, seq, d = x.shape
    tm = MOE_TM
    r = jnp.pad(router, ((0, 0), (0, LANE - router.shape[1])))
    upper = np.triu(np.ones((tm, tm), np.float32), 1)
    up, lo = jnp.asarray(upper, BF16), jnp.asarray(upper.T, BF16)
    row = lambda n: pl.BlockSpec((None, tm, n), lambda b, i: (b, i, 0))
    vec = pl.BlockSpec((None, 1, d), lambda b, i: (b, 0, 0))
    full = lambda a, c: pl.BlockSpec((a, c), lambda b, i: (0, 0))
    n_tile = seq // tm
    return pl.pallas_call(
        _router_kernel,
        grid=(bsz, n_tile),
        in_specs=[row(d), vec, vec, full(d, LANE), full(tm, tm), full(tm, tm)],
        out_specs=[row(LANE), row(LANE), pl.BlockSpec((None, None, LANE, tm), lambda b, i: (b, i, 0, 0)),
                   pl.BlockSpec((None, None, 8, LANE), lambda b, i: (b, i, 0, 0))],
        out_shape=[jax.ShapeDtypeStruct((bsz, seq, LANE), F32), jax.ShapeDtypeStruct((bsz, seq, LANE), F32),
                   jax.ShapeDtypeStruct((bsz, n_tile, LANE, tm), F32), jax.ShapeDtypeStruct((bsz, n_tile, 8, LANE), jnp.int32)],
        compiler_params=_params("arbitrary", "arbitrary"),
        name="moe_router",
    )(x, sc, sh, r, lo, up)


def _moe_kernel(cnt_ref, x_ref, sc_ref, sh_ref, g_ref, gate_ref, slot_ref, slot_t_ref, wgu_ref, wd_ref, lng_ref, lnb_ref,
                o_ref, u_scr, xg_scr, acc_scr):
    tm, rows = MOE_TM, MOE_ROWS
    e = pl.program_id(2)
    f = pl.program_id(3)
    tile = pl.program_id(0) * pl.num_programs(1) + pl.program_id(1)
    n_pass = (cnt_ref[tile * N_EXPERTS + e] + rows - 1) // rows

    @pl.when((e == 0) & (f == 0))
    def _():
        u_scr[...] = (x_ref[...] * (1.0 + sc_ref[...]) + sh_ref[...]).astype(BF16)
        o_ref[...] = jnp.zeros_like(o_ref)

    @pl.when(f == 0)
    def _():
        slot_row = slot_t_ref[pl.ds(e, 1), :]

        def gather(p, carry):
            want = p * rows + lax.broadcasted_iota(jnp.int32, (rows, 1), 0)
            pick = jnp.where(slot_row == want.astype(F32), 1.0, 0.0).astype(BF16)
            xg_scr[p] = jnp.dot(pick, u_scr[...], preferred_element_type=F32).astype(BF16)
            acc_scr[p] = jnp.zeros((rows, x_ref.shape[-1]), F32)
            return carry

        lax.fori_loop(0, n_pass, gather, 0)

    def ffn(p, carry):
        acc_scr[p] += _swiglu_chunk(xg_scr[p], wgu_ref, wd_ref)
        return carry

    lax.fori_loop(0, n_pass, ffn, 0)

    @pl.when(f == pl.num_programs(3) - 1)
    def _():
        lane = lax.broadcasted_iota(jnp.int32, (tm, LANE), 1)
        slot_col = jnp.sum(jnp.where(lane == e, slot_ref[...], 0.0), axis=1, keepdims=True)
        gate_col = jnp.sum(jnp.where(lane == e, gate_ref[...], 0.0), axis=1, keepdims=True)

        def scatter(p, carry):
            want = p * rows + lax.broadcasted_iota(jnp.int32, (1, rows), 1)
            put = jnp.where(slot_col == want.astype(F32), 1.0, 0.0).astype(BF16)
            y = acc_scr[p]
            hi = y.astype(BF16)
            lo = (y - hi.astype(F32)).astype(BF16)
            back = jnp.dot(put, hi, preferred_element_type=F32) + jnp.dot(put, lo, preferred_element_type=F32)
            o_ref[...] += gate_col * back
            return carry

        lax.fori_loop(0, n_pass, scatter, 0)

    @pl.when((e == pl.num_programs(2) - 1) & (f == pl.num_programs(3) - 1))
    def _():
        z = ALPHA * x_ref[...] + (1.0 + g_ref[...]) * o_ref[...]
        o_ref[...] = _layer_norm_rows(z, lng_ref[...], lnb_ref[...])


def _moe_ffn(x, sc, sh, g, routing, wg, wu, wd, ln_g, ln_b):
    gate, slot, slot_t, cnt = routing
    bsz, seq, d = x.shape
    n_e, _, ff = wg.shape
    tm, tf, rows = MOE_TM, MOE_TF, MOE_ROWS
    max_pass = -(-tm // rows)
    counts = cnt[:, :, 0, :n_e].reshape(-1)
    row = lambda n: pl.BlockSpec((None, tm, n), lambda b, i, e, f, c: (b, i, 0))
    vec = pl.BlockSpec((None, 1, d), lambda b, i, e, f, c: (b, 0, 0))
    one = pl.BlockSpec((1, d), lambda b, i, e, f, c: (0, 0))
    grid_spec = pltpu.PrefetchScalarGridSpec(
        num_scalar_prefetch=1,
        grid=(bsz, seq // tm, n_e, ff // tf),
        in_specs=[row(d), vec, vec, vec, row(LANE), row(LANE),
                  pl.BlockSpec((None, None, LANE, tm), lambda b, i, e, f, c: (b, i, 0, 0)),
                  pl.BlockSpec((None, None, d, 2 * tf), lambda b, i, e, f, c: (e, f, 0, 0)),
                  pl.BlockSpec((None, tf, d), lambda b, i, e, f, c: (e, f, 0)),
                  one, one],
        out_specs=row(d),
        scratch_shapes=[pltpu.VMEM((tm, d), BF16), pltpu.VMEM((max_pass, rows, d), BF16), pltpu.VMEM((max_pass, rows, d), F32)],
    )
    return pl.pallas_call(
        _moe_kernel,
        grid_spec=grid_spec,
        out_shape=jax.ShapeDtypeStruct((bsz, seq, d), F32),
        compiler_params=_params("arbitrary", "arbitrary", "arbitrary", "arbitrary"),
        name="moe_swiglu_ln",
    )(counts, x, sc, sh, g, gate, slot, slot_t, _tile_gate_up(wg, wu, tf), wd, ln_g.reshape(1, d), ln_b.reshape(1, d))


ML_TT = 256
ML_SUB = 128


ML_NB = 1


def _mlstm_kernel(qk_ref, tail_ref, v_ref, og_ref, g_ref, cw_ref, cb_ref, gb_ref, gn_ref, y_ref,
                  ct_scr, n_scr, m_scr, q_scr, k_scr):
    @pl.when(pl.program_id(1) == 0)
    def _():
        ct_scr[...] = jnp.zeros_like(ct_scr)
        n_scr[...] = jnp.zeros_like(n_scr)
        m_scr[...] = jnp.zeros_like(m_scr)

    for nb in range(ML_NB):
        _mlstm_rows(qk_ref.at[nb], tail_ref.at[nb], v_ref.at[nb], og_ref.at[nb], g_ref.at[nb], cw_ref, cb_ref, gb_ref, gn_ref,
                    y_ref.at[nb], ct_scr.at[nb], n_scr.at[nb], m_scr.at[nb], q_scr.at[nb], k_scr.at[nb])


def _mlstm_rows(qk_ref, tail_ref, v_ref, og_ref, g_ref, cw_ref, cb_ref, gb_ref, gn_ref, y_ref,
                ct_scr, n_scr, m_scr, q_scr, k_scr):
    tt = ML_TT
    step = pl.program_id(1)

    x = qk_ref[...]
    tail = jnp.where(step == 0, 0.0, tail_ref[...])
    row8 = lax.broadcasted_iota(jnp.int32, (8, 1), 0)
    pre = x * cw_ref[CONV_W - 1:CONV_W, :] + cb_ref[...]
    for s in range(1, CONV_W):
        rolled = pltpu.roll(x, s, 0)
        head = jnp.where(row8 < s, pltpu.roll(tail, s, 0), rolled[0:8])
        pre = pre + jnp.concatenate([head, rolled[8:]], axis=0) * cw_ref[CONV_W - 1 - s:CONV_W - s, :]
    act = pre * jax.nn.sigmoid(pre)
    q_scr[...] = act[:, 0:ML_D].astype(BF16)
    k_scr[...] = (act[:, ML_D:2 * ML_D] * (DH_B ** -0.5)).astype(BF16)

    lane = lax.broadcasted_iota(jnp.int32, (1, LANE), 1)
    tok = lane % CHUNK
    jj = lax.broadcasted_iota(jnp.int32, (CHUNK, CHUNK), 0)
    ss = lax.broadcasted_iota(jnp.int32, (CHUNK, CHUNK), 1)
    nt = (((1,), (1,)), ((), ()))
    tn = (((0,), (0,)), ((), ()))
    ct_state = [ct_scr[h] for h in range(H_B)]
    n_state = [n_scr[h:h + 1, :] for h in range(H_B)]
    m_state = [m_scr[h:h + 1, 0:1] for h in range(H_B)]
    for sub in range(tt // ML_SUB):
        r0 = sub * ML_SUB
        gp = g_ref[r0:r0 + ML_SUB, :] + gb_ref[...]
        lsig = jnp.minimum(gp, 0.0) - jnp.log(1.0 + jnp.exp(-jnp.abs(gp)))
        col = jnp.where(lane < H_B, gp, lsig)
        rowl = col.T
        b = rowl[0:8]
        for sft in (1, 2, 4, 8, 16, 32):
            b = b + jnp.where(tok >= sft, pltpu.roll(b, sft, 1), 0.0)
        bcol = jnp.concatenate([b, jnp.zeros((LANE - 8, ML_SUB), F32)], axis=0).T
        for ci in range(ML_SUB // CHUNK):
            c0 = ci * CHUNK
            rows = slice(r0 + c0, r0 + c0 + CHUNK)
            for h in range(H_B):
                hs = slice(h * DH_B, (h + 1) * DH_B)
                b_col = bcol[c0:c0 + CHUNK, H_B + h:H_B + h + 1]
                ig_col = col[c0:c0 + CHUNK, h:h + 1]
                b_row = b[H_B + h:H_B + h + 1, c0:c0 + CHUNK]
                ig_row = rowl[h:h + 1, c0:c0 + CHUNK]
                qh, kh, vh = q_scr[rows, hs], k_scr[rows, hs], v_ref[rows, hs]
                log_d = jnp.where(jj >= ss, b_col - b_row + ig_row, NEG)
                m_loc = jnp.max(log_d, -1, keepdims=True)
                s_loc = lax.dot_general(qh, kh, nt, preferred_element_type=F32) * jnp.exp(log_d - m_loc)
                sv_loc = jnp.dot(s_loc.astype(BF16), vh, preferred_element_type=F32)
                rs_loc = jnp.sum(s_loc, -1, keepdims=True)
                b_last = b_row[:, CHUNK - 1:CHUNK]
                w_max = jnp.max(b_last - b_row + ig_row, -1, keepdims=True)
                w_loc = jnp.exp(b_last - b_col + ig_col - w_max)
                inc_c = lax.dot_general(kh, (w_loc * vh.astype(F32)).astype(BF16), tn, preferred_element_type=F32)
                inc_n = jnp.sum(w_loc * kh.astype(F32), axis=0, keepdims=True)
                m_old, ct, n_row = m_state[h], ct_state[h], n_state[h]
                log_inter = b_col + m_old
                m_out = jnp.maximum(log_inter, m_loc)
                w_inter = jnp.exp(log_inter - m_out)
                w_intra = jnp.exp(m_loc - m_out)
                num = w_inter * jnp.dot(qh, ct.astype(BF16), preferred_element_type=F32) + w_intra * sv_loc
                den = w_inter * jnp.sum(qh.astype(F32) * n_row, -1, keepdims=True) + w_intra * rs_loc
                hid = num / jnp.maximum(jnp.abs(den), jnp.exp(-m_out))
                m_new = jnp.maximum(b_last + m_old, w_max)
                decay = jnp.exp(b_last + m_old - m_new)
                grow = jnp.exp(w_max - m_new)
                ct_state[h] = decay * ct + grow * inc_c
                n_state[h] = decay * n_row + grow * inc_n
                m_state[h] = m_new
                ms = jnp.mean(hid * hid, -1, keepdims=True)
                y = hid * lax.rsqrt(ms + 1e-6) * gn_ref[:, hs] * jax.nn.sigmoid(og_ref[rows, hs])
                y_ref[rows, hs] = y.astype(y_ref.dtype)
    for h in range(H_B):
        ct_scr[h] = ct_state[h]
        n_scr[h:h + 1, :] = n_state[h]
        m_scr[h:h + 1, :] = jnp.broadcast_to(m_state[h], (1, LANE))


def _mlstm(mqk, mv, mo, mg, conv_w, conv_b, gate_b, g_b):
    bsz, seq, _ = mqk.shape
    tt = ML_TT
    gb = jnp.pad(gate_b.reshape(1, 2 * H_B), ((0, 0), (0, LANE - 2 * H_B)))
    nb = ML_NB
    row = lambda n: pl.BlockSpec((nb, tt, n), lambda b, i: (b, i, 0))
    full = lambda r, c: pl.BlockSpec((r, c), lambda b, i: (0, 0))
    tail = pl.BlockSpec((nb, 8, 2 * ML_D), lambda b, i: (b, jnp.maximum(i * (tt // 8) - 1, 0), 0))
    return pl.pallas_call(
        _mlstm_kernel,
        grid=(bsz // nb, seq // tt),
        in_specs=[row(2 * ML_D), tail, row(ML_D), row(ML_D), row(LANE),
                  full(CONV_W, 2 * ML_D), full(1, 2 * ML_D), full(1, LANE), full(1, ML_D)],
        out_specs=row(ML_D),
        out_shape=jax.ShapeDtypeStruct((bsz, seq, ML_D), BF16),
        scratch_shapes=[pltpu.VMEM((nb, H_B, DH_B, DH_B), F32), pltpu.VMEM((nb, 8, DH_B), F32), pltpu.VMEM((nb, 8, LANE), F32),
                        pltpu.VMEM((nb, tt, ML_D), BF16), pltpu.VMEM((nb, tt, ML_D), BF16)],
        compiler_params=_params("arbitrary", "arbitrary"),
        name="mlstm_scan",
    )(mqk, mqk, mv, mo, mg, conv_w, conv_b.reshape(1, 2 * ML_D), gb, g_b.reshape(1, ML_D))


def _mixers(pa, mls, nsa, rope, w_uk, w_uv, kv_norm, conv_w, conv_b, gate_b, cmp_pos, cmp_w1, cmp_w2, grp_norm):
    g_a, g_b, g_c = jnp.split(grp_norm, [H_A * DH_V, H_A * DH_V + H_B * DH_B])
    qc, iq, iw, kc_a, ik, ct_a = _dsa_prep(pa, rope[0], rope[1], w_uk, kv_norm)
    y_a = _dsa_attention(qc, iq, iw, kc_a, ct_a, ik, w_uv, g_a)
    y_b = _mlstm(*mls, conv_w, conv_b, gate_b, g_b)
    nq, ncmp, nks, nvs, nkw, nvw, ng = nsa
    kc, vc = _nsa_compress(ncmp, cmp_pos, cmp_w1, cmp_w2)
    y_c = _nsa_attention(nq, ng, kc, vc, nks, nvs, nkw, nvw, g_c)
    return y_a, y_b, y_c


def _pad_cols(w, n):
    return jnp.pad(w, ((0, 0), (0, n - w.shape[1])))


def kernel(x, c, positions, w_mod, b_mod, w_in, dsa_w_uk, dsa_w_uv, dsa_kv_norm, mlstm_conv_w, mlstm_conv_b, mlstm_gate_b, nsa_cmp_pos, nsa_cmp_w1, nsa_cmp_w2, grp_norm, w_out, ln_g, ln_b, ffn_w_gate, ffn_w_up, ffn_w_down, moe_router, moe_w_gate, moe_w_up, moe_w_down):
    bsz = x.shape[0]
    mod = _modulation(c, w_mod, b_mod).reshape(bsz, DEPTH, N_MOD, 1, D_MODEL)
    rope = _rope_table(positions)
    for l in range(DEPTH):
        sh1, sc1, g1, sh2, sc2, g2 = [mod[:, l, j] for j in range(N_MOD)]
        w = w_in[l].astype(BF16)
        wa = _dsa_weight_layout(w[:, :N_GROUP_A])
        wb = _mlstm_weight_layout(w[:, N_GROUP_A:N_GROUP_A + N_GROUP_B])
        wc, wvt = _nsa_weight_layout(w[:, N_GROUP_A + N_GROUP_B:])
        pa, mqk, mv, mo, mg, *nsa = _input_projection(x, sc1, sh1, wa, wb, wc, wvt)
        ya, yb, yc = _mixers(pa, (mqk, mv, mo, mg), nsa, rope, dsa_w_uk[l], dsa_w_uv[l], dsa_kv_norm[l], mlstm_conv_w[l], mlstm_conv_b[l], mlstm_gate_b[l], nsa_cmp_pos[l], nsa_cmp_w1[l], nsa_cmp_w2[l], grp_norm[l])
        wo = w_out[l].astype(BF16)
        n_a, n_b = H_A * DH_V, H_B * DH_B
        x = _output_projection(ya, yb, yc, x, g1, _pad_head_rows(wo[:n_a], H_A, DH_V), wo[n_a:n_a + n_b], _pad_head_rows(wo[n_a + n_b:], H_C, DH_C), ln_g[l, 0], ln_b[l, 0])
        if l % 2 == 0:
            k = l // 2
            x = _dense_ffn(x, sc2, sh2, g2, ffn_w_gate[k].astype(BF16), ffn_w_up[k].astype(BF16), ffn_w_down[k].astype(BF16), ln_g[l, 1], ln_b[l, 1])
        else:
            k = l // 2
            routing = _moe_router(x, sc2, sh2, moe_router[k])
            x = _moe_ffn(x, sc2, sh2, g2, routing,moe_w_gate[k].astype(BF16), moe_w_up[k].astype(BF16), moe_w_down[k].astype(BF16), ln_g[l, 1], ln_b[l, 1])
    return x
```

```python
import functools

import numpy as np
import jax
import jax.numpy as jnp
from jax import lax
from jax.experimental import pallas as pl
from jax.experimental.pallas import tpu as pltpu

F32 = jnp.float32
BF16 = jnp.bfloat16

D_MODEL = 1024
DEPTH = 2
H_A, DH_NOPE, DH_ROPE, D_C, DH_V, H_I, D_I = 4, 64, 32, 128, 64, 4, 64
K_SEL_MAX = 256
ROPE_THETA = 10000.0
H_B, DH_B, CONV_W, CHUNK = 4, 128, 4, 64
H_C, DH_C, L_CMP, D_STRIDE, CMP_HIDDEN, L_SEL, N_TOP_MAX, WINDOW, Q_BLOCK = 4, 64, 32, 16, 128, 64, 16, 512, 128
D_FF = 2816
N_EXPERTS = 8
D_FF_EXPERT = 3584
N_MOD = 6
ALPHA = (2 * DEPTH) ** 0.25
SPLIT_SIZES = (H_A * DH_NOPE, H_A * DH_ROPE, D_C, DH_ROPE, H_I * D_I, D_I, H_I, H_B * DH_B, H_B * DH_B, H_B * DH_B, H_B, H_B, H_B * DH_B, H_C * DH_C, DH_C, DH_C, DH_C, DH_C, DH_C, DH_C, 3 * H_C)
N_GROUP_A = sum(SPLIT_SIZES[:7])
N_GROUP_B = sum(SPLIT_SIZES[7:13])
N_GROUP_C = sum(SPLIT_SIZES[13:])

LANE = 128
VMEM_LIMIT = 56 * 1024 * 1024


def _round_up(n, m):
    return (n + m - 1) // m * m


def _params(*sem):
    return pltpu.CompilerParams(dimension_semantics=sem, vmem_limit_bytes=VMEM_LIMIT)


FOLD_ROWS = 64


def _fold_rows(x, op):
    parts = [x[i:i + FOLD_ROWS] for i in range(0, x.shape[0], FOLD_ROWS)]
    while len(parts) > 1:
        parts = [op(parts[i], parts[i + 1]) if i + 1 < len(parts) else parts[i] for i in range(0, len(parts), 2)]
    return parts[0]


def _attend_block(s_buf, b_buf, v_t, m, l, acc_scr, n_head, tq):
    bias = b_buf[...]
    m_new, l_new, ps = [], [], []
    for h in range(n_head):
        hs = slice(h * tq, (h + 1) * tq)
        sh = s_buf[:, hs] + bias
        mh = jnp.maximum(m[:, hs], jnp.max(_fold_rows(sh, jnp.maximum), axis=0, keepdims=True))
        ph = jnp.exp(sh - mh)
        l_new.append(jnp.exp(m[:, hs] - mh) * l[:, hs] + jnp.sum(_fold_rows(ph, jnp.add), axis=0, keepdims=True))
        m_new.append(mh)
        ps.append(ph.astype(BF16))
    m_new = jnp.concatenate(m_new, axis=1)
    acc_scr[...] = jnp.exp(m - m_new) * acc_scr[...] + jnp.dot(v_t, jnp.concatenate(ps, axis=1), preferred_element_type=F32)
    return m_new, jnp.concatenate(l_new, axis=1)


def _layer_norm_rows(z, g, b):
    mu = jnp.mean(z, -1, keepdims=True)
    zc = z - mu
    var = jnp.mean(zc * zc, -1, keepdims=True)
    return zc * lax.rsqrt(var + 1e-5) * g + b


def _mod_kernel(c_ref, w_ref, b_ref, o_ref):
    c = c_ref[...]
    a = c * jax.nn.sigmoid(c)
    o_ref[...] = jnp.dot(a, w_ref[...], preferred_element_type=F32, precision=lax.Precision.HIGHEST) + b_ref[...]


def _modulation(c, w_mod, b_mod):
    bsz, d = c.shape
    n = w_mod.shape[1]
    tn = 1024
    return pl.pallas_call(
        _mod_kernel,
        grid=(n // tn,),
        in_specs=[pl.BlockSpec((bsz, d), lambda j: (0, 0)),
                  pl.BlockSpec((d, tn), lambda j: (0, j)),
                  pl.BlockSpec((1, tn), lambda j: (0, j))],
        out_specs=pl.BlockSpec((bsz, tn), lambda j: (0, j)),
        out_shape=jax.ShapeDtypeStruct((bsz, n), F32),
        compiler_params=_params("arbitrary"),
        name="adaln_mod",
    )(c, w_mod, b_mod.reshape(1, n))


NSA_Q0 = 0
NSA_CMP0 = H_C * LANE
NSA_KS0 = NSA_CMP0 + LANE
NSA_KW0 = NSA_KS0 + LANE
NSA_G0 = NSA_KW0 + LANE
NSA_COLS = NSA_G0 + LANE


def _nsa_weight_layout(w):
    d = w.shape[0]
    nq, nkc, nvc, nks, nvs, nkw, nvw, ng = jnp.split(w, np.cumsum(SPLIT_SIZES[13:])[:-1].tolist(), axis=1)
    z = lambda n: jnp.zeros((d, n), w.dtype)
    half = LANE - DH_C
    cols = []
    for h in range(H_C):
        cols += [nq[:, h * DH_C:(h + 1) * DH_C], z(half)]
    cols += [nkc, nvc, nks, z(half), nkw, z(half), ng, z(LANE - 3 * H_C)]
    values_t = jnp.concatenate([nvs, z(half), nvw, z(half)], axis=1).T
    return jnp.concatenate(cols, axis=1), values_t


def _mlstm_weight_layout(w):
    mq, mk, mv, mi, mf, mo = jnp.split(w, np.cumsum(SPLIT_SIZES[7:13])[:-1].tolist(), axis=1)
    return jnp.concatenate([mq, mk, mv, mo, mi, mf, jnp.zeros((w.shape[0], LANE - 2 * H_B), w.dtype)], axis=1)


ML_D = H_B * DH_B


def _inproj_kernel(x_ref, sc_ref, sh_ref, wa_ref, wb_ref, wc_ref, wvt_ref, oa_ref, mqk_ref, mv_ref, mo_ref, mg_ref,
                   nq_ref, ncmp_ref, nks_ref, nvs_ref, nkw_ref, nvw_ref, ng_ref):
    u = (x_ref[...] * (1.0 + sc_ref[...]) + sh_ref[...]).astype(BF16)
    oa_ref[...] = jnp.dot(u, wa_ref[...], preferred_element_type=F32)
    ob = jnp.dot(u, wb_ref[...], preferred_element_type=F32)
    mqk_ref[...] = ob[:, 0:2 * ML_D]
    mv_ref[...] = ob[:, 2 * ML_D:3 * ML_D].astype(BF16)
    mo_ref[...] = ob[:, 3 * ML_D:4 * ML_D]
    mg_ref[...] = ob[:, 4 * ML_D:4 * ML_D + LANE]
    oc = jnp.dot(u, wc_ref[...], preferred_element_type=F32)
    nq_ref[...] = oc[:, NSA_Q0:NSA_Q0 + H_C * LANE].astype(BF16)
    ncmp_ref[...] = oc[:, NSA_CMP0:NSA_CMP0 + LANE]
    nks_ref[...] = oc[:, NSA_KS0:NSA_KS0 + LANE].astype(BF16)
    nkw_ref[...] = oc[:, NSA_KW0:NSA_KW0 + LANE].astype(BF16)
    ng_ref[...] = oc[:, NSA_G0:NSA_G0 + LANE]
    vt = lax.dot_general(wvt_ref[...], u, (((1,), (1,)), ((), ())), preferred_element_type=F32)
    nvs_ref[...] = vt[0:LANE].astype(BF16)
    nvw_ref[...] = vt[LANE:2 * LANE].astype(BF16)


def _input_projection(x, sc, sh, wa, wb, wc, wvt):
    bsz, seq, d = x.shape
    tm = 512
    na, nb, nc = wa.shape[1], wb.shape[1], wc.shape[1]
    row = lambda n: pl.BlockSpec((None, tm, n), lambda b, i: (b, i, 0))
    col = pl.BlockSpec((None, LANE, tm), lambda b, i: (b, 0, i))
    vec = pl.BlockSpec((None, 1, d), lambda b, i: (b, 0, 0))
    full = lambda n: pl.BlockSpec((d, n), lambda b, i: (0, 0))
    tok = lambda n, dt: (row(n), jax.ShapeDtypeStruct((bsz, seq, n), dt))
    feat = (col, jax.ShapeDtypeStruct((bsz, LANE, seq), BF16))
    outs = [tok(na, F32), tok(2 * ML_D, F32), tok(ML_D, BF16), tok(ML_D, F32), tok(LANE, F32),
            tok(H_C * LANE, BF16), tok(LANE, F32), tok(LANE, BF16), feat, tok(LANE, BF16), feat, tok(LANE, F32)]
    return pl.pallas_call(
        _inproj_kernel,
        grid=(bsz, seq // tm),
        in_specs=[row(d), vec, vec, full(na), full(nb), full(nc), pl.BlockSpec((2 * LANE, d), lambda b, i: (0, 0))],
        out_specs=[spec for spec, _ in outs],
        out_shape=[shape for _, shape in outs],
        compiler_params=_params("arbitrary", "arbitrary"),
        name="input_projection",
    )(x, sc, sh, wa, wb, wc, wvt)


DSA_QN0 = 0
DSA_QR0 = H_A * LANE
DSA_CKV0 = DSA_QR0 + LANE
DSA_IQ0 = DSA_CKV0 + LANE
DSA_G0 = DSA_IQ0 + H_I * LANE
DSA_COLS = DSA_G0 + LANE
DSA_KR_LANE = D_I
DSA_IW_LANE = D_I + DH_ROPE
DSA_TQ = 128
DSA_KC = 512
I16 = jnp.int16
I16_MIN = -2 ** 15
KEY_NEG_INF = int(np.int32(np.float32(-np.inf).view(np.int32)) ^ np.int32(0x7FFFFFFF))


def _dsa_weight_layout(w):
    d = w.shape[0]
    qn, qr, ckv, kr, iq, ik, iw = jnp.split(w, np.cumsum(SPLIT_SIZES[:7])[:-1].tolist(), axis=1)
    z = lambda n: jnp.zeros((d, n), w.dtype)
    cols = []
    for h in range(H_A):
        cols += [qn[:, h * DH_NOPE:(h + 1) * DH_NOPE], z(LANE - DH_NOPE)]
    cols += [qr, ckv]
    for h in range(H_I):
        cols += [iq[:, h * D_I:(h + 1) * D_I], z(LANE - D_I)]
    cols += [ik, kr, iw, z(LANE - D_I - DH_ROPE - H_I)]
    return jnp.concatenate(cols, axis=1)


def _rope_table_kernel(pos_ref, freq_ref, cos_ref, sin_ref):
    ang = pos_ref[...].astype(F32) * freq_ref[...]
    lane = lax.broadcasted_iota(jnp.int32, (1, LANE), 1)
    first = (lane % DH_ROPE) < DH_ROPE // 2
    cos_ref[...] = jnp.cos(ang)
    sin_ref[...] = jnp.where(first, -jnp.sin(ang), jnp.sin(ang))


def _rope_table(positions):
    bsz, seq = positions.shape
    tm = 512
    inv_freq = ROPE_THETA ** (-jnp.arange(0, DH_ROPE, 2, dtype=F32) / DH_ROPE)
    freq = jnp.tile(inv_freq, LANE // (DH_ROPE // 2)).reshape(1, LANE)
    out = pl.BlockSpec((None, tm, LANE), lambda b, i: (b, i, 0))
    return pl.pallas_call(
        _rope_table_kernel,
        grid=(bsz, seq // tm),
        in_specs=[pl.BlockSpec((None, tm, 1), lambda b, i: (b, i, 0)), pl.BlockSpec((1, LANE), lambda b, i: (0, 0))],
        out_specs=[out, out],
        out_shape=[jax.ShapeDtypeStruct((bsz, seq, LANE), F32)] * 2,
        compiler_params=_params("arbitrary", "arbitrary"),
        name="rope_table",
    )(positions.reshape(bsz, seq, 1), freq)


def _dsa_prep_kernel(pa_ref, cos_ref, sin_ref, wuk_ref, kvn_ref, qc_ref, iq_ref, iw_ref, kc_ref, ik_ref, ct_ref):
    scale = (DH_NOPE + DH_ROPE) ** -0.5
    cos, sin = cos_ref[...], sin_ref[...]
    lane = lax.broadcasted_iota(jnp.int32, (1, LANE), 1)
    first = (lane % DH_ROPE) < DH_ROPE // 2
    rope_lanes = (lane >= DSA_KR_LANE) & (lane < DSA_KR_LANE + DH_ROPE)

    def rope(v):
        partner = jnp.where(first, pltpu.roll(v, LANE - DH_ROPE // 2, 1), pltpu.roll(v, DH_ROPE // 2, 1))
        return v * cos + partner * sin

    g = pa_ref[:, DSA_G0:DSA_G0 + LANE]
    ckv = pa_ref[:, DSA_CKV0:DSA_CKV0 + LANE]
    ckv_n = ckv * lax.rsqrt(jnp.mean(ckv * ckv, -1, keepdims=True) + 1e-6) * kvn_ref[...]
    kc_ref[:, 0:LANE] = ckv_n.astype(BF16)
    ct_ref[...] = ckv_n.T.astype(BF16)
    kc_ref[:, LANE:2 * LANE] = jnp.where(rope_lanes, rope(g), 0.0).astype(BF16)
    ik_ref[...] = jnp.where(lane < D_I, g, 0.0).astype(BF16)
    iw_ref[...] = g
    iq_ref[...] = pa_ref[:, DSA_IQ0:DSA_IQ0 + H_I * LANE].astype(BF16)
    qr = rope(pa_ref[:, DSA_QR0:DSA_QR0 + LANE]) * scale
    for h in range(H_A):
        qn = pa_ref[:, DSA_QN0 + h * LANE:DSA_QN0 + (h + 1) * LANE].astype(BF16)
        q_abs = jnp.dot(qn, wuk_ref[h], preferred_element_type=F32) * scale
        shift = (DSA_KR_LANE - DH_ROPE * h) % LANE
        qr_h = pltpu.roll(qr, shift, 1) if shift else qr
        qc_ref[:, 2 * h * LANE:(2 * h + 1) * LANE] = q_abs.astype(BF16)
        qc_ref[:, (2 * h + 1) * LANE:(2 * h + 2) * LANE] = jnp.where(rope_lanes, qr_h, 0.0).astype(BF16)


def _dsa_prep(pa, cos, sin, w_uk, kv_norm):
    bsz, seq, _ = pa.shape
    tm = 512
    wuk = jnp.pad(w_uk, ((0, 0), (0, LANE - DH_NOPE), (0, 0))).astype(BF16)
    row = lambda n: pl.BlockSpec((None, tm, n), lambda b, i: (b, i, 0))
    outs = [(2 * H_A * LANE, BF16), (H_I * LANE, BF16), (LANE, F32), (2 * LANE, BF16), (LANE, BF16)]
    return pl.pallas_call(
        _dsa_prep_kernel,
        grid=(bsz, seq // tm),
        in_specs=[row(DSA_COLS), row(LANE), row(LANE),
                  pl.BlockSpec((H_A, LANE, D_C), lambda b, i: (0, 0, 0)), pl.BlockSpec((1, D_C), lambda b, i: (0, 0))],
        out_specs=[row(n) for n, _ in outs] + [pl.BlockSpec((None, D_C, tm), lambda b, i: (b, 0, i))],
        out_shape=[jax.ShapeDtypeStruct((bsz, seq, n), dt) for n, dt in outs] + [jax.ShapeDtypeStruct((bsz, D_C, seq), BF16)],
        compiler_params=_params("arbitrary", "arbitrary"),
        name="dsa_prep",
    )(pa, cos, sin, wuk, kv_norm.reshape(1, D_C))


def _dsa_kernel(k_sel, qc_ref, iq_ref, iw_ref, kc_ref, ct_ref, ik_ref, tri_ref, wuv_ref, gn_ref, o_ref,
                key_scr, hi_scr, lo_scr, acc_scr, s0_scr, s1_scr, b0_scr, b1_scr):
    tq, kcs = DSA_TQ, DSA_KC
    t0 = pl.program_id(1) * tq
    n_chunk = (t0 + tq + kcs - 1) // kcs
    nt = (((1,), (1,)), ((), ()))
    qpos = t0 + lax.broadcasted_iota(jnp.int32, (1, tq), 1)
    iw_t = iw_ref[...].T

    def score_chunk(c, carry):
        k0 = pl.multiple_of(c * kcs, kcs)
        ikc = ik_ref[pl.ds(k0, kcs), :]
        sc = jnp.zeros((kcs, tq), F32)
        for h in range(H_I):
            lg = lax.dot_general(ikc, iq_ref[:, h * LANE:(h + 1) * LANE], nt, preferred_element_type=F32)
            sc = sc + jnp.maximum(lg, 0.0) * iw_t[DSA_IW_LANE + h:DSA_IW_LANE + h + 1, :]
        sc = jnp.where(sc == 0.0, 0.0, sc)
        kpos = k0 + lax.broadcasted_iota(jnp.int32, (kcs, 1), 0)
        sc = jnp.where(kpos <= qpos, sc, -jnp.inf)
        bits = pltpu.bitcast(sc, jnp.int32)
        key = jnp.where(bits < 0, bits ^ 0x7FFFFFFF, bits)
        key_scr[pl.ds(k0, kcs), :] = key
        hi_scr[pl.ds(k0, kcs), :] = (key >> 16).astype(I16)
        lo_scr[pl.ds(k0, kcs), :] = ((key & 0xFFFF) + I16_MIN).astype(I16)
        return carry

    lax.fori_loop(0, n_chunk, score_chunk, 0)

    def count(ref, pred, one, zero):
        def body(c, acc):
            k0 = pl.multiple_of(c * kcs, kcs)
            return acc + _fold_rows(jnp.where(pred(ref[pl.ds(k0, kcs), :]), one, zero), jnp.add)
        acc = lax.fori_loop(0, n_chunk, body, jnp.zeros((FOLD_ROWS, tq), one.dtype))
        return jnp.sum(acc.astype(F32), axis=0, keepdims=True)

    def search16(ref, base):
        count16 = lambda pred: count(ref, pred, I16(1), I16(0))
        t = jnp.where(base + count16(lambda v: v >= 0) >= k_sel, jnp.int32(0), jnp.int32(I16_MIN))

        def bit_step(i, t):
            cand = t | jnp.left_shift(jnp.int32(1), 14 - i)
            cand16 = cand.astype(I16)
            return jnp.where(base + count16(lambda v: v >= cand16) >= k_sel, cand, t)

        return lax.fori_loop(0, 15, bit_step, t)

    t_hi = search16(hi_scr, 0.0)
    t_hi16 = t_hi.astype(I16)
    above = count(hi_scr, lambda v: v > t_hi16, I16(1), I16(0))

    def keep_matching(c, carry):
        k0 = pl.multiple_of(c * kcs, kcs)
        lo_scr[pl.ds(k0, kcs), :] = jnp.where(hi_scr[pl.ds(k0, kcs), :] == t_hi16, lo_scr[pl.ds(k0, kcs), :], I16(I16_MIN))
        return carry

    lax.fori_loop(0, n_chunk, keep_matching, 0)
    t_lo = search16(lo_scr, above)
    thr = (t_hi << 16) | (t_lo - I16_MIN)
    room = k_sel - count(key_scr, lambda k: k > thr, jnp.float32(1), jnp.float32(0))

    qall = jnp.concatenate([qc_ref[:, 2 * h * LANE:(2 * h + 2) * LANE] for h in range(H_A)], axis=0)

    room = jnp.where(thr > KEY_NEG_INF, room, 0.0)
    acc_scr[...] = jnp.zeros_like(acc_scr)

    half = kcs // 2

    def prepare(k0, seen, s_buf, b_buf):
        key = key_scr[pl.ds(k0, half), :]
        tie = jnp.where(key == thr, 1.0, 0.0)
        prefix = jnp.dot(tri_ref[...], tie.astype(BF16), preferred_element_type=F32)
        tie_bias = jnp.where(seen + prefix <= room, jnp.where(key == thr, 0.0, NEG), NEG)
        b_buf[...] = jnp.where(key > thr, 0.0, tie_bias)
        s_buf[...] = lax.dot_general(kc_ref[pl.ds(k0, half), :], qall, nt, preferred_element_type=F32)
        return seen + jnp.sum(_fold_rows(tie, jnp.add), axis=0, keepdims=True)

    def attn_chunk(c, carry):
        m, l, seen = carry
        k0 = pl.multiple_of(c * kcs, kcs)
        k1 = pl.multiple_of(k0 + half, half)
        seen = prepare(k1, seen, s1_scr, b1_scr)
        m, l = _attend_block(s0_scr, b0_scr, ct_ref[:, pl.ds(k0, half)], m, l, acc_scr, H_A, tq)
        k2 = pl.multiple_of(jnp.minimum(c + 1, n_chunk - 1) * kcs, kcs)
        seen = prepare(k2, seen, s0_scr, b0_scr)
        m, l = _attend_block(s1_scr, b1_scr, ct_ref[:, pl.ds(k1, half)], m, l, acc_scr, H_A, tq)
        return m, l, seen

    seen0 = prepare(0, jnp.zeros((1, tq), F32), s0_scr, b0_scr)
    init = (jnp.full((1, H_A * tq), M_FLOOR, F32), jnp.zeros((1, H_A * tq), F32), seen0)
    _, l, _ = lax.fori_loop(0, n_chunk, attn_chunk, init)
    o_lat = (acc_scr[...] / jnp.maximum(l, 1e-30)).astype(BF16)
    for h in range(H_A):
        o = lax.dot_general(o_lat[:, h * tq:(h + 1) * tq], wuv_ref[h], (((0,), (0,)), ((), ())),
                            preferred_element_type=F32)
        ms = jnp.sum(o * o, -1, keepdims=True) * (1.0 / DH_V)
        o_ref[:, h * LANE:(h + 1) * LANE] = (o * lax.rsqrt(ms + 1e-6) * gn_ref[h:h + 1, :]).astype(o_ref.dtype)


def _dsa_attention(qc, iq, iw, kc, ct, ik, w_uv, g_a):
    bsz, seq, _ = qc.shape
    k_sel = float(min(K_SEL_MAX, seq // 4))
    tq, kcs = DSA_TQ, DSA_KC
    half = kcs // 2
    tri = jnp.asarray(np.tril(np.ones((half, half), np.float32)), BF16)
    wuv = jnp.pad(w_uv, ((0, 0), (0, 0), (0, LANE - DH_V))).astype(BF16)
    gn = jnp.pad(g_a.reshape(H_A, DH_V), ((0, 0), (0, LANE - DH_V)))
    row = lambda n: pl.BlockSpec((None, tq, n), lambda b, i: (b, i, 0))
    per_b = lambda n: pl.BlockSpec((None, seq, n), lambda b, i: (b, 0, 0))
    return pl.pallas_call(
        functools.partial(_dsa_kernel, k_sel),
        grid=(bsz, seq // tq),
        in_specs=[row(2 * H_A * LANE), row(H_I * LANE), row(LANE), per_b(2 * LANE),
                  pl.BlockSpec((None, D_C, seq), lambda b, i: (b, 0, 0)), per_b(LANE),
                  pl.BlockSpec((half, half), lambda b, i: (0, 0)), pl.BlockSpec((H_A, D_C, LANE), lambda b, i: (0, 0, 0)),
                  pl.BlockSpec((H_A, LANE), lambda b, i: (0, 0))],
        out_specs=row(H_A * LANE),
        out_shape=jax.ShapeDtypeStruct((bsz, seq, H_A * LANE), BF16),
        scratch_shapes=[pltpu.VMEM((seq, tq), jnp.int32), pltpu.VMEM((seq, tq), I16), pltpu.VMEM((seq, tq), I16),
                        pltpu.VMEM((D_C, H_A * tq), F32),
                        pltpu.VMEM((half, H_A * tq), F32), pltpu.VMEM((half, H_A * tq), F32),
                        pltpu.VMEM((half, tq), F32), pltpu.VMEM((half, tq), F32)],
        compiler_params=_params("arbitrary", "arbitrary"),
        name="dsa_attention",
    )(qc, iq, iw, kc, ct, ik, tri, wuv, gn)


def _nsa_compress_kernel(a_ref, pos_ref, w1t_ref, w1b_ref, w2k_ref, w2v_ref, kc_ref, vc_ref):
    a = a_ref[...]
    top = jnp.dot((a + pos_ref[0:1, :]).astype(BF16), w1t_ref[...], preferred_element_type=F32)
    bot = jnp.dot((a + pos_ref[1:2, :]).astype(BF16), w1b_ref[...], preferred_element_type=F32)
    n = a.shape[0]
    pre = top + jnp.concatenate([bot[1:], bot[:1]], axis=0)
    h = (pre * jax.nn.sigmoid(pre)).astype(BF16)
    hid = w2k_ref.shape[0]
    kc_ref[...] = jnp.dot(h[:, :hid], w2k_ref[...], preferred_element_type=F32).astype(BF16)
    vc_ref[...] = lax.dot_general(w2v_ref[...], h[:, hid:], (((1,), (1,)), ((), ())), preferred_element_type=F32).astype(BF16)


def _nsa_compress(ncmp, cmp_pos, cmp_w1, cmp_w2):
    bsz, seq, _ = ncmp.shape
    n_grp = seq // D_STRIDE
    per = L_CMP // D_STRIDE
    width = D_STRIDE * LANE
    a = ncmp.reshape(bsz, n_grp, width)
    w1 = cmp_w1.reshape(2, per, D_STRIDE, DH_C, CMP_HIDDEN)
    zer = jnp.zeros((D_STRIDE, DH_C, CMP_HIDDEN), cmp_w1.dtype)

    def expand(p):
        wk = jnp.concatenate([w1[0, p], zer], axis=1)
        wv = jnp.concatenate([zer, w1[1, p]], axis=1)
        return jnp.concatenate([wk, wv], axis=2).reshape(width, 2 * CMP_HIDDEN).astype(BF16)

    pos = cmp_pos.reshape(2, per, D_STRIDE, DH_C)
    pos = jnp.concatenate([pos[0], pos[1]], axis=-1).reshape(per, width)
    pad_out = ((0, 0), (0, LANE - DH_C))
    w2k = jnp.pad(cmp_w2[0], pad_out).astype(BF16)
    w2v = jnp.pad(cmp_w2[1], pad_out).astype(BF16).T
    full = lambda r, c: pl.BlockSpec((r, c), lambda b: (0, 0))
    return pl.pallas_call(
        _nsa_compress_kernel,
        grid=(bsz,),
        in_specs=[pl.BlockSpec((None, n_grp, width), lambda b: (b, 0, 0)), full(per, width),
                  full(width, 2 * CMP_HIDDEN), full(width, 2 * CMP_HIDDEN), full(CMP_HIDDEN, LANE), full(LANE, CMP_HIDDEN)],
        out_specs=[pl.BlockSpec((None, n_grp, LANE), lambda b: (b, 0, 0)), pl.BlockSpec((None, LANE, n_grp), lambda b: (b, 0, 0))],
        out_shape=[jax.ShapeDtypeStruct((bsz, n_grp, LANE), BF16), jax.ShapeDtypeStruct((bsz, LANE, n_grp), BF16)],
        compiler_params=_params("arbitrary"),
        name="nsa_compress",
    )(a, pos, expand(0), expand(1), w2k, w2v)


NSA_TQ = 128
NSA_KC = 512
NEG = -1e30
M_FLOOR = -1e20


def _softmax_cols(s, bias, n_head, tq):
    out = []
    for h in range(n_head):
        sh = s[:, h * tq:(h + 1) * tq] + bias
        m = jnp.maximum(jnp.max(_fold_rows(sh, jnp.maximum), axis=0, keepdims=True), M_FLOOR)
        e = jnp.exp(sh - m)
        den = jnp.sum(_fold_rows(e, jnp.add), axis=0, keepdims=True)
        out.append(e * (1.0 / jnp.maximum(den, 1e-30)))
    return out


def _nsa_kernel(q_ref, g_ref, kc_ref, vct_ref, ks_ref, vst_ref, kw_ref, vwt_ref, cover_ref, expand_ref, gn_ref, o_ref,
                acc_scr, s0_scr, s1_scr, b0_scr, b1_scr):
    tq, kc_sz = NSA_TQ, NSA_KC
    t0 = pl.program_id(1) * tq
    nt = (((1,), (1,)), ((), ()))
    q = q_ref[...] * (DH_C ** -0.5)
    qa = jnp.concatenate([q[:, h * LANE:(h + 1) * LANE] for h in range(H_C)], axis=0)
    qpos = t0 + lax.broadcasted_iota(jnp.int32, (1, tq), 1)

    n_grp = kc_ref.shape[0]
    s_c = lax.dot_general(kc_ref[...], qa, nt, preferred_element_type=F32)
    n_idx = lax.broadcasted_iota(jnp.int32, (n_grp, 1), 0)
    visible = jnp.where(n_idx < n_grp - 1, n_idx * D_STRIDE + (L_CMP - 1), 2 ** 30)
    p_c = _softmax_cols(s_c, jnp.where(visible <= qpos, 0.0, NEG), H_C, tq)
    o_cmp = jnp.dot(vct_ref[...], jnp.concatenate([p.astype(BF16) for p in p_c], axis=1), preferred_element_type=F32)

    p_sum = p_c[0]
    for h in range(1, H_C):
        p_sum = p_sum + p_c[h]
    hi = p_sum.astype(BF16)
    lo = (p_sum - hi.astype(F32)).astype(BF16)
    cov = cover_ref[...]
    imp_t = jnp.dot(cov, hi, preferred_element_type=F32) + jnp.dot(cov, lo, preferred_element_type=F32)
    n_sel = cov.shape[0]
    jrow = lax.broadcasted_iota(jnp.int32, (n_sel, tq), 0)
    cur = (t0 + lax.broadcasted_iota(jnp.int32, (n_sel, tq), 1)) // L_SEL
    adm = jrow <= cur
    forced = (jrow == 0) | (jrow == cur) | (jrow == cur - 1)
    val = jnp.where(adm & forced, jnp.inf, jnp.where(adm, imp_t, -jnp.inf))
    rank = jnp.zeros((n_sel, tq), F32)
    for jp in range(n_sel):
        r = val[jp:jp + 1, :]
        rank = rank + jnp.where(r == val, jnp.where(jrow > jp, 1.0, 0.0), jnp.where(r > val, 1.0, 0.0))
    sel_t = jnp.where(rank < min(N_TOP_MAX, n_sel), jnp.where(val > -jnp.inf, 1.0, 0.0), 0.0)
    if n_sel < LANE:
        sel_t = jnp.concatenate([sel_t, jnp.zeros((LANE - n_sel, tq), F32)], axis=0)
    sel_t = sel_t.astype(BF16)

    acc_scr[...] = jnp.zeros_like(acc_scr)

    half = kc_sz // 2
    n_chunk = (t0 + tq + kc_sz - 1) // kc_sz

    def prepare(k0, s_buf, b_buf):
        s_buf[...] = lax.dot_general(ks_ref[pl.ds(k0, half), :], qa, nt, preferred_element_type=F32)
        picked = jnp.dot(expand_ref[pl.ds(k0, half), :], sel_t, preferred_element_type=F32)
        kpos = k0 + lax.broadcasted_iota(jnp.int32, (half, 1), 0)
        b_buf[...] = jnp.where(kpos <= qpos, jnp.where(picked > 0.5, 0.0, NEG), NEG)

    def chunk(c, carry):
        m, l = carry
        k0 = pl.multiple_of(c * kc_sz, kc_sz)
        k1 = pl.multiple_of(k0 + half, half)
        prepare(k1, s1_scr, b1_scr)
        m, l = _attend_block(s0_scr, b0_scr, vst_ref[:, pl.ds(k0, half)], m, l, acc_scr, H_C, tq)
        prepare(pl.multiple_of(jnp.minimum(c + 1, n_chunk - 1) * kc_sz, kc_sz), s0_scr, b0_scr)
        return _attend_block(s1_scr, b1_scr, vst_ref[:, pl.ds(k1, half)], m, l, acc_scr, H_C, tq)

    prepare(0, s0_scr, b0_scr)
    init = (jnp.full((1, H_C * tq), M_FLOOR, F32), jnp.zeros((1, H_C * tq), F32))
    _, l_s = lax.fori_loop(0, n_chunk, chunk, init)
    o_slc = acc_scr[...] * (1.0 / jnp.maximum(l_s, 1e-30))

    span = WINDOW + tq
    w0 = pl.multiple_of(jnp.maximum(t0 - WINDOW, 0), tq)
    s_w = lax.dot_general(kw_ref[pl.ds(w0, span), :], qa, nt, preferred_element_type=F32)
    kpos_w = w0 + lax.broadcasted_iota(jnp.int32, (span, 1), 0)
    wbias = jnp.where(kpos_w <= qpos, jnp.where(qpos - kpos_w < WINDOW, 0.0, NEG), NEG)
    p_w = _softmax_cols(s_w, wbias, H_C, tq)
    o_win = jnp.dot(vwt_ref[:, pl.ds(w0, span)], jnp.concatenate([p.astype(BF16) for p in p_w], axis=1), preferred_element_type=F32)

    gates_t = jax.nn.sigmoid(g_ref[...]).T
    for h in range(H_C):
        hs = slice(h * tq, (h + 1) * tq)
        o_t = (gates_t[h:h + 1] * o_cmp[:, hs] + gates_t[H_C + h:H_C + h + 1] * o_slc[:, hs]
               + gates_t[2 * H_C + h:2 * H_C + h + 1] * o_win[:, hs])
        o = o_t.T
        ms = jnp.sum(o * o, -1, keepdims=True) * (1.0 / DH_C)
        o_ref[:, h * LANE:(h + 1) * LANE] = (o * lax.rsqrt(ms + 1e-6) * gn_ref[h:h + 1, :]).astype(o_ref.dtype)


def _nsa_attention(nq, ng, kc, vc_t, nks, nvs_t, nkw, nvw_t, g_c):
    bsz, seq, _ = nq.shape
    n_grp = kc.shape[1]
    n_sel = seq // L_SEL
    grp_start = np.arange(n_grp) * D_STRIDE
    sel_start = np.arange(n_sel) * L_SEL
    cover_t = ((grp_start[None, :] < sel_start[:, None] + L_SEL) & (grp_start[None, :] + L_CMP > sel_start[:, None]))
    cover_t = jnp.asarray(cover_t.astype(np.float32), BF16)
    expand = (np.arange(seq)[:, None] // L_SEL == np.arange(LANE)[None, :]).astype(np.float32)
    expand = jnp.asarray(expand, BF16)
    gn = jnp.pad(g_c.reshape(H_C, DH_C), ((0, 0), (0, LANE - DH_C)))
    tq = NSA_TQ
    row = lambda n: pl.BlockSpec((None, tq, n), lambda b, i: (b, i, 0))
    per_b = lambda r, c: pl.BlockSpec((None, r, c), lambda b, i: (b, 0, 0))
    full = lambda r, c: pl.BlockSpec((r, c), lambda b, i: (0, 0))
    return pl.pallas_call(
        _nsa_kernel,
        grid=(bsz, seq // tq),
        in_specs=[row(H_C * LANE), row(LANE), per_b(n_grp, LANE), per_b(LANE, n_grp), per_b(seq, LANE), per_b(LANE, seq),
                  per_b(seq, LANE), per_b(LANE, seq), full(n_sel, n_grp), full(seq, LANE), full(H_C, LANE)],
        out_specs=row(H_C * LANE),
        out_shape=jax.ShapeDtypeStruct((bsz, seq, H_C * LANE), BF16),
        scratch_shapes=[pltpu.VMEM((LANE, H_C * tq), F32),
                        pltpu.VMEM((NSA_KC // 2, H_C * tq), F32), pltpu.VMEM((NSA_KC // 2, H_C * tq), F32),
                        pltpu.VMEM((NSA_KC // 2, tq), F32), pltpu.VMEM((NSA_KC // 2, tq), F32)],
        compiler_params=_params("arbitrary", "arbitrary"),
        name="nsa_attention",
    )(nq, ng, kc, vc_t, nks, nvs_t, nkw, nvw_t, cover_t, expand, gn)


def _outproj_kernel(ya_ref, yb_ref, yc_ref, x_ref, g_ref, wa_ref, wb_ref, wc_ref, lng_ref, lnb_ref, o_ref):
    y = jnp.dot(ya_ref[...], wa_ref[...], preferred_element_type=F32)
    y += jnp.dot(yb_ref[...], wb_ref[...], preferred_element_type=F32)
    y += jnp.dot(yc_ref[...], wc_ref[...], preferred_element_type=F32)
    z = ALPHA * x_ref[...] + (1.0 + g_ref[...]) * y
    o_ref[...] = _layer_norm_rows(z, lng_ref[...], lnb_ref[...])


def _pad_head_rows(w, n_head, dh):
    d = w.shape[1]
    return jnp.pad(w.reshape(n_head, dh, d), ((0, 0), (0, LANE - dh), (0, 0))).reshape(n_head * LANE, d)


def _output_projection(ya, yb, yc, x, g, wa, wb, wc, ln_g, ln_b):
    bsz, seq, d = x.shape
    tm = 512
    na, nb, nc = ya.shape[-1], yb.shape[-1], yc.shape[-1]
    row = lambda n: pl.BlockSpec((None, tm, n), lambda b, i: (b, i, 0))
    vec = pl.BlockSpec((None, 1, d), lambda b, i: (b, 0, 0))
    full = lambda r, c: pl.BlockSpec((r, c), lambda b, i: (0, 0))
    return pl.pallas_call(
        _outproj_kernel,
        grid=(bsz, seq // tm),
        in_specs=[row(na), row(nb), row(nc), row(d), vec, full(na, d), full(nb, d), full(nc, d), full(1, d), full(1, d)],
        out_specs=row(d),
        out_shape=jax.ShapeDtypeStruct((bsz, seq, d), F32),
        compiler_params=_params("arbitrary", "arbitrary"),
        name="output_projection_ln",
    )(ya, yb, yc, x, g, wa, wb, wc, ln_g.reshape(1, d), ln_b.reshape(1, d))


def _tile_gate_up(wg, wu, tf):
    *lead, d, ff = wg.shape
    split = lambda w: jnp.moveaxis(w.reshape(*lead, d, ff // tf, tf), -2, -3)
    return jnp.concatenate([split(wg), split(wu)], axis=-1)


def _swiglu_chunk(u, wgu_ref, wd_ref):
    tf = wd_ref.shape[0]
    ab = jnp.dot(u, wgu_ref[...], preferred_element_type=F32)
    a, b = ab[:, :tf], ab[:, tf:]
    return jnp.dot((a * jax.nn.sigmoid(a) * b).astype(BF16), wd_ref[...], preferred_element_type=F32)


def _ffn_kernel(x_ref, sc_ref, sh_ref, g_ref, wgu_ref, wd_ref, lng_ref, lnb_ref, o_ref, u_scr, acc_scr):
    f = pl.program_id(2)

    @pl.when(f == 0)
    def _():
        u_scr[...] = (x_ref[...] * (1.0 + sc_ref[...]) + sh_ref[...]).astype(BF16)
        acc_scr[...] = jnp.zeros_like(acc_scr)

    acc_scr[...] += _swiglu_chunk(u_scr[...], wgu_ref, wd_ref)

    @pl.when(f == pl.num_programs(2) - 1)
    def _():
        z = ALPHA * x_ref[...] + (1.0 + g_ref[...]) * acc_scr[...]
        o_ref[...] = _layer_norm_rows(z, lng_ref[...], lnb_ref[...])


def _dense_ffn(x, sc, sh, g, wg, wu, wd, ln_g, ln_b):
    bsz, seq, d = x.shape
    ff = wg.shape[1]
    tm, tf = 1024, 256
    row = pl.BlockSpec((None, tm, d), lambda b, i, f: (b, i, 0))
    vec = pl.BlockSpec((None, 1, d), lambda b, i, f: (b, 0, 0))
    one = pl.BlockSpec((1, d), lambda b, i, f: (0, 0))
    return pl.pallas_call(
        _ffn_kernel,
        grid=(bsz, seq // tm, ff // tf),
        in_specs=[row, vec, vec, vec,
                  pl.BlockSpec((None, d, 2 * tf), lambda b, i, f: (f, 0, 0)),
                  pl.BlockSpec((tf, d), lambda b, i, f: (f, 0)),
                  one, one],
        out_specs=row,
        out_shape=jax.ShapeDtypeStruct((bsz, seq, d), F32),
        scratch_shapes=[pltpu.VMEM((tm, d), BF16), pltpu.VMEM((tm, d), F32)],
        compiler_params=_params("arbitrary", "arbitrary", "arbitrary"),
        name="dense_swiglu_ln",
    )(x, sc, sh, g, _tile_gate_up(wg, wu, tf), wd, ln_g.reshape(1, d), ln_b.reshape(1, d))


def _router_kernel(x_ref, sc_ref, sh_ref, r_ref, lo_ref, up_ref, gate_ref, slot_ref, slot_t_ref, cnt_ref):
    u = x_ref[...] * (1.0 + sc_ref[...]) + sh_ref[...]
    logits = jnp.dot(u, r_ref[...], preferred_element_type=F32, precision=lax.Precision.HIGHEST)
    lane = lax.broadcasted_iota(jnp.int32, logits.shape, 1)
    neg = -jnp.inf
    l1 = jnp.where(lane < N_EXPERTS, logits, neg)
    m1 = jnp.max(l1, -1, keepdims=True)
    i1 = jnp.min(jnp.where(l1 == m1, lane, LANE), -1, keepdims=True)
    l2 = jnp.where(lane == i1, neg, l1)
    m2 = jnp.max(l2, -1, keepdims=True)
    i2 = jnp.min(jnp.where(l2 == m2, lane, LANE), -1, keepdims=True)
    e2 = jnp.exp(m2 - m1)
    w1 = 1.0 / (1.0 + e2)
    w2 = e2 / (1.0 + e2)
    gate_ref[...] = jnp.where(lane == i1, w1, jnp.where(lane == i2, w2, 0.0))
    routed = jnp.where((lane == i1) | (lane == i2), 1.0, 0.0)
    before = jnp.dot(lo_ref[...], routed.astype(BF16), preferred_element_type=F32)
    slot_ref[...] = jnp.where(routed > 0.5, before, -1.0)
    routed_t = routed.T
    before_t = jnp.dot(routed_t.astype(BF16), up_ref[...], preferred_element_type=F32)
    slot_t_ref[...] = jnp.where(routed_t > 0.5, before_t, -1.0)
    cnt_ref[...] = jnp.broadcast_to(jnp.sum(routed, axis=0, keepdims=True), cnt_ref.shape).astype(jnp.int32)


MOE_TM = 1024
MOE_ROWS = 320
MOE_TF = 896


def _moe_router(x, sc, sh, router):
    bsz, seq, d = x.shape
    tm = MOE_TM
    r = jnp.pad(router, ((0, 0), (0, LANE - router.shape[1])))
    upper = np.triu(np.ones((tm, tm), np.float32), 1)
    up, lo = jnp.asarray(upper, BF16), jnp.asarray(upper.T, BF16)
    row = lambda n: pl.BlockSpec((None, tm, n), lambda b, i: (b, i, 0))
    vec = pl.BlockSpec((None, 1, d), lambda b, i: (b, 0, 0))
    full = lambda a, c: pl.BlockSpec((a, c), lambda b, i: (0, 0))
    n_tile = seq // tm
    return pl.pallas_call(
        _router_kernel,
        grid=(bsz, n_tile),
        in_specs=[row(d), vec, vec, full(d, LANE), full(tm, tm), full(tm, tm)],
        out_specs=[row(LANE), row(LANE), pl.BlockSpec((None, None, LANE, tm), lambda b, i: (b, i, 0, 0)),
                   pl.BlockSpec((None, None, 8, LANE), lambda b, i: (b, i, 0, 0))],
        out_shape=[jax.ShapeDtypeStruct((bsz, seq, LANE), F32), jax.ShapeDtypeStruct((bsz, seq, LANE), F32),
                   jax.ShapeDtypeStruct((bsz, n_tile, LANE, tm), F32), jax.ShapeDtypeStruct((bsz, n_tile, 8, LANE), jnp.int32)],
        compiler_params=_params("arbitrary", "arbitrary"),
        name="moe_router",
    )(x, sc, sh, r, lo, up)


def _moe_kernel(cnt_ref, x_ref, sc_ref, sh_ref, g_ref, gate_ref, slot_ref, slot_t_ref, wgu_ref, wd_ref, lng_ref, lnb_ref,
                o_ref, u_scr, xg_scr, acc_scr):
    tm, rows = MOE_TM, MOE_ROWS
    e = pl.program_id(2)
    f = pl.program_id(3)
    tile = pl.program_id(0) * pl.num_programs(1) + pl.program_id(1)
    n_pass = (cnt_ref[tile * N_EXPERTS + e] + rows - 1) // rows

    @pl.when((e == 0) & (f == 0))
    def _():
        u_scr[...] = (x_ref[...] * (1.0 + sc_ref[...]) + sh_ref[...]).astype(BF16)
        o_ref[...] = jnp.zeros_like(o_ref)

    @pl.when(f == 0)
    def _():
        slot_row = slot_t_ref[pl.ds(e, 1), :]

        def gather(p, carry):
            want = p * rows + lax.broadcasted_iota(jnp.int32, (rows, 1), 0)
            pick = jnp.where(slot_row == want.astype(F32), 1.0, 0.0).astype(BF16)
            xg_scr[p] = jnp.dot(pick, u_scr[...], preferred_element_type=F32).astype(BF16)
            acc_scr[p] = jnp.zeros((rows, x_ref.shape[-1]), F32)
            return carry

        lax.fori_loop(0, n_pass, gather, 0)

    def ffn(p, carry):
        acc_scr[p] += _swiglu_chunk(xg_scr[p], wgu_ref, wd_ref)
        return carry

    lax.fori_loop(0, n_pass, ffn, 0)

    @pl.when(f == pl.num_programs(3) - 1)
    def _():
        lane = lax.broadcasted_iota(jnp.int32, (tm, LANE), 1)
        slot_col = jnp.sum(jnp.where(lane == e, slot_ref[...], 0.0), axis=1, keepdims=True)
        gate_col = jnp.sum(jnp.where(lane == e, gate_ref[...], 0.0), axis=1, keepdims=True)

        def scatter(p, carry):
            want = p * rows + lax.broadcasted_iota(jnp.int32, (1, rows), 1)
            put = jnp.where(slot_col == want.astype(F32), 1.0, 0.0).astype(BF16)
            y = acc_scr[p]
            hi = y.astype(BF16)
            lo = (y - hi.astype(F32)).astype(BF16)
            back = jnp.dot(put, hi, preferred_element_type=F32) + jnp.dot(put, lo, preferred_element_type=F32)
            o_ref[...] += gate_col * back
            return carry

        lax.fori_loop(0, n_pass, scatter, 0)

    @pl.when((e == pl.num_programs(2) - 1) & (f == pl.num_programs(3) - 1))
    def _():
        z = ALPHA * x_ref[...] + (1.0 + g_ref[...]) * o_ref[...]
        o_ref[...] = _layer_norm_rows(z, lng_ref[...], lnb_ref[...])


def _moe_ffn(x, sc, sh, g, routing, wg, wu, wd, ln_g, ln_b):
    gate, slot, slot_t, cnt = routing
    bsz, seq, d = x.shape
    n_e, _, ff = wg.shape
    tm, tf, rows = MOE_TM, MOE_TF, MOE_ROWS
    max_pass = -(-tm // rows)
    counts = cnt[:, :, 0, :n_e].reshape(-1)
    row = lambda n: pl.BlockSpec((None, tm, n), lambda b, i, e, f, c: (b, i, 0))
    vec = pl.BlockSpec((None, 1, d), lambda b, i, e, f, c: (b, 0, 0))
    one = pl.BlockSpec((1, d), lambda b, i, e, f, c: (0, 0))
    grid_spec = pltpu.PrefetchScalarGridSpec(
        num_scalar_prefetch=1,
        grid=(bsz, seq // tm, n_e, ff // tf),
        in_specs=[row(d), vec, vec, vec, row(LANE), row(LANE),
                  pl.BlockSpec((None, None, LANE, tm), lambda b, i, e, f, c: (b, i, 0, 0)),
                  pl.BlockSpec((None, None, d, 2 * tf), lambda b, i, e, f, c: (e, f, 0, 0)),
                  pl.BlockSpec((None, tf, d), lambda b, i, e, f, c: (e, f, 0)),
                  one, one],
        out_specs=row(d),
        scratch_shapes=[pltpu.VMEM((tm, d), BF16), pltpu.VMEM((max_pass, rows, d), BF16), pltpu.VMEM((max_pass, rows, d), F32)],
    )
    return pl.pallas_call(
        _moe_kernel,
        grid_spec=grid_spec,
        out_shape=jax.ShapeDtypeStruct((bsz, seq, d), F32),
        compiler_params=_params("arbitrary", "arbitrary", "arbitrary", "arbitrary"),
        name="moe_swiglu_ln",
    )(counts, x, sc, sh, g, gate, slot, slot_t, _tile_gate_up(wg, wu, tf), wd, ln_g.reshape(1, d), ln_b.reshape(1, d))


ML_TT = 256
ML_SUB = 128


ML_NB = 1


def _mlstm_kernel(qk_ref, tail_ref, v_ref, og_ref, g_ref, cw_ref, cb_ref, gb_ref, gn_ref, y_ref,
                  ct_scr, n_scr, m_scr, q_scr, k_scr):
    @pl.when(pl.program_id(1) == 0)
    def _():
        ct_scr[...] = jnp.zeros_like(ct_scr)
        n_scr[...] = jnp.zeros_like(n_scr)
        m_scr[...] = jnp.zeros_like(m_scr)

    for nb in range(ML_NB):
        _mlstm_rows(qk_ref.at[nb], tail_ref.at[nb], v_ref.at[nb], og_ref.at[nb], g_ref.at[nb], cw_ref, cb_ref, gb_ref, gn_ref,
                    y_ref.at[nb], ct_scr.at[nb], n_scr.at[nb], m_scr.at[nb], q_scr.at[nb], k_scr.at[nb])


def _mlstm_rows(qk_ref, tail_ref, v_ref, og_ref, g_ref, cw_ref, cb_ref, gb_ref, gn_ref, y_ref,
                ct_scr, n_scr, m_scr, q_scr, k_scr):
    tt = ML_TT
    step = pl.program_id(1)

    x = qk_ref[...]
    tail = jnp.where(step == 0, 0.0, tail_ref[...])
    row8 = lax.broadcasted_iota(jnp.int32, (8, 1), 0)
    pre = x * cw_ref[CONV_W - 1:CONV_W, :] + cb_ref[...]
    for s in range(1, CONV_W):
        rolled = pltpu.roll(x, s, 0)
        head = jnp.where(row8 < s, pltpu.roll(tail, s, 0), rolled[0:8])
        pre = pre + jnp.concatenate([head, rolled[8:]], axis=0) * cw_ref[CONV_W - 1 - s:CONV_W - s, :]
    act = pre * jax.nn.sigmoid(pre)
    q_scr[...] = act[:, 0:ML_D].astype(BF16)
    k_scr[...] = (act[:, ML_D:2 * ML_D] * (DH_B ** -0.5)).astype(BF16)

    lane = lax.broadcasted_iota(jnp.int32, (1, LANE), 1)
    tok = lane % CHUNK
    jj = lax.broadcasted_iota(jnp.int32, (CHUNK, CHUNK), 0)
    ss = lax.broadcasted_iota(jnp.int32, (CHUNK, CHUNK), 1)
    nt = (((1,), (1,)), ((), ()))
    tn = (((0,), (0,)), ((), ()))
    ct_state = [ct_scr[h] for h in range(H_B)]
    n_state = [n_scr[h:h + 1, :] for h in range(H_B)]
    m_state = [m_scr[h:h + 1, 0:1] for h in range(H_B)]
    for sub in range(tt // ML_SUB):
        r0 = sub * ML_SUB
        gp = g_ref[r0:r0 + ML_SUB, :] + gb_ref[...]
        lsig = jnp.minimum(gp, 0.0) - jnp.log(1.0 + jnp.exp(-jnp.abs(gp)))
        col = jnp.where(lane < H_B, gp, lsig)
        rowl = col.T
        b = rowl[0:8]
        for sft in (1, 2, 4, 8, 16, 32):
            b = b + jnp.where(tok >= sft, pltpu.roll(b, sft, 1), 0.0)
        bcol = jnp.concatenate([b, jnp.zeros((LANE - 8, ML_SUB), F32)], axis=0).T
        for ci in range(ML_SUB // CHUNK):
            c0 = ci * CHUNK
            rows = slice(r0 + c0, r0 + c0 + CHUNK)
            for h in range(H_B):
                hs = slice(h * DH_B, (h + 1) * DH_B)
                b_col = bcol[c0:c0 + CHUNK, H_B + h:H_B + h + 1]
                ig_col = col[c0:c0 + CHUNK, h:h + 1]
                b_row = b[H_B + h:H_B + h + 1, c0:c0 + CHUNK]
                ig_row = rowl[h:h + 1, c0:c0 + CHUNK]
                qh, kh, vh = q_scr[rows, hs], k_scr[rows, hs], v_ref[rows, hs]
                log_d = jnp.where(jj >= ss, b_col - b_row + ig_row, NEG)
                m_loc = jnp.max(log_d, -1, keepdims=True)
                s_loc = lax.dot_general(qh, kh, nt, preferred_element_type=F32) * jnp.exp(log_d - m_loc)
                sv_loc = jnp.dot(s_loc.astype(BF16), vh, preferred_element_type=F32)
                rs_loc = jnp.sum(s_loc, -1, keepdims=True)
                b_last = b_row[:, CHUNK - 1:CHUNK]
                w_max = jnp.max(b_last - b_row + ig_row, -1, keepdims=True)
                w_loc = jnp.exp(b_last - b_col + ig_col - w_max)
                inc_c = lax.dot_general(kh, (w_loc * vh.astype(F32)).astype(BF16), tn, preferred_element_type=F32)
                inc_n = jnp.sum(w_loc * kh.astype(F32), axis=0, keepdims=True)
                m_old, ct, n_row = m_state[h], ct_state[h], n_state[h]
                log_inter = b_col + m_old
                m_out = jnp.maximum(log_inter, m_loc)
                w_inter = jnp.exp(log_inter - m_out)
                w_intra = jnp.exp(m_loc - m_out)
                num = w_inter * jnp.dot(qh, ct.astype(BF16), preferred_element_type=F32) + w_intra * sv_loc
                den = w_inter * jnp.sum(qh.astype(F32) * n_row, -1, keepdims=True) + w_intra * rs_loc
                hid = num / jnp.maximum(jnp.abs(den), jnp.exp(-m_out))
                m_new = jnp.maximum(b_last + m_old, w_max)
                decay = jnp.exp(b_last + m_old - m_new)
                grow = jnp.exp(w_max - m_new)
                ct_state[h] = decay * ct + grow * inc_c
                n_state[h] = decay * n_row + grow * inc_n
                m_state[h] = m_new
                ms = jnp.mean(hid * hid, -1, keepdims=True)
                y = hid * lax.rsqrt(ms + 1e-6) * gn_ref[:, hs] * jax.nn.sigmoid(og_ref[rows, hs])
                y_ref[rows, hs] = y.astype(y_ref.dtype)
    for h in range(H_B):
        ct_scr[h] = ct_state[h]
        n_scr[h:h + 1, :] = n_state[h]
        m_scr[h:h + 1, :] = jnp.broadcast_to(m_state[h], (1, LANE))


def _mlstm(mqk, mv, mo, mg, conv_w, conv_b, gate_b, g_b):
    bsz, seq, _ = mqk.shape
    tt = ML_TT
    gb = jnp.pad(gate_b.reshape(1, 2 * H_B), ((0, 0), (0, LANE - 2 * H_B)))
    nb = ML_NB
    row = lambda n: pl.BlockSpec((nb, tt, n), lambda b, i: (b, i, 0))
    full = lambda r, c: pl.BlockSpec((r, c), lambda b, i: (0, 0))
    tail = pl.BlockSpec((nb, 8, 2 * ML_D), lambda b, i: (b, jnp.maximum(i * (tt // 8) - 1, 0), 0))
    return pl.pallas_call(
        _mlstm_kernel,
        grid=(bsz // nb, seq // tt),
        in_specs=[row(2 * ML_D), tail, row(ML_D), row(ML_D), row(LANE),
                  full(CONV_W, 2 * ML_D), full(1, 2 * ML_D), full(1, LANE), full(1, ML_D)],
        out_specs=row(ML_D),
        out_shape=jax.ShapeDtypeStruct((bsz, seq, ML_D), BF16),
        scratch_shapes=[pltpu.VMEM((nb, H_B, DH_B, DH_B), F32), pltpu.VMEM((nb, 8, DH_B), F32), pltpu.VMEM((nb, 8, LANE), F32),
                        pltpu.VMEM((nb, tt, ML_D), BF16), pltpu.VMEM((nb, tt, ML_D), BF16)],
        compiler_params=_params("arbitrary", "arbitrary"),
        name="mlstm_scan",
    )(mqk, mqk, mv, mo, mg, conv_w, conv_b.reshape(1, 2 * ML_D), gb, g_b.reshape(1, ML_D))


def _mixers(pa, mls, nsa, rope, w_uk, w_uv, kv_norm, conv_w, conv_b, gate_b, cmp_pos, cmp_w1, cmp_w2, grp_norm):
    g_a, g_b, g_c = jnp.split(grp_norm, [H_A * DH_V, H_A * DH_V + H_B * DH_B])
    qc, iq, iw, kc_a, ik, ct_a = _dsa_prep(pa, rope[0], rope[1], w_uk, kv_norm)
    y_a = _dsa_attention(qc, iq, iw, kc_a, ct_a, ik, w_uv, g_a)
    y_b = _mlstm(*mls, conv_w, conv_b, gate_b, g_b)
    nq, ncmp, nks, nvs, nkw, nvw, ng = nsa
    kc, vc = _nsa_compress(ncmp, cmp_pos, cmp_w1, cmp_w2)
    y_c = _nsa_attention(nq, ng, kc, vc, nks, nvs, nkw, nvw, g_c)
    return y_a, y_b, y_c


def _pad_cols(w, n):
    return jnp.pad(w, ((0, 0), (0, n - w.shape[1])))


def kernel(x, c, positions, w_mod, b_mod, w_in, dsa_w_uk, dsa_w_uv, dsa_kv_norm, mlstm_conv_w, mlstm_conv_b, mlstm_gate_b, nsa_cmp_pos, nsa_cmp_w1, nsa_cmp_w2, grp_norm, w_out, ln_g, ln_b, ffn_w_gate, ffn_w_up, ffn_w_down, moe_router, moe_w_gate, moe_w_up, moe_w_down):
    bsz = x.shape[0]
    mod = _modulation(c, w_mod, b_mod).reshape(bsz, DEPTH, N_MOD, 1, D_MODEL)
    rope = _rope_table(positions)
    for l in range(DEPTH):
        sh1, sc1, g1, sh2, sc2, g2 = [mod[:, l, j] for j in range(N_MOD)]
        w = w_in[l].astype(BF16)
        wa = _dsa_weight_layout(w[:, :N_GROUP_A])
        wb = _mlstm_weight_layout(w[:, N_GROUP_A:N_GROUP_A + N_GROUP_B])
        wc, wvt = _nsa_weight_layout(w[:, N_GROUP_A + N_GROUP_B:])
        pa, mqk, mv, mo, mg, *nsa = _input_projection(x, sc1, sh1, wa, wb, wc, wvt)
        ya, yb, yc = _mixers(pa, (mqk, mv, mo, mg), nsa, rope, dsa_w_uk[l], dsa_w_uv[l], dsa_kv_norm[l], mlstm_conv_w[l], mlstm_conv_b[l], mlstm_gate_b[l], nsa_cmp_pos[l], nsa_cmp_w1[l], nsa_cmp_w2[l], grp_norm[l])
        wo = w_out[l].astype(BF16)
        n_a, n_b = H_A * DH_V, H_B * DH_B
        x = _output_projection(ya, yb, yc, x, g1, _pad_head_rows(wo[:n_a], H_A, DH_V), wo[n_a:n_a + n_b], _pad_head_rows(wo[n_a + n_b:], H_C, DH_C), ln_g[l, 0], ln_b[l, 0])
        if l % 2 == 0:
            k = l // 2
            x = _dense_ffn(x, sc2, sh2, g2, ffn_w_gate[k].astype(BF16), ffn_w_up[k].astype(BF16), ffn_w_down[k].astype(BF16), ln_g[l, 1], ln_b[l, 1])
        else:
            k = l // 2
            routing = _moe_router(x, sc2, sh2, moe_router[k])
            x = _moe_ffn(x, sc2, sh2, g2, routing,moe_w_gate[k].astype(BF16), moe_w_up[k].astype(BF16), moe_w_down[k].astype(BF16), ln_g[l, 1], ln_b[l, 1])
    return x
```

```python
import functools

import numpy as np
import jax
import jax.numpy as jnp
from jax import lax
from jax.experimental import pallas as pl
from jax.experimental.pallas import tpu as pltpu

F32 = jnp.float32
BF16 = jnp.bfloat16

D_MODEL = 1024
DEPTH = 2
H_A, DH_NOPE, DH_ROPE, D_C, DH_V, H_I, D_I = 4, 64, 32, 128, 64, 4, 64
K_SEL_MAX = 256
ROPE_THETA = 10000.0
H_B, DH_B, CONV_W, CHUNK = 4, 128, 4, 64
H_C, DH_C, L_CMP, D_STRIDE, CMP_HIDDEN, L_SEL, N_TOP_MAX, WINDOW, Q_BLOCK = 4, 64, 32, 16, 128, 64, 16, 512, 128
D_FF = 2816
N_EXPERTS = 8
D_FF_EXPERT = 3584
N_MOD = 6
ALPHA = (2 * DEPTH) ** 0.25
SPLIT_SIZES = (H_A * DH_NOPE, H_A * DH_ROPE, D_C, DH_ROPE, H_I * D_I, D_I, H_I, H_B * DH_B, H_B * DH_B, H_B * DH_B, H_B, H_B, H_B * DH_B, H_C * DH_C, DH_C, DH_C, DH_C, DH_C, DH_C, DH_C, 3 * H_C)
N_GROUP_A = sum(SPLIT_SIZES[:7])
N_GROUP_B = sum(SPLIT_SIZES[7:13])
N_GROUP_C = sum(SPLIT_SIZES[13:])

LOG2E = 1.4426950408889634
LANE = 128
VMEM_LIMIT = 56 * 1024 * 1024


def _round_up(n, m):
    return (n + m - 1) // m * m


def _params(*sem):
    return pltpu.CompilerParams(dimension_semantics=sem, vmem_limit_bytes=VMEM_LIMIT)


FOLD_ROWS = 64


def _fold_rows(x, op):
    parts = [x[i:i + FOLD_ROWS] for i in range(0, x.shape[0], FOLD_ROWS)]
    while len(parts) > 1:
        parts = [op(parts[i], parts[i + 1]) if i + 1 < len(parts) else parts[i] for i in range(0, len(parts), 2)]
    return parts[0]


def _attend_block(s_buf, b_buf, v_t, m, l, acc_scr, n_head, tq):
    bias = b_buf[...]
    m_new, l_new, ps = [], [], []
    for h in range(n_head):
        hs = slice(h * tq, (h + 1) * tq)
        sh = s_buf[:, hs] + bias
        mh = jnp.maximum(m[:, hs], jnp.max(_fold_rows(sh, jnp.maximum), axis=0, keepdims=True))
        ph = jnp.exp2(sh - mh)
        l_new.append(jnp.exp2(m[:, hs] - mh) * l[:, hs] + jnp.sum(_fold_rows(ph, jnp.add), axis=0, keepdims=True))
        m_new.append(mh)
        ps.append(ph.astype(BF16))
    m_new = jnp.concatenate(m_new, axis=1)
    acc_scr[...] = jnp.exp2(m - m_new) * acc_scr[...] + jnp.dot(v_t, jnp.concatenate(ps, axis=1), preferred_element_type=F32)
    return m_new, jnp.concatenate(l_new, axis=1)


PLANE_KEYS = 256


def _bit_transpose32(x):
    w = [x[8 * i:8 * i + 8, :] for i in range(32)]
    j, m = 16, 0x0000FFFF
    while j:
        k = 0
        while k < 32:
            t = (w[k] ^ lax.shift_right_logical(w[k + j], j)) & m
            w[k] = w[k] ^ t
            w[k + j] = w[k + j] ^ jnp.left_shift(t, j)
            k = (k + j + 1) & ~j
        j >>= 1
        m ^= (m << j) & 0xFFFFFFFF
    return w


def _layer_norm_rows(z, g, b):
    mu = jnp.mean(z, -1, keepdims=True)
    zc = z - mu
    var = jnp.mean(zc * zc, -1, keepdims=True)
    return zc * lax.rsqrt(var + 1e-5) * g + b


def _mod_kernel(c_ref, w_ref, b_ref, o_ref):
    c = c_ref[...]
    a = c * jax.nn.sigmoid(c)
    o_ref[...] = jnp.dot(a, w_ref[...], preferred_element_type=F32, precision=lax.Precision.HIGHEST) + b_ref[...]


def _modulation(c, w_mod, b_mod):
    bsz, d = c.shape
    n = w_mod.shape[1]
    tn = 1024
    return pl.pallas_call(
        _mod_kernel,
        grid=(n // tn,),
        in_specs=[pl.BlockSpec((bsz, d), lambda j: (0, 0)),
                  pl.BlockSpec((d, tn), lambda j: (0, j)),
                  pl.BlockSpec((1, tn), lambda j: (0, j))],
        out_specs=pl.BlockSpec((bsz, tn), lambda j: (0, j)),
        out_shape=jax.ShapeDtypeStruct((bsz, n), F32),
        compiler_params=_params("arbitrary"),
        name="adaln_mod",
    )(c, w_mod, b_mod.reshape(1, n))


NSA_Q0 = 0
NSA_CMP0 = H_C * LANE
NSA_KS0 = NSA_CMP0 + LANE
NSA_KW0 = NSA_KS0 + LANE
NSA_G0 = NSA_KW0 + LANE
NSA_COLS = NSA_G0 + LANE


def _nsa_weight_layout(w):
    d = w.shape[0]
    nq, nkc, nvc, nks, nvs, nkw, nvw, ng = jnp.split(w, np.cumsum(SPLIT_SIZES[13:])[:-1].tolist(), axis=1)
    z = lambda n: jnp.zeros((d, n), w.dtype)
    half = LANE - DH_C
    cols = []
    for h in range(H_C):
        cols += [nq[:, h * DH_C:(h + 1) * DH_C], z(half)]
    cols += [nkc, nvc, nks, z(half), nkw, z(half), ng, z(LANE - 3 * H_C)]
    values_t = jnp.concatenate([nvs, z(half), nvw, z(half)], axis=1).T
    return jnp.concatenate(cols, axis=1), values_t


def _mlstm_weight_layout(w):
    mq, mk, mv, mi, mf, mo = jnp.split(w, np.cumsum(SPLIT_SIZES[7:13])[:-1].tolist(), axis=1)
    return jnp.concatenate([mq, mk, mv, mo, mi, mf, jnp.zeros((w.shape[0], LANE - 2 * H_B), w.dtype)], axis=1)


ML_D = H_B * DH_B


def _inproj_kernel(x_ref, sc_ref, sh_ref, wa_ref, wb_ref, wc_ref, wvt_ref, oa_ref, mqk_ref, mv_ref, mo_ref, mg_ref,
                   nq_ref, ncmp_ref, nks_ref, nvs_ref, nkw_ref, nvw_ref, ng_ref):
    u = (x_ref[...] * (1.0 + sc_ref[...]) + sh_ref[...]).astype(BF16)
    oa_ref[...] = jnp.dot(u, wa_ref[...], preferred_element_type=F32)
    ob = jnp.dot(u, wb_ref[...], preferred_element_type=F32)
    mqk_ref[...] = ob[:, 0:2 * ML_D]
    mv_ref[...] = ob[:, 2 * ML_D:3 * ML_D].astype(BF16)
    mo_ref[...] = ob[:, 3 * ML_D:4 * ML_D]
    mg_ref[...] = ob[:, 4 * ML_D:4 * ML_D + LANE]
    oc = jnp.dot(u, wc_ref[...], preferred_element_type=F32)
    nq_ref[...] = (oc[:, NSA_Q0:NSA_Q0 + H_C * LANE] * (DH_C ** -0.5 * LOG2E)).astype(BF16)
    ncmp_ref[...] = oc[:, NSA_CMP0:NSA_CMP0 + LANE]
    nks_ref[...] = oc[:, NSA_KS0:NSA_KS0 + LANE].astype(BF16)
    nkw_ref[...] = oc[:, NSA_KW0:NSA_KW0 + LANE].astype(BF16)
    ng_ref[...] = oc[:, NSA_G0:NSA_G0 + LANE]
    vt = lax.dot_general(wvt_ref[...], u, (((1,), (1,)), ((), ())), preferred_element_type=F32)
    nvs_ref[...] = vt[0:LANE].astype(BF16)
    nvw_ref[...] = vt[LANE:2 * LANE].astype(BF16)


def _input_projection(x, sc, sh, wa, wb, wc, wvt):
    bsz, seq, d = x.shape
    tm = 512
    na, nb, nc = wa.shape[1], wb.shape[1], wc.shape[1]
    row = lambda n: pl.BlockSpec((None, tm, n), lambda b, i: (b, i, 0))
    col = pl.BlockSpec((None, LANE, tm), lambda b, i: (b, 0, i))
    vec = pl.BlockSpec((None, 1, d), lambda b, i: (b, 0, 0))
    full = lambda n: pl.BlockSpec((d, n), lambda b, i: (0, 0))
    tok = lambda n, dt: (row(n), jax.ShapeDtypeStruct((bsz, seq, n), dt))
    feat = (col, jax.ShapeDtypeStruct((bsz, LANE, seq), BF16))
    outs = [tok(na, F32), tok(2 * ML_D, F32), tok(ML_D, BF16), tok(ML_D, F32), tok(LANE, F32),
            tok(H_C * LANE, BF16), tok(LANE, F32), tok(LANE, BF16), feat, tok(LANE, BF16), feat, tok(LANE, F32)]
    return pl.pallas_call(
        _inproj_kernel,
        grid=(bsz, seq // tm),
        in_specs=[row(d), vec, vec, full(na), full(nb), full(nc), pl.BlockSpec((2 * LANE, d), lambda b, i: (0, 0))],
        out_specs=[spec for spec, _ in outs],
        out_shape=[shape for _, shape in outs],
        compiler_params=_params("arbitrary", "arbitrary"),
        name="input_projection",
    )(x, sc, sh, wa, wb, wc, wvt)


DSA_QN0 = 0
DSA_QR0 = H_A * LANE
DSA_CKV0 = DSA_QR0 + LANE
DSA_IQ0 = DSA_CKV0 + LANE
DSA_G0 = DSA_IQ0 + H_I * LANE
DSA_COLS = DSA_G0 + LANE
DSA_KR_LANE = D_I
DSA_IW_LANE = D_I + DH_ROPE
DSA_TQ = 128
DSA_KC = 512
INT_MIN = -2 ** 31
KEY_NEG_INF = int(np.int32(np.float32(-np.inf).view(np.int32)) ^ np.int32(0x7FFFFFFF))


def _dsa_weight_layout(w):
    d = w.shape[0]
    qn, qr, ckv, kr, iq, ik, iw = jnp.split(w, np.cumsum(SPLIT_SIZES[:7])[:-1].tolist(), axis=1)
    z = lambda n: jnp.zeros((d, n), w.dtype)
    cols = []
    for h in range(H_A):
        cols += [qn[:, h * DH_NOPE:(h + 1) * DH_NOPE], z(LANE - DH_NOPE)]
    cols += [qr, ckv]
    for h in range(H_I):
        cols += [iq[:, h * D_I:(h + 1) * D_I], z(LANE - D_I)]
    cols += [ik, kr, iw, z(LANE - D_I - DH_ROPE - H_I)]
    return jnp.concatenate(cols, axis=1)


def _rope_table_kernel(pos_ref, freq_ref, cos_ref, sin_ref):
    ang = pos_ref[...].astype(F32) * freq_ref[...]
    lane = lax.broadcasted_iota(jnp.int32, (1, LANE), 1)
    first = (lane % DH_ROPE) < DH_ROPE // 2
    cos_ref[...] = jnp.cos(ang)
    sin_ref[...] = jnp.where(first, -jnp.sin(ang), jnp.sin(ang))


def _rope_table(positions):
    bsz, seq = positions.shape
    tm = 512
    inv_freq = ROPE_THETA ** (-jnp.arange(0, DH_ROPE, 2, dtype=F32) / DH_ROPE)
    freq = jnp.tile(inv_freq, LANE // (DH_ROPE // 2)).reshape(1, LANE)
    out = pl.BlockSpec((None, tm, LANE), lambda b, i: (b, i, 0))
    return pl.pallas_call(
        _rope_table_kernel,
        grid=(bsz, seq // tm),
        in_specs=[pl.BlockSpec((None, tm, 1), lambda b, i: (b, i, 0)), pl.BlockSpec((1, LANE), lambda b, i: (0, 0))],
        out_specs=[out, out],
        out_shape=[jax.ShapeDtypeStruct((bsz, seq, LANE), F32)] * 2,
        compiler_params=_params("arbitrary", "arbitrary"),
        name="rope_table",
    )(positions.reshape(bsz, seq, 1), freq)


def _dsa_prep_kernel(pa_ref, cos_ref, sin_ref, wuk_ref, kvn_ref, qc_ref, iq_ref, iw_ref, kc_ref, ik_ref, ct_ref):
    scale = (DH_NOPE + DH_ROPE) ** -0.5 * LOG2E
    cos, sin = cos_ref[...], sin_ref[...]
    lane = lax.broadcasted_iota(jnp.int32, (1, LANE), 1)
    first = (lane % DH_ROPE) < DH_ROPE // 2
    rope_lanes = (lane >= DSA_KR_LANE) & (lane < DSA_KR_LANE + DH_ROPE)

    def rope(v):
        partner = jnp.where(first, pltpu.roll(v, LANE - DH_ROPE // 2, 1), pltpu.roll(v, DH_ROPE // 2, 1))
        return v * cos + partner * sin

    g = pa_ref[:, DSA_G0:DSA_G0 + LANE]
    ckv = pa_ref[:, DSA_CKV0:DSA_CKV0 + LANE]
    ckv_n = ckv * lax.rsqrt(jnp.mean(ckv * ckv, -1, keepdims=True) + 1e-6) * kvn_ref[...]
    kc_ref[:, 0:LANE] = ckv_n.astype(BF16)
    ct_ref[...] = ckv_n.T.astype(BF16)
    kc_ref[:, LANE:2 * LANE] = jnp.where(rope_lanes, rope(g), 0.0).astype(BF16)
    ik_ref[...] = jnp.where(lane < D_I, g, 0.0).astype(BF16)
    iw_ref[...] = g
    iq_ref[...] = pa_ref[:, DSA_IQ0:DSA_IQ0 + H_I * LANE].astype(BF16)
    qr = rope(pa_ref[:, DSA_QR0:DSA_QR0 + LANE]) * scale
    for h in range(H_A):
        qn = pa_ref[:, DSA_QN0 + h * LANE:DSA_QN0 + (h + 1) * LANE].astype(BF16)
        q_abs = jnp.dot(qn, wuk_ref[h], preferred_element_type=F32) * scale
        shift = (DSA_KR_LANE - DH_ROPE * h) % LANE
        qr_h = pltpu.roll(qr, shift, 1) if shift else qr
        qc_ref[:, 2 * h * LANE:(2 * h + 1) * LANE] = q_abs.astype(BF16)
        qc_ref[:, (2 * h + 1) * LANE:(2 * h + 2) * LANE] = jnp.where(rope_lanes, qr_h, 0.0).astype(BF16)


def _dsa_prep(pa, cos, sin, w_uk, kv_norm):
    bsz, seq, _ = pa.shape
    tm = 512
    wuk = jnp.pad(w_uk, ((0, 0), (0, LANE - DH_NOPE), (0, 0))).astype(BF16)
    row = lambda n: pl.BlockSpec((None, tm, n), lambda b, i: (b, i, 0))
    outs = [(2 * H_A * LANE, BF16), (H_I * LANE, BF16), (LANE, F32), (2 * LANE, BF16), (LANE, BF16)]
    return pl.pallas_call(
        _dsa_prep_kernel,
        grid=(bsz, seq // tm),
        in_specs=[row(DSA_COLS), row(LANE), row(LANE),
                  pl.BlockSpec((H_A, LANE, D_C), lambda b, i: (0, 0, 0)), pl.BlockSpec((1, D_C), lambda b, i: (0, 0))],
        out_specs=[row(n) for n, _ in outs] + [pl.BlockSpec((None, D_C, tm), lambda b, i: (b, 0, i))],
        out_shape=[jax.ShapeDtypeStruct((bsz, seq, n), dt) for n, dt in outs] + [jax.ShapeDtypeStruct((bsz, D_C, seq), BF16)],
        compiler_params=_params("arbitrary", "arbitrary"),
        name="dsa_prep",
    )(pa, cos, sin, wuk, kv_norm.reshape(1, D_C))


def _dsa_kernel(k_sel, qc_ref, iq_ref, iw_ref, kc_ref, ct_ref, ik_ref, tri_ref, wuv_ref, gn_ref, o_ref,
                key_scr, planes_scr, acc_scr, s0_scr, s1_scr, b0_scr, b1_scr):
    tq, kcs = DSA_TQ, DSA_KC
    t0 = pl.program_id(1) * tq
    n_chunk = (t0 + tq + kcs - 1) // kcs
    nt = (((1,), (1,)), ((), ()))
    qpos = t0 + lax.broadcasted_iota(jnp.int32, (1, tq), 1)
    iw_t = iw_ref[...].T

    def score_chunk(c, carry):
        k0 = pl.multiple_of(c * kcs, kcs)
        ikc = ik_ref[pl.ds(k0, kcs), :]
        sc = jnp.zeros((kcs, tq), F32)
        for h in range(H_I):
            lg = lax.dot_general(ikc, iq_ref[:, h * LANE:(h + 1) * LANE], nt, preferred_element_type=F32)
            sc = sc + jnp.maximum(lg, 0.0) * iw_t[DSA_IW_LANE + h:DSA_IW_LANE + h + 1, :]
        sc = jnp.where(sc == 0.0, 0.0, sc)
        kpos = k0 + lax.broadcasted_iota(jnp.int32, (kcs, 1), 0)
        sc = jnp.where(kpos <= qpos, sc, -jnp.inf)
        bits = pltpu.bitcast(sc, jnp.int32)
        key = jnp.where(bits < 0, bits ^ 0x7FFFFFFF, bits)
        key_scr[pl.ds(k0, kcs), :] = key
        ukey = key ^ INT_MIN
        for blk in range(kcs // PLANE_KEYS):
            words = _bit_transpose32(ukey[blk * PLANE_KEYS:(blk + 1) * PLANE_KEYS])
            row0 = pl.multiple_of((c * (kcs // PLANE_KEYS) + blk) * 8, 8)
            for r in range(32):
                planes_scr[r, pl.ds(row0, 8), :] = words[r]
        return carry

    lax.fori_loop(0, n_chunk, score_chunk, 0)

    n_words = planes_scr.shape[1]
    word_row = lax.broadcasted_iota(jnp.int32, (n_words, tq), 0)
    alive0 = jnp.where(word_row < n_chunk * (kcs // 32), jnp.int32(-1), jnp.int32(0))
    k_int = int(k_sel)

    def bit_pass(i, carry):
        thr_u, above, alive = carry
        ones = alive & planes_scr[i]
        seen_ones = above + jnp.sum(lax.population_count(ones), axis=0, keepdims=True)
        take = seen_ones >= k_int
        alive = jnp.where(take, ones, alive & ~planes_scr[i])
        above = jnp.where(take, above, seen_ones)
        thr_u = jnp.where(take, thr_u | jnp.left_shift(jnp.int32(1), 31 - i), thr_u)
        return thr_u, above, alive

    thr_u, above, _ = lax.fori_loop(0, 32, bit_pass, (jnp.zeros((1, tq), jnp.int32), jnp.zeros((1, tq), jnp.int32), alive0))
    thr = thr_u ^ INT_MIN
    room = (k_int - above).astype(F32)

    qall = jnp.concatenate([qc_ref[:, 2 * h * LANE:(2 * h + 2) * LANE] for h in range(H_A)], axis=0)

    room = jnp.where(thr > KEY_NEG_INF, room, 0.0)
    acc_scr[...] = jnp.zeros_like(acc_scr)

    half = kcs // 2

    def prepare(k0, seen, s_buf, b_buf):
        key = key_scr[pl.ds(k0, half), :]
        tie = jnp.where(key == thr, 1.0, 0.0)
        prefix = jnp.dot(tri_ref[...], tie.astype(BF16), preferred_element_type=F32)
        tie_bias = jnp.where(seen + prefix <= room, jnp.where(key == thr, 0.0, NEG), NEG)
        b_buf[...] = jnp.where(key > thr, 0.0, tie_bias)
        s_buf[...] = lax.dot_general(kc_ref[pl.ds(k0, half), :], qall, nt, preferred_element_type=F32)
        return seen + jnp.sum(_fold_rows(tie, jnp.add), axis=0, keepdims=True)

    def attn_chunk(c, carry):
        m, l, seen = carry
        k0 = pl.multiple_of(c * kcs, kcs)
        k1 = pl.multiple_of(k0 + half, half)
        seen = prepare(k1, seen, s1_scr, b1_scr)
        m, l = _attend_block(s0_scr, b0_scr, ct_ref[:, pl.ds(k0, half)], m, l, acc_scr, H_A, tq)
        k2 = pl.multiple_of(jnp.minimum(c + 1, n_chunk - 1) * kcs, kcs)
        seen = prepare(k2, seen, s0_scr, b0_scr)
        m, l = _attend_block(s1_scr, b1_scr, ct_ref[:, pl.ds(k1, half)], m, l, acc_scr, H_A, tq)
        return m, l, seen

    seen0 = prepare(0, jnp.zeros((1, tq), F32), s0_scr, b0_scr)
    init = (jnp.full((1, H_A * tq), M_FLOOR, F32), jnp.zeros((1, H_A * tq), F32), seen0)
    _, l, _ = lax.fori_loop(0, n_chunk, attn_chunk, init)
    o_lat = (acc_scr[...] / jnp.maximum(l, 1e-30)).astype(BF16)
    for h in range(H_A):
        o = lax.dot_general(o_lat[:, h * tq:(h + 1) * tq], wuv_ref[h], (((0,), (0,)), ((), ())),
                            preferred_element_type=F32)
        ms = jnp.sum(o * o, -1, keepdims=True) * (1.0 / DH_V)
        o_ref[:, h * LANE:(h + 1) * LANE] = (o * lax.rsqrt(ms + 1e-6) * gn_ref[h:h + 1, :]).astype(o_ref.dtype)


def _dsa_attention(qc, iq, iw, kc, ct, ik, w_uv, g_a):
    bsz, seq, _ = qc.shape
    k_sel = float(min(K_SEL_MAX, seq // 4))
    tq, kcs = DSA_TQ, DSA_KC
    half = kcs // 2
    tri = jnp.asarray(np.tril(np.ones((half, half), np.float32)), BF16)
    wuv = jnp.pad(w_uv, ((0, 0), (0, 0), (0, LANE - DH_V))).astype(BF16)
    gn = jnp.pad(g_a.reshape(H_A, DH_V), ((0, 0), (0, LANE - DH_V)))
    row = lambda n: pl.BlockSpec((None, tq, n), lambda b, i: (b, i, 0))
    per_b = lambda n: pl.BlockSpec((None, seq, n), lambda b, i: (b, 0, 0))
    return pl.pallas_call(
        functools.partial(_dsa_kernel, k_sel),
        grid=(bsz, seq // tq),
        in_specs=[row(2 * H_A * LANE), row(H_I * LANE), row(LANE), per_b(2 * LANE),
                  pl.BlockSpec((None, D_C, seq), lambda b, i: (b, 0, 0)), per_b(LANE),
                  pl.BlockSpec((half, half), lambda b, i: (0, 0)), pl.BlockSpec((H_A, D_C, LANE), lambda b, i: (0, 0, 0)),
                  pl.BlockSpec((H_A, LANE), lambda b, i: (0, 0))],
        out_specs=row(H_A * LANE),
        out_shape=jax.ShapeDtypeStruct((bsz, seq, H_A * LANE), BF16),
        scratch_shapes=[pltpu.VMEM((seq, tq), jnp.int32), pltpu.VMEM((32, seq // 32, tq), jnp.int32),
                        pltpu.VMEM((D_C, H_A * tq), F32),
                        pltpu.VMEM((half, H_A * tq), F32), pltpu.VMEM((half, H_A * tq), F32),
                        pltpu.VMEM((half, tq), F32), pltpu.VMEM((half, tq), F32)],
        compiler_params=_params("arbitrary", "arbitrary"),
        name="dsa_attention",
    )(qc, iq, iw, kc, ct, ik, tri, wuv, gn)


def _nsa_compress_kernel(a_ref, pos_ref, w1t_ref, w1b_ref, w2k_ref, w2v_ref, kc_ref, vc_ref):
    a = a_ref[...]
    top = jnp.dot((a + pos_ref[0:1, :]).astype(BF16), w1t_ref[...], preferred_element_type=F32)
    bot = jnp.dot((a + pos_ref[1:2, :]).astype(BF16), w1b_ref[...], preferred_element_type=F32)
    n = a.shape[0]
    pre = top + jnp.concatenate([bot[1:], bot[:1]], axis=0)
    h = (pre * jax.nn.sigmoid(pre)).astype(BF16)
    hid = w2k_ref.shape[0]
    kc_ref[...] = jnp.dot(h[:, :hid], w2k_ref[...], preferred_element_type=F32).astype(BF16)
    vc_ref[...] = lax.dot_general(w2v_ref[...], h[:, hid:], (((1,), (1,)), ((), ())), preferred_element_type=F32).astype(BF16)


def _nsa_compress(ncmp, cmp_pos, cmp_w1, cmp_w2):
    bsz, seq, _ = ncmp.shape
    n_grp = seq // D_STRIDE
    per = L_CMP // D_STRIDE
    width = D_STRIDE * LANE
    a = ncmp.reshape(bsz, n_grp, width)
    w1 = cmp_w1.reshape(2, per, D_STRIDE, DH_C, CMP_HIDDEN)
    zer = jnp.zeros((D_STRIDE, DH_C, CMP_HIDDEN), cmp_w1.dtype)

    def expand(p):
        wk = jnp.concatenate([w1[0, p], zer], axis=1)
        wv = jnp.concatenate([zer, w1[1, p]], axis=1)
        return jnp.concatenate([wk, wv], axis=2).reshape(width, 2 * CMP_HIDDEN).astype(BF16)

    pos = cmp_pos.reshape(2, per, D_STRIDE, DH_C)
    pos = jnp.concatenate([pos[0], pos[1]], axis=-1).reshape(per, width)
    pad_out = ((0, 0), (0, LANE - DH_C))
    w2k = jnp.pad(cmp_w2[0], pad_out).astype(BF16)
    w2v = jnp.pad(cmp_w2[1], pad_out).astype(BF16).T
    full = lambda r, c: pl.BlockSpec((r, c), lambda b: (0, 0))
    return pl.pallas_call(
        _nsa_compress_kernel,
        grid=(bsz,),
        in_specs=[pl.BlockSpec((None, n_grp, width), lambda b: (b, 0, 0)), full(per, width),
                  full(width, 2 * CMP_HIDDEN), full(width, 2 * CMP_HIDDEN), full(CMP_HIDDEN, LANE), full(LANE, CMP_HIDDEN)],
        out_specs=[pl.BlockSpec((None, n_grp, LANE), lambda b: (b, 0, 0)), pl.BlockSpec((None, LANE, n_grp), lambda b: (b, 0, 0))],
        out_shape=[jax.ShapeDtypeStruct((bsz, n_grp, LANE), BF16), jax.ShapeDtypeStruct((bsz, LANE, n_grp), BF16)],
        compiler_params=_params("arbitrary"),
        name="nsa_compress",
    )(a, pos, expand(0), expand(1), w2k, w2v)


NSA_TQ = 128
NSA_KC = 512
NEG = -1e30
M_FLOOR = -1e20


def _softmax_cols(s, bias, n_head, tq):
    out = []
    for h in range(n_head):
        sh = s[:, h * tq:(h + 1) * tq] + bias
        m = jnp.maximum(jnp.max(_fold_rows(sh, jnp.maximum), axis=0, keepdims=True), M_FLOOR)
        e = jnp.exp2(sh - m)
        den = jnp.sum(_fold_rows(e, jnp.add), axis=0, keepdims=True)
        out.append(e * (1.0 / jnp.maximum(den, 1e-30)))
    return out


def _nsa_kernel(q_ref, g_ref, kc_ref, vct_ref, ks_ref, vst_ref, kw_ref, vwt_ref, cover_ref, expand_ref, gn_ref, o_ref,
                acc_scr, s0_scr, s1_scr, b0_scr, b1_scr):
    tq, kc_sz = NSA_TQ, NSA_KC
    t0 = pl.program_id(1) * tq
    nt = (((1,), (1,)), ((), ()))
    q = q_ref[...]
    qa = jnp.concatenate([q[:, h * LANE:(h + 1) * LANE] for h in range(H_C)], axis=0)
    qpos = t0 + lax.broadcasted_iota(jnp.int32, (1, tq), 1)

    n_grp = kc_ref.shape[0]
    s_c = lax.dot_general(kc_ref[...], qa, nt, preferred_element_type=F32)
    n_idx = lax.broadcasted_iota(jnp.int32, (n_grp, 1), 0)
    visible = jnp.where(n_idx < n_grp - 1, n_idx * D_STRIDE + (L_CMP - 1), 2 ** 30)
    p_c = _softmax_cols(s_c, jnp.where(visible <= qpos, 0.0, NEG), H_C, tq)
    o_cmp = jnp.dot(vct_ref[...], jnp.concatenate([p.astype(BF16) for p in p_c], axis=1), preferred_element_type=F32)

    p_sum = p_c[0]
    for h in range(1, H_C):
        p_sum = p_sum + p_c[h]
    hi = p_sum.astype(BF16)
    lo = (p_sum - hi.astype(F32)).astype(BF16)
    cov = cover_ref[...]
    imp_t = jnp.dot(cov, hi, preferred_element_type=F32) + jnp.dot(cov, lo, preferred_element_type=F32)
    n_sel = cov.shape[0]
    jrow = lax.broadcasted_iota(jnp.int32, (n_sel, tq), 0)
    cur = (t0 + lax.broadcasted_iota(jnp.int32, (n_sel, tq), 1)) // L_SEL
    adm = jrow <= cur
    forced = (jrow == 0) | (jrow == cur) | (jrow == cur - 1)
    val = jnp.where(adm & forced, jnp.inf, jnp.where(adm, imp_t, -jnp.inf))
    rank = jnp.zeros((n_sel, tq), F32)
    for jp in range(n_sel):
        r = val[jp:jp + 1, :]
        rank = rank + jnp.where(r == val, jnp.where(jrow > jp, 1.0, 0.0), jnp.where(r > val, 1.0, 0.0))
    sel_t = jnp.where(rank < min(N_TOP_MAX, n_sel), jnp.where(val > -jnp.inf, 1.0, 0.0), 0.0)
    if n_sel < LANE:
        sel_t = jnp.concatenate([sel_t, jnp.zeros((LANE - n_sel, tq), F32)], axis=0)
    sel_t = sel_t.astype(BF16)

    acc_scr[...] = jnp.zeros_like(acc_scr)

    half = kc_sz // 2
    n_chunk = (t0 + tq + kc_sz - 1) // kc_sz

    def prepare(k0, s_buf, b_buf):
        s_buf[...] = lax.dot_general(ks_ref[pl.ds(k0, half), :], qa, nt, preferred_element_type=F32)
        picked = jnp.dot(expand_ref[pl.ds(k0, half), :], sel_t, preferred_element_type=F32)
        kpos = k0 + lax.broadcasted_iota(jnp.int32, (half, 1), 0)
        b_buf[...] = jnp.where(kpos <= qpos, jnp.where(picked > 0.5, 0.0, NEG), NEG)

    def chunk(c, carry):
        m, l = carry
        k0 = pl.multiple_of(c * kc_sz, kc_sz)
        k1 = pl.multiple_of(k0 + half, half)
        prepare(k1, s1_scr, b1_scr)
        m, l = _attend_block(s0_scr, b0_scr, vst_ref[:, pl.ds(k0, half)], m, l, acc_scr, H_C, tq)
        prepare(pl.multiple_of(jnp.minimum(c + 1, n_chunk - 1) * kc_sz, kc_sz), s0_scr, b0_scr)
        return _attend_block(s1_scr, b1_scr, vst_ref[:, pl.ds(k1, half)], m, l, acc_scr, H_C, tq)

    prepare(0, s0_scr, b0_scr)
    init = (jnp.full((1, H_C * tq), M_FLOOR, F32), jnp.zeros((1, H_C * tq), F32))
    _, l_s = lax.fori_loop(0, n_chunk, chunk, init)
    o_slc = acc_scr[...] * (1.0 / jnp.maximum(l_s, 1e-30))

    span = WINDOW + tq
    w0 = pl.multiple_of(jnp.maximum(t0 - WINDOW, 0), tq)
    s_w = lax.dot_general(kw_ref[pl.ds(w0, span), :], qa, nt, preferred_element_type=F32)
    kpos_w = w0 + lax.broadcasted_iota(jnp.int32, (span, 1), 0)
    wbias = jnp.where(kpos_w <= qpos, jnp.where(qpos - kpos_w < WINDOW, 0.0, NEG), NEG)
    p_w = _softmax_cols(s_w, wbias, H_C, tq)
    o_win = jnp.dot(vwt_ref[:, pl.ds(w0, span)], jnp.concatenate([p.astype(BF16) for p in p_w], axis=1), preferred_element_type=F32)

    gates_t = jax.nn.sigmoid(g_ref[...]).T
    for h in range(H_C):
        hs = slice(h * tq, (h + 1) * tq)
        o_t = (gates_t[h:h + 1] * o_cmp[:, hs] + gates_t[H_C + h:H_C + h + 1] * o_slc[:, hs]
               + gates_t[2 * H_C + h:2 * H_C + h + 1] * o_win[:, hs])
        o = o_t.T
        ms = jnp.sum(o * o, -1, keepdims=True) * (1.0 / DH_C)
        o_ref[:, h * LANE:(h + 1) * LANE] = (o * lax.rsqrt(ms + 1e-6) * gn_ref[h:h + 1, :]).astype(o_ref.dtype)


def _nsa_attention(nq, ng, kc, vc_t, nks, nvs_t, nkw, nvw_t, g_c):
    bsz, seq, _ = nq.shape
    n_grp = kc.shape[1]
    n_sel = seq // L_SEL
    grp_start = np.arange(n_grp) * D_STRIDE
    sel_start = np.arange(n_sel) * L_SEL
    cover_t = ((grp_start[None, :] < sel_start[:, None] + L_SEL) & (grp_start[None, :] + L_CMP > sel_start[:, None]))
    cover_t = jnp.asarray(cover_t.astype(np.float32), BF16)
    expand = (np.arange(seq)[:, None] // L_SEL == np.arange(LANE)[None, :]).astype(np.float32)
    expand = jnp.asarray(expand, BF16)
    gn = jnp.pad(g_c.reshape(H_C, DH_C), ((0, 0), (0, LANE - DH_C)))
    tq = NSA_TQ
    row = lambda n: pl.BlockSpec((None, tq, n), lambda b, i: (b, i, 0))
    per_b = lambda r, c: pl.BlockSpec((None, r, c), lambda b, i: (b, 0, 0))
    full = lambda r, c: pl.BlockSpec((r, c), lambda b, i: (0, 0))
    return pl.pallas_call(
        _nsa_kernel,
        grid=(bsz, seq // tq),
        in_specs=[row(H_C * LANE), row(LANE), per_b(n_grp, LANE), per_b(LANE, n_grp), per_b(seq, LANE), per_b(LANE, seq),
                  per_b(seq, LANE), per_b(LANE, seq), full(n_sel, n_grp), full(seq, LANE), full(H_C, LANE)],
        out_specs=row(H_C * LANE),
        out_shape=jax.ShapeDtypeStruct((bsz, seq, H_C * LANE), BF16),
        scratch_shapes=[pltpu.VMEM((LANE, H_C * tq), F32),
                        pltpu.VMEM((NSA_KC // 2, H_C * tq), F32), pltpu.VMEM((NSA_KC // 2, H_C * tq), F32),
                        pltpu.VMEM((NSA_KC // 2, tq), F32), pltpu.VMEM((NSA_KC // 2, tq), F32)],
        compiler_params=_params("arbitrary", "arbitrary"),
        name="nsa_attention",
    )(nq, ng, kc, vc_t, nks, nvs_t, nkw, nvw_t, cover_t, expand, gn)


def _outproj_kernel(ya_ref, yb_ref, yc_ref, x_ref, g_ref, wa_ref, wb_ref, wc_ref, lng_ref, lnb_ref, o_ref):
    y = jnp.dot(ya_ref[...], wa_ref[...], preferred_element_type=F32)
    y += jnp.dot(yb_ref[...], wb_ref[...], preferred_element_type=F32)
    y += jnp.dot(yc_ref[...], wc_ref[...], preferred_element_type=F32)
    z = ALPHA * x_ref[...] + (1.0 + g_ref[...]) * y
    o_ref[...] = _layer_norm_rows(z, lng_ref[...], lnb_ref[...])


def _pad_head_rows(w, n_head, dh):
    d = w.shape[1]
    return jnp.pad(w.reshape(n_head, dh, d), ((0, 0), (0, LANE - dh), (0, 0))).reshape(n_head * LANE, d)


def _output_projection(ya, yb, yc, x, g, wa, wb, wc, ln_g, ln_b):
    bsz, seq, d = x.shape
    tm = 512
    na, nb, nc = ya.shape[-1], yb.shape[-1], yc.shape[-1]
    row = lambda n: pl.BlockSpec((None, tm, n), lambda b, i: (b, i, 0))
    vec = pl.BlockSpec((None, 1, d), lambda b, i: (b, 0, 0))
    full = lambda r, c: pl.BlockSpec((r, c), lambda b, i: (0, 0))
    return pl.pallas_call(
        _outproj_kernel,
        grid=(bsz, seq // tm),
        in_specs=[row(na), row(nb), row(nc), row(d), vec, full(na, d), full(nb, d), full(nc, d), full(1, d), full(1, d)],
        out_specs=row(d),
        out_shape=jax.ShapeDtypeStruct((bsz, seq, d), F32),
        compiler_params=_params("arbitrary", "arbitrary"),
        name="output_projection_ln",
    )(ya, yb, yc, x, g, wa, wb, wc, ln_g.reshape(1, d), ln_b.reshape(1, d))


def _tile_gate_up(wg, wu, tf):
    *lead, d, ff = wg.shape
    split = lambda w: jnp.moveaxis(w.reshape(*lead, d, ff // tf, tf), -2, -3)
    return jnp.concatenate([split(wg), split(wu)], axis=-1)


def _swiglu_chunk(u, wgu_ref, wd_ref):
    tf = wd_ref.shape[0]
    ab = jnp.dot(u, wgu_ref[...], preferred_element_type=F32)
    a, b = ab[:, :tf], ab[:, tf:]
    return jnp.dot((a * jax.nn.sigmoid(a) * b).astype(BF16), wd_ref[...], preferred_element_type=F32)


def _ffn_kernel(x_ref, sc_ref, sh_ref, g_ref, wgu_ref, wd_ref, lng_ref, lnb_ref, o_ref, u_scr, acc_scr):
    f = pl.program_id(2)

    @pl.when(f == 0)
    def _():
        u_scr[...] = (x_ref[...] * (1.0 + sc_ref[...]) + sh_ref[...]).astype(BF16)
        acc_scr[...] = jnp.zeros_like(acc_scr)

    acc_scr[...] += _swiglu_chunk(u_scr[...], wgu_ref, wd_ref)

    @pl.when(f == pl.num_programs(2) - 1)
    def _():
        z = ALPHA * x_ref[...] + (1.0 + g_ref[...]) * acc_scr[...]
        o_ref[...] = _layer_norm_rows(z, lng_ref[...], lnb_ref[...])


def _dense_ffn(x, sc, sh, g, wg, wu, wd, ln_g, ln_b):
    bsz, seq, d = x.shape
    ff = wg.shape[1]
    tm, tf = 1024, 256
    row = pl.BlockSpec((None, tm, d), lambda b, i, f: (b, i, 0))
    vec = pl.BlockSpec((None, 1, d), lambda b, i, f: (b, 0, 0))
    one = pl.BlockSpec((1, d), lambda b, i, f: (0, 0))
    return pl.pallas_call(
        _ffn_kernel,
        grid=(bsz, seq // tm, ff // tf),
        in_specs=[row, vec, vec, vec,
                  pl.BlockSpec((None, d, 2 * tf), lambda b, i, f: (f, 0, 0)),
                  pl.BlockSpec((tf, d), lambda b, i, f: (f, 0)),
                  one, one],
        out_specs=row,
        out_shape=jax.ShapeDtypeStruct((bsz, seq, d), F32),
        scratch_shapes=[pltpu.VMEM((tm, d), BF16), pltpu.VMEM((tm, d), F32)],
        compiler_params=_params("arbitrary", "arbitrary", "arbitrary"),
        name="dense_swiglu_ln",
    )(x, sc, sh, g, _tile_gate_up(wg, wu, tf), wd, ln_g.reshape(1, d), ln_b.reshape(1, d))


def _router_kernel(x_ref, sc_ref, sh_ref, r_ref, lo_ref, up_ref, gate_ref, slot_ref, slot_t_ref, cnt_ref):
    u = x_ref[...] * (1.0 + sc_ref[...]) + sh_ref[...]
    logits = jnp.dot(u, r_ref[...], preferred_element_type=F32, precision=lax.Precision.HIGHEST)
    lane = lax.broadcasted_iota(jnp.int32, logits.shape, 1)
    neg = -jnp.inf
    l1 = jnp.where(lane < N_EXPERTS, logits, neg)
    m1 = jnp.max(l1, -1, keepdims=True)
    i1 = jnp.min(jnp.where(l1 == m1, lane, LANE), -1, keepdims=True)
    l2 = jnp.where(lane == i1, neg, l1)
    m2 = jnp.max(l2, -1, keepdims=True)
    i2 = jnp.min(jnp.where(l2 == m2, lane, LANE), -1, keepdims=True)
    e2 = jnp.exp(m2 - m1)
    w1 = 1.0 / (1.0 + e2)
    w2 = e2 / (1.0 + e2)
    gate_ref[...] = jnp.where(lane == i1, w1, jnp.where(lane == i2, w2, 0.0))
    routed = jnp.where((lane == i1) | (lane == i2), 1.0, 0.0)
    before = jnp.dot(lo_ref[...], routed.astype(BF16), preferred_element_type=F32)
    slot_ref[...] = jnp.where(routed > 0.5, before, -1.0)
    routed_t = routed.T
    before_t = jnp.dot(routed_t.astype(BF16), up_ref[...], preferred_element_type=F32)
    slot_t_ref[...] = jnp.where(routed_t > 0.5, before_t, -1.0)
    cnt_ref[...] = jnp.broadcast_to(jnp.sum(routed, axis=0, keepdims=True), cnt_ref.shape).astype(jnp.int32)


MOE_TM = 1024
MOE_ROWS = 320
MOE_TF = 896


def _moe_router(x, sc, sh, router):
    bsz, seq, d = x.shape
    tm = MOE_TM
    r = jnp.pad(router, ((0, 0), (0, LANE - router.shape[1])))
    upper = np.triu(np.ones((tm, tm), np.float32), 1)
    up, lo = jnp.asarray(upper, BF16), jnp.asarray(upper.T, BF16)
    row = lambda n: pl.BlockSpec((None, tm, n), lambda b, i: (b, i, 0))
    vec = pl.BlockSpec((None, 1, d), lambda b, i: (b, 0, 0))
    full = lambda a, c: pl.BlockSpec((a, c), lambda b, i: (0, 0))
    n_tile = seq // tm
    return pl.pallas_call(
        _router_kernel,
        grid=(bsz, n_tile),
        in_specs=[row(d), vec, vec, full(d, LANE), full(tm, tm), full(tm, tm)],
        out_specs=[row(LANE), row(LANE), pl.BlockSpec((None, None, LANE, tm), lambda b, i: (b, i, 0, 0)),
                   pl.BlockSpec((None, None, 8, LANE), lambda b, i: (b, i, 0, 0))],
        out_shape=[jax.ShapeDtypeStruct((bsz, seq, LANE), F32), jax.ShapeDtypeStruct((bsz, seq, LANE), F32),
                   jax.ShapeDtypeStruct((bsz, n_tile, LANE, tm), F32), jax.ShapeDtypeStruct((bsz, n_tile, 8, LANE), jnp.int32)],
        compiler_params=_params("arbitrary", "arbitrary"),
        name="moe_router",
    )(x, sc, sh, r, lo, up)


def _moe_kernel(cnt_ref, x_ref, sc_ref, sh_ref, g_ref, gate_ref, slot_ref, slot_t_ref, wgu_ref, wd_ref, lng_ref, lnb_ref,
                o_ref, u_scr, xg_scr, acc_scr):
    tm, rows = MOE_TM, MOE_ROWS
    e = pl.program_id(2)
    f = pl.program_id(3)
    tile = pl.program_id(0) * pl.num_programs(1) + pl.program_id(1)
    n_pass = (cnt_ref[tile * N_EXPERTS + e] + rows - 1) // rows

    @pl.when((e == 0) & (f == 0))
    def _():
        u_scr[...] = (x_ref[...] * (1.0 + sc_ref[...]) + sh_ref[...]).astype(BF16)
        o_ref[...] = jnp.zeros_like(o_ref)

    @pl.when(f == 0)
    def _():
        slot_row = slot_t_ref[pl.ds(e, 1), :]

        def gather(p, carry):
            want = p * rows + lax.broadcasted_iota(jnp.int32, (rows, 1), 0)
            pick = jnp.where(slot_row == want.astype(F32), 1.0, 0.0).astype(BF16)
            xg_scr[p] = jnp.dot(pick, u_scr[...], preferred_element_type=F32).astype(BF16)
            acc_scr[p] = jnp.zeros((rows, x_ref.shape[-1]), F32)
            return carry

        lax.fori_loop(0, n_pass, gather, 0)

    def ffn(p, carry):
        acc_scr[p] += _swiglu_chunk(xg_scr[p], wgu_ref, wd_ref)
        return carry

    lax.fori_loop(0, n_pass, ffn, 0)

    @pl.when(f == pl.num_programs(3) - 1)
    def _():
        lane = lax.broadcasted_iota(jnp.int32, (tm, LANE), 1)
        slot_col = jnp.sum(jnp.where(lane == e, slot_ref[...], 0.0), axis=1, keepdims=True)
        gate_col = jnp.sum(jnp.where(lane == e, gate_ref[...], 0.0), axis=1, keepdims=True)

        def scatter(p, carry):
            want = p * rows + lax.broadcasted_iota(jnp.int32, (1, rows), 1)
            put = jnp.where(slot_col == want.astype(F32), 1.0, 0.0).astype(BF16)
            y = acc_scr[p]
            hi = y.astype(BF16)
            lo = (y - hi.astype(F32)).astype(BF16)
            back = jnp.dot(put, hi, preferred_element_type=F32) + jnp.dot(put, lo, preferred_element_type=F32)
            o_ref[...] += gate_col * back
            return carry

        lax.fori_loop(0, n_pass, scatter, 0)

    @pl.when((e == pl.num_programs(2) - 1) & (f == pl.num_programs(3) - 1))
    def _():
        z = ALPHA * x_ref[...] + (1.0 + g_ref[...]) * o_ref[...]
        o_ref[...] = _layer_norm_rows(z, lng_ref[...], lnb_ref[...])


def _moe_ffn(x, sc, sh, g, routing, wg, wu, wd, ln_g, ln_b):
    gate, slot, slot_t, cnt = routing
    bsz, seq, d = x.shape
    n_e, _, ff = wg.shape
    tm, tf, rows = MOE_TM, MOE_TF, MOE_ROWS
    max_pass = -(-tm // rows)
    counts = cnt[:, :, 0, :n_e].reshape(-1)
    row = lambda n: pl.BlockSpec((None, tm, n), lambda b, i, e, f, c: (b, i, 0))
    vec = pl.BlockSpec((None, 1, d), lambda b, i, e, f, c: (b, 0, 0))
    one = pl.BlockSpec((1, d), lambda b, i, e, f, c: (0, 0))
    grid_spec = pltpu.PrefetchScalarGridSpec(
        num_scalar_prefetch=1,
        grid=(bsz, seq // tm, n_e, ff // tf),
        in_specs=[row(d), vec, vec, vec, row(LANE), row(LANE),
                  pl.BlockSpec((None, None, LANE, tm), lambda b, i, e, f, c: (b, i, 0, 0)),
                  pl.BlockSpec((None, None, d, 2 * tf), lambda b, i, e, f, c: (e, f, 0, 0)),
                  pl.BlockSpec((None, tf, d), lambda b, i, e, f, c: (e, f, 0)),
                  one, one],
        out_specs=row(d),
        scratch_shapes=[pltpu.VMEM((tm, d), BF16), pltpu.VMEM((max_pass, rows, d), BF16), pltpu.VMEM((max_pass, rows, d), F32)],
    )
    return pl.pallas_call(
        _moe_kernel,
        grid_spec=grid_spec,
        out_shape=jax.ShapeDtypeStruct((bsz, seq, d), F32),
        compiler_params=_params("arbitrary", "arbitrary", "arbitrary", "arbitrary"),
        name="moe_swiglu_ln",
    )(counts, x, sc, sh, g, gate, slot, slot_t, _tile_gate_up(wg, wu, tf), wd, ln_g.reshape(1, d), ln_b.reshape(1, d))


ML_TT = 256
ML_SUB = 128


ML_NB = 1


def _mlstm_kernel(qk_ref, tail_ref, v_ref, og_ref, g_ref, cw_ref, cb_ref, gb_ref, gn_ref, y_ref,
                  ct_scr, n_scr, m_scr, q_scr, k_scr):
    @pl.when(pl.program_id(1) == 0)
    def _():
        ct_scr[...] = jnp.zeros_like(ct_scr)
        n_scr[...] = jnp.zeros_like(n_scr)
        m_scr[...] = jnp.zeros_like(m_scr)

    for nb in range(ML_NB):
        _mlstm_rows(qk_ref.at[nb], tail_ref.at[nb], v_ref.at[nb], og_ref.at[nb], g_ref.at[nb], cw_ref, cb_ref, gb_ref, gn_ref,
                    y_ref.at[nb], ct_scr.at[nb], n_scr.at[nb], m_scr.at[nb], q_scr.at[nb], k_scr.at[nb])


def _mlstm_rows(qk_ref, tail_ref, v_ref, og_ref, g_ref, cw_ref, cb_ref, gb_ref, gn_ref, y_ref,
                ct_scr, n_scr, m_scr, q_scr, k_scr):
    tt = ML_TT
    step = pl.program_id(1)

    x = qk_ref[...]
    tail = jnp.where(step == 0, 0.0, tail_ref[...])
    row8 = lax.broadcasted_iota(jnp.int32, (8, 1), 0)
    pre = x * cw_ref[CONV_W - 1:CONV_W, :] + cb_ref[...]
    for s in range(1, CONV_W):
        rolled = pltpu.roll(x, s, 0)
        head = jnp.where(row8 < s, pltpu.roll(tail, s, 0), rolled[0:8])
        pre = pre + jnp.concatenate([head, rolled[8:]], axis=0) * cw_ref[CONV_W - 1 - s:CONV_W - s, :]
    act = pre * jax.nn.sigmoid(pre)
    q_scr[...] = act[:, 0:ML_D].astype(BF16)
    k_scr[...] = (act[:, ML_D:2 * ML_D] * (DH_B ** -0.5)).astype(BF16)

    lane = lax.broadcasted_iota(jnp.int32, (1, LANE), 1)
    tok = lane % CHUNK
    jj = lax.broadcasted_iota(jnp.int32, (CHUNK, CHUNK), 0)
    ss = lax.broadcasted_iota(jnp.int32, (CHUNK, CHUNK), 1)
    nt = (((1,), (1,)), ((), ()))
    tn = (((0,), (0,)), ((), ()))
    ct_state = [ct_scr[h] for h in range(H_B)]
    n_state = [n_scr[h:h + 1, :] for h in range(H_B)]
    m_state = [m_scr[h:h + 1, 0:1] for h in range(H_B)]
    for sub in range(tt // ML_SUB):
        r0 = sub * ML_SUB
        gp = g_ref[r0:r0 + ML_SUB, :] + gb_ref[...]
        lsig = jnp.minimum(gp, 0.0) - jnp.log(1.0 + jnp.exp(-jnp.abs(gp)))
        col = jnp.where(lane < H_B, gp, lsig)
        rowl = col.T
        b = rowl[0:8]
        for sft in (1, 2, 4, 8, 16, 32):
            b = b + jnp.where(tok >= sft, pltpu.roll(b, sft, 1), 0.0)
        bcol = jnp.concatenate([b, jnp.zeros((LANE - 8, ML_SUB), F32)], axis=0).T
        for ci in range(ML_SUB // CHUNK):
            c0 = ci * CHUNK
            rows = slice(r0 + c0, r0 + c0 + CHUNK)
            for h in range(H_B):
                hs = slice(h * DH_B, (h + 1) * DH_B)
                b_col = bcol[c0:c0 + CHUNK, H_B + h:H_B + h + 1]
                ig_col = col[c0:c0 + CHUNK, h:h + 1]
                b_row = b[H_B + h:H_B + h + 1, c0:c0 + CHUNK]
                ig_row = rowl[h:h + 1, c0:c0 + CHUNK]
                qh, kh, vh = q_scr[rows, hs], k_scr[rows, hs], v_ref[rows, hs]
                log_d = jnp.where(jj >= ss, b_col - b_row + ig_row, NEG)
                m_loc = jnp.max(log_d, -1, keepdims=True)
                s_loc = lax.dot_general(qh, kh, nt, preferred_element_type=F32) * jnp.exp(log_d - m_loc)
                sv_loc = jnp.dot(s_loc.astype(BF16), vh, preferred_element_type=F32)
                rs_loc = jnp.sum(s_loc, -1, keepdims=True)
                b_last = b_row[:, CHUNK - 1:CHUNK]
                w_max = jnp.max(b_last - b_row + ig_row, -1, keepdims=True)
                w_loc = jnp.exp(b_last - b_col + ig_col - w_max)
                inc_c = lax.dot_general(kh, (w_loc * vh.astype(F32)).astype(BF16), tn, preferred_element_type=F32)
                inc_n = jnp.sum(w_loc * kh.astype(F32), axis=0, keepdims=True)
                m_old, ct, n_row = m_state[h], ct_state[h], n_state[h]
                log_inter = b_col + m_old
                m_out = jnp.maximum(log_inter, m_loc)
                w_inter = jnp.exp(log_inter - m_out)
                w_intra = jnp.exp(m_loc - m_out)
                num = w_inter * jnp.dot(qh, ct.astype(BF16), preferred_element_type=F32) + w_intra * sv_loc
                den = w_inter * jnp.sum(qh.astype(F32) * n_row, -1, keepdims=True) + w_intra * rs_loc
                hid = num / jnp.maximum(jnp.abs(den), jnp.exp(-m_out))
                m_new = jnp.maximum(b_last + m_old, w_max)
                decay = jnp.exp(b_last + m_old - m_new)
                grow = jnp.exp(w_max - m_new)
                ct_state[h] = decay * ct + grow * inc_c
                n_state[h] = decay * n_row + grow * inc_n
                m_state[h] = m_new
                ms = jnp.mean(hid * hid, -1, keepdims=True)
                y = hid * lax.rsqrt(ms + 1e-6) * gn_ref[:, hs] * jax.nn.sigmoid(og_ref[rows, hs])
                y_ref[rows, hs] = y.astype(y_ref.dtype)
    for h in range(H_B):
        ct_scr[h] = ct_state[h]
        n_scr[h:h + 1, :] = n_state[h]
        m_scr[h:h + 1, :] = jnp.broadcast_to(m_state[h], (1, LANE))


def _mlstm(mqk, mv, mo, mg, conv_w, conv_b, gate_b, g_b):
    bsz, seq, _ = mqk.shape
    tt = ML_TT
    gb = jnp.pad(gate_b.reshape(1, 2 * H_B), ((0, 0), (0, LANE - 2 * H_B)))
    nb = ML_NB
    row = lambda n: pl.BlockSpec((nb, tt, n), lambda b, i: (b, i, 0))
    full = lambda r, c: pl.BlockSpec((r, c), lambda b, i: (0, 0))
    tail = pl.BlockSpec((nb, 8, 2 * ML_D), lambda b, i: (b, jnp.maximum(i * (tt // 8) - 1, 0), 0))
    return pl.pallas_call(
        _mlstm_kernel,
        grid=(bsz // nb, seq // tt),
        in_specs=[row(2 * ML_D), tail, row(ML_D), row(ML_D), row(LANE),
                  full(CONV_W, 2 * ML_D), full(1, 2 * ML_D), full(1, LANE), full(1, ML_D)],
        out_specs=row(ML_D),
        out_shape=jax.ShapeDtypeStruct((bsz, seq, ML_D), BF16),
        scratch_shapes=[pltpu.VMEM((nb, H_B, DH_B, DH_B), F32), pltpu.VMEM((nb, 8, DH_B), F32), pltpu.VMEM((nb, 8, LANE), F32),
                        pltpu.VMEM((nb, tt, ML_D), BF16), pltpu.VMEM((nb, tt, ML_D), BF16)],
        compiler_params=_params("arbitrary", "arbitrary"),
        name="mlstm_scan",
    )(mqk, mqk, mv, mo, mg, conv_w, conv_b.reshape(1, 2 * ML_D), gb, g_b.reshape(1, ML_D))


def _mixers(pa, mls, nsa, rope, w_uk, w_uv, kv_norm, conv_w, conv_b, gate_b, cmp_pos, cmp_w1, cmp_w2, grp_norm):
    g_a, g_b, g_c = jnp.split(grp_norm, [H_A * DH_V, H_A * DH_V + H_B * DH_B])
    qc, iq, iw, kc_a, ik, ct_a = _dsa_prep(pa, rope[0], rope[1], w_uk, kv_norm)
    y_a = _dsa_attention(qc, iq, iw, kc_a, ct_a, ik, w_uv, g_a)
    y_b = _mlstm(*mls, conv_w, conv_b, gate_b, g_b)
    nq, ncmp, nks, nvs, nkw, nvw, ng = nsa
    kc, vc = _nsa_compress(ncmp, cmp_pos, cmp_w1, cmp_w2)
    y_c = _nsa_attention(nq, ng, kc, vc, nks, nvs, nkw, nvw, g_c)
    return y_a, y_b, y_c


def _pad_cols(w, n):
    return jnp.pad(w, ((0, 0), (0, n - w.shape[1])))


def kernel(x, c, positions, w_mod, b_mod, w_in, dsa_w_uk, dsa_w_uv, dsa_kv_norm, mlstm_conv_w, mlstm_conv_b, mlstm_gate_b, nsa_cmp_pos, nsa_cmp_w1, nsa_cmp_w2, grp_norm, w_out, ln_g, ln_b, ffn_w_gate, ffn_w_up, ffn_w_down, moe_router, moe_w_gate, moe_w_up, moe_w_down):
    bsz = x.shape[0]
    mod = _modulation(c, w_mod, b_mod).reshape(bsz, DEPTH, N_MOD, 1, D_MODEL)
    rope = _rope_table(positions)
    for l in range(DEPTH):
        sh1, sc1, g1, sh2, sc2, g2 = [mod[:, l, j] for j in range(N_MOD)]
        w = w_in[l].astype(BF16)
        wa = _dsa_weight_layout(w[:, :N_GROUP_A])
        wb = _mlstm_weight_layout(w[:, N_GROUP_A:N_GROUP_A + N_GROUP_B])
        wc, wvt = _nsa_weight_layout(w[:, N_GROUP_A + N_GROUP_B:])
        pa, mqk, mv, mo, mg, *nsa = _input_projection(x, sc1, sh1, wa, wb, wc, wvt)
        ya, yb, yc = _mixers(pa, (mqk, mv, mo, mg), nsa, rope, dsa_w_uk[l], dsa_w_uv[l], dsa_kv_norm[l], mlstm_conv_w[l], mlstm_conv_b[l], mlstm_gate_b[l], nsa_cmp_pos[l], nsa_cmp_w1[l], nsa_cmp_w2[l], grp_norm[l])
        wo = w_out[l].astype(BF16)
        n_a, n_b = H_A * DH_V, H_B * DH_B
        x = _output_projection(ya, yb, yc, x, g1, _pad_head_rows(wo[:n_a], H_A, DH_V), wo[n_a:n_a + n_b], _pad_head_rows(wo[n_a + n_b:], H_C, DH_C), ln_g[l, 0], ln_b[l, 0])
        if l % 2 == 0:
            k = l // 2
            x = _dense_ffn(x, sc2, sh2, g2, ffn_w_gate[k].astype(BF16), ffn_w_up[k].astype(BF16), ffn_w_down[k].astype(BF16), ln_g[l, 1], ln_b[l, 1])
        else:
            k = l // 2
            routing = _moe_router(x, sc2, sh2, moe_router[k])
            x = _moe_ffn(x, sc2, sh2, g2, routing,moe_w_gate[k].astype(BF16), moe_w_up[k].astype(BF16), moe_w_down[k].astype(BF16), ln_g[l, 1], ln_b[l, 1])
    return x
```

```python
import functools

import numpy as np
import jax
import jax.numpy as jnp
from jax import lax
from jax.experimental import pallas as pl
from jax.experimental.pallas import tpu as pltpu

F32 = jnp.float32
BF16 = jnp.bfloat16

D_MODEL = 1024
DEPTH = 2
H_A, DH_NOPE, DH_ROPE, D_C, DH_V, H_I, D_I = 4, 64, 32, 128, 64, 4, 64
K_SEL_MAX = 256
ROPE_THETA = 10000.0
H_B, DH_B, CONV_W, CHUNK = 4, 128, 4, 64
H_C, DH_C, L_CMP, D_STRIDE, CMP_HIDDEN, L_SEL, N_TOP_MAX, WINDOW, Q_BLOCK = 4, 64, 32, 16, 128, 64, 16, 512, 128
D_FF = 2816
N_EXPERTS = 8
D_FF_EXPERT = 3584
N_MOD = 6
ALPHA = (2 * DEPTH) ** 0.25
SPLIT_SIZES = (H_A * DH_NOPE, H_A * DH_ROPE, D_C, DH_ROPE, H_I * D_I, D_I, H_I, H_B * DH_B, H_B * DH_B, H_B * DH_B, H_B, H_B, H_B * DH_B, H_C * DH_C, DH_C, DH_C, DH_C, DH_C, DH_C, DH_C, 3 * H_C)
N_GROUP_A = sum(SPLIT_SIZES[:7])
N_GROUP_B = sum(SPLIT_SIZES[7:13])
N_GROUP_C = sum(SPLIT_SIZES[13:])

LOG2E = 1.4426950408889634
LANE = 128
VMEM_LIMIT = 56 * 1024 * 1024


def _round_up(n, m):
    return (n + m - 1) // m * m


def _params(*sem):
    return pltpu.CompilerParams(dimension_semantics=sem, vmem_limit_bytes=VMEM_LIMIT)


FOLD_ROWS = 64


def _fold_rows(x, op):
    parts = [x[i:i + FOLD_ROWS] for i in range(0, x.shape[0], FOLD_ROWS)]
    while len(parts) > 1:
        parts = [op(parts[i], parts[i + 1]) if i + 1 < len(parts) else parts[i] for i in range(0, len(parts), 2)]
    return parts[0]


def _attend_block(s_buf, b_buf, v_t, m, l, acc_scr, n_head, tq):
    bias = b_buf[...]
    m_new, l_new, ps = [], [], []
    for h in range(n_head):
        hs = slice(h * tq, (h + 1) * tq)
        sh = s_buf[:, hs] + bias
        mh = jnp.maximum(m[:, hs], jnp.max(_fold_rows(sh, jnp.maximum), axis=0, keepdims=True))
        ph = jnp.exp2(sh - mh)
        l_new.append(jnp.exp2(m[:, hs] - mh) * l[:, hs] + jnp.sum(_fold_rows(ph, jnp.add), axis=0, keepdims=True))
        m_new.append(mh)
        ps.append(ph.astype(BF16))
    m_new = jnp.concatenate(m_new, axis=1)
    acc_scr[...] = jnp.exp2(m - m_new) * acc_scr[...] + jnp.dot(v_t, jnp.concatenate(ps, axis=1), preferred_element_type=F32)
    return m_new, jnp.concatenate(l_new, axis=1)


PLANE_KEYS = 256


def _bit_transpose32(x):
    w = [x[8 * i:8 * i + 8, :] for i in range(32)]
    j, m = 16, 0x0000FFFF
    while j:
        k = 0
        while k < 32:
            t = (w[k] ^ lax.shift_right_logical(w[k + j], j)) & m
            w[k] = w[k] ^ t
            w[k + j] = w[k + j] ^ jnp.left_shift(t, j)
            k = (k + j + 1) & ~j
        j >>= 1
        m ^= (m << j) & 0xFFFFFFFF
    return w


def _layer_norm_rows(z, g, b):
    mu = jnp.mean(z, -1, keepdims=True)
    zc = z - mu
    var = jnp.mean(zc * zc, -1, keepdims=True)
    return zc * lax.rsqrt(var + 1e-5) * g + b


def _mod_kernel(c_ref, w_ref, b_ref, o_ref):
    c = c_ref[...]
    a = c * jax.nn.sigmoid(c)
    o_ref[...] = jnp.dot(a, w_ref[...], preferred_element_type=F32, precision=lax.Precision.HIGHEST) + b_ref[...]


def _modulation(c, w_mod, b_mod):
    bsz, d = c.shape
    n = w_mod.shape[1]
    tn = 1024
    return pl.pallas_call(
        _mod_kernel,
        grid=(n // tn,),
        in_specs=[pl.BlockSpec((bsz, d), lambda j: (0, 0)),
                  pl.BlockSpec((d, tn), lambda j: (0, j)),
                  pl.BlockSpec((1, tn), lambda j: (0, j))],
        out_specs=pl.BlockSpec((bsz, tn), lambda j: (0, j)),
        out_shape=jax.ShapeDtypeStruct((bsz, n), F32),
        compiler_params=_params("arbitrary"),
        name="adaln_mod",
    )(c, w_mod, b_mod.reshape(1, n))


NSA_Q0 = 0
NSA_CMP0 = H_C * LANE
NSA_KS0 = NSA_CMP0 + LANE
NSA_KW0 = NSA_KS0 + LANE
NSA_G0 = NSA_KW0 + LANE
NSA_COLS = NSA_G0 + LANE


def _nsa_weight_layout(w):
    d = w.shape[0]
    nq, nkc, nvc, nks, nvs, nkw, nvw, ng = jnp.split(w, np.cumsum(SPLIT_SIZES[13:])[:-1].tolist(), axis=1)
    z = lambda n: jnp.zeros((d, n), w.dtype)
    half = LANE - DH_C
    cols = []
    for h in range(H_C):
        cols += [nq[:, h * DH_C:(h + 1) * DH_C], z(half)]
    cols += [nkc, nvc, nks, z(half), nkw, z(half), ng, z(LANE - 3 * H_C)]
    values_t = jnp.concatenate([nvs, z(half), nvw, z(half)], axis=1).T
    return jnp.concatenate(cols, axis=1), values_t


def _mlstm_weight_layout(w):
    mq, mk, mv, mi, mf, mo = jnp.split(w, np.cumsum(SPLIT_SIZES[7:13])[:-1].tolist(), axis=1)
    return jnp.concatenate([mq, mk, mv, mo, mi, mf, jnp.zeros((w.shape[0], LANE - 2 * H_B), w.dtype)], axis=1)


ML_D = H_B * DH_B


def _inproj_kernel(x_ref, sc_ref, sh_ref, wa_ref, wb_ref, wc_ref, wvt_ref, oa_ref, mqk_ref, mv_ref, mo_ref, mg_ref,
                   nq_ref, ncmp_ref, nks_ref, nvs_ref, nkw_ref, nvw_ref, ng_ref):
    u = (x_ref[...] * (1.0 + sc_ref[...]) + sh_ref[...]).astype(BF16)
    oa_ref[...] = jnp.dot(u, wa_ref[...], preferred_element_type=F32)
    ob = jnp.dot(u, wb_ref[...], preferred_element_type=F32)
    mqk_ref[...] = ob[:, 0:2 * ML_D]
    mv_ref[...] = ob[:, 2 * ML_D:3 * ML_D].astype(BF16)
    mo_ref[...] = ob[:, 3 * ML_D:4 * ML_D]
    mg_ref[...] = ob[:, 4 * ML_D:4 * ML_D + LANE]
    oc = jnp.dot(u, wc_ref[...], preferred_element_type=F32)
    nq_ref[...] = (oc[:, NSA_Q0:NSA_Q0 + H_C * LANE] * (DH_C ** -0.5 * LOG2E)).astype(BF16)
    ncmp_ref[...] = oc[:, NSA_CMP0:NSA_CMP0 + LANE]
    nks_ref[...] = oc[:, NSA_KS0:NSA_KS0 + LANE].astype(BF16)
    nkw_ref[...] = oc[:, NSA_KW0:NSA_KW0 + LANE].astype(BF16)
    ng_ref[...] = oc[:, NSA_G0:NSA_G0 + LANE]
    vt = lax.dot_general(wvt_ref[...], u, (((1,), (1,)), ((), ())), preferred_element_type=F32)
    nvs_ref[...] = vt[0:LANE].astype(BF16)
    nvw_ref[...] = vt[LANE:2 * LANE].astype(BF16)


def _input_projection(x, sc, sh, wa, wb, wc, wvt):
    bsz, seq, d = x.shape
    tm = 512
    na, nb, nc = wa.shape[1], wb.shape[1], wc.shape[1]
    row = lambda n: pl.BlockSpec((None, tm, n), lambda b, i: (b, i, 0))
    col = pl.BlockSpec((None, LANE, tm), lambda b, i: (b, 0, i))
    vec = pl.BlockSpec((None, 1, d), lambda b, i: (b, 0, 0))
    full = lambda n: pl.BlockSpec((d, n), lambda b, i: (0, 0))
    tok = lambda n, dt: (row(n), jax.ShapeDtypeStruct((bsz, seq, n), dt))
    feat = (col, jax.ShapeDtypeStruct((bsz, LANE, seq), BF16))
    outs = [tok(na, F32), tok(2 * ML_D, F32), tok(ML_D, BF16), tok(ML_D, F32), tok(LANE, F32),
            tok(H_C * LANE, BF16), tok(LANE, F32), tok(LANE, BF16), feat, tok(LANE, BF16), feat, tok(LANE, F32)]
    return pl.pallas_call(
        _inproj_kernel,
        grid=(bsz, seq // tm),
        in_specs=[row(d), vec, vec, full(na), full(nb), full(nc), pl.BlockSpec((2 * LANE, d), lambda b, i: (0, 0))],
        out_specs=[spec for spec, _ in outs],
        out_shape=[shape for _, shape in outs],
        compiler_params=_params("arbitrary", "arbitrary"),
        name="input_projection",
    )(x, sc, sh, wa, wb, wc, wvt)


DSA_QN0 = 0
DSA_QR0 = H_A * LANE
DSA_CKV0 = DSA_QR0 + LANE
DSA_IQ0 = DSA_CKV0 + LANE
DSA_G0 = DSA_IQ0 + H_I * LANE
DSA_COLS = DSA_G0 + LANE
DSA_KR_LANE = D_I
DSA_IW_LANE = D_I + DH_ROPE
DSA_TQ = 128
DSA_KC = 512
INT_MIN = -2 ** 31
KEY_NEG_INF = int(np.int32(np.float32(-np.inf).view(np.int32)) ^ np.int32(0x7FFFFFFF))


def _dsa_weight_layout(w):
    d = w.shape[0]
    qn, qr, ckv, kr, iq, ik, iw = jnp.split(w, np.cumsum(SPLIT_SIZES[:7])[:-1].tolist(), axis=1)
    z = lambda n: jnp.zeros((d, n), w.dtype)
    cols = []
    for h in range(H_A):
        cols += [qn[:, h * DH_NOPE:(h + 1) * DH_NOPE], z(LANE - DH_NOPE)]
    cols += [qr, ckv]
    for h in range(H_I):
        cols += [iq[:, h * D_I:(h + 1) * D_I], z(LANE - D_I)]
    cols += [ik, kr, iw, z(LANE - D_I - DH_ROPE - H_I)]
    return jnp.concatenate(cols, axis=1)


def _rope_table_kernel(pos_ref, freq_ref, cos_ref, sin_ref):
    ang = pos_ref[...].astype(F32) * freq_ref[...]
    lane = lax.broadcasted_iota(jnp.int32, (1, LANE), 1)
    first = (lane % DH_ROPE) < DH_ROPE // 2
    cos_ref[...] = jnp.cos(ang)
    sin_ref[...] = jnp.where(first, -jnp.sin(ang), jnp.sin(ang))


def _rope_table(positions):
    bsz, seq = positions.shape
    tm = 512
    inv_freq = ROPE_THETA ** (-jnp.arange(0, DH_ROPE, 2, dtype=F32) / DH_ROPE)
    freq = jnp.tile(inv_freq, LANE // (DH_ROPE // 2)).reshape(1, LANE)
    out = pl.BlockSpec((None, tm, LANE), lambda b, i: (b, i, 0))
    return pl.pallas_call(
        _rope_table_kernel,
        grid=(bsz, seq // tm),
        in_specs=[pl.BlockSpec((None, tm, 1), lambda b, i: (b, i, 0)), pl.BlockSpec((1, LANE), lambda b, i: (0, 0))],
        out_specs=[out, out],
        out_shape=[jax.ShapeDtypeStruct((bsz, seq, LANE), F32)] * 2,
        compiler_params=_params("arbitrary", "arbitrary"),
        name="rope_table",
    )(positions.reshape(bsz, seq, 1), freq)


def _dsa_prep_kernel(pa_ref, cos_ref, sin_ref, wuk_ref, kvn_ref, qc_ref, iq_ref, iw_ref, kc_ref, ik_ref, ct_ref):
    scale = (DH_NOPE + DH_ROPE) ** -0.5 * LOG2E
    cos, sin = cos_ref[...], sin_ref[...]
    lane = lax.broadcasted_iota(jnp.int32, (1, LANE), 1)
    first = (lane % DH_ROPE) < DH_ROPE // 2
    rope_lanes = (lane >= DSA_KR_LANE) & (lane < DSA_KR_LANE + DH_ROPE)

    def rope(v):
        partner = jnp.where(first, pltpu.roll(v, LANE - DH_ROPE // 2, 1), pltpu.roll(v, DH_ROPE // 2, 1))
        return v * cos + partner * sin

    g = pa_ref[:, DSA_G0:DSA_G0 + LANE]
    ckv = pa_ref[:, DSA_CKV0:DSA_CKV0 + LANE]
    ckv_n = ckv * lax.rsqrt(jnp.mean(ckv * ckv, -1, keepdims=True) + 1e-6) * kvn_ref[...]
    kc_ref[:, 0:LANE] = ckv_n.astype(BF16)
    ct_ref[...] = ckv_n.T.astype(BF16)
    kc_ref[:, LANE:2 * LANE] = jnp.where(rope_lanes, rope(g), 0.0).astype(BF16)
    ik_ref[...] = jnp.where(lane < D_I, g, 0.0).astype(BF16)
    iw_ref[...] = g
    iq_ref[...] = pa_ref[:, DSA_IQ0:DSA_IQ0 + H_I * LANE].astype(BF16)
    qr = rope(pa_ref[:, DSA_QR0:DSA_QR0 + LANE]) * scale
    for h in range(H_A):
        qn = pa_ref[:, DSA_QN0 + h * LANE:DSA_QN0 + (h + 1) * LANE].astype(BF16)
        q_abs = jnp.dot(qn, wuk_ref[h], preferred_element_type=F32) * scale
        shift = (DSA_KR_LANE - DH_ROPE * h) % LANE
        qr_h = pltpu.roll(qr, shift, 1) if shift else qr
        qc_ref[:, 2 * h * LANE:(2 * h + 1) * LANE] = q_abs.astype(BF16)
        qc_ref[:, (2 * h + 1) * LANE:(2 * h + 2) * LANE] = jnp.where(rope_lanes, qr_h, 0.0).astype(BF16)


def _dsa_prep(pa, cos, sin, w_uk, kv_norm):
    bsz, seq, _ = pa.shape
    tm = 512
    wuk = jnp.pad(w_uk, ((0, 0), (0, LANE - DH_NOPE), (0, 0))).astype(BF16)
    row = lambda n: pl.BlockSpec((None, tm, n), lambda b, i: (b, i, 0))
    outs = [(2 * H_A * LANE, BF16), (H_I * LANE, BF16), (LANE, F32), (2 * LANE, BF16), (LANE, BF16)]
    return pl.pallas_call(
        _dsa_prep_kernel,
        grid=(bsz, seq // tm),
        in_specs=[row(DSA_COLS), row(LANE), row(LANE),
                  pl.BlockSpec((H_A, LANE, D_C), lambda b, i: (0, 0, 0)), pl.BlockSpec((1, D_C), lambda b, i: (0, 0))],
        out_specs=[row(n) for n, _ in outs] + [pl.BlockSpec((None, D_C, tm), lambda b, i: (b, 0, i))],
        out_shape=[jax.ShapeDtypeStruct((bsz, seq, n), dt) for n, dt in outs] + [jax.ShapeDtypeStruct((bsz, D_C, seq), BF16)],
        compiler_params=_params("arbitrary", "arbitrary"),
        name="dsa_prep",
    )(pa, cos, sin, wuk, kv_norm.reshape(1, D_C))


def _dsa_kernel(k_sel, qc_ref, iq_ref, iw_ref, kc_ref, ct_ref, ik_ref, tri_ref, wuv_ref, gn_ref, o_ref,
                key_scr, planes_scr, acc_scr, s0_scr, s1_scr, b0_scr, b1_scr):
    tq, kcs = DSA_TQ, DSA_KC
    t0 = pl.program_id(1) * tq
    n_chunk = (t0 + tq + kcs - 1) // kcs
    nt = (((1,), (1,)), ((), ()))
    qpos = t0 + lax.broadcasted_iota(jnp.int32, (1, tq), 1)
    iw_t = iw_ref[...].T

    def score_chunk(c, carry):
        k0 = pl.multiple_of(c * kcs, kcs)
        ikc = ik_ref[pl.ds(k0, kcs), :]
        sc = jnp.zeros((kcs, tq), F32)
        for h in range(H_I):
            lg = lax.dot_general(ikc, iq_ref[:, h * LANE:(h + 1) * LANE], nt, preferred_element_type=F32)
            sc = sc + jnp.maximum(lg, 0.0) * iw_t[DSA_IW_LANE + h:DSA_IW_LANE + h + 1, :]
        sc = jnp.where(sc == 0.0, 0.0, sc)
        kpos = k0 + lax.broadcasted_iota(jnp.int32, (kcs, 1), 0)
        sc = jnp.where(kpos <= qpos, sc, -jnp.inf)
        bits = pltpu.bitcast(sc, jnp.int32)
        key = jnp.where(bits < 0, bits ^ 0x7FFFFFFF, bits)
        key_scr[pl.ds(k0, kcs), :] = key
        ukey = key ^ INT_MIN
        for blk in range(kcs // PLANE_KEYS):
            words = _bit_transpose32(ukey[blk * PLANE_KEYS:(blk + 1) * PLANE_KEYS])
            row0 = pl.multiple_of((c * (kcs // PLANE_KEYS) + blk) * 8, 8)
            for r in range(32):
                planes_scr[r, pl.ds(row0, 8), :] = words[r]
        return carry

    lax.fori_loop(0, n_chunk, score_chunk, 0)

    n_words = planes_scr.shape[1]
    word_row = lax.broadcasted_iota(jnp.int32, (n_words, tq), 0)
    alive0 = jnp.where(word_row < n_chunk * (kcs // 32), jnp.int32(-1), jnp.int32(0))
    k_int = int(k_sel)

    def bit_pass(i, carry):
        thr_u, above, alive = carry
        ones = alive & planes_scr[i]
        seen_ones = above + jnp.sum(lax.population_count(ones), axis=0, keepdims=True)
        take = seen_ones >= k_int
        alive = jnp.where(take, ones, alive & ~planes_scr[i])
        above = jnp.where(take, above, seen_ones)
        thr_u = jnp.where(take, thr_u | jnp.left_shift(jnp.int32(1), 31 - i), thr_u)
        return thr_u, above, alive

    thr_u, above, _ = lax.fori_loop(0, 32, bit_pass, (jnp.zeros((1, tq), jnp.int32), jnp.zeros((1, tq), jnp.int32), alive0))
    thr = thr_u ^ INT_MIN
    room = (k_int - above).astype(F32)

    qall = jnp.concatenate([qc_ref[:, 2 * h * LANE:(2 * h + 2) * LANE] for h in range(H_A)], axis=0)

    room = jnp.where(thr > KEY_NEG_INF, room, 0.0)
    acc_scr[...] = jnp.zeros_like(acc_scr)

    half = kcs // 2

    def prepare(k0, seen, s_buf, b_buf):
        key = key_scr[pl.ds(k0, half), :]
        tie = jnp.where(key == thr, 1.0, 0.0)
        prefix = jnp.dot(tri_ref[...], tie.astype(BF16), preferred_element_type=F32)
        tie_bias = jnp.where(seen + prefix <= room, jnp.where(key == thr, 0.0, NEG), NEG)
        b_buf[...] = jnp.where(key > thr, 0.0, tie_bias)
        s_buf[...] = lax.dot_general(kc_ref[pl.ds(k0, half), :], qall, nt, preferred_element_type=F32)
        return seen + jnp.sum(_fold_rows(tie, jnp.add), axis=0, keepdims=True)

    def attn_chunk(c, carry):
        m, l, seen = carry
        k0 = pl.multiple_of(c * kcs, kcs)
        k1 = pl.multiple_of(k0 + half, half)
        seen = prepare(k1, seen, s1_scr, b1_scr)
        m, l = _attend_block(s0_scr, b0_scr, ct_ref[:, pl.ds(k0, half)], m, l, acc_scr, H_A, tq)
        k2 = pl.multiple_of(jnp.minimum(c + 1, n_chunk - 1) * kcs, kcs)
        seen = prepare(k2, seen, s0_scr, b0_scr)
        m, l = _attend_block(s1_scr, b1_scr, ct_ref[:, pl.ds(k1, half)], m, l, acc_scr, H_A, tq)
        return m, l, seen

    seen0 = prepare(0, jnp.zeros((1, tq), F32), s0_scr, b0_scr)
    init = (jnp.full((1, H_A * tq), M_FLOOR, F32), jnp.zeros((1, H_A * tq), F32), seen0)
    _, l, _ = lax.fori_loop(0, n_chunk, attn_chunk, init)
    o_lat = (acc_scr[...] / jnp.maximum(l, 1e-30)).astype(BF16)
    for h in range(H_A):
        o = lax.dot_general(o_lat[:, h * tq:(h + 1) * tq], wuv_ref[h], (((0,), (0,)), ((), ())),
                            preferred_element_type=F32)
        ms = jnp.sum(o * o, -1, keepdims=True) * (1.0 / DH_V)
        o_ref[:, h * LANE:(h + 1) * LANE] = (o * lax.rsqrt(ms + 1e-6) * gn_ref[h:h + 1, :]).astype(o_ref.dtype)


def _dsa_attention(qc, iq, iw, kc, ct, ik, w_uv, g_a):
    bsz, seq, _ = qc.shape
    k_sel = float(min(K_SEL_MAX, seq // 4))
    tq, kcs = DSA_TQ, DSA_KC
    half = kcs // 2
    tri = jnp.asarray(np.tril(np.ones((half, half), np.float32)), BF16)
    wuv = jnp.pad(w_uv, ((0, 0), (0, 0), (0, LANE - DH_V))).astype(BF16)
    gn = jnp.pad(g_a.reshape(H_A, DH_V), ((0, 0), (0, LANE - DH_V)))
    row = lambda n: pl.BlockSpec((None, tq, n), lambda b, i: (b, i, 0))
    per_b = lambda n: pl.BlockSpec((None, seq, n), lambda b, i: (b, 0, 0))
    return pl.pallas_call(
        functools.partial(_dsa_kernel, k_sel),
        grid=(bsz, seq // tq),
        in_specs=[row(2 * H_A * LANE), row(H_I * LANE), row(LANE), per_b(2 * LANE),
                  pl.BlockSpec((None, D_C, seq), lambda b, i: (b, 0, 0)), per_b(LANE),
                  pl.BlockSpec((half, half), lambda b, i: (0, 0)), pl.BlockSpec((H_A, D_C, LANE), lambda b, i: (0, 0, 0)),
                  pl.BlockSpec((H_A, LANE), lambda b, i: (0, 0))],
        out_specs=row(H_A * LANE),
        out_shape=jax.ShapeDtypeStruct((bsz, seq, H_A * LANE), BF16),
        scratch_shapes=[pltpu.VMEM((seq, tq), jnp.int32), pltpu.VMEM((32, seq // 32, tq), jnp.int32),
                        pltpu.VMEM((D_C, H_A * tq), F32),
                        pltpu.VMEM((half, H_A * tq), F32), pltpu.VMEM((half, H_A * tq), F32),
                        pltpu.VMEM((half, tq), F32), pltpu.VMEM((half, tq), F32)],
        compiler_params=_params("arbitrary", "arbitrary"),
        name="dsa_attention",
    )(qc, iq, iw, kc, ct, ik, tri, wuv, gn)


def _nsa_compress_kernel(a_ref, pos_ref, w1t_ref, w1b_ref, w2k_ref, w2v_ref, kc_ref, vc_ref):
    a = a_ref[...]
    top = jnp.dot((a + pos_ref[0:1, :]).astype(BF16), w1t_ref[...], preferred_element_type=F32)
    bot = jnp.dot((a + pos_ref[1:2, :]).astype(BF16), w1b_ref[...], preferred_element_type=F32)
    n = a.shape[0]
    pre = top + jnp.concatenate([bot[1:], bot[:1]], axis=0)
    h = (pre * jax.nn.sigmoid(pre)).astype(BF16)
    hid = w2k_ref.shape[0]
    kc_ref[...] = jnp.dot(h[:, :hid], w2k_ref[...], preferred_element_type=F32).astype(BF16)
    vc_ref[...] = lax.dot_general(w2v_ref[...], h[:, hid:], (((1,), (1,)), ((), ())), preferred_element_type=F32).astype(BF16)


def _nsa_compress(ncmp, cmp_pos, cmp_w1, cmp_w2):
    bsz, seq, _ = ncmp.shape
    n_grp = seq // D_STRIDE
    per = L_CMP // D_STRIDE
    width = D_STRIDE * LANE
    a = ncmp.reshape(bsz, n_grp, width)
    w1 = cmp_w1.reshape(2, per, D_STRIDE, DH_C, CMP_HIDDEN)
    zer = jnp.zeros((D_STRIDE, DH_C, CMP_HIDDEN), cmp_w1.dtype)

    def expand(p):
        wk = jnp.concatenate([w1[0, p], zer], axis=1)
        wv = jnp.concatenate([zer, w1[1, p]], axis=1)
        return jnp.concatenate([wk, wv], axis=2).reshape(width, 2 * CMP_HIDDEN).astype(BF16)

    pos = cmp_pos.reshape(2, per, D_STRIDE, DH_C)
    pos = jnp.concatenate([pos[0], pos[1]], axis=-1).reshape(per, width)
    pad_out = ((0, 0), (0, LANE - DH_C))
    w2k = jnp.pad(cmp_w2[0], pad_out).astype(BF16)
    w2v = jnp.pad(cmp_w2[1], pad_out).astype(BF16).T
    full = lambda r, c: pl.BlockSpec((r, c), lambda b: (0, 0))
    return pl.pallas_call(
        _nsa_compress_kernel,
        grid=(bsz,),
        in_specs=[pl.BlockSpec((None, n_grp, width), lambda b: (b, 0, 0)), full(per, width),
                  full(width, 2 * CMP_HIDDEN), full(width, 2 * CMP_HIDDEN), full(CMP_HIDDEN, LANE), full(LANE, CMP_HIDDEN)],
        out_specs=[pl.BlockSpec((None, n_grp, LANE), lambda b: (b, 0, 0)), pl.BlockSpec((None, LANE, n_grp), lambda b: (b, 0, 0))],
        out_shape=[jax.ShapeDtypeStruct((bsz, n_grp, LANE), BF16), jax.ShapeDtypeStruct((bsz, LANE, n_grp), BF16)],
        compiler_params=_params("arbitrary"),
        name="nsa_compress",
    )(a, pos, expand(0), expand(1), w2k, w2v)


NSA_TQ = 128
NSA_KC = 512
NEG = -1e30
M_FLOOR = -1e20


def _softmax_cols(s, bias, n_head, tq):
    out = []
    for h in range(n_head):
        sh = s[:, h * tq:(h + 1) * tq] + bias
        m = jnp.maximum(jnp.max(_fold_rows(sh, jnp.maximum), axis=0, keepdims=True), M_FLOOR)
        e = jnp.exp2(sh - m)
        den = jnp.sum(_fold_rows(e, jnp.add), axis=0, keepdims=True)
        out.append(e * (1.0 / jnp.maximum(den, 1e-30)))
    return out


def _nsa_kernel(q_ref, g_ref, kc_ref, vct_ref, ks_ref, vst_ref, kw_ref, vwt_ref, cover_ref, expand_ref, gn_ref, o_ref,
                acc_scr, s0_scr, s1_scr, b0_scr, b1_scr):
    tq, kc_sz = NSA_TQ, NSA_KC
    t0 = pl.program_id(1) * tq
    nt = (((1,), (1,)), ((), ()))
    q = q_ref[...]
    qa = jnp.concatenate([q[:, h * LANE:(h + 1) * LANE] for h in range(H_C)], axis=0)
    qpos = t0 + lax.broadcasted_iota(jnp.int32, (1, tq), 1)

    n_grp = kc_ref.shape[0]
    s_c = lax.dot_general(kc_ref[...], qa, nt, preferred_element_type=F32)
    n_idx = lax.broadcasted_iota(jnp.int32, (n_grp, 1), 0)
    visible = jnp.where(n_idx < n_grp - 1, n_idx * D_STRIDE + (L_CMP - 1), 2 ** 30)
    p_c = _softmax_cols(s_c, jnp.where(visible <= qpos, 0.0, NEG), H_C, tq)
    o_cmp = jnp.dot(vct_ref[...], jnp.concatenate([p.astype(BF16) for p in p_c], axis=1), preferred_element_type=F32)

    p_sum = p_c[0]
    for h in range(1, H_C):
        p_sum = p_sum + p_c[h]
    hi = p_sum.astype(BF16)
    lo = (p_sum - hi.astype(F32)).astype(BF16)
    cov = cover_ref[...]
    imp_t = jnp.dot(cov, hi, preferred_element_type=F32) + jnp.dot(cov, lo, preferred_element_type=F32)
    n_sel = cov.shape[0]
    jrow = lax.broadcasted_iota(jnp.int32, (n_sel, tq), 0)
    cur = (t0 + lax.broadcasted_iota(jnp.int32, (n_sel, tq), 1)) // L_SEL
    adm = jrow <= cur
    forced = (jrow == 0) | (jrow == cur) | (jrow == cur - 1)
    val = jnp.where(adm & forced, jnp.inf, jnp.where(adm, imp_t, -jnp.inf))
    rank = jnp.zeros((n_sel, tq), F32)
    for jp in range(n_sel):
        r = val[jp:jp + 1, :]
        rank = rank + jnp.where(r == val, jnp.where(jrow > jp, 1.0, 0.0), jnp.where(r > val, 1.0, 0.0))
    sel_t = jnp.where(rank < min(N_TOP_MAX, n_sel), jnp.where(val > -jnp.inf, 1.0, 0.0), 0.0)
    if n_sel < LANE:
        sel_t = jnp.concatenate([sel_t, jnp.zeros((LANE - n_sel, tq), F32)], axis=0)
    sel_t = sel_t.astype(BF16)

    acc_scr[...] = jnp.zeros_like(acc_scr)

    half = kc_sz // 2
    n_chunk = (t0 + tq + kc_sz - 1) // kc_sz

    def prepare(k0, s_buf, b_buf):
        s_buf[...] = lax.dot_general(ks_ref[pl.ds(k0, half), :], qa, nt, preferred_element_type=F32)
        picked = jnp.dot(expand_ref[pl.ds(k0, half), :], sel_t, preferred_element_type=F32)
        kpos = k0 + lax.broadcasted_iota(jnp.int32, (half, 1), 0)
        b_buf[...] = jnp.where(kpos <= qpos, jnp.where(picked > 0.5, 0.0, NEG), NEG)

    def chunk(c, carry):
        m, l = carry
        k0 = pl.multiple_of(c * kc_sz, kc_sz)
        k1 = pl.multiple_of(k0 + half, half)
        prepare(k1, s1_scr, b1_scr)
        m, l = _attend_block(s0_scr, b0_scr, vst_ref[:, pl.ds(k0, half)], m, l, acc_scr, H_C, tq)
        prepare(pl.multiple_of(jnp.minimum(c + 1, n_chunk - 1) * kc_sz, kc_sz), s0_scr, b0_scr)
        return _attend_block(s1_scr, b1_scr, vst_ref[:, pl.ds(k1, half)], m, l, acc_scr, H_C, tq)

    prepare(0, s0_scr, b0_scr)
    init = (jnp.full((1, H_C * tq), M_FLOOR, F32), jnp.zeros((1, H_C * tq), F32))
    _, l_s = lax.fori_loop(0, n_chunk, chunk, init)
    o_slc = acc_scr[...] * (1.0 / jnp.maximum(l_s, 1e-30))

    span = WINDOW + tq
    w0 = pl.multiple_of(jnp.maximum(t0 - WINDOW, 0), tq)
    s_w = lax.dot_general(kw_ref[pl.ds(w0, span), :], qa, nt, preferred_element_type=F32)
    kpos_w = w0 + lax.broadcasted_iota(jnp.int32, (span, 1), 0)
    wbias = jnp.where(kpos_w <= qpos, jnp.where(qpos - kpos_w < WINDOW, 0.0, NEG), NEG)
    p_w = _softmax_cols(s_w, wbias, H_C, tq)
    o_win = jnp.dot(vwt_ref[:, pl.ds(w0, span)], jnp.concatenate([p.astype(BF16) for p in p_w], axis=1), preferred_element_type=F32)

    gates_t = jax.nn.sigmoid(g_ref[...]).T
    for h in range(H_C):
        hs = slice(h * tq, (h + 1) * tq)
        o_t = (gates_t[h:h + 1] * o_cmp[:, hs] + gates_t[H_C + h:H_C + h + 1] * o_slc[:, hs]
               + gates_t[2 * H_C + h:2 * H_C + h + 1] * o_win[:, hs])
        o = o_t.T
        ms = jnp.sum(o * o, -1, keepdims=True) * (1.0 / DH_C)
        o_ref[:, h * LANE:(h + 1) * LANE] = (o * lax.rsqrt(ms + 1e-6) * gn_ref[h:h + 1, :]).astype(o_ref.dtype)


def _nsa_attention(nq, ng, kc, vc_t, nks, nvs_t, nkw, nvw_t, g_c):
    bsz, seq, _ = nq.shape
    n_grp = kc.shape[1]
    n_sel = seq // L_SEL
    grp_start = np.arange(n_grp) * D_STRIDE
    sel_start = np.arange(n_sel) * L_SEL
    cover_t = ((grp_start[None, :] < sel_start[:, None] + L_SEL) & (grp_start[None, :] + L_CMP > sel_start[:, None]))
    cover_t = jnp.asarray(cover_t.astype(np.float32), BF16)
    expand = (np.arange(seq)[:, None] // L_SEL == np.arange(LANE)[None, :]).astype(np.float32)
    expand = jnp.asarray(expand, BF16)
    gn = jnp.pad(g_c.reshape(H_C, DH_C), ((0, 0), (0, LANE - DH_C)))
    tq = NSA_TQ
    row = lambda n: pl.BlockSpec((None, tq, n), lambda b, i: (b, i, 0))
    per_b = lambda r, c: pl.BlockSpec((None, r, c), lambda b, i: (b, 0, 0))
    full = lambda r, c: pl.BlockSpec((r, c), lambda b, i: (0, 0))
    return pl.pallas_call(
        _nsa_kernel,
        grid=(bsz, seq // tq),
        in_specs=[row(H_C * LANE), row(LANE), per_b(n_grp, LANE), per_b(LANE, n_grp), per_b(seq, LANE), per_b(LANE, seq),
                  per_b(seq, LANE), per_b(LANE, seq), full(n_sel, n_grp), full(seq, LANE), full(H_C, LANE)],
        out_specs=row(H_C * LANE),
        out_shape=jax.ShapeDtypeStruct((bsz, seq, H_C * LANE), BF16),
        scratch_shapes=[pltpu.VMEM((LANE, H_C * tq), F32),
                        pltpu.VMEM((NSA_KC // 2, H_C * tq), F32), pltpu.VMEM((NSA_KC // 2, H_C * tq), F32),
                        pltpu.VMEM((NSA_KC // 2, tq), F32), pltpu.VMEM((NSA_KC // 2, tq), F32)],
        compiler_params=_params("arbitrary", "arbitrary"),
        name="nsa_attention",
    )(nq, ng, kc, vc_t, nks, nvs_t, nkw, nvw_t, cover_t, expand, gn)


def _outproj_kernel(ya_ref, yb_ref, yc_ref, x_ref, g_ref, wa_ref, wb_ref, wc_ref, lng_ref, lnb_ref, o_ref):
    y = jnp.dot(ya_ref[...], wa_ref[...], preferred_element_type=F32)
    y += jnp.dot(yb_ref[...], wb_ref[...], preferred_element_type=F32)
    y += jnp.dot(yc_ref[...], wc_ref[...], preferred_element_type=F32)
    z = ALPHA * x_ref[...] + (1.0 + g_ref[...]) * y
    o_ref[...] = _layer_norm_rows(z, lng_ref[...], lnb_ref[...])


def _pad_head_rows(w, n_head, dh):
    d = w.shape[1]
    return jnp.pad(w.reshape(n_head, dh, d), ((0, 0), (0, LANE - dh), (0, 0))).reshape(n_head * LANE, d)


def _output_projection(ya, yb, yc, x, g, wa, wb, wc, ln_g, ln_b):
    bsz, seq, d = x.shape
    tm = 512
    na, nb, nc = ya.shape[-1], yb.shape[-1], yc.shape[-1]
    row = lambda n: pl.BlockSpec((None, tm, n), lambda b, i: (b, i, 0))
    vec = pl.BlockSpec((None, 1, d), lambda b, i: (b, 0, 0))
    full = lambda r, c: pl.BlockSpec((r, c), lambda b, i: (0, 0))
    return pl.pallas_call(
        _outproj_kernel,
        grid=(bsz, seq // tm),
        in_specs=[row(na), row(nb), row(nc), row(d), vec, full(na, d), full(nb, d), full(nc, d), full(1, d), full(1, d)],
        out_specs=row(d),
        out_shape=jax.ShapeDtypeStruct((bsz, seq, d), F32),
        compiler_params=_params("arbitrary", "arbitrary"),
        name="output_projection_ln",
    )(ya, yb, yc, x, g, wa, wb, wc, ln_g.reshape(1, d), ln_b.reshape(1, d))


def _tile_gate_up(wg, wu, tf):
    *lead, d, ff = wg.shape
    split = lambda w: jnp.moveaxis(w.reshape(*lead, d, ff // tf, tf), -2, -3)
    return jnp.concatenate([split(wg), split(wu)], axis=-1)


def _swiglu_chunk(u, wgu_ref, wd_ref):
    tf = wd_ref.shape[0]
    ab = jnp.dot(u, wgu_ref[...], preferred_element_type=F32)
    a, b = ab[:, :tf], ab[:, tf:]
    return jnp.dot((a * jax.nn.sigmoid(a) * b).astype(BF16), wd_ref[...], preferred_element_type=F32)


def _ffn_kernel(x_ref, sc_ref, sh_ref, g_ref, wgu_ref, wd_ref, lng_ref, lnb_ref, o_ref, u_scr, acc_scr):
    f = pl.program_id(2)

    @pl.when(f == 0)
    def _():
        u_scr[...] = (x_ref[...] * (1.0 + sc_ref[...]) + sh_ref[...]).astype(BF16)
        acc_scr[...] = jnp.zeros_like(acc_scr)

    acc_scr[...] += _swiglu_chunk(u_scr[...], wgu_ref, wd_ref)

    @pl.when(f == pl.num_programs(2) - 1)
    def _():
        z = ALPHA * x_ref[...] + (1.0 + g_ref[...]) * acc_scr[...]
        o_ref[...] = _layer_norm_rows(z, lng_ref[...], lnb_ref[...])


def _dense_ffn(x, sc, sh, g, wg, wu, wd, ln_g, ln_b):
    bsz, seq, d = x.shape
    ff = wg.shape[1]
    tm, tf = 1024, 256
    row = pl.BlockSpec((None, tm, d), lambda b, i, f: (b, i, 0))
    vec = pl.BlockSpec((None, 1, d), lambda b, i, f: (b, 0, 0))
    one = pl.BlockSpec((1, d), lambda b, i, f: (0, 0))
    return pl.pallas_call(
        _ffn_kernel,
        grid=(bsz, seq // tm, ff // tf),
        in_specs=[row, vec, vec, vec,
                  pl.BlockSpec((None, d, 2 * tf), lambda b, i, f: (f, 0, 0)),
                  pl.BlockSpec((tf, d), lambda b, i, f: (f, 0)),
                  one, one],
        out_specs=row,
        out_shape=jax.ShapeDtypeStruct((bsz, seq, d), F32),
        scratch_shapes=[pltpu.VMEM((tm, d), BF16), pltpu.VMEM((tm, d), F32)],
        compiler_params=_params("arbitrary", "arbitrary", "arbitrary"),
        name="dense_swiglu_ln",
    )(x, sc, sh, g, _tile_gate_up(wg, wu, tf), wd, ln_g.reshape(1, d), ln_b.reshape(1, d))


def _router_kernel(x_ref, sc_ref, sh_ref, r_ref, lo_ref, up_ref, gate_ref, slot_ref, slot_t_ref, cnt_ref):
    u = x_ref[...] * (1.0 + sc_ref[...]) + sh_ref[...]
    logits = jnp.dot(u, r_ref[...], preferred_element_type=F32, precision=lax.Precision.HIGHEST)
    lane = lax.broadcasted_iota(jnp.int32, logits.shape, 1)
    neg = -jnp.inf
    l1 = jnp.where(lane < N_EXPERTS, logits, neg)
    m1 = jnp.max(l1, -1, keepdims=True)
    i1 = jnp.min(jnp.where(l1 == m1, lane, LANE), -1, keepdims=True)
    l2 = jnp.where(lane == i1, neg, l1)
    m2 = jnp.max(l2, -1, keepdims=True)
    i2 = jnp.min(jnp.where(l2 == m2, lane, LANE), -1, keepdims=True)
    e2 = jnp.exp(m2 - m1)
    w1 = 1.0 / (1.0 + e2)
    w2 = e2 / (1.0 + e2)
    gate_ref[...] = jnp.where(lane == i1, w1, jnp.where(lane == i2, w2, 0.0))
    routed = jnp.where((lane == i1) | (lane == i2), 1.0, 0.0)
    before = jnp.dot(lo_ref[...], routed.astype(BF16), preferred_element_type=F32)
    slot_ref[...] = jnp.where(routed > 0.5, before, -1.0)
    routed_t = routed.T
    before_t = jnp.dot(routed_t.astype(BF16), up_ref[...], preferred_element_type=F32)
    slot_t_ref[...] = jnp.where(routed_t > 0.5, before_t, -1.0)
    cnt_ref[...] = jnp.broadcast_to(jnp.sum(routed, axis=0, keepdims=True), cnt_ref.shape).astype(jnp.int32)


MOE_TM = 1024
MOE_ROWS = 288
MOE_TF = 896


def _moe_router(x, sc, sh, router):
    bsz, seq, d = x.shape
    tm = MOE_TM
    r = jnp.pad(router, ((0, 0), (0, LANE - router.shape[1])))
    upper = np.triu(np.ones((tm, tm), np.float32), 1)
    up, lo = jnp.asarray(upper, BF16), jnp.asarray(upper.T, BF16)
    row = lambda n: pl.BlockSpec((None, tm, n), lambda b, i: (b, i, 0))
    vec = pl.BlockSpec((None, 1, d), lambda b, i: (b, 0, 0))
    full = lambda a, c: pl.BlockSpec((a, c), lambda b, i: (0, 0))
    n_tile = seq // tm
    return pl.pallas_call(
        _router_kernel,
        grid=(bsz, n_tile),
        in_specs=[row(d), vec, vec, full(d, LANE), full(tm, tm), full(tm, tm)],
        out_specs=[row(LANE), row(LANE), pl.BlockSpec((None, None, LANE, tm), lambda b, i: (b, i, 0, 0)),
                   pl.BlockSpec((None, None, 8, LANE), lambda b, i: (b, i, 0, 0))],
        out_shape=[jax.ShapeDtypeStruct((bsz, seq, LANE), F32), jax.ShapeDtypeStruct((bsz, seq, LANE), F32),
                   jax.ShapeDtypeStruct((bsz, n_tile, LANE, tm), F32), jax.ShapeDtypeStruct((bsz, n_tile, 8, LANE), jnp.int32)],
        compiler_params=_params("arbitrary", "arbitrary"),
        name="moe_router",
    )(x, sc, sh, r, lo, up)


def _moe_kernel(cnt_ref, x_ref, sc_ref, sh_ref, g_ref, gate_ref, slot_ref, slot_t_ref, wgu_ref, wd_ref, lng_ref, lnb_ref,
                o_ref, u_scr, xg_scr, acc_scr):
    tm, rows = MOE_TM, MOE_ROWS
    e = pl.program_id(2)
    f = pl.program_id(3)
    tile = pl.program_id(0) * pl.num_programs(1) + pl.program_id(1)
    n_pass = (cnt_ref[tile * N_EXPERTS + e] + rows - 1) // rows

    @pl.when((e == 0) & (f == 0))
    def _():
        u_scr[...] = (x_ref[...] * (1.0 + sc_ref[...]) + sh_ref[...]).astype(BF16)
        o_ref[...] = jnp.zeros_like(o_ref)

    @pl.when(f == 0)
    def _():
        slot_row = slot_t_ref[pl.ds(e, 1), :]

        def gather(p, carry):
            want = p * rows + lax.broadcasted_iota(jnp.int32, (rows, 1), 0)
            pick = jnp.where(slot_row == want.astype(F32), 1.0, 0.0).astype(BF16)
            xg_scr[p] = jnp.dot(pick, u_scr[...], preferred_element_type=F32).astype(BF16)
            acc_scr[p] = jnp.zeros((rows, x_ref.shape[-1]), F32)
            return carry

        lax.fori_loop(0, n_pass, gather, 0)

    def ffn(p, carry):
        acc_scr[p] += _swiglu_chunk(xg_scr[p], wgu_ref, wd_ref)
        return carry

    lax.fori_loop(0, n_pass, ffn, 0)

    @pl.when(f == pl.num_programs(3) - 1)
    def _():
        lane = lax.broadcasted_iota(jnp.int32, (tm, LANE), 1)
        slot_col = jnp.sum(jnp.where(lane == e, slot_ref[...], 0.0), axis=1, keepdims=True)
        gate_col = jnp.sum(jnp.where(lane == e, gate_ref[...], 0.0), axis=1, keepdims=True)

        def scatter(p, carry):
            col = lax.broadcasted_iota(jnp.int32, (1, 2 * rows), 1)
            want = p * rows + jnp.where(col >= rows, col - rows, col)
            put = jnp.where(slot_col == want.astype(F32), 1.0, 0.0).astype(BF16)
            y = acc_scr[p]
            hi = y.astype(BF16)
            lo = (y - hi.astype(F32)).astype(BF16)
            back = jnp.dot(put, jnp.concatenate([hi, lo], axis=0), preferred_element_type=F32)
            o_ref[...] += gate_col * back
            return carry

        lax.fori_loop(0, n_pass, scatter, 0)

    @pl.when((e == pl.num_programs(2) - 1) & (f == pl.num_programs(3) - 1))
    def _():
        z = ALPHA * x_ref[...] + (1.0 + g_ref[...]) * o_ref[...]
        o_ref[...] = _layer_norm_rows(z, lng_ref[...], lnb_ref[...])


def _moe_ffn(x, sc, sh, g, routing, wg, wu, wd, ln_g, ln_b):
    gate, slot, slot_t, cnt = routing
    bsz, seq, d = x.shape
    n_e, _, ff = wg.shape
    tm, tf, rows = MOE_TM, MOE_TF, MOE_ROWS
    max_pass = -(-tm // rows)
    counts = cnt[:, :, 0, :n_e].reshape(-1)
    row = lambda n: pl.BlockSpec((None, tm, n), lambda b, i, e, f, c: (b, i, 0))
    vec = pl.BlockSpec((None, 1, d), lambda b, i, e, f, c: (b, 0, 0))
    one = pl.BlockSpec((1, d), lambda b, i, e, f, c: (0, 0))
    grid_spec = pltpu.PrefetchScalarGridSpec(
        num_scalar_prefetch=1,
        grid=(bsz, seq // tm, n_e, ff // tf),
        in_specs=[row(d), vec, vec, vec, row(LANE), row(LANE),
                  pl.BlockSpec((None, None, LANE, tm), lambda b, i, e, f, c: (b, i, 0, 0)),
                  pl.BlockSpec((None, None, d, 2 * tf), lambda b, i, e, f, c: (e, f, 0, 0)),
                  pl.BlockSpec((None, tf, d), lambda b, i, e, f, c: (e, f, 0)),
                  one, one],
        out_specs=row(d),
        scratch_shapes=[pltpu.VMEM((tm, d), BF16), pltpu.VMEM((max_pass, rows, d), BF16), pltpu.VMEM((max_pass, rows, d), F32)],
    )
    return pl.pallas_call(
        _moe_kernel,
        grid_spec=grid_spec,
        out_shape=jax.ShapeDtypeStruct((bsz, seq, d), F32),
        compiler_params=_params("arbitrary", "arbitrary", "arbitrary", "arbitrary"),
        name="moe_swiglu_ln",
    )(counts, x, sc, sh, g, gate, slot, slot_t, _tile_gate_up(wg, wu, tf), wd, ln_g.reshape(1, d), ln_b.reshape(1, d))


ML_TT = 256
ML_SUB = 128


ML_NB = 1


def _mlstm_kernel(qk_ref, tail_ref, v_ref, og_ref, g_ref, cw_ref, cb_ref, gb_ref, gn_ref, y_ref,
                  ct_scr, n_scr, m_scr, q_scr, k_scr):
    @pl.when(pl.program_id(1) == 0)
    def _():
        ct_scr[...] = jnp.zeros_like(ct_scr)
        n_scr[...] = jnp.zeros_like(n_scr)
        m_scr[...] = jnp.zeros_like(m_scr)

    for nb in range(ML_NB):
        _mlstm_rows(qk_ref.at[nb], tail_ref.at[nb], v_ref.at[nb], og_ref.at[nb], g_ref.at[nb], cw_ref, cb_ref, gb_ref, gn_ref,
                    y_ref.at[nb], ct_scr.at[nb], n_scr.at[nb], m_scr.at[nb], q_scr.at[nb], k_scr.at[nb])


def _mlstm_rows(qk_ref, tail_ref, v_ref, og_ref, g_ref, cw_ref, cb_ref, gb_ref, gn_ref, y_ref,
                ct_scr, n_scr, m_scr, q_scr, k_scr):
    tt = ML_TT
    step = pl.program_id(1)

    x = qk_ref[...]
    tail = jnp.where(step == 0, 0.0, tail_ref[...])
    row8 = lax.broadcasted_iota(jnp.int32, (8, 1), 0)
    pre = x * cw_ref[CONV_W - 1:CONV_W, :] + cb_ref[...]
    for s in range(1, CONV_W):
        rolled = pltpu.roll(x, s, 0)
        head = jnp.where(row8 < s, pltpu.roll(tail, s, 0), rolled[0:8])
        pre = pre + jnp.concatenate([head, rolled[8:]], axis=0) * cw_ref[CONV_W - 1 - s:CONV_W - s, :]
    act = pre * jax.nn.sigmoid(pre)
    q_scr[...] = act[:, 0:ML_D].astype(BF16)
    k_scr[...] = (act[:, ML_D:2 * ML_D] * (DH_B ** -0.5)).astype(BF16)

    lane = lax.broadcasted_iota(jnp.int32, (1, LANE), 1)
    tok = lane % CHUNK
    jj = lax.broadcasted_iota(jnp.int32, (CHUNK, CHUNK), 0)
    ss = lax.broadcasted_iota(jnp.int32, (CHUNK, CHUNK), 1)
    nt = (((1,), (1,)), ((), ()))
    tn = (((0,), (0,)), ((), ()))
    ct_state = [ct_scr[h] for h in range(H_B)]
    n_state = [n_scr[h:h + 1, :] for h in range(H_B)]
    m_state = [m_scr[h:h + 1, 0:1] for h in range(H_B)]
    for sub in range(tt // ML_SUB):
        r0 = sub * ML_SUB
        gp = g_ref[r0:r0 + ML_SUB, :] + gb_ref[...]
        lsig = jnp.minimum(gp, 0.0) - jnp.log(1.0 + jnp.exp(-jnp.abs(gp)))
        col = jnp.where(lane < H_B, gp, lsig)
        rowl = col.T
        b = rowl[0:8]
        for sft in (1, 2, 4, 8, 16, 32):
            b = b + jnp.where(tok >= sft, pltpu.roll(b, sft, 1), 0.0)
        bcol = jnp.concatenate([b, jnp.zeros((LANE - 8, ML_SUB), F32)], axis=0).T
        for ci in range(ML_SUB // CHUNK):
            c0 = ci * CHUNK
            rows = slice(r0 + c0, r0 + c0 + CHUNK)
            for h in range(H_B):
                hs = slice(h * DH_B, (h + 1) * DH_B)
                b_col = bcol[c0:c0 + CHUNK, H_B + h:H_B + h + 1]
                ig_col = col[c0:c0 + CHUNK, h:h + 1]
                b_row = b[H_B + h:H_B + h + 1, c0:c0 + CHUNK]
                ig_row = rowl[h:h + 1, c0:c0 + CHUNK]
                qh, kh, vh = q_scr[rows, hs], k_scr[rows, hs], v_ref[rows, hs]
                log_d = jnp.where(jj >= ss, b_col - b_row + ig_row, NEG)
                m_loc = jnp.max(log_d, -1, keepdims=True)
                s_loc = lax.dot_general(qh, kh, nt, preferred_element_type=F32) * jnp.exp(log_d - m_loc)
                sv_loc = jnp.dot(s_loc.astype(BF16), vh, preferred_element_type=F32)
                rs_loc = jnp.sum(s_loc, -1, keepdims=True)
                b_last = b_row[:, CHUNK - 1:CHUNK]
                w_max = jnp.max(b_last - b_row + ig_row, -1, keepdims=True)
                w_loc = jnp.exp(b_last - b_col + ig_col - w_max)
                inc_c = lax.dot_general(kh, (w_loc * vh.astype(F32)).astype(BF16), tn, preferred_element_type=F32)
                inc_n = jnp.sum(w_loc * kh.astype(F32), axis=0, keepdims=True)
                m_old, ct, n_row = m_state[h], ct_state[h], n_state[h]
                log_inter = b_col + m_old
                m_out = jnp.maximum(log_inter, m_loc)
                w_inter = jnp.exp(log_inter - m_out)
                w_intra = jnp.exp(m_loc - m_out)
                num = w_inter * jnp.dot(qh, ct.astype(BF16), preferred_element_type=F32) + w_intra * sv_loc
                den = w_inter * jnp.sum(qh.astype(F32) * n_row, -1, keepdims=True) + w_intra * rs_loc
                hid = num / jnp.maximum(jnp.abs(den), jnp.exp(-m_out))
                m_new = jnp.maximum(b_last + m_old, w_max)
                decay = jnp.exp(b_last + m_old - m_new)
                grow = jnp.exp(w_max - m_new)
                ct_state[h] = decay * ct + grow * inc_c
                n_state[h] = decay * n_row + grow * inc_n
                m_state[h] = m_new
                ms = jnp.mean(hid * hid, -1, keepdims=True)
                y = hid * lax.rsqrt(ms + 1e-6) * gn_ref[:, hs] * jax.nn.sigmoid(og_ref[rows, hs])
                y_ref[rows, hs] = y.astype(y_ref.dtype)
    for h in range(H_B):
        ct_scr[h] = ct_state[h]
        n_scr[h:h + 1, :] = n_state[h]
        m_scr[h:h + 1, :] = jnp.broadcast_to(m_state[h], (1, LANE))


def _mlstm(mqk, mv, mo, mg, conv_w, conv_b, gate_b, g_b):
    bsz, seq, _ = mqk.shape
    tt = ML_TT
    gb = jnp.pad(gate_b.reshape(1, 2 * H_B), ((0, 0), (0, LANE - 2 * H_B)))
    nb = ML_NB
    row = lambda n: pl.BlockSpec((nb, tt, n), lambda b, i: (b, i, 0))
    full = lambda r, c: pl.BlockSpec((r, c), lambda b, i: (0, 0))
    tail = pl.BlockSpec((nb, 8, 2 * ML_D), lambda b, i: (b, jnp.maximum(i * (tt // 8) - 1, 0), 0))
    return pl.pallas_call(
        _mlstm_kernel,
        grid=(bsz // nb, seq // tt),
        in_specs=[row(2 * ML_D), tail, row(ML_D), row(ML_D), row(LANE),
                  full(CONV_W, 2 * ML_D), full(1, 2 * ML_D), full(1, LANE), full(1, ML_D)],
        out_specs=row(ML_D),
        out_shape=jax.ShapeDtypeStruct((bsz, seq, ML_D), BF16),
        scratch_shapes=[pltpu.VMEM((nb, H_B, DH_B, DH_B), F32), pltpu.VMEM((nb, 8, DH_B), F32), pltpu.VMEM((nb, 8, LANE), F32),
                        pltpu.VMEM((nb, tt, ML_D), BF16), pltpu.VMEM((nb, tt, ML_D), BF16)],
        compiler_params=_params("arbitrary", "arbitrary"),
        name="mlstm_scan",
    )(mqk, mqk, mv, mo, mg, conv_w, conv_b.reshape(1, 2 * ML_D), gb, g_b.reshape(1, ML_D))


def _mixers(pa, mls, nsa, rope, w_uk, w_uv, kv_norm, conv_w, conv_b, gate_b, cmp_pos, cmp_w1, cmp_w2, grp_norm):
    g_a, g_b, g_c = jnp.split(grp_norm, [H_A * DH_V, H_A * DH_V + H_B * DH_B])
    qc, iq, iw, kc_a, ik, ct_a = _dsa_prep(pa, rope[0], rope[1], w_uk, kv_norm)
    y_a = _dsa_attention(qc, iq, iw, kc_a, ct_a, ik, w_uv, g_a)
    y_b = _mlstm(*mls, conv_w, conv_b, gate_b, g_b)
    nq, ncmp, nks, nvs, nkw, nvw, ng = nsa
    kc, vc = _nsa_compress(ncmp, cmp_pos, cmp_w1, cmp_w2)
    y_c = _nsa_attention(nq, ng, kc, vc, nks, nvs, nkw, nvw, g_c)
    return y_a, y_b, y_c


def _pad_cols(w, n):
    return jnp.pad(w, ((0, 0), (0, n - w.shape[1])))


def kernel(x, c, positions, w_mod, b_mod, w_in, dsa_w_uk, dsa_w_uv, dsa_kv_norm, mlstm_conv_w, mlstm_conv_b, mlstm_gate_b, nsa_cmp_pos, nsa_cmp_w1, nsa_cmp_w2, grp_norm, w_out, ln_g, ln_b, ffn_w_gate, ffn_w_up, ffn_w_down, moe_router, moe_w_gate, moe_w_up, moe_w_down):
    bsz = x.shape[0]
    mod = _modulation(c, w_mod, b_mod).reshape(bsz, DEPTH, N_MOD, 1, D_MODEL)
    rope = _rope_table(positions)
    for l in range(DEPTH):
        sh1, sc1, g1, sh2, sc2, g2 = [mod[:, l, j] for j in range(N_MOD)]
        w = w_in[l].astype(BF16)
        wa = _dsa_weight_layout(w[:, :N_GROUP_A])
        wb = _mlstm_weight_layout(w[:, N_GROUP_A:N_GROUP_A + N_GROUP_B])
        wc, wvt = _nsa_weight_layout(w[:, N_GROUP_A + N_GROUP_B:])
        pa, mqk, mv, mo, mg, *nsa = _input_projection(x, sc1, sh1, wa, wb, wc, wvt)
        ya, yb, yc = _mixers(pa, (mqk, mv, mo, mg), nsa, rope, dsa_w_uk[l], dsa_w_uv[l], dsa_kv_norm[l], mlstm_conv_w[l], mlstm_conv_b[l], mlstm_gate_b[l], nsa_cmp_pos[l], nsa_cmp_w1[l], nsa_cmp_w2[l], grp_norm[l])
        wo = w_out[l].astype(BF16)
        n_a, n_b = H_A * DH_V, H_B * DH_B
        x = _output_projection(ya, yb, yc, x, g1, _pad_head_rows(wo[:n_a], H_A, DH_V), wo[n_a:n_a + n_b], _pad_head_rows(wo[n_a + n_b:], H_C, DH_C), ln_g[l, 0], ln_b[l, 0])
        if l % 2 == 0:
            k = l // 2
            x = _dense_ffn(x, sc2, sh2, g2, ffn_w_gate[k].astype(BF16), ffn_w_up[k].astype(BF16), ffn_w_down[k].astype(BF16), ln_g[l, 1], ln_b[l, 1])
        else:
            k = l // 2
            routing = _moe_router(x, sc2, sh2, moe_router[k])
            x = _moe_ffn(x, sc2, sh2, g2, routing,moe_w_gate[k].astype(BF16), moe_w_up[k].astype(BF16), moe_w_down[k].astype(BF16), ln_g[l, 1], ln_b[l, 1])
    return x
```

```python
import functools

import numpy as np
import jax
import jax.numpy as jnp
from jax import lax
from jax.experimental import pallas as pl
from jax.experimental.pallas import tpu as pltpu

F32 = jnp.float32
BF16 = jnp.bfloat16

D_MODEL = 1024
DEPTH = 2
H_A, DH_NOPE, DH_ROPE, D_C, DH_V, H_I, D_I = 4, 64, 32, 128, 64, 4, 64
K_SEL_MAX = 256
ROPE_THETA = 10000.0
H_B, DH_B, CONV_W, CHUNK = 4, 128, 4, 64
H_C, DH_C, L_CMP, D_STRIDE, CMP_HIDDEN, L_SEL, N_TOP_MAX, WINDOW, Q_BLOCK = 4, 64, 32, 16, 128, 64, 16, 512, 128
D_FF = 2816
N_EXPERTS = 8
D_FF_EXPERT = 3584
N_MOD = 6
ALPHA = (2 * DEPTH) ** 0.25
SPLIT_SIZES = (H_A * DH_NOPE, H_A * DH_ROPE, D_C, DH_ROPE, H_I * D_I, D_I, H_I, H_B * DH_B, H_B * DH_B, H_B * DH_B, H_B, H_B, H_B * DH_B, H_C * DH_C, DH_C, DH_C, DH_C, DH_C, DH_C, DH_C, 3 * H_C)
N_GROUP_A = sum(SPLIT_SIZES[:7])
N_GROUP_B = sum(SPLIT_SIZES[7:13])
N_GROUP_C = sum(SPLIT_SIZES[13:])

LOG2E = 1.4426950408889634
LANE = 128
VMEM_LIMIT = 56 * 1024 * 1024


def _round_up(n, m):
    return (n + m - 1) // m * m


def _params(*sem):
    return pltpu.CompilerParams(dimension_semantics=sem, vmem_limit_bytes=VMEM_LIMIT)


FOLD_ROWS = 64


def _fold_rows(x, op):
    parts = [x[i:i + FOLD_ROWS] for i in range(0, x.shape[0], FOLD_ROWS)]
    while len(parts) > 1:
        parts = [op(parts[i], parts[i + 1]) if i + 1 < len(parts) else parts[i] for i in range(0, len(parts), 2)]
    return parts[0]


def _attend_block(s_buf, b_buf, v_t, m, l, acc_scr, n_head, tq):
    bias = b_buf[...]
    m_new, l_new, ps = [], [], []
    for h in range(n_head):
        hs = slice(h * tq, (h + 1) * tq)
        sh = s_buf[:, hs] + bias
        mh = jnp.maximum(m[:, hs], jnp.max(_fold_rows(sh, jnp.maximum), axis=0, keepdims=True))
        ph = jnp.exp2(sh - mh)
        l_new.append(jnp.exp2(m[:, hs] - mh) * l[:, hs] + jnp.sum(_fold_rows(ph, jnp.add), axis=0, keepdims=True))
        m_new.append(mh)
        ps.append(ph.astype(BF16))
    m_new = jnp.concatenate(m_new, axis=1)
    acc_scr[...] = jnp.exp2(m - m_new) * acc_scr[...] + jnp.dot(v_t, jnp.concatenate(ps, axis=1), preferred_element_type=F32)
    return m_new, jnp.concatenate(l_new, axis=1)


PLANE_KEYS = 256


def _bit_transpose32(x):
    w = [x[8 * i:8 * i + 8, :] for i in range(32)]
    j, m = 16, 0x0000FFFF
    while j:
        k = 0
        while k < 32:
            t = (w[k] ^ lax.shift_right_logical(w[k + j], j)) & m
            w[k] = w[k] ^ t
            w[k + j] = w[k + j] ^ jnp.left_shift(t, j)
            k = (k + j + 1) & ~j
        j >>= 1
        m ^= (m << j) & 0xFFFFFFFF
    return w


def _layer_norm_rows(z, g, b):
    mu = jnp.mean(z, -1, keepdims=True)
    zc = z - mu
    var = jnp.mean(zc * zc, -1, keepdims=True)
    return zc * lax.rsqrt(var + 1e-5) * g + b


def _mod_kernel(c_ref, w_ref, b_ref, o_ref):
    c = c_ref[...]
    a = c * jax.nn.sigmoid(c)
    o_ref[...] = jnp.dot(a, w_ref[...], preferred_element_type=F32, precision=lax.Precision.HIGHEST) + b_ref[...]


def _modulation(c, w_mod, b_mod):
    bsz, d = c.shape
    n = w_mod.shape[1]
    tn = 1024
    return pl.pallas_call(
        _mod_kernel,
        grid=(n // tn,),
        in_specs=[pl.BlockSpec((bsz, d), lambda j: (0, 0)),
                  pl.BlockSpec((d, tn), lambda j: (0, j)),
                  pl.BlockSpec((1, tn), lambda j: (0, j))],
        out_specs=pl.BlockSpec((bsz, tn), lambda j: (0, j)),
        out_shape=jax.ShapeDtypeStruct((bsz, n), F32),
        compiler_params=_params("arbitrary"),
        name="adaln_mod",
    )(c, w_mod, b_mod.reshape(1, n))


NSA_Q0 = 0
NSA_CMP0 = H_C * LANE
NSA_KS0 = NSA_CMP0 + LANE
NSA_KW0 = NSA_KS0 + LANE
NSA_G0 = NSA_KW0 + LANE
NSA_COLS = NSA_G0 + LANE


def _nsa_weight_layout(w):
    d = w.shape[0]
    nq, nkc, nvc, nks, nvs, nkw, nvw, ng = jnp.split(w, np.cumsum(SPLIT_SIZES[13:])[:-1].tolist(), axis=1)
    z = lambda n: jnp.zeros((d, n), w.dtype)
    half = LANE - DH_C
    cols = []
    for h in range(H_C):
        cols += [nq[:, h * DH_C:(h + 1) * DH_C], z(half)]
    cols += [nkc, nvc, nks, z(half), nkw, z(half), ng, z(LANE - 3 * H_C)]
    values_t = jnp.concatenate([nvs, z(half), nvw, z(half)], axis=1).T
    return jnp.concatenate(cols, axis=1), values_t


def _mlstm_weight_layout(w):
    mq, mk, mv, mi, mf, mo = jnp.split(w, np.cumsum(SPLIT_SIZES[7:13])[:-1].tolist(), axis=1)
    return jnp.concatenate([mq, mk, mv, mo, mi, mf, jnp.zeros((w.shape[0], LANE - 2 * H_B), w.dtype)], axis=1)


ML_D = H_B * DH_B


def _inproj_kernel(x_ref, sc_ref, sh_ref, wa_ref, wb_ref, wc_ref, wvt_ref, oa_ref, mqk_ref, mv_ref, mo_ref, mg_ref,
                   nq_ref, ncmp_ref, nks_ref, nvs_ref, nkw_ref, nvw_ref, ng_ref):
    u = (x_ref[...] * (1.0 + sc_ref[...]) + sh_ref[...]).astype(BF16)
    oa_ref[...] = jnp.dot(u, wa_ref[...], preferred_element_type=F32)
    ob = jnp.dot(u, wb_ref[...], preferred_element_type=F32)
    mqk_ref[...] = ob[:, 0:2 * ML_D]
    mv_ref[...] = ob[:, 2 * ML_D:3 * ML_D].astype(BF16)
    mo_ref[...] = ob[:, 3 * ML_D:4 * ML_D]
    mg_ref[...] = ob[:, 4 * ML_D:4 * ML_D + LANE]
    oc = jnp.dot(u, wc_ref[...], preferred_element_type=F32)
    nq_ref[...] = (oc[:, NSA_Q0:NSA_Q0 + H_C * LANE] * (DH_C ** -0.5 * LOG2E)).astype(BF16)
    ncmp_ref[...] = oc[:, NSA_CMP0:NSA_CMP0 + LANE]
    nks_ref[...] = oc[:, NSA_KS0:NSA_KS0 + LANE].astype(BF16)
    nkw_ref[...] = oc[:, NSA_KW0:NSA_KW0 + LANE].astype(BF16)
    ng_ref[...] = oc[:, NSA_G0:NSA_G0 + LANE]
    vt = lax.dot_general(wvt_ref[...], u, (((1,), (1,)), ((), ())), preferred_element_type=F32)
    nvs_ref[...] = vt[0:LANE].astype(BF16)
    nvw_ref[...] = vt[LANE:2 * LANE].astype(BF16)


def _input_projection(x, sc, sh, wa, wb, wc, wvt):
    bsz, seq, d = x.shape
    tm = 512
    na, nb, nc = wa.shape[1], wb.shape[1], wc.shape[1]
    row = lambda n: pl.BlockSpec((None, tm, n), lambda b, i: (b, i, 0))
    col = pl.BlockSpec((None, LANE, tm), lambda b, i: (b, 0, i))
    vec = pl.BlockSpec((None, 1, d), lambda b, i: (b, 0, 0))
    full = lambda n: pl.BlockSpec((d, n), lambda b, i: (0, 0))
    tok = lambda n, dt: (row(n), jax.ShapeDtypeStruct((bsz, seq, n), dt))
    feat = (col, jax.ShapeDtypeStruct((bsz, LANE, seq), BF16))
    outs = [tok(na, F32), tok(2 * ML_D, F32), tok(ML_D, BF16), tok(ML_D, F32), tok(LANE, F32),
            tok(H_C * LANE, BF16), tok(LANE, F32), tok(LANE, BF16), feat, tok(LANE, BF16), feat, tok(LANE, F32)]
    return pl.pallas_call(
        _inproj_kernel,
        grid=(bsz, seq // tm),
        in_specs=[row(d), vec, vec, full(na), full(nb), full(nc), pl.BlockSpec((2 * LANE, d), lambda b, i: (0, 0))],
        out_specs=[spec for spec, _ in outs],
        out_shape=[shape for _, shape in outs],
        compiler_params=_params("arbitrary", "arbitrary"),
        name="input_projection",
    )(x, sc, sh, wa, wb, wc, wvt)


DSA_QN0 = 0
DSA_QR0 = H_A * LANE
DSA_CKV0 = DSA_QR0 + LANE
DSA_IQ0 = DSA_CKV0 + LANE
DSA_G0 = DSA_IQ0 + H_I * LANE
DSA_COLS = DSA_G0 + LANE
DSA_KR_LANE = D_I
DSA_IW_LANE = D_I + DH_ROPE
DSA_TQ = 128
DSA_KC = 512
INT_MIN = -2 ** 31
KEY_NEG_INF = int(np.int32(np.float32(-np.inf).view(np.int32)) ^ np.int32(0x7FFFFFFF))


def _dsa_weight_layout(w):
    d = w.shape[0]
    qn, qr, ckv, kr, iq, ik, iw = jnp.split(w, np.cumsum(SPLIT_SIZES[:7])[:-1].tolist(), axis=1)
    z = lambda n: jnp.zeros((d, n), w.dtype)
    cols = []
    for h in range(H_A):
        cols += [qn[:, h * DH_NOPE:(h + 1) * DH_NOPE], z(LANE - DH_NOPE)]
    cols += [qr, ckv]
    for h in range(H_I):
        cols += [iq[:, h * D_I:(h + 1) * D_I], z(LANE - D_I)]
    cols += [ik, kr, iw, z(LANE - D_I - DH_ROPE - H_I)]
    return jnp.concatenate(cols, axis=1)


def _rope_table_kernel(pos_ref, freq_ref, cos_ref, sin_ref):
    ang = pos_ref[...].astype(F32) * freq_ref[...]
    lane = lax.broadcasted_iota(jnp.int32, (1, LANE), 1)
    first = (lane % DH_ROPE) < DH_ROPE // 2
    cos_ref[...] = jnp.cos(ang)
    sin_ref[...] = jnp.where(first, -jnp.sin(ang), jnp.sin(ang))


def _rope_table(positions):
    bsz, seq = positions.shape
    tm = 512
    inv_freq = ROPE_THETA ** (-jnp.arange(0, DH_ROPE, 2, dtype=F32) / DH_ROPE)
    freq = jnp.tile(inv_freq, LANE // (DH_ROPE // 2)).reshape(1, LANE)
    out = pl.BlockSpec((None, tm, LANE), lambda b, i: (b, i, 0))
    return pl.pallas_call(
        _rope_table_kernel,
        grid=(bsz, seq // tm),
        in_specs=[pl.BlockSpec((None, tm, 1), lambda b, i: (b, i, 0)), pl.BlockSpec((1, LANE), lambda b, i: (0, 0))],
        out_specs=[out, out],
        out_shape=[jax.ShapeDtypeStruct((bsz, seq, LANE), F32)] * 2,
        compiler_params=_params("arbitrary", "arbitrary"),
        name="rope_table",
    )(positions.reshape(bsz, seq, 1), freq)


def _dsa_prep_kernel(pa_ref, cos_ref, sin_ref, wuk_ref, kvn_ref, qc_ref, iq_ref, iw_ref, kc_ref, ik_ref, ct_ref):
    scale = (DH_NOPE + DH_ROPE) ** -0.5 * LOG2E
    cos, sin = cos_ref[...], sin_ref[...]
    lane = lax.broadcasted_iota(jnp.int32, (1, LANE), 1)
    first = (lane % DH_ROPE) < DH_ROPE // 2
    rope_lanes = (lane >= DSA_KR_LANE) & (lane < DSA_KR_LANE + DH_ROPE)

    def rope(v):
        partner = jnp.where(first, pltpu.roll(v, LANE - DH_ROPE // 2, 1), pltpu.roll(v, DH_ROPE // 2, 1))
        return v * cos + partner * sin

    g = pa_ref[:, DSA_G0:DSA_G0 + LANE]
    ckv = pa_ref[:, DSA_CKV0:DSA_CKV0 + LANE]
    ckv_n = ckv * lax.rsqrt(jnp.mean(ckv * ckv, -1, keepdims=True) + 1e-6) * kvn_ref[...]
    kc_ref[:, 0:LANE] = ckv_n.astype(BF16)
    ct_ref[...] = ckv_n.T.astype(BF16)
    kc_ref[:, LANE:2 * LANE] = jnp.where(rope_lanes, rope(g), 0.0).astype(BF16)
    ik_ref[...] = jnp.where(lane < D_I, g, 0.0).astype(BF16)
    iw_ref[...] = g
    iq_ref[...] = pa_ref[:, DSA_IQ0:DSA_IQ0 + H_I * LANE].astype(BF16)
    qr = rope(pa_ref[:, DSA_QR0:DSA_QR0 + LANE]) * scale
    for h in range(H_A):
        qn = pa_ref[:, DSA_QN0 + h * LANE:DSA_QN0 + (h + 1) * LANE].astype(BF16)
        q_abs = jnp.dot(qn, wuk_ref[h], preferred_element_type=F32) * scale
        shift = (DSA_KR_LANE - DH_ROPE * h) % LANE
        qr_h = pltpu.roll(qr, shift, 1) if shift else qr
        qc_ref[:, 2 * h * LANE:(2 * h + 1) * LANE] = q_abs.astype(BF16)
        qc_ref[:, (2 * h + 1) * LANE:(2 * h + 2) * LANE] = jnp.where(rope_lanes, qr_h, 0.0).astype(BF16)


def _dsa_prep(pa, cos, sin, w_uk, kv_norm):
    bsz, seq, _ = pa.shape
    tm = 512
    wuk = jnp.pad(w_uk, ((0, 0), (0, LANE - DH_NOPE), (0, 0))).astype(BF16)
    row = lambda n: pl.BlockSpec((None, tm, n), lambda b, i: (b, i, 0))
    outs = [(2 * H_A * LANE, BF16), (H_I * LANE, BF16), (LANE, F32), (2 * LANE, BF16), (LANE, BF16)]
    return pl.pallas_call(
        _dsa_prep_kernel,
        grid=(bsz, seq // tm),
        in_specs=[row(DSA_COLS), row(LANE), row(LANE),
                  pl.BlockSpec((H_A, LANE, D_C), lambda b, i: (0, 0, 0)), pl.BlockSpec((1, D_C), lambda b, i: (0, 0))],
        out_specs=[row(n) for n, _ in outs] + [pl.BlockSpec((None, D_C, tm), lambda b, i: (b, 0, i))],
        out_shape=[jax.ShapeDtypeStruct((bsz, seq, n), dt) for n, dt in outs] + [jax.ShapeDtypeStruct((bsz, D_C, seq), BF16)],
        compiler_params=_params("arbitrary", "arbitrary"),
        name="dsa_prep",
    )(pa, cos, sin, wuk, kv_norm.reshape(1, D_C))


def _dsa_kernel(k_sel, qc_ref, iq_ref, iw_ref, kc_ref, ct_ref, ik_ref, tri_ref, wuv_ref, gn_ref, o_ref,
                key_scr, planes_scr, acc_scr, s0_scr, s1_scr, b0_scr, b1_scr):
    tq, kcs = DSA_TQ, DSA_KC
    t0 = pl.program_id(1) * tq
    n_chunk = (t0 + tq + kcs - 1) // kcs
    nt = (((1,), (1,)), ((), ()))
    qpos = t0 + lax.broadcasted_iota(jnp.int32, (1, tq), 1)
    iw_t = iw_ref[...].T

    def score_chunk(c, carry):
        k0 = pl.multiple_of(c * kcs, kcs)
        ikc = ik_ref[pl.ds(k0, kcs), :]
        sc = jnp.zeros((kcs, tq), F32)
        for h in range(H_I):
            lg = lax.dot_general(ikc, iq_ref[:, h * LANE:(h + 1) * LANE], nt, preferred_element_type=F32)
            sc = sc + jnp.maximum(lg, 0.0) * iw_t[DSA_IW_LANE + h:DSA_IW_LANE + h + 1, :]
        sc = jnp.where(sc == 0.0, 0.0, sc)
        kpos = k0 + lax.broadcasted_iota(jnp.int32, (kcs, 1), 0)
        sc = jnp.where(kpos <= qpos, sc, -jnp.inf)
        bits = pltpu.bitcast(sc, jnp.int32)
        key = jnp.where(bits < 0, bits ^ 0x7FFFFFFF, bits)
        key_scr[pl.ds(k0, kcs), :] = key
        ukey = key ^ INT_MIN
        for blk in range(kcs // PLANE_KEYS):
            words = _bit_transpose32(ukey[blk * PLANE_KEYS:(blk + 1) * PLANE_KEYS])
            row0 = pl.multiple_of((c * (kcs // PLANE_KEYS) + blk) * 8, 8)
            for r in range(32):
                planes_scr[r, pl.ds(row0, 8), :] = words[r]
        return carry

    lax.fori_loop(0, n_chunk, score_chunk, 0)

    n_words = planes_scr.shape[1]
    word_row = lax.broadcasted_iota(jnp.int32, (n_words, tq), 0)
    alive0 = jnp.where(word_row < n_chunk * (kcs // 32), jnp.int32(-1), jnp.int32(0))
    k_int = int(k_sel)

    def bit_pass(i, carry):
        thr_u, above, alive = carry
        ones = alive & planes_scr[i]
        seen_ones = above + jnp.sum(lax.population_count(ones), axis=0, keepdims=True)
        take = seen_ones >= k_int
        alive = jnp.where(take, ones, alive & ~planes_scr[i])
        above = jnp.where(take, above, seen_ones)
        thr_u = jnp.where(take, thr_u | jnp.left_shift(jnp.int32(1), 31 - i), thr_u)
        return thr_u, above, alive

    thr_u, above, _ = lax.fori_loop(0, 32, bit_pass, (jnp.zeros((1, tq), jnp.int32), jnp.zeros((1, tq), jnp.int32), alive0))
    thr = thr_u ^ INT_MIN
    room = (k_int - above).astype(F32)

    qall = jnp.concatenate([qc_ref[:, 2 * h * LANE:(2 * h + 2) * LANE] for h in range(H_A)], axis=0)

    room = jnp.where(thr > KEY_NEG_INF, room, 0.0)
    acc_scr[...] = jnp.zeros_like(acc_scr)

    half = kcs // 2

    def prepare(k0, seen, s_buf, b_buf):
        key = key_scr[pl.ds(k0, half), :]
        tie = jnp.where(key == thr, 1.0, 0.0)
        prefix = jnp.dot(tri_ref[...], tie.astype(BF16), preferred_element_type=F32)
        tie_bias = jnp.where(seen + prefix <= room, jnp.where(key == thr, 0.0, NEG), NEG)
        b_buf[...] = jnp.where(key > thr, 0.0, tie_bias)
        s_buf[...] = lax.dot_general(kc_ref[pl.ds(k0, half), :], qall, nt, preferred_element_type=F32)
        return seen + jnp.sum(_fold_rows(tie, jnp.add), axis=0, keepdims=True)

    def attn_chunk(c, carry):
        m, l, seen = carry
        k0 = pl.multiple_of(c * kcs, kcs)
        k1 = pl.multiple_of(k0 + half, half)
        seen = prepare(k1, seen, s1_scr, b1_scr)
        m, l = _attend_block(s0_scr, b0_scr, ct_ref[:, pl.ds(k0, half)], m, l, acc_scr, H_A, tq)
        k2 = pl.multiple_of(jnp.minimum(c + 1, n_chunk - 1) * kcs, kcs)
        seen = prepare(k2, seen, s0_scr, b0_scr)
        m, l = _attend_block(s1_scr, b1_scr, ct_ref[:, pl.ds(k1, half)], m, l, acc_scr, H_A, tq)
        return m, l, seen

    seen0 = prepare(0, jnp.zeros((1, tq), F32), s0_scr, b0_scr)
    init = (jnp.full((1, H_A * tq), M_FLOOR, F32), jnp.zeros((1, H_A * tq), F32), seen0)
    _, l, _ = lax.fori_loop(0, n_chunk, attn_chunk, init)
    o_lat = (acc_scr[...] / jnp.maximum(l, 1e-30)).astype(BF16)
    for h in range(H_A):
        o = lax.dot_general(o_lat[:, h * tq:(h + 1) * tq], wuv_ref[h], (((0,), (0,)), ((), ())),
                            preferred_element_type=F32)
        ms = jnp.sum(o * o, -1, keepdims=True) * (1.0 / DH_V)
        o_ref[:, h * LANE:(h + 1) * LANE] = (o * lax.rsqrt(ms + 1e-6) * gn_ref[h:h + 1, :]).astype(o_ref.dtype)


def _dsa_attention(qc, iq, iw, kc, ct, ik, w_uv, g_a):
    bsz, seq, _ = qc.shape
    k_sel = float(min(K_SEL_MAX, seq // 4))
    tq, kcs = DSA_TQ, DSA_KC
    half = kcs // 2
    tri = jnp.asarray(np.tril(np.ones((half, half), np.float32)), BF16)
    wuv = jnp.pad(w_uv, ((0, 0), (0, 0), (0, LANE - DH_V))).astype(BF16)
    gn = jnp.pad(g_a.reshape(H_A, DH_V), ((0, 0), (0, LANE - DH_V)))
    row = lambda n: pl.BlockSpec((None, tq, n), lambda b, i: (b, i, 0))
    per_b = lambda n: pl.BlockSpec((None, seq, n), lambda b, i: (b, 0, 0))
    return pl.pallas_call(
        functools.partial(_dsa_kernel, k_sel),
        grid=(bsz, seq // tq),
        in_specs=[row(2 * H_A * LANE), row(H_I * LANE), row(LANE), per_b(2 * LANE),
                  pl.BlockSpec((None, D_C, seq), lambda b, i: (b, 0, 0)), per_b(LANE),
                  pl.BlockSpec((half, half), lambda b, i: (0, 0)), pl.BlockSpec((H_A, D_C, LANE), lambda b, i: (0, 0, 0)),
                  pl.BlockSpec((H_A, LANE), lambda b, i: (0, 0))],
        out_specs=row(H_A * LANE),
        out_shape=jax.ShapeDtypeStruct((bsz, seq, H_A * LANE), BF16),
        scratch_shapes=[pltpu.VMEM((seq, tq), jnp.int32), pltpu.VMEM((32, seq // 32, tq), jnp.int32),
                        pltpu.VMEM((D_C, H_A * tq), F32),
                        pltpu.VMEM((half, H_A * tq), F32), pltpu.VMEM((half, H_A * tq), F32),
                        pltpu.VMEM((half, tq), F32), pltpu.VMEM((half, tq), F32)],
        compiler_params=_params("arbitrary", "arbitrary"),
        name="dsa_attention",
    )(qc, iq, iw, kc, ct, ik, tri, wuv, gn)


def _nsa_compress_kernel(a_ref, pos_ref, w1t_ref, w1b_ref, w2k_ref, w2v_ref, kc_ref, vc_ref):
    a = a_ref[...]
    top = jnp.dot((a + pos_ref[0:1, :]).astype(BF16), w1t_ref[...], preferred_element_type=F32)
    bot = jnp.dot((a + pos_ref[1:2, :]).astype(BF16), w1b_ref[...], preferred_element_type=F32)
    n = a.shape[0]
    pre = top + jnp.concatenate([bot[1:], bot[:1]], axis=0)
    h = (pre * jax.nn.sigmoid(pre)).astype(BF16)
    hid = w2k_ref.shape[0]
    kc_ref[...] = jnp.dot(h[:, :hid], w2k_ref[...], preferred_element_type=F32).astype(BF16)
    vc_ref[...] = lax.dot_general(w2v_ref[...], h[:, hid:], (((1,), (1,)), ((), ())), preferred_element_type=F32).astype(BF16)


def _nsa_compress(ncmp, cmp_pos, cmp_w1, cmp_w2):
    bsz, seq, _ = ncmp.shape
    n_grp = seq // D_STRIDE
    per = L_CMP // D_STRIDE
    width = D_STRIDE * LANE
    a = ncmp.reshape(bsz, n_grp, width)
    w1 = cmp_w1.reshape(2, per, D_STRIDE, DH_C, CMP_HIDDEN)
    zer = jnp.zeros((D_STRIDE, DH_C, CMP_HIDDEN), cmp_w1.dtype)

    def expand(p):
        wk = jnp.concatenate([w1[0, p], zer], axis=1)
        wv = jnp.concatenate([zer, w1[1, p]], axis=1)
        return jnp.concatenate([wk, wv], axis=2).reshape(width, 2 * CMP_HIDDEN).astype(BF16)

    pos = cmp_pos.reshape(2, per, D_STRIDE, DH_C)
    pos = jnp.concatenate([pos[0], pos[1]], axis=-1).reshape(per, width)
    pad_out = ((0, 0), (0, LANE - DH_C))
    w2k = jnp.pad(cmp_w2[0], pad_out).astype(BF16)
    w2v = jnp.pad(cmp_w2[1], pad_out).astype(BF16).T
    full = lambda r, c: pl.BlockSpec((r, c), lambda b: (0, 0))
    return pl.pallas_call(
        _nsa_compress_kernel,
        grid=(bsz,),
        in_specs=[pl.BlockSpec((None, n_grp, width), lambda b: (b, 0, 0)), full(per, width),
                  full(width, 2 * CMP_HIDDEN), full(width, 2 * CMP_HIDDEN), full(CMP_HIDDEN, LANE), full(LANE, CMP_HIDDEN)],
        out_specs=[pl.BlockSpec((None, n_grp, LANE), lambda b: (b, 0, 0)), pl.BlockSpec((None, LANE, n_grp), lambda b: (b, 0, 0))],
        out_shape=[jax.ShapeDtypeStruct((bsz, n_grp, LANE), BF16), jax.ShapeDtypeStruct((bsz, LANE, n_grp), BF16)],
        compiler_params=_params("arbitrary"),
        name="nsa_compress",
    )(a, pos, expand(0), expand(1), w2k, w2v)


NSA_TQ = 128
NSA_KC = 512
NEG = -1e30
M_FLOOR = -1e20


def _softmax_cols(s, bias, n_head, tq):
    out = []
    for h in range(n_head):
        sh = s[:, h * tq:(h + 1) * tq] + bias
        m = jnp.maximum(jnp.max(_fold_rows(sh, jnp.maximum), axis=0, keepdims=True), M_FLOOR)
        e = jnp.exp2(sh - m)
        den = jnp.sum(_fold_rows(e, jnp.add), axis=0, keepdims=True)
        out.append(e * (1.0 / jnp.maximum(den, 1e-30)))
    return out


def _nsa_kernel(q_ref, g_ref, kc_ref, vct_ref, ks_ref, vst_ref, kw_ref, vwt_ref, cover_ref, expand_ref, gn_ref, o_ref,
                acc_scr, s0_scr, s1_scr, b0_scr, b1_scr):
    tq, kc_sz = NSA_TQ, NSA_KC
    t0 = pl.program_id(1) * tq
    nt = (((1,), (1,)), ((), ()))
    q = q_ref[...]
    qa = jnp.concatenate([q[:, h * LANE:(h + 1) * LANE] for h in range(H_C)], axis=0)
    qpos = t0 + lax.broadcasted_iota(jnp.int32, (1, tq), 1)

    n_grp = kc_ref.shape[0]
    s_c = lax.dot_general(kc_ref[...], qa, nt, preferred_element_type=F32)
    n_idx = lax.broadcasted_iota(jnp.int32, (n_grp, 1), 0)
    visible = jnp.where(n_idx < n_grp - 1, n_idx * D_STRIDE + (L_CMP - 1), 2 ** 30)
    p_c = _softmax_cols(s_c, jnp.where(visible <= qpos, 0.0, NEG), H_C, tq)
    o_cmp = jnp.dot(vct_ref[...], jnp.concatenate([p.astype(BF16) for p in p_c], axis=1), preferred_element_type=F32)

    p_sum = p_c[0]
    for h in range(1, H_C):
        p_sum = p_sum + p_c[h]
    hi = p_sum.astype(BF16)
    lo = (p_sum - hi.astype(F32)).astype(BF16)
    cov = cover_ref[...]
    imp_t = jnp.dot(cov, hi, preferred_element_type=F32) + jnp.dot(cov, lo, preferred_element_type=F32)
    n_sel = cov.shape[0]
    jrow = lax.broadcasted_iota(jnp.int32, (n_sel, tq), 0)
    cur = (t0 + lax.broadcasted_iota(jnp.int32, (n_sel, tq), 1)) // L_SEL
    adm = jrow <= cur
    forced = (jrow == 0) | (jrow == cur) | (jrow == cur - 1)
    val = jnp.where(adm & forced, jnp.inf, jnp.where(adm, imp_t, -jnp.inf))
    rank = jnp.zeros((n_sel, tq), F32)
    for jp in range(n_sel):
        r = val[jp:jp + 1, :]
        rank = rank + jnp.where(r == val, jnp.where(jrow > jp, 1.0, 0.0), jnp.where(r > val, 1.0, 0.0))
    sel_t = jnp.where(rank < min(N_TOP_MAX, n_sel), jnp.where(val > -jnp.inf, 1.0, 0.0), 0.0)
    if n_sel < LANE:
        sel_t = jnp.concatenate([sel_t, jnp.zeros((LANE - n_sel, tq), F32)], axis=0)
    sel_t = sel_t.astype(BF16)

    acc_scr[...] = jnp.zeros_like(acc_scr)

    half = kc_sz // 2
    n_chunk = (t0 + tq + kc_sz - 1) // kc_sz

    def prepare(k0, s_buf, b_buf):
        s_buf[...] = lax.dot_general(ks_ref[pl.ds(k0, half), :], qa, nt, preferred_element_type=F32)
        picked = jnp.dot(expand_ref[pl.ds(k0, half), :], sel_t, preferred_element_type=F32)
        kpos = k0 + lax.broadcasted_iota(jnp.int32, (half, 1), 0)
        b_buf[...] = jnp.where(kpos <= qpos, jnp.where(picked > 0.5, 0.0, NEG), NEG)

    def chunk(c, carry):
        m, l = carry
        k0 = pl.multiple_of(c * kc_sz, kc_sz)
        k1 = pl.multiple_of(k0 + half, half)
        prepare(k1, s1_scr, b1_scr)
        m, l = _attend_block(s0_scr, b0_scr, vst_ref[:, pl.ds(k0, half)], m, l, acc_scr, H_C, tq)
        prepare(pl.multiple_of(jnp.minimum(c + 1, n_chunk - 1) * kc_sz, kc_sz), s0_scr, b0_scr)
        return _attend_block(s1_scr, b1_scr, vst_ref[:, pl.ds(k1, half)], m, l, acc_scr, H_C, tq)

    prepare(0, s0_scr, b0_scr)
    init = (jnp.full((1, H_C * tq), M_FLOOR, F32), jnp.zeros((1, H_C * tq), F32))
    _, l_s = lax.fori_loop(0, n_chunk, chunk, init)
    o_slc = acc_scr[...] * (1.0 / jnp.maximum(l_s, 1e-30))

    span = WINDOW + tq
    w0 = pl.multiple_of(jnp.maximum(t0 - WINDOW, 0), tq)
    s_w = lax.dot_general(kw_ref[pl.ds(w0, span), :], qa, nt, preferred_element_type=F32)
    kpos_w = w0 + lax.broadcasted_iota(jnp.int32, (span, 1), 0)
    wbias = jnp.where(kpos_w <= qpos, jnp.where(qpos - kpos_w < WINDOW, 0.0, NEG), NEG)
    p_w = _softmax_cols(s_w, wbias, H_C, tq)
    o_win = jnp.dot(vwt_ref[:, pl.ds(w0, span)], jnp.concatenate([p.astype(BF16) for p in p_w], axis=1), preferred_element_type=F32)

    gates_t = jax.nn.sigmoid(g_ref[...]).T
    for h in range(H_C):
        hs = slice(h * tq, (h + 1) * tq)
        o_t = (gates_t[h:h + 1] * o_cmp[:, hs] + gates_t[H_C + h:H_C + h + 1] * o_slc[:, hs]
               + gates_t[2 * H_C + h:2 * H_C + h + 1] * o_win[:, hs])
        o = o_t.T
        ms = jnp.sum(o * o, -1, keepdims=True) * (1.0 / DH_C)
        o_ref[:, h * LANE:(h + 1) * LANE] = (o * lax.rsqrt(ms + 1e-6) * gn_ref[h:h + 1, :]).astype(o_ref.dtype)


def _nsa_attention(nq, ng, kc, vc_t, nks, nvs_t, nkw, nvw_t, g_c):
    bsz, seq, _ = nq.shape
    n_grp = kc.shape[1]
    n_sel = seq // L_SEL
    grp_start = np.arange(n_grp) * D_STRIDE
    sel_start = np.arange(n_sel) * L_SEL
    cover_t = ((grp_start[None, :] < sel_start[:, None] + L_SEL) & (grp_start[None, :] + L_CMP > sel_start[:, None]))
    cover_t = jnp.asarray(cover_t.astype(np.float32), BF16)
    expand = (np.arange(seq)[:, None] // L_SEL == np.arange(LANE)[None, :]).astype(np.float32)
    expand = jnp.asarray(expand, BF16)
    gn = jnp.pad(g_c.reshape(H_C, DH_C), ((0, 0), (0, LANE - DH_C)))
    tq = NSA_TQ
    row = lambda n: pl.BlockSpec((None, tq, n), lambda b, i: (b, i, 0))
    per_b = lambda r, c: pl.BlockSpec((None, r, c), lambda b, i: (b, 0, 0))
    full = lambda r, c: pl.BlockSpec((r, c), lambda b, i: (0, 0))
    return pl.pallas_call(
        _nsa_kernel,
        grid=(bsz, seq // tq),
        in_specs=[row(H_C * LANE), row(LANE), per_b(n_grp, LANE), per_b(LANE, n_grp), per_b(seq, LANE), per_b(LANE, seq),
                  per_b(seq, LANE), per_b(LANE, seq), full(n_sel, n_grp), full(seq, LANE), full(H_C, LANE)],
        out_specs=row(H_C * LANE),
        out_shape=jax.ShapeDtypeStruct((bsz, seq, H_C * LANE), BF16),
        scratch_shapes=[pltpu.VMEM((LANE, H_C * tq), F32),
                        pltpu.VMEM((NSA_KC // 2, H_C * tq), F32), pltpu.VMEM((NSA_KC // 2, H_C * tq), F32),
                        pltpu.VMEM((NSA_KC // 2, tq), F32), pltpu.VMEM((NSA_KC // 2, tq), F32)],
        compiler_params=_params("arbitrary", "arbitrary"),
        name="nsa_attention",
    )(nq, ng, kc, vc_t, nks, nvs_t, nkw, nvw_t, cover_t, expand, gn)


def _outproj_kernel(ya_ref, yb_ref, yc_ref, x_ref, g_ref, wa_ref, wb_ref, wc_ref, lng_ref, lnb_ref, o_ref):
    y = jnp.dot(ya_ref[...], wa_ref[...], preferred_element_type=F32)
    y += jnp.dot(yb_ref[...], wb_ref[...], preferred_element_type=F32)
    y += jnp.dot(yc_ref[...], wc_ref[...], preferred_element_type=F32)
    z = ALPHA * x_ref[...] + (1.0 + g_ref[...]) * y
    o_ref[...] = _layer_norm_rows(z, lng_ref[...], lnb_ref[...])


def _pad_head_rows(w, n_head, dh):
    d = w.shape[1]
    return jnp.pad(w.reshape(n_head, dh, d), ((0, 0), (0, LANE - dh), (0, 0))).reshape(n_head * LANE, d)


def _output_projection(ya, yb, yc, x, g, wa, wb, wc, ln_g, ln_b):
    bsz, seq, d = x.shape
    tm = 512
    na, nb, nc = ya.shape[-1], yb.shape[-1], yc.shape[-1]
    row = lambda n: pl.BlockSpec((None, tm, n), lambda b, i: (b, i, 0))
    vec = pl.BlockSpec((None, 1, d), lambda b, i: (b, 0, 0))
    full = lambda r, c: pl.BlockSpec((r, c), lambda b, i: (0, 0))
    return pl.pallas_call(
        _outproj_kernel,
        grid=(bsz, seq // tm),
        in_specs=[row(na), row(nb), row(nc), row(d), vec, full(na, d), full(nb, d), full(nc, d), full(1, d), full(1, d)],
        out_specs=row(d),
        out_shape=jax.ShapeDtypeStruct((bsz, seq, d), F32),
        compiler_params=_params("arbitrary", "arbitrary"),
        name="output_projection_ln",
    )(ya, yb, yc, x, g, wa, wb, wc, ln_g.reshape(1, d), ln_b.reshape(1, d))


def _tile_gate_up(wg, wu, tf):
    *lead, d, ff = wg.shape
    split = lambda w: jnp.moveaxis(w.reshape(*lead, d, ff // tf, tf), -2, -3)
    return jnp.concatenate([split(wg), split(wu)], axis=-1)


def _swiglu_chunk(u, wgu_ref, wd_ref):
    tf = wd_ref.shape[0]
    ab = jnp.dot(u, wgu_ref[...], preferred_element_type=F32)
    a, b = ab[:, :tf], ab[:, tf:]
    return jnp.dot((a * jax.nn.sigmoid(a) * b).astype(BF16), wd_ref[...], preferred_element_type=F32)


def _ffn_kernel(x_ref, sc_ref, sh_ref, g_ref, wgu_ref, wd_ref, lng_ref, lnb_ref, o_ref, u_scr, acc_scr):
    f = pl.program_id(2)

    @pl.when(f == 0)
    def _():
        u_scr[...] = (x_ref[...] * (1.0 + sc_ref[...]) + sh_ref[...]).astype(BF16)
        acc_scr[...] = jnp.zeros_like(acc_scr)

    acc_scr[...] += _swiglu_chunk(u_scr[...], wgu_ref, wd_ref)

    @pl.when(f == pl.num_programs(2) - 1)
    def _():
        z = ALPHA * x_ref[...] + (1.0 + g_ref[...]) * acc_scr[...]
        o_ref[...] = _layer_norm_rows(z, lng_ref[...], lnb_ref[...])


def _dense_ffn(x, sc, sh, g, wg, wu, wd, ln_g, ln_b):
    bsz, seq, d = x.shape
    ff = wg.shape[1]
    tm, tf = 1024, 256
    row = pl.BlockSpec((None, tm, d), lambda b, i, f: (b, i, 0))
    vec = pl.BlockSpec((None, 1, d), lambda b, i, f: (b, 0, 0))
    one = pl.BlockSpec((1, d), lambda b, i, f: (0, 0))
    return pl.pallas_call(
        _ffn_kernel,
        grid=(bsz, seq // tm, ff // tf),
        in_specs=[row, vec, vec, vec,
                  pl.BlockSpec((None, d, 2 * tf), lambda b, i, f: (f, 0, 0)),
                  pl.BlockSpec((tf, d), lambda b, i, f: (f, 0)),
                  one, one],
        out_specs=row,
        out_shape=jax.ShapeDtypeStruct((bsz, seq, d), F32),
        scratch_shapes=[pltpu.VMEM((tm, d), BF16), pltpu.VMEM((tm, d), F32)],
        compiler_params=_params("arbitrary", "arbitrary", "arbitrary"),
        name="dense_swiglu_ln",
    )(x, sc, sh, g, _tile_gate_up(wg, wu, tf), wd, ln_g.reshape(1, d), ln_b.reshape(1, d))


def _router_kernel(x_ref, sc_ref, sh_ref, r_ref, lo_ref, up_ref, gate_ref, slot_ref, slot_t_ref, cnt_ref):
    u = x_ref[...] * (1.0 + sc_ref[...]) + sh_ref[...]
    logits = jnp.dot(u, r_ref[...], preferred_element_type=F32, precision=lax.Precision.HIGHEST)
    lane = lax.broadcasted_iota(jnp.int32, logits.shape, 1)
    neg = -jnp.inf
    l1 = jnp.where(lane < N_EXPERTS, logits, neg)
    m1 = jnp.max(l1, -1, keepdims=True)
    i1 = jnp.min(jnp.where(l1 == m1, lane, LANE), -1, keepdims=True)
    l2 = jnp.where(lane == i1, neg, l1)
    m2 = jnp.max(l2, -1, keepdims=True)
    i2 = jnp.min(jnp.where(l2 == m2, lane, LANE), -1, keepdims=True)
    e2 = jnp.exp(m2 - m1)
    w1 = 1.0 / (1.0 + e2)
    w2 = e2 / (1.0 + e2)
    gate_ref[...] = jnp.where(lane == i1, w1, jnp.where(lane == i2, w2, 0.0))
    routed = jnp.where((lane == i1) | (lane == i2), 1.0, 0.0)
    before = jnp.dot(lo_ref[...], routed.astype(BF16), preferred_element_type=F32)
    slot_ref[...] = jnp.where(routed > 0.5, before, -1.0)
    routed_t = routed.T
    before_t = jnp.dot(routed_t.astype(BF16), up_ref[...], preferred_element_type=F32)
    slot_t_ref[...] = jnp.where(routed_t > 0.5, before_t, -1.0)
    cnt_ref[...] = jnp.broadcast_to(jnp.sum(routed, axis=0, keepdims=True), cnt_ref.shape).astype(jnp.int32)


MOE_TM = 1024
MOE_ROWS = 288
MOE_TF = 896
MOE_GROUP = 2


def _moe_router(x, sc, sh, router):
    bsz, seq, d = x.shape
    tm = MOE_TM
    r = jnp.pad(router, ((0, 0), (0, LANE - router.shape[1])))
    upper = np.triu(np.ones((tm, tm), np.float32), 1)
    up, lo = jnp.asarray(upper, BF16), jnp.asarray(upper.T, BF16)
    row = lambda n: pl.BlockSpec((None, tm, n), lambda b, i: (b, i, 0))
    vec = pl.BlockSpec((None, 1, d), lambda b, i: (b, 0, 0))
    full = lambda a, c: pl.BlockSpec((a, c), lambda b, i: (0, 0))
    n_tile = seq // tm
    return pl.pallas_call(
        _router_kernel,
        grid=(bsz, n_tile),
        in_specs=[row(d), vec, vec, full(d, LANE), full(tm, tm), full(tm, tm)],
        out_specs=[row(LANE), row(LANE), pl.BlockSpec((None, None, LANE, tm), lambda b, i: (b, i, 0, 0)),
                   pl.BlockSpec((None, None, 8, LANE), lambda b, i: (b, i, 0, 0))],
        out_shape=[jax.ShapeDtypeStruct((bsz, seq, LANE), F32), jax.ShapeDtypeStruct((bsz, seq, LANE), F32),
                   jax.ShapeDtypeStruct((bsz, n_tile, LANE, tm), F32), jax.ShapeDtypeStruct((bsz, n_tile, 8, LANE), jnp.int32)],
        compiler_params=_params("arbitrary", "arbitrary"),
        name="moe_router",
    )(x, sc, sh, r, lo, up)


def _moe_kernel(cnt_ref, x_ref, sc_ref, sh_ref, g_ref, gate_ref, slot_ref, slot_t_ref, wgu_ref, wd_ref, lng_ref, lnb_ref,
                o_ref, u_scr, xg_scr, acc_scr):
    tm, rows = MOE_TM, MOE_ROWS
    max_pass = xg_scr.shape[0] // MOE_GROUP
    e = pl.program_id(2)
    f = pl.program_id(3)
    first_tile = (pl.program_id(0) * pl.num_programs(1) + pl.program_id(1)) * MOE_GROUP
    n_pass = [(cnt_ref[(first_tile + t) * N_EXPERTS + e] + rows - 1) // rows for t in range(MOE_GROUP)]

    @pl.when((e == 0) & (f == 0))
    def _():
        u_scr[...] = (x_ref[...] * (1.0 + sc_ref[...]) + sh_ref[...]).astype(BF16)
        o_ref[...] = jnp.zeros_like(o_ref)

    @pl.when(f == 0)
    def _():
        for t in range(MOE_GROUP):
            slot_row = slot_t_ref[t, pl.ds(e, 1), :]

            def gather(p, carry, t=t, slot_row=slot_row):
                want = p * rows + lax.broadcasted_iota(jnp.int32, (rows, 1), 0)
                pick = jnp.where(slot_row == want.astype(F32), 1.0, 0.0).astype(BF16)
                xg_scr[t * max_pass + p] = jnp.dot(pick, u_scr[t * tm:(t + 1) * tm, :], preferred_element_type=F32).astype(BF16)
                acc_scr[t * max_pass + p] = jnp.zeros((rows, x_ref.shape[-1]), F32)
                return carry

            lax.fori_loop(0, n_pass[t], gather, 0)

    for t in range(MOE_GROUP):
        def ffn(p, carry, t=t):
            acc_scr[t * max_pass + p] += _swiglu_chunk(xg_scr[t * max_pass + p], wgu_ref, wd_ref)
            return carry

        lax.fori_loop(0, n_pass[t], ffn, 0)

    @pl.when(f == pl.num_programs(3) - 1)
    def _():
        lane = lax.broadcasted_iota(jnp.int32, (tm, LANE), 1)
        for t in range(MOE_GROUP):
            tok = slice(t * tm, (t + 1) * tm)
            slot_col = jnp.sum(jnp.where(lane == e, slot_ref[tok, :], 0.0), axis=1, keepdims=True)
            gate_col = jnp.sum(jnp.where(lane == e, gate_ref[tok, :], 0.0), axis=1, keepdims=True)

            def scatter(p, carry, t=t, tok=tok, slot_col=slot_col, gate_col=gate_col):
                col = lax.broadcasted_iota(jnp.int32, (1, 2 * rows), 1)
                want = p * rows + jnp.where(col >= rows, col - rows, col)
                put = jnp.where(slot_col == want.astype(F32), 1.0, 0.0).astype(BF16)
                y = acc_scr[t * max_pass + p]
                hi = y.astype(BF16)
                lo = (y - hi.astype(F32)).astype(BF16)
                back = jnp.dot(put, jnp.concatenate([hi, lo], axis=0), preferred_element_type=F32)
                o_ref[tok, :] += gate_col * back
                return carry

            lax.fori_loop(0, n_pass[t], scatter, 0)

    @pl.when((e == pl.num_programs(2) - 1) & (f == pl.num_programs(3) - 1))
    def _():
        z = ALPHA * x_ref[...] + (1.0 + g_ref[...]) * o_ref[...]
        o_ref[...] = _layer_norm_rows(z, lng_ref[...], lnb_ref[...])


def _moe_ffn(x, sc, sh, g, routing, wg, wu, wd, ln_g, ln_b):
    gate, slot, slot_t, cnt = routing
    bsz, seq, d = x.shape
    n_e, _, ff = wg.shape
    tm, tf, rows, grp = MOE_TM, MOE_TF, MOE_ROWS, MOE_GROUP
    max_pass = -(-tm // rows)
    counts = cnt[:, :, 0, :n_e].reshape(-1)
    once = pl.Buffered(1)
    row = lambda n: pl.BlockSpec((None, grp * tm, n), lambda b, i, e, f, c: (b, i, 0), pipeline_mode=once)
    vec = pl.BlockSpec((None, 1, d), lambda b, i, e, f, c: (b, 0, 0))
    one = pl.BlockSpec((1, d), lambda b, i, e, f, c: (0, 0))
    grid_spec = pltpu.PrefetchScalarGridSpec(
        num_scalar_prefetch=1,
        grid=(bsz, seq // (grp * tm), n_e, ff // tf),
        in_specs=[row(d), vec, vec, vec, row(LANE), row(LANE),
                  pl.BlockSpec((None, grp, LANE, tm), lambda b, i, e, f, c: (b, i, 0, 0), pipeline_mode=once),
                  pl.BlockSpec((None, None, d, 2 * tf), lambda b, i, e, f, c: (e, f, 0, 0)),
                  pl.BlockSpec((None, tf, d), lambda b, i, e, f, c: (e, f, 0)),
                  one, one],
        out_specs=row(d),
        scratch_shapes=[pltpu.VMEM((grp * tm, d), BF16), pltpu.VMEM((grp * max_pass, rows, d), BF16),
                        pltpu.VMEM((grp * max_pass, rows, d), F32)],
    )
    return pl.pallas_call(
        _moe_kernel,
        grid_spec=grid_spec,
        out_shape=jax.ShapeDtypeStruct((bsz, seq, d), F32),
        compiler_params=_params("arbitrary", "arbitrary", "arbitrary", "arbitrary"),
        name="moe_swiglu_ln",
    )(counts, x, sc, sh, g, gate, slot, slot_t, _tile_gate_up(wg, wu, tf), wd, ln_g.reshape(1, d), ln_b.reshape(1, d))


ML_TT = 256
ML_SUB = 128


ML_NB = 1


def _mlstm_kernel(qk_ref, tail_ref, v_ref, og_ref, g_ref, cw_ref, cb_ref, gb_ref, gn_ref, y_ref,
                  ct_scr, n_scr, m_scr, q_scr, k_scr):
    @pl.when(pl.program_id(1) == 0)
    def _():
        ct_scr[...] = jnp.zeros_like(ct_scr)
        n_scr[...] = jnp.zeros_like(n_scr)
        m_scr[...] = jnp.zeros_like(m_scr)

    for nb in range(ML_NB):
        _mlstm_rows(qk_ref.at[nb], tail_ref.at[nb], v_ref.at[nb], og_ref.at[nb], g_ref.at[nb], cw_ref, cb_ref, gb_ref, gn_ref,
                    y_ref.at[nb], ct_scr.at[nb], n_scr.at[nb], m_scr.at[nb], q_scr.at[nb], k_scr.at[nb])


def _mlstm_rows(qk_ref, tail_ref, v_ref, og_ref, g_ref, cw_ref, cb_ref, gb_ref, gn_ref, y_ref,
                ct_scr, n_scr, m_scr, q_scr, k_scr):
    tt = ML_TT
    step = pl.program_id(1)

    x = qk_ref[...]
    tail = jnp.where(step == 0, 0.0, tail_ref[...])
    row8 = lax.broadcasted_iota(jnp.int32, (8, 1), 0)
    pre = x * cw_ref[CONV_W - 1:CONV_W, :] + cb_ref[...]
    for s in range(1, CONV_W):
        rolled = pltpu.roll(x, s, 0)
        head = jnp.where(row8 < s, pltpu.roll(tail, s, 0), rolled[0:8])
        pre = pre + jnp.concatenate([head, rolled[8:]], axis=0) * cw_ref[CONV_W - 1 - s:CONV_W - s, :]
    act = pre * jax.nn.sigmoid(pre)
    q_scr[...] = act[:, 0:ML_D].astype(BF16)
    k_scr[...] = (act[:, ML_D:2 * ML_D] * (DH_B ** -0.5)).astype(BF16)

    lane = lax.broadcasted_iota(jnp.int32, (1, LANE), 1)
    tok = lane % CHUNK
    jj = lax.broadcasted_iota(jnp.int32, (CHUNK, CHUNK), 0)
    ss = lax.broadcasted_iota(jnp.int32, (CHUNK, CHUNK), 1)
    nt = (((1,), (1,)), ((), ()))
    tn = (((0,), (0,)), ((), ()))
    ct_state = [ct_scr[h] for h in range(H_B)]
    n_state = [n_scr[h:h + 1, :] for h in range(H_B)]
    m_state = [m_scr[h:h + 1, 0:1] for h in range(H_B)]
    for sub in range(tt // ML_SUB):
        r0 = sub * ML_SUB
        gp = g_ref[r0:r0 + ML_SUB, :] + gb_ref[...]
        lsig = jnp.minimum(gp, 0.0) - jnp.log(1.0 + jnp.exp(-jnp.abs(gp)))
        col = jnp.where(lane < H_B, gp, lsig)
        rowl = col.T
        b = rowl[0:8]
        for sft in (1, 2, 4, 8, 16, 32):
            b = b + jnp.where(tok >= sft, pltpu.roll(b, sft, 1), 0.0)
        bcol = jnp.concatenate([b, jnp.zeros((LANE - 8, ML_SUB), F32)], axis=0).T
        for ci in range(ML_SUB // CHUNK):
            c0 = ci * CHUNK
            rows = slice(r0 + c0, r0 + c0 + CHUNK)
            for h in range(H_B):
                hs = slice(h * DH_B, (h + 1) * DH_B)
                b_col = bcol[c0:c0 + CHUNK, H_B + h:H_B + h + 1]
                ig_col = col[c0:c0 + CHUNK, h:h + 1]
                b_row = b[H_B + h:H_B + h + 1, c0:c0 + CHUNK]
                ig_row = rowl[h:h + 1, c0:c0 + CHUNK]
                qh, kh, vh = q_scr[rows, hs], k_scr[rows, hs], v_ref[rows, hs]
                log_d = jnp.where(jj >= ss, b_col - b_row + ig_row, NEG)
                m_loc = jnp.max(log_d, -1, keepdims=True)
                s_loc = lax.dot_general(qh, kh, nt, preferred_element_type=F32) * jnp.exp(log_d - m_loc)
                sv_loc = jnp.dot(s_loc.astype(BF16), vh, preferred_element_type=F32)
                rs_loc = jnp.sum(s_loc, -1, keepdims=True)
                b_last = b_row[:, CHUNK - 1:CHUNK]
                w_max = jnp.max(b_last - b_row + ig_row, -1, keepdims=True)
                w_loc = jnp.exp(b_last - b_col + ig_col - w_max)
                inc_c = lax.dot_general(kh, (w_loc * vh.astype(F32)).astype(BF16), tn, preferred_element_type=F32)
                inc_n = jnp.sum(w_loc * kh.astype(F32), axis=0, keepdims=True)
                m_old, ct, n_row = m_state[h], ct_state[h], n_state[h]
                log_inter = b_col + m_old
                m_out = jnp.maximum(log_inter, m_loc)
                w_inter = jnp.exp(log_inter - m_out)
                w_intra = jnp.exp(m_loc - m_out)
                num = w_inter * jnp.dot(qh, ct.astype(BF16), preferred_element_type=F32) + w_intra * sv_loc
                den = w_inter * jnp.sum(qh.astype(F32) * n_row, -1, keepdims=True) + w_intra * rs_loc
                hid = num / jnp.maximum(jnp.abs(den), jnp.exp(-m_out))
                m_new = jnp.maximum(b_last + m_old, w_max)
                decay = jnp.exp(b_last + m_old - m_new)
                grow = jnp.exp(w_max - m_new)
                ct_state[h] = decay * ct + grow * inc_c
                n_state[h] = decay * n_row + grow * inc_n
                m_state[h] = m_new
                ms = jnp.mean(hid * hid, -1, keepdims=True)
                y = hid * lax.rsqrt(ms + 1e-6) * gn_ref[:, hs] * jax.nn.sigmoid(og_ref[rows, hs])
                y_ref[rows, hs] = y.astype(y_ref.dtype)
    for h in range(H_B):
        ct_scr[h] = ct_state[h]
        n_scr[h:h + 1, :] = n_state[h]
        m_scr[h:h + 1, :] = jnp.broadcast_to(m_state[h], (1, LANE))


def _mlstm(mqk, mv, mo, mg, conv_w, conv_b, gate_b, g_b):
    bsz, seq, _ = mqk.shape
    tt = ML_TT
    gb = jnp.pad(gate_b.reshape(1, 2 * H_B), ((0, 0), (0, LANE - 2 * H_B)))
    nb = ML_NB
    row = lambda n: pl.BlockSpec((nb, tt, n), lambda b, i: (b, i, 0))
    full = lambda r, c: pl.BlockSpec((r, c), lambda b, i: (0, 0))
    tail = pl.BlockSpec((nb, 8, 2 * ML_D), lambda b, i: (b, jnp.maximum(i * (tt // 8) - 1, 0), 0))
    return pl.pallas_call(
        _mlstm_kernel,
        grid=(bsz // nb, seq // tt),
        in_specs=[row(2 * ML_D), tail, row(ML_D), row(ML_D), row(LANE),
                  full(CONV_W, 2 * ML_D), full(1, 2 * ML_D), full(1, LANE), full(1, ML_D)],
        out_specs=row(ML_D),
        out_shape=jax.ShapeDtypeStruct((bsz, seq, ML_D), BF16),
        scratch_shapes=[pltpu.VMEM((nb, H_B, DH_B, DH_B), F32), pltpu.VMEM((nb, 8, DH_B), F32), pltpu.VMEM((nb, 8, LANE), F32),
                        pltpu.VMEM((nb, tt, ML_D), BF16), pltpu.VMEM((nb, tt, ML_D), BF16)],
        compiler_params=_params("arbitrary", "arbitrary"),
        name="mlstm_scan",
    )(mqk, mqk, mv, mo, mg, conv_w, conv_b.reshape(1, 2 * ML_D), gb, g_b.reshape(1, ML_D))


def _mixers(pa, mls, nsa, rope, w_uk, w_uv, kv_norm, conv_w, conv_b, gate_b, cmp_pos, cmp_w1, cmp_w2, grp_norm):
    g_a, g_b, g_c = jnp.split(grp_norm, [H_A * DH_V, H_A * DH_V + H_B * DH_B])
    qc, iq, iw, kc_a, ik, ct_a = _dsa_prep(pa, rope[0], rope[1], w_uk, kv_norm)
    y_a = _dsa_attention(qc, iq, iw, kc_a, ct_a, ik, w_uv, g_a)
    y_b = _mlstm(*mls, conv_w, conv_b, gate_b, g_b)
    nq, ncmp, nks, nvs, nkw, nvw, ng = nsa
    kc, vc = _nsa_compress(ncmp, cmp_pos, cmp_w1, cmp_w2)
    y_c = _nsa_attention(nq, ng, kc, vc, nks, nvs, nkw, nvw, g_c)
    return y_a, y_b, y_c


def _pad_cols(w, n):
    return jnp.pad(w, ((0, 0), (0, n - w.shape[1])))


def kernel(x, c, positions, w_mod, b_mod, w_in, dsa_w_uk, dsa_w_uv, dsa_kv_norm, mlstm_conv_w, mlstm_conv_b, mlstm_gate_b, nsa_cmp_pos, nsa_cmp_w1, nsa_cmp_w2, grp_norm, w_out, ln_g, ln_b, ffn_w_gate, ffn_w_up, ffn_w_down, moe_router, moe_w_gate, moe_w_up, moe_w_down):
    bsz = x.shape[0]
    mod = _modulation(c, w_mod, b_mod).reshape(bsz, DEPTH, N_MOD, 1, D_MODEL)
    rope = _rope_table(positions)
    for l in range(DEPTH):
        sh1, sc1, g1, sh2, sc2, g2 = [mod[:, l, j] for j in range(N_MOD)]
        w = w_in[l].astype(BF16)
        wa = _dsa_weight_layout(w[:, :N_GROUP_A])
        wb = _mlstm_weight_layout(w[:, N_GROUP_A:N_GROUP_A + N_GROUP_B])
        wc, wvt = _nsa_weight_layout(w[:, N_GROUP_A + N_GROUP_B:])
        pa, mqk, mv, mo, mg, *nsa = _input_projection(x, sc1, sh1, wa, wb, wc, wvt)
        ya, yb, yc = _mixers(pa, (mqk, mv, mo, mg), nsa, rope, dsa_w_uk[l], dsa_w_uv[l], dsa_kv_norm[l], mlstm_conv_w[l], mlstm_conv_b[l], mlstm_gate_b[l], nsa_cmp_pos[l], nsa_cmp_w1[l], nsa_cmp_w2[l], grp_norm[l])
        wo = w_out[l].astype(BF16)
        n_a, n_b = H_A * DH_V, H_B * DH_B
        x = _output_projection(ya, yb, yc, x, g1, _pad_head_rows(wo[:n_a], H_A, DH_V), wo[n_a:n_a + n_b], _pad_head_rows(wo[n_a + n_b:], H_C, DH_C), ln_g[l, 0], ln_b[l, 0])
        if l % 2 == 0:
            k = l // 2
            x = _dense_ffn(x, sc2, sh2, g2, ffn_w_gate[k].astype(BF16), ffn_w_up[k].astype(BF16), ffn_w_down[k].astype(BF16), ln_g[l, 1], ln_b[l, 1])
        else:
            k = l // 2
            routing = _moe_router(x, sc2, sh2, moe_router[k])
            x = _moe_ffn(x, sc2, sh2, g2, routing,moe_w_gate[k].astype(BF16), moe_w_up[k].astype(BF16), moe_w_down[k].astype(BF16), ln_g[l, 1], ln_b[l, 1])
    return x
```

```python
import functools

import numpy as np
import jax
import jax.numpy as jnp
from jax import lax
from jax.experimental import pallas as pl
from jax.experimental.pallas import tpu as pltpu

F32 = jnp.float32
BF16 = jnp.bfloat16

D_MODEL = 1024
DEPTH = 2
H_A, DH_NOPE, DH_ROPE, D_C, DH_V, H_I, D_I = 4, 64, 32, 128, 64, 4, 64
K_SEL_MAX = 256
ROPE_THETA = 10000.0
H_B, DH_B, CONV_W, CHUNK = 4, 128, 4, 64
H_C, DH_C, L_CMP, D_STRIDE, CMP_HIDDEN, L_SEL, N_TOP_MAX, WINDOW, Q_BLOCK = 4, 64, 32, 16, 128, 64, 16, 512, 128
D_FF = 2816
N_EXPERTS = 8
D_FF_EXPERT = 3584
N_MOD = 6
ALPHA = (2 * DEPTH) ** 0.25
SPLIT_SIZES = (H_A * DH_NOPE, H_A * DH_ROPE, D_C, DH_ROPE, H_I * D_I, D_I, H_I, H_B * DH_B, H_B * DH_B, H_B * DH_B, H_B, H_B, H_B * DH_B, H_C * DH_C, DH_C, DH_C, DH_C, DH_C, DH_C, DH_C, 3 * H_C)
N_GROUP_A = sum(SPLIT_SIZES[:7])
N_GROUP_B = sum(SPLIT_SIZES[7:13])
N_GROUP_C = sum(SPLIT_SIZES[13:])

LOG2E = 1.4426950408889634
LANE = 128
VMEM_LIMIT = 56 * 1024 * 1024


def _round_up(n, m):
    return (n + m - 1) // m * m


def _params(*sem):
    return pltpu.CompilerParams(dimension_semantics=sem, vmem_limit_bytes=VMEM_LIMIT)


FOLD_ROWS = 64


def _fold_rows(x, op):
    parts = [x[i:i + FOLD_ROWS] for i in range(0, x.shape[0], FOLD_ROWS)]
    while len(parts) > 1:
        parts = [op(parts[i], parts[i + 1]) if i + 1 < len(parts) else parts[i] for i in range(0, len(parts), 2)]
    return parts[0]


def _attend_block(s_buf, b_buf, v_t, m, l, acc_scr, n_head, tq):
    bias = b_buf[...]
    m_new, l_new, ps = [], [], []
    for h in range(n_head):
        hs = slice(h * tq, (h + 1) * tq)
        sh = s_buf[:, hs] + bias
        mh = jnp.maximum(m[:, hs], jnp.max(_fold_rows(sh, jnp.maximum), axis=0, keepdims=True))
        ph = jnp.exp2(sh - mh)
        l_new.append(jnp.exp2(m[:, hs] - mh) * l[:, hs] + jnp.sum(_fold_rows(ph, jnp.add), axis=0, keepdims=True))
        m_new.append(mh)
        ps.append(ph.astype(BF16))
    m_new = jnp.concatenate(m_new, axis=1)
    acc_scr[...] = jnp.exp2(m - m_new) * acc_scr[...] + jnp.dot(v_t, jnp.concatenate(ps, axis=1), preferred_element_type=F32)
    return m_new, jnp.concatenate(l_new, axis=1)


PLANE_KEYS = 256


def _bit_transpose32(x):
    w = [x[8 * i:8 * i + 8, :] for i in range(32)]
    j, m = 16, 0x0000FFFF
    while j:
        k = 0
        while k < 32:
            t = (w[k] ^ lax.shift_right_logical(w[k + j], j)) & m
            w[k] = w[k] ^ t
            w[k + j] = w[k + j] ^ jnp.left_shift(t, j)
            k = (k + j + 1) & ~j
        j >>= 1
        m ^= (m << j) & 0xFFFFFFFF
    return w


def _layer_norm_rows(z, g, b):
    mu = jnp.mean(z, -1, keepdims=True)
    zc = z - mu
    var = jnp.mean(zc * zc, -1, keepdims=True)
    return zc * lax.rsqrt(var + 1e-5) * g + b


def _mod_kernel(c_ref, w_ref, b_ref, o_ref):
    c = c_ref[...]
    a = c * jax.nn.sigmoid(c)
    o_ref[...] = jnp.dot(a, w_ref[...], preferred_element_type=F32, precision=lax.Precision.HIGHEST) + b_ref[...]


def _modulation(c, w_mod, b_mod):
    bsz, d = c.shape
    n = w_mod.shape[1]
    tn = 1024
    return pl.pallas_call(
        _mod_kernel,
        grid=(n // tn,),
        in_specs=[pl.BlockSpec((bsz, d), lambda j: (0, 0)),
                  pl.BlockSpec((d, tn), lambda j: (0, j)),
                  pl.BlockSpec((1, tn), lambda j: (0, j))],
        out_specs=pl.BlockSpec((bsz, tn), lambda j: (0, j)),
        out_shape=jax.ShapeDtypeStruct((bsz, n), F32),
        compiler_params=_params("arbitrary"),
        name="adaln_mod",
    )(c, w_mod, b_mod.reshape(1, n))


NSA_Q0 = 0
NSA_CMP0 = H_C * LANE
NSA_KS0 = NSA_CMP0 + LANE
NSA_KW0 = NSA_KS0 + LANE
NSA_G0 = NSA_KW0 + LANE
NSA_COLS = NSA_G0 + LANE


def _nsa_weight_layout(w):
    d = w.shape[0]
    nq, nkc, nvc, nks, nvs, nkw, nvw, ng = jnp.split(w, np.cumsum(SPLIT_SIZES[13:])[:-1].tolist(), axis=1)
    z = lambda n: jnp.zeros((d, n), w.dtype)
    half = LANE - DH_C
    cols = []
    for h in range(H_C):
        cols += [nq[:, h * DH_C:(h + 1) * DH_C], z(half)]
    cols += [nkc, nvc, nks, z(half), nkw, z(half), ng, z(LANE - 3 * H_C)]
    values_t = jnp.concatenate([nvs, z(half), nvw, z(half)], axis=1).T
    return jnp.concatenate(cols, axis=1), values_t


def _mlstm_weight_layout(w):
    mq, mk, mv, mi, mf, mo = jnp.split(w, np.cumsum(SPLIT_SIZES[7:13])[:-1].tolist(), axis=1)
    return jnp.concatenate([mq, mk, mv, mo, mi, mf, jnp.zeros((w.shape[0], LANE - 2 * H_B), w.dtype)], axis=1)


ML_D = H_B * DH_B


def _inproj_kernel(x_ref, sc_ref, sh_ref, wa_ref, wb_ref, wc_ref, wvt_ref, oa_ref, mqk_ref, mv_ref, mo_ref, mg_ref,
                   nq_ref, ncmp_ref, nks_ref, nvs_ref, nkw_ref, nvw_ref, ng_ref):
    u = (x_ref[...] * (1.0 + sc_ref[...]) + sh_ref[...]).astype(BF16)
    oa_ref[...] = jnp.dot(u, wa_ref[...], preferred_element_type=F32)
    ob = jnp.dot(u, wb_ref[...], preferred_element_type=F32)
    mqk_ref[...] = ob[:, 0:2 * ML_D]
    mv_ref[...] = ob[:, 2 * ML_D:3 * ML_D].astype(BF16)
    mo_ref[...] = ob[:, 3 * ML_D:4 * ML_D]
    mg_ref[...] = ob[:, 4 * ML_D:4 * ML_D + LANE]
    oc = jnp.dot(u, wc_ref[...], preferred_element_type=F32)
    nq_ref[...] = (oc[:, NSA_Q0:NSA_Q0 + H_C * LANE] * (DH_C ** -0.5 * LOG2E)).astype(BF16)
    ncmp_ref[...] = oc[:, NSA_CMP0:NSA_CMP0 + LANE]
    nks_ref[...] = oc[:, NSA_KS0:NSA_KS0 + LANE].astype(BF16)
    nkw_ref[...] = oc[:, NSA_KW0:NSA_KW0 + LANE].astype(BF16)
    ng_ref[...] = oc[:, NSA_G0:NSA_G0 + LANE]
    vt = lax.dot_general(wvt_ref[...], u, (((1,), (1,)), ((), ())), preferred_element_type=F32)
    nvs_ref[...] = vt[0:LANE].astype(BF16)
    nvw_ref[...] = vt[LANE:2 * LANE].astype(BF16)


def _input_projection(x, sc, sh, wa, wb, wc, wvt):
    bsz, seq, d = x.shape
    tm = 512
    na, nb, nc = wa.shape[1], wb.shape[1], wc.shape[1]
    row = lambda n: pl.BlockSpec((None, tm, n), lambda b, i: (b, i, 0))
    col = pl.BlockSpec((None, LANE, tm), lambda b, i: (b, 0, i))
    vec = pl.BlockSpec((None, 1, d), lambda b, i: (b, 0, 0))
    full = lambda n: pl.BlockSpec((d, n), lambda b, i: (0, 0))
    tok = lambda n, dt: (row(n), jax.ShapeDtypeStruct((bsz, seq, n), dt))
    feat = (col, jax.ShapeDtypeStruct((bsz, LANE, seq), BF16))
    outs = [tok(na, F32), tok(2 * ML_D, F32), tok(ML_D, BF16), tok(ML_D, F32), tok(LANE, F32),
            tok(H_C * LANE, BF16), tok(LANE, F32), tok(LANE, BF16), feat, tok(LANE, BF16), feat, tok(LANE, F32)]
    return pl.pallas_call(
        _inproj_kernel,
        grid=(bsz, seq // tm),
        in_specs=[row(d), vec, vec, full(na), full(nb), full(nc), pl.BlockSpec((2 * LANE, d), lambda b, i: (0, 0))],
        out_specs=[spec for spec, _ in outs],
        out_shape=[shape for _, shape in outs],
        compiler_params=_params("arbitrary", "arbitrary"),
        name="input_projection",
    )(x, sc, sh, wa, wb, wc, wvt)


DSA_QN0 = 0
DSA_QR0 = H_A * LANE
DSA_CKV0 = DSA_QR0 + LANE
DSA_IQ0 = DSA_CKV0 + LANE
DSA_G0 = DSA_IQ0 + H_I * LANE
DSA_COLS = DSA_G0 + LANE
DSA_KR_LANE = D_I
DSA_IW_LANE = D_I + DH_ROPE
DSA_TQ = 128
DSA_KC = 512
INT_MIN = -2 ** 31
KEY_NEG_INF = int(np.int32(np.float32(-np.inf).view(np.int32)) ^ np.int32(0x7FFFFFFF))


def _dsa_weight_layout(w):
    d = w.shape[0]
    qn, qr, ckv, kr, iq, ik, iw = jnp.split(w, np.cumsum(SPLIT_SIZES[:7])[:-1].tolist(), axis=1)
    z = lambda n: jnp.zeros((d, n), w.dtype)
    cols = []
    for h in range(H_A):
        cols += [qn[:, h * DH_NOPE:(h + 1) * DH_NOPE], z(LANE - DH_NOPE)]
    cols += [qr, ckv]
    for h in range(H_I):
        cols += [iq[:, h * D_I:(h + 1) * D_I], z(LANE - D_I)]
    cols += [ik, kr, iw, z(LANE - D_I - DH_ROPE - H_I)]
    return jnp.concatenate(cols, axis=1)


def _rope_table_kernel(pos_ref, freq_ref, cos_ref, sin_ref):
    ang = pos_ref[...].astype(F32) * freq_ref[...]
    lane = lax.broadcasted_iota(jnp.int32, (1, LANE), 1)
    first = (lane % DH_ROPE) < DH_ROPE // 2
    cos_ref[...] = jnp.cos(ang)
    sin_ref[...] = jnp.where(first, -jnp.sin(ang), jnp.sin(ang))


def _rope_table(positions):
    bsz, seq = positions.shape
    tm = 512
    inv_freq = ROPE_THETA ** (-jnp.arange(0, DH_ROPE, 2, dtype=F32) / DH_ROPE)
    freq = jnp.tile(inv_freq, LANE // (DH_ROPE // 2)).reshape(1, LANE)
    out = pl.BlockSpec((None, tm, LANE), lambda b, i: (b, i, 0))
    return pl.pallas_call(
        _rope_table_kernel,
        grid=(bsz, seq // tm),
        in_specs=[pl.BlockSpec((None, tm, 1), lambda b, i: (b, i, 0)), pl.BlockSpec((1, LANE), lambda b, i: (0, 0))],
        out_specs=[out, out],
        out_shape=[jax.ShapeDtypeStruct((bsz, seq, LANE), F32)] * 2,
        compiler_params=_params("arbitrary", "arbitrary"),
        name="rope_table",
    )(positions.reshape(bsz, seq, 1), freq)


def _dsa_prep_kernel(pa_ref, cos_ref, sin_ref, wuk_ref, kvn_ref, qc_ref, iq_ref, iw_ref, kc_ref, ik_ref, ct_ref):
    scale = (DH_NOPE + DH_ROPE) ** -0.5 * LOG2E
    cos, sin = cos_ref[...], sin_ref[...]
    lane = lax.broadcasted_iota(jnp.int32, (1, LANE), 1)
    first = (lane % DH_ROPE) < DH_ROPE // 2
    rope_lanes = (lane >= DSA_KR_LANE) & (lane < DSA_KR_LANE + DH_ROPE)

    def rope(v):
        partner = jnp.where(first, pltpu.roll(v, LANE - DH_ROPE // 2, 1), pltpu.roll(v, DH_ROPE // 2, 1))
        return v * cos + partner * sin

    g = pa_ref[:, DSA_G0:DSA_G0 + LANE]
    ckv = pa_ref[:, DSA_CKV0:DSA_CKV0 + LANE]
    ckv_n = ckv * lax.rsqrt(jnp.mean(ckv * ckv, -1, keepdims=True) + 1e-6) * kvn_ref[...]
    kc_ref[:, 0:LANE] = ckv_n.astype(BF16)
    ct_ref[...] = ckv_n.T.astype(BF16)
    kc_ref[:, LANE:2 * LANE] = jnp.where(rope_lanes, rope(g), 0.0).astype(BF16)
    ik_ref[...] = jnp.where(lane < D_I, g, 0.0).astype(BF16)
    iw_ref[...] = g
    iq_ref[...] = pa_ref[:, DSA_IQ0:DSA_IQ0 + H_I * LANE].astype(BF16)
    qr = rope(pa_ref[:, DSA_QR0:DSA_QR0 + LANE]) * scale
    for h in range(H_A):
        qn = pa_ref[:, DSA_QN0 + h * LANE:DSA_QN0 + (h + 1) * LANE].astype(BF16)
        q_abs = jnp.dot(qn, wuk_ref[h], preferred_element_type=F32) * scale
        shift = (DSA_KR_LANE - DH_ROPE * h) % LANE
        qr_h = pltpu.roll(qr, shift, 1) if shift else qr
        qc_ref[:, 2 * h * LANE:(2 * h + 1) * LANE] = q_abs.astype(BF16)
        qc_ref[:, (2 * h + 1) * LANE:(2 * h + 2) * LANE] = jnp.where(rope_lanes, qr_h, 0.0).astype(BF16)


def _dsa_prep(pa, cos, sin, w_uk, kv_norm):
    bsz, seq, _ = pa.shape
    tm = 512
    wuk = jnp.pad(w_uk, ((0, 0), (0, LANE - DH_NOPE), (0, 0))).astype(BF16)
    row = lambda n: pl.BlockSpec((None, tm, n), lambda b, i: (b, i, 0))
    outs = [(2 * H_A * LANE, BF16), (H_I * LANE, BF16), (LANE, F32), (2 * LANE, BF16), (LANE, BF16)]
    return pl.pallas_call(
        _dsa_prep_kernel,
        grid=(bsz, seq // tm),
        in_specs=[row(DSA_COLS), row(LANE), row(LANE),
                  pl.BlockSpec((H_A, LANE, D_C), lambda b, i: (0, 0, 0)), pl.BlockSpec((1, D_C), lambda b, i: (0, 0))],
        out_specs=[row(n) for n, _ in outs] + [pl.BlockSpec((None, D_C, tm), lambda b, i: (b, 0, i))],
        out_shape=[jax.ShapeDtypeStruct((bsz, seq, n), dt) for n, dt in outs] + [jax.ShapeDtypeStruct((bsz, D_C, seq), BF16)],
        compiler_params=_params("arbitrary", "arbitrary"),
        name="dsa_prep",
    )(pa, cos, sin, wuk, kv_norm.reshape(1, D_C))


def _dsa_kernel(k_sel, qc_ref, iq_ref, iw_ref, kc_ref, ct_ref, ik_ref, tri_ref, wuv_ref, gn_ref, o_ref,
                key_scr, planes_scr, acc_scr, s0_scr, s1_scr, b0_scr, b1_scr):
    tq, kcs = DSA_TQ, DSA_KC
    t0 = pl.program_id(1) * tq
    n_chunk = (t0 + tq + kcs - 1) // kcs
    nt = (((1,), (1,)), ((), ()))
    qpos = t0 + lax.broadcasted_iota(jnp.int32, (1, tq), 1)
    iw_t = iw_ref[...].T

    def score_chunk(c, carry):
        k0 = pl.multiple_of(c * kcs, kcs)
        ikc = ik_ref[pl.ds(k0, kcs), :]
        sc = jnp.zeros((kcs, tq), F32)
        for h in range(H_I):
            lg = lax.dot_general(ikc, iq_ref[:, h * LANE:(h + 1) * LANE], nt, preferred_element_type=F32)
            sc = sc + jnp.maximum(lg, 0.0) * iw_t[DSA_IW_LANE + h:DSA_IW_LANE + h + 1, :]
        sc = jnp.where(sc == 0.0, 0.0, sc)
        kpos = k0 + lax.broadcasted_iota(jnp.int32, (kcs, 1), 0)
        sc = jnp.where(kpos <= qpos, sc, -jnp.inf)
        bits = pltpu.bitcast(sc, jnp.int32)
        key = jnp.where(bits < 0, bits ^ 0x7FFFFFFF, bits)
        key_scr[pl.ds(k0, kcs), :] = key
        ukey = key ^ INT_MIN
        for blk in range(kcs // PLANE_KEYS):
            words = _bit_transpose32(ukey[blk * PLANE_KEYS:(blk + 1) * PLANE_KEYS])
            row0 = pl.multiple_of((c * (kcs // PLANE_KEYS) + blk) * 8, 8)
            for r in range(32):
                planes_scr[r, pl.ds(row0, 8), :] = words[r]
        return carry

    lax.fori_loop(0, n_chunk, score_chunk, 0)

    n_words = planes_scr.shape[1]
    word_row = lax.broadcasted_iota(jnp.int32, (n_words, tq), 0)
    alive0 = jnp.where(word_row < n_chunk * (kcs // 32), jnp.int32(-1), jnp.int32(0))
    k_int = int(k_sel)

    def bit_pass(i, carry):
        thr_u, above, alive = carry
        ones = alive & planes_scr[i]
        seen_ones = above + jnp.sum(lax.population_count(ones), axis=0, keepdims=True)
        take = seen_ones >= k_int
        alive = jnp.where(take, ones, alive & ~planes_scr[i])
        above = jnp.where(take, above, seen_ones)
        thr_u = jnp.where(take, thr_u | jnp.left_shift(jnp.int32(1), 31 - i), thr_u)
        return thr_u, above, alive

    thr_u, above, _ = lax.fori_loop(0, 32, bit_pass, (jnp.zeros((1, tq), jnp.int32), jnp.zeros((1, tq), jnp.int32), alive0))
    thr = thr_u ^ INT_MIN
    room = (k_int - above).astype(F32)

    qall = jnp.concatenate([qc_ref[:, 2 * h * LANE:(2 * h + 2) * LANE] for h in range(H_A)], axis=0)

    room = jnp.where(thr > KEY_NEG_INF, room, 0.0)
    acc_scr[...] = jnp.zeros_like(acc_scr)

    half = kcs // 2

    def prepare(k0, seen, s_buf, b_buf):
        key = key_scr[pl.ds(k0, half), :]
        tie = jnp.where(key == thr, 1.0, 0.0)
        prefix = jnp.dot(tri_ref[...], tie.astype(BF16), preferred_element_type=F32)
        tie_bias = jnp.where(seen + prefix <= room, jnp.where(key == thr, 0.0, NEG), NEG)
        b_buf[...] = jnp.where(key > thr, 0.0, tie_bias)
        s_buf[...] = lax.dot_general(kc_ref[pl.ds(k0, half), :], qall, nt, preferred_element_type=F32)
        return seen + jnp.sum(_fold_rows(tie, jnp.add), axis=0, keepdims=True)

    def attn_chunk(c, carry):
        m, l, seen = carry
        k0 = pl.multiple_of(c * kcs, kcs)
        k1 = pl.multiple_of(k0 + half, half)
        seen = prepare(k1, seen, s1_scr, b1_scr)
        m, l = _attend_block(s0_scr, b0_scr, ct_ref[:, pl.ds(k0, half)], m, l, acc_scr, H_A, tq)
        k2 = pl.multiple_of(jnp.minimum(c + 1, n_chunk - 1) * kcs, kcs)
        seen = prepare(k2, seen, s0_scr, b0_scr)
        m, l = _attend_block(s1_scr, b1_scr, ct_ref[:, pl.ds(k1, half)], m, l, acc_scr, H_A, tq)
        return m, l, seen

    seen0 = prepare(0, jnp.zeros((1, tq), F32), s0_scr, b0_scr)
    init = (jnp.full((1, H_A * tq), M_FLOOR, F32), jnp.zeros((1, H_A * tq), F32), seen0)
    _, l, _ = lax.fori_loop(0, n_chunk, attn_chunk, init)
    o_lat = (acc_scr[...] / jnp.maximum(l, 1e-30)).astype(BF16)
    for h in range(H_A):
        o = lax.dot_general(o_lat[:, h * tq:(h + 1) * tq], wuv_ref[h], (((0,), (0,)), ((), ())),
                            preferred_element_type=F32)
        ms = jnp.sum(o * o, -1, keepdims=True) * (1.0 / DH_V)
        o_ref[:, h * LANE:(h + 1) * LANE] = (o * lax.rsqrt(ms + 1e-6) * gn_ref[h:h + 1, :]).astype(o_ref.dtype)


def _dsa_attention(qc, iq, iw, kc, ct, ik, w_uv, g_a):
    bsz, seq, _ = qc.shape
    k_sel = float(min(K_SEL_MAX, seq // 4))
    tq, kcs = DSA_TQ, DSA_KC
    half = kcs // 2
    tri = jnp.asarray(np.tril(np.ones((half, half), np.float32)), BF16)
    wuv = jnp.pad(w_uv, ((0, 0), (0, 0), (0, LANE - DH_V))).astype(BF16)
    gn = jnp.pad(g_a.reshape(H_A, DH_V), ((0, 0), (0, LANE - DH_V)))
    row = lambda n: pl.BlockSpec((None, tq, n), lambda b, i: (b, i, 0))
    per_b = lambda n: pl.BlockSpec((None, seq, n), lambda b, i: (b, 0, 0))
    return pl.pallas_call(
        functools.partial(_dsa_kernel, k_sel),
        grid=(bsz, seq // tq),
        in_specs=[row(2 * H_A * LANE), row(H_I * LANE), row(LANE), per_b(2 * LANE),
                  pl.BlockSpec((None, D_C, seq), lambda b, i: (b, 0, 0)), per_b(LANE),
                  pl.BlockSpec((half, half), lambda b, i: (0, 0)), pl.BlockSpec((H_A, D_C, LANE), lambda b, i: (0, 0, 0)),
                  pl.BlockSpec((H_A, LANE), lambda b, i: (0, 0))],
        out_specs=row(H_A * LANE),
        out_shape=jax.ShapeDtypeStruct((bsz, seq, H_A * LANE), BF16),
        scratch_shapes=[pltpu.VMEM((seq, tq), jnp.int32), pltpu.VMEM((32, seq // 32, tq), jnp.int32),
                        pltpu.VMEM((D_C, H_A * tq), F32),
                        pltpu.VMEM((half, H_A * tq), F32), pltpu.VMEM((half, H_A * tq), F32),
                        pltpu.VMEM((half, tq), F32), pltpu.VMEM((half, tq), F32)],
        compiler_params=_params("arbitrary", "arbitrary"),
        name="dsa_attention",
    )(qc, iq, iw, kc, ct, ik, tri, wuv, gn)


def _nsa_compress_kernel(a_ref, pos_ref, w1t_ref, w1b_ref, w2k_ref, w2v_ref, kc_ref, vc_ref):
    a = a_ref[...]
    top = jnp.dot((a + pos_ref[0:1, :]).astype(BF16), w1t_ref[...], preferred_element_type=F32)
    bot = jnp.dot((a + pos_ref[1:2, :]).astype(BF16), w1b_ref[...], preferred_element_type=F32)
    n = a.shape[0]
    pre = top + jnp.concatenate([bot[1:], bot[:1]], axis=0)
    h = (pre * jax.nn.sigmoid(pre)).astype(BF16)
    hid = w2k_ref.shape[0]
    kc_ref[...] = jnp.dot(h[:, :hid], w2k_ref[...], preferred_element_type=F32).astype(BF16)
    vc_ref[...] = lax.dot_general(w2v_ref[...], h[:, hid:], (((1,), (1,)), ((), ())), preferred_element_type=F32).astype(BF16)


def _nsa_compress(ncmp, cmp_pos, cmp_w1, cmp_w2):
    bsz, seq, _ = ncmp.shape
    n_grp = seq // D_STRIDE
    per = L_CMP // D_STRIDE
    width = D_STRIDE * LANE
    a = ncmp.reshape(bsz, n_grp, width)
    w1 = cmp_w1.reshape(2, per, D_STRIDE, DH_C, CMP_HIDDEN)
    zer = jnp.zeros((D_STRIDE, DH_C, CMP_HIDDEN), cmp_w1.dtype)

    def expand(p):
        wk = jnp.concatenate([w1[0, p], zer], axis=1)
        wv = jnp.concatenate([zer, w1[1, p]], axis=1)
        return jnp.concatenate([wk, wv], axis=2).reshape(width, 2 * CMP_HIDDEN).astype(BF16)

    pos = cmp_pos.reshape(2, per, D_STRIDE, DH_C)
    pos = jnp.concatenate([pos[0], pos[1]], axis=-1).reshape(per, width)
    pad_out = ((0, 0), (0, LANE - DH_C))
    w2k = jnp.pad(cmp_w2[0], pad_out).astype(BF16)
    w2v = jnp.pad(cmp_w2[1], pad_out).astype(BF16).T
    full = lambda r, c: pl.BlockSpec((r, c), lambda b: (0, 0))
    return pl.pallas_call(
        _nsa_compress_kernel,
        grid=(bsz,),
        in_specs=[pl.BlockSpec((None, n_grp, width), lambda b: (b, 0, 0)), full(per, width),
                  full(width, 2 * CMP_HIDDEN), full(width, 2 * CMP_HIDDEN), full(CMP_HIDDEN, LANE), full(LANE, CMP_HIDDEN)],
        out_specs=[pl.BlockSpec((None, n_grp, LANE), lambda b: (b, 0, 0)), pl.BlockSpec((None, LANE, n_grp), lambda b: (b, 0, 0))],
        out_shape=[jax.ShapeDtypeStruct((bsz, n_grp, LANE), BF16), jax.ShapeDtypeStruct((bsz, LANE, n_grp), BF16)],
        compiler_params=_params("arbitrary"),
        name="nsa_compress",
    )(a, pos, expand(0), expand(1), w2k, w2v)


NSA_TQ = 128
NSA_KC = 512
NEG = -1e30
M_FLOOR = -1e20


def _softmax_cols(s, bias, n_head, tq):
    out = []
    for h in range(n_head):
        sh = s[:, h * tq:(h + 1) * tq] + bias
        m = jnp.maximum(jnp.max(_fold_rows(sh, jnp.maximum), axis=0, keepdims=True), M_FLOOR)
        e = jnp.exp2(sh - m)
        den = jnp.sum(_fold_rows(e, jnp.add), axis=0, keepdims=True)
        out.append(e * (1.0 / jnp.maximum(den, 1e-30)))
    return out


def _nsa_kernel(q_ref, g_ref, kc_ref, vct_ref, ks_ref, vst_ref, kw_ref, vwt_ref, cover_ref, expand_ref, gn_ref, o_ref,
                acc_scr, s0_scr, s1_scr, b0_scr, b1_scr):
    tq, kc_sz = NSA_TQ, NSA_KC
    t0 = pl.program_id(1) * tq
    nt = (((1,), (1,)), ((), ()))
    q = q_ref[...]
    qa = jnp.concatenate([q[:, h * LANE:(h + 1) * LANE] for h in range(H_C)], axis=0)
    qpos = t0 + lax.broadcasted_iota(jnp.int32, (1, tq), 1)

    n_grp = kc_ref.shape[0]
    s_c = lax.dot_general(kc_ref[...], qa, nt, preferred_element_type=F32)
    n_idx = lax.broadcasted_iota(jnp.int32, (n_grp, 1), 0)
    visible = jnp.where(n_idx < n_grp - 1, n_idx * D_STRIDE + (L_CMP - 1), 2 ** 30)
    p_c = _softmax_cols(s_c, jnp.where(visible <= qpos, 0.0, NEG), H_C, tq)
    o_cmp = jnp.dot(vct_ref[...], jnp.concatenate([p.astype(BF16) for p in p_c], axis=1), preferred_element_type=F32)

    p_sum = p_c[0]
    for h in range(1, H_C):
        p_sum = p_sum + p_c[h]
    hi = p_sum.astype(BF16)
    lo = (p_sum - hi.astype(F32)).astype(BF16)
    cov = cover_ref[...]
    imp_t = jnp.dot(cov, hi, preferred_element_type=F32) + jnp.dot(cov, lo, preferred_element_type=F32)
    n_sel = cov.shape[0]
    jrow = lax.broadcasted_iota(jnp.int32, (n_sel, tq), 0)
    cur = (t0 + lax.broadcasted_iota(jnp.int32, (n_sel, tq), 1)) // L_SEL
    adm = jrow <= cur
    forced = (jrow == 0) | (jrow == cur) | (jrow == cur - 1)
    val = jnp.where(adm & forced, jnp.inf, jnp.where(adm, imp_t, -jnp.inf))
    rank = jnp.zeros((n_sel, tq), F32)
    for jp in range(n_sel):
        r = val[jp:jp + 1, :]
        rank = rank + jnp.where(r == val, jnp.where(jrow > jp, 1.0, 0.0), jnp.where(r > val, 1.0, 0.0))
    sel_t = jnp.where(rank < min(N_TOP_MAX, n_sel), jnp.where(val > -jnp.inf, 1.0, 0.0), 0.0)
    if n_sel < LANE:
        sel_t = jnp.concatenate([sel_t, jnp.zeros((LANE - n_sel, tq), F32)], axis=0)
    sel_t = sel_t.astype(BF16)

    acc_scr[...] = jnp.zeros_like(acc_scr)

    half = kc_sz // 2
    n_chunk = (t0 + tq + kc_sz - 1) // kc_sz

    def prepare(k0, s_buf, b_buf):
        s_buf[...] = lax.dot_general(ks_ref[pl.ds(k0, half), :], qa, nt, preferred_element_type=F32)
        picked = jnp.dot(expand_ref[pl.ds(k0, half), :], sel_t, preferred_element_type=F32)
        kpos = k0 + lax.broadcasted_iota(jnp.int32, (half, 1), 0)
        b_buf[...] = jnp.where(kpos <= qpos, jnp.where(picked > 0.5, 0.0, NEG), NEG)

    def chunk(c, carry):
        m, l = carry
        k0 = pl.multiple_of(c * kc_sz, kc_sz)
        k1 = pl.multiple_of(k0 + half, half)
        prepare(k1, s1_scr, b1_scr)
        m, l = _attend_block(s0_scr, b0_scr, vst_ref[:, pl.ds(k0, half)], m, l, acc_scr, H_C, tq)
        prepare(pl.multiple_of(jnp.minimum(c + 1, n_chunk - 1) * kc_sz, kc_sz), s0_scr, b0_scr)
        return _attend_block(s1_scr, b1_scr, vst_ref[:, pl.ds(k1, half)], m, l, acc_scr, H_C, tq)

    prepare(0, s0_scr, b0_scr)
    init = (jnp.full((1, H_C * tq), M_FLOOR, F32), jnp.zeros((1, H_C * tq), F32))
    _, l_s = lax.fori_loop(0, n_chunk, chunk, init)
    o_slc = acc_scr[...] * (1.0 / jnp.maximum(l_s, 1e-30))

    span = WINDOW + tq
    w0 = pl.multiple_of(jnp.maximum(t0 - WINDOW, 0), tq)
    s_w = lax.dot_general(kw_ref[pl.ds(w0, span), :], qa, nt, preferred_element_type=F32)
    kpos_w = w0 + lax.broadcasted_iota(jnp.int32, (span, 1), 0)
    wbias = jnp.where(kpos_w <= qpos, jnp.where(qpos - kpos_w < WINDOW, 0.0, NEG), NEG)
    p_w = _softmax_cols(s_w, wbias, H_C, tq)
    o_win = jnp.dot(vwt_ref[:, pl.ds(w0, span)], jnp.concatenate([p.astype(BF16) for p in p_w], axis=1), preferred_element_type=F32)

    gates_t = jax.nn.sigmoid(g_ref[...]).T
    for h in range(H_C):
        hs = slice(h * tq, (h + 1) * tq)
        o_t = (gates_t[h:h + 1] * o_cmp[:, hs] + gates_t[H_C + h:H_C + h + 1] * o_slc[:, hs]
               + gates_t[2 * H_C + h:2 * H_C + h + 1] * o_win[:, hs])
        o = o_t.T
        ms = jnp.sum(o * o, -1, keepdims=True) * (1.0 / DH_C)
        o_ref[:, h * LANE:(h + 1) * LANE] = (o * lax.rsqrt(ms + 1e-6) * gn_ref[h:h + 1, :]).astype(o_ref.dtype)


def _nsa_attention(nq, ng, kc, vc_t, nks, nvs_t, nkw, nvw_t, g_c):
    bsz, seq, _ = nq.shape
    n_grp = kc.shape[1]
    n_sel = seq // L_SEL
    grp_start = np.arange(n_grp) * D_STRIDE
    sel_start = np.arange(n_sel) * L_SEL
    cover_t = ((grp_start[None, :] < sel_start[:, None] + L_SEL) & (grp_start[None, :] + L_CMP > sel_start[:, None]))
    cover_t = jnp.asarray(cover_t.astype(np.float32), BF16)
    expand = (np.arange(seq)[:, None] // L_SEL == np.arange(LANE)[None, :]).astype(np.float32)
    expand = jnp.asarray(expand, BF16)
    gn = jnp.pad(g_c.reshape(H_C, DH_C), ((0, 0), (0, LANE - DH_C)))
    tq = NSA_TQ
    row = lambda n: pl.BlockSpec((None, tq, n), lambda b, i: (b, i, 0))
    per_b = lambda r, c: pl.BlockSpec((None, r, c), lambda b, i: (b, 0, 0))
    full = lambda r, c: pl.BlockSpec((r, c), lambda b, i: (0, 0))
    return pl.pallas_call(
        _nsa_kernel,
        grid=(bsz, seq // tq),
        in_specs=[row(H_C * LANE), row(LANE), per_b(n_grp, LANE), per_b(LANE, n_grp), per_b(seq, LANE), per_b(LANE, seq),
                  per_b(seq, LANE), per_b(LANE, seq), full(n_sel, n_grp), full(seq, LANE), full(H_C, LANE)],
        out_specs=row(H_C * LANE),
        out_shape=jax.ShapeDtypeStruct((bsz, seq, H_C * LANE), BF16),
        scratch_shapes=[pltpu.VMEM((LANE, H_C * tq), F32),
                        pltpu.VMEM((NSA_KC // 2, H_C * tq), F32), pltpu.VMEM((NSA_KC // 2, H_C * tq), F32),
                        pltpu.VMEM((NSA_KC // 2, tq), F32), pltpu.VMEM((NSA_KC // 2, tq), F32)],
        compiler_params=_params("arbitrary", "arbitrary"),
        name="nsa_attention",
    )(nq, ng, kc, vc_t, nks, nvs_t, nkw, nvw_t, cover_t, expand, gn)


def _outproj_kernel(ya_ref, yb_ref, yc_ref, x_ref, g_ref, wa_ref, wb_ref, wc_ref, lng_ref, lnb_ref, o_ref):
    y = jnp.dot(ya_ref[...], wa_ref[...], preferred_element_type=F32)
    y += jnp.dot(yb_ref[...], wb_ref[...], preferred_element_type=F32)
    y += jnp.dot(yc_ref[...], wc_ref[...], preferred_element_type=F32)
    z = ALPHA * x_ref[...] + (1.0 + g_ref[...]) * y
    o_ref[...] = _layer_norm_rows(z, lng_ref[...], lnb_ref[...])


def _pad_head_rows(w, n_head, dh):
    d = w.shape[1]
    return jnp.pad(w.reshape(n_head, dh, d), ((0, 0), (0, LANE - dh), (0, 0))).reshape(n_head * LANE, d)


def _output_projection(ya, yb, yc, x, g, wa, wb, wc, ln_g, ln_b):
    bsz, seq, d = x.shape
    tm = 512
    na, nb, nc = ya.shape[-1], yb.shape[-1], yc.shape[-1]
    row = lambda n: pl.BlockSpec((None, tm, n), lambda b, i: (b, i, 0))
    vec = pl.BlockSpec((None, 1, d), lambda b, i: (b, 0, 0))
    full = lambda r, c: pl.BlockSpec((r, c), lambda b, i: (0, 0))
    return pl.pallas_call(
        _outproj_kernel,
        grid=(bsz, seq // tm),
        in_specs=[row(na), row(nb), row(nc), row(d), vec, full(na, d), full(nb, d), full(nc, d), full(1, d), full(1, d)],
        out_specs=row(d),
        out_shape=jax.ShapeDtypeStruct((bsz, seq, d), F32),
        compiler_params=_params("arbitrary", "arbitrary"),
        name="output_projection_ln",
    )(ya, yb, yc, x, g, wa, wb, wc, ln_g.reshape(1, d), ln_b.reshape(1, d))


def _tile_gate_up(wg, wu, tf):
    *lead, d, ff = wg.shape
    split = lambda w: jnp.moveaxis(w.reshape(*lead, d, ff // tf, tf), -2, -3)
    return jnp.concatenate([split(wg), split(wu)], axis=-1)


def _swiglu_chunk(u, wgu_ref, wd_ref):
    tf = wd_ref.shape[0]
    ab = jnp.dot(u, wgu_ref[...], preferred_element_type=F32)
    a, b = ab[:, :tf], ab[:, tf:]
    return jnp.dot((a * jax.nn.sigmoid(a) * b).astype(BF16), wd_ref[...], preferred_element_type=F32)


def _ffn_kernel(x_ref, sc_ref, sh_ref, g_ref, wgu_ref, wd_ref, lng_ref, lnb_ref, o_ref, u_scr, acc_scr):
    f = pl.program_id(2)

    @pl.when(f == 0)
    def _():
        u_scr[...] = (x_ref[...] * (1.0 + sc_ref[...]) + sh_ref[...]).astype(BF16)
        acc_scr[...] = jnp.zeros_like(acc_scr)

    acc_scr[...] += _swiglu_chunk(u_scr[...], wgu_ref, wd_ref)

    @pl.when(f == pl.num_programs(2) - 1)
    def _():
        z = ALPHA * x_ref[...] + (1.0 + g_ref[...]) * acc_scr[...]
        o_ref[...] = _layer_norm_rows(z, lng_ref[...], lnb_ref[...])


def _dense_ffn(x, sc, sh, g, wg, wu, wd, ln_g, ln_b):
    bsz, seq, d = x.shape
    ff = wg.shape[1]
    tm, tf = 1024, 256
    row = pl.BlockSpec((None, tm, d), lambda b, i, f: (b, i, 0))
    vec = pl.BlockSpec((None, 1, d), lambda b, i, f: (b, 0, 0))
    one = pl.BlockSpec((1, d), lambda b, i, f: (0, 0))
    return pl.pallas_call(
        _ffn_kernel,
        grid=(bsz, seq // tm, ff // tf),
        in_specs=[row, vec, vec, vec,
                  pl.BlockSpec((None, d, 2 * tf), lambda b, i, f: (f, 0, 0)),
                  pl.BlockSpec((tf, d), lambda b, i, f: (f, 0)),
                  one, one],
        out_specs=row,
        out_shape=jax.ShapeDtypeStruct((bsz, seq, d), F32),
        scratch_shapes=[pltpu.VMEM((tm, d), BF16), pltpu.VMEM((tm, d), F32)],
        compiler_params=_params("arbitrary", "arbitrary", "arbitrary"),
        name="dense_swiglu_ln",
    )(x, sc, sh, g, _tile_gate_up(wg, wu, tf), wd, ln_g.reshape(1, d), ln_b.reshape(1, d))


def _router_kernel(x_ref, sc_ref, sh_ref, r_ref, lo_ref, up_ref, gate_ref, slot_ref, slot_t_ref, cnt_ref):
    u = x_ref[...] * (1.0 + sc_ref[...]) + sh_ref[...]
    logits = jnp.dot(u, r_ref[...], preferred_element_type=F32, precision=lax.Precision.HIGHEST)
    lane = lax.broadcasted_iota(jnp.int32, logits.shape, 1)
    neg = -jnp.inf
    l1 = jnp.where(lane < N_EXPERTS, logits, neg)
    m1 = jnp.max(l1, -1, keepdims=True)
    i1 = jnp.min(jnp.where(l1 == m1, lane, LANE), -1, keepdims=True)
    l2 = jnp.where(lane == i1, neg, l1)
    m2 = jnp.max(l2, -1, keepdims=True)
    i2 = jnp.min(jnp.where(l2 == m2, lane, LANE), -1, keepdims=True)
    e2 = jnp.exp(m2 - m1)
    w1 = 1.0 / (1.0 + e2)
    w2 = e2 / (1.0 + e2)
    gate_ref[...] = jnp.where(lane == i1, w1, jnp.where(lane == i2, w2, 0.0))
    routed = jnp.where((lane == i1) | (lane == i2), 1.0, 0.0)
    before = jnp.dot(lo_ref[...], routed.astype(BF16), preferred_element_type=F32)
    slot_ref[...] = jnp.where(routed > 0.5, before, -1.0)
    routed_t = routed.T
    before_t = jnp.dot(routed_t.astype(BF16), up_ref[...], preferred_element_type=F32)
    slot_t_ref[...] = jnp.where(routed_t > 0.5, before_t, -1.0)
    cnt_ref[...] = jnp.broadcast_to(jnp.sum(routed, axis=0, keepdims=True), cnt_ref.shape).astype(jnp.int32)


MOE_TM = 1024
MOE_ROWS = 288
MOE_TF = 896
MOE_GROUP = 2


def _moe_router(x, sc, sh, router):
    bsz, seq, d = x.shape
    tm = MOE_TM
    r = jnp.pad(router, ((0, 0), (0, LANE - router.shape[1])))
    upper = np.triu(np.ones((tm, tm), np.float32), 1)
    up, lo = jnp.asarray(upper, BF16), jnp.asarray(upper.T, BF16)
    row = lambda n: pl.BlockSpec((None, tm, n), lambda b, i: (b, i, 0))
    vec = pl.BlockSpec((None, 1, d), lambda b, i: (b, 0, 0))
    full = lambda a, c: pl.BlockSpec((a, c), lambda b, i: (0, 0))
    n_tile = seq // tm
    return pl.pallas_call(
        _router_kernel,
        grid=(bsz, n_tile),
        in_specs=[row(d), vec, vec, full(d, LANE), full(tm, tm), full(tm, tm)],
        out_specs=[row(LANE), row(LANE), pl.BlockSpec((None, None, LANE, tm), lambda b, i: (b, i, 0, 0)),
                   pl.BlockSpec((None, None, 8, LANE), lambda b, i: (b, i, 0, 0))],
        out_shape=[jax.ShapeDtypeStruct((bsz, seq, LANE), F32), jax.ShapeDtypeStruct((bsz, seq, LANE), F32),
                   jax.ShapeDtypeStruct((bsz, n_tile, LANE, tm), F32), jax.ShapeDtypeStruct((bsz, n_tile, 8, LANE), jnp.int32)],
        compiler_params=_params("arbitrary", "arbitrary"),
        name="moe_router",
    )(x, sc, sh, r, lo, up)


def _moe_kernel(cnt_ref, x_ref, sc_ref, sh_ref, g_ref, gate_ref, slot_ref, slot_t_ref, wgu_ref, wd_ref, lng_ref, lnb_ref,
                o_ref, u_scr, xg_scr, acc_scr):
    tm, rows = MOE_TM, MOE_ROWS
    max_pass = xg_scr.shape[0] // MOE_GROUP
    e = pl.program_id(2)
    f = pl.program_id(3)
    first_tile = (pl.program_id(0) * pl.num_programs(1) + pl.program_id(1)) * MOE_GROUP
    n_pass = [(cnt_ref[(first_tile + t) * N_EXPERTS + e] + rows - 1) // rows for t in range(MOE_GROUP)]
    n_all = functools.reduce(jnp.maximum, n_pass)

    @pl.when((e == 0) & (f == 0))
    def _():
        u_scr[...] = (x_ref[...] * (1.0 + sc_ref[...]) + sh_ref[...]).astype(BF16)
        o_ref[...] = jnp.zeros_like(o_ref)

    @pl.when(f == 0)
    def _():
        for t in range(MOE_GROUP):
            slot_row = slot_t_ref[t, pl.ds(e, 1), :]

            def gather(p, carry, t=t, slot_row=slot_row):
                want = p * rows + lax.broadcasted_iota(jnp.int32, (rows, 1), 0)
                pick = jnp.where(slot_row == want.astype(F32), 1.0, 0.0).astype(BF16)
                xg_scr[p * MOE_GROUP + t] = jnp.dot(pick, u_scr[t * tm:(t + 1) * tm, :], preferred_element_type=F32).astype(BF16)
                acc_scr[p * MOE_GROUP + t] = jnp.zeros((rows, x_ref.shape[-1]), F32)
                return carry

            lax.fori_loop(0, n_pass[t], gather, 0)

            def blank(p, carry, t=t):
                xg_scr[p * MOE_GROUP + t] = jnp.zeros((rows, x_ref.shape[-1]), BF16)
                acc_scr[p * MOE_GROUP + t] = jnp.zeros((rows, x_ref.shape[-1]), F32)
                return carry

            lax.fori_loop(n_pass[t], n_all, blank, 0)

    def ffn(p, carry):
        base = pl.multiple_of(p * MOE_GROUP, MOE_GROUP)
        xg = xg_scr[pl.ds(base, MOE_GROUP)].reshape(MOE_GROUP * rows, x_ref.shape[-1])
        y = _swiglu_chunk(xg, wgu_ref, wd_ref)
        acc_scr[pl.ds(base, MOE_GROUP)] += y.reshape(MOE_GROUP, rows, x_ref.shape[-1])
        return carry

    lax.fori_loop(0, n_all, ffn, 0)

    @pl.when(f == pl.num_programs(3) - 1)
    def _():
        lane = lax.broadcasted_iota(jnp.int32, (tm, LANE), 1)
        for t in range(MOE_GROUP):
            tok = slice(t * tm, (t + 1) * tm)
            slot_col = jnp.sum(jnp.where(lane == e, slot_ref[tok, :], 0.0), axis=1, keepdims=True)
            gate_col = jnp.sum(jnp.where(lane == e, gate_ref[tok, :], 0.0), axis=1, keepdims=True)

            def scatter(p, carry, t=t, tok=tok, slot_col=slot_col, gate_col=gate_col):
                col = lax.broadcasted_iota(jnp.int32, (1, 2 * rows), 1)
                want = p * rows + jnp.where(col >= rows, col - rows, col)
                put = jnp.where(slot_col == want.astype(F32), 1.0, 0.0).astype(BF16)
                y = acc_scr[p * MOE_GROUP + t]
                hi = y.astype(BF16)
                lo = (y - hi.astype(F32)).astype(BF16)
                back = jnp.dot(put, jnp.concatenate([hi, lo], axis=0), preferred_element_type=F32)
                o_ref[tok, :] += gate_col * back
                return carry

            lax.fori_loop(0, n_pass[t], scatter, 0)

    @pl.when((e == pl.num_programs(2) - 1) & (f == pl.num_programs(3) - 1))
    def _():
        z = ALPHA * x_ref[...] + (1.0 + g_ref[...]) * o_ref[...]
        o_ref[...] = _layer_norm_rows(z, lng_ref[...], lnb_ref[...])


def _moe_ffn(x, sc, sh, g, routing, wg, wu, wd, ln_g, ln_b):
    gate, slot, slot_t, cnt = routing
    bsz, seq, d = x.shape
    n_e, _, ff = wg.shape
    tm, tf, rows, grp = MOE_TM, MOE_TF, MOE_ROWS, MOE_GROUP
    max_pass = -(-tm // rows)
    counts = cnt[:, :, 0, :n_e].reshape(-1)
    once = pl.Buffered(1)
    row = lambda n: pl.BlockSpec((None, grp * tm, n), lambda b, i, e, f, c: (b, i, 0), pipeline_mode=once)
    vec = pl.BlockSpec((None, 1, d), lambda b, i, e, f, c: (b, 0, 0))
    one = pl.BlockSpec((1, d), lambda b, i, e, f, c: (0, 0))
    grid_spec = pltpu.PrefetchScalarGridSpec(
        num_scalar_prefetch=1,
        grid=(bsz, seq // (grp * tm), n_e, ff // tf),
        in_specs=[row(d), vec, vec, vec, row(LANE), row(LANE),
                  pl.BlockSpec((None, grp, LANE, tm), lambda b, i, e, f, c: (b, i, 0, 0), pipeline_mode=once),
                  pl.BlockSpec((None, None, d, 2 * tf), lambda b, i, e, f, c: (e, f, 0, 0)),
                  pl.BlockSpec((None, tf, d), lambda b, i, e, f, c: (e, f, 0)),
                  one, one],
        out_specs=row(d),
        scratch_shapes=[pltpu.VMEM((grp * tm, d), BF16), pltpu.VMEM((grp * max_pass, rows, d), BF16),
                        pltpu.VMEM((grp * max_pass, rows, d), F32)],
    )
    return pl.pallas_call(
        _moe_kernel,
        grid_spec=grid_spec,
        out_shape=jax.ShapeDtypeStruct((bsz, seq, d), F32),
        compiler_params=_params("arbitrary", "arbitrary", "arbitrary", "arbitrary"),
        name="moe_swiglu_ln",
    )(counts, x, sc, sh, g, gate, slot, slot_t, _tile_gate_up(wg, wu, tf), wd, ln_g.reshape(1, d), ln_b.reshape(1, d))


ML_TT = 256
ML_SUB = 128


ML_NB = 1


def _mlstm_kernel(qk_ref, tail_ref, v_ref, og_ref, g_ref, cw_ref, cb_ref, gb_ref, gn_ref, y_ref,
                  ct_scr, n_scr, m_scr, q_scr, k_scr):
    @pl.when(pl.program_id(1) == 0)
    def _():
        ct_scr[...] = jnp.zeros_like(ct_scr)
        n_scr[...] = jnp.zeros_like(n_scr)
        m_scr[...] = jnp.zeros_like(m_scr)

    for nb in range(ML_NB):
        _mlstm_rows(qk_ref.at[nb], tail_ref.at[nb], v_ref.at[nb], og_ref.at[nb], g_ref.at[nb], cw_ref, cb_ref, gb_ref, gn_ref,
                    y_ref.at[nb], ct_scr.at[nb], n_scr.at[nb], m_scr.at[nb], q_scr.at[nb], k_scr.at[nb])


def _mlstm_rows(qk_ref, tail_ref, v_ref, og_ref, g_ref, cw_ref, cb_ref, gb_ref, gn_ref, y_ref,
                ct_scr, n_scr, m_scr, q_scr, k_scr):
    tt = ML_TT
    step = pl.program_id(1)

    x = qk_ref[...]
    tail = jnp.where(step == 0, 0.0, tail_ref[...])
    row8 = lax.broadcasted_iota(jnp.int32, (8, 1), 0)
    pre = x * cw_ref[CONV_W - 1:CONV_W, :] + cb_ref[...]
    for s in range(1, CONV_W):
        rolled = pltpu.roll(x, s, 0)
        head = jnp.where(row8 < s, pltpu.roll(tail, s, 0), rolled[0:8])
        pre = pre + jnp.concatenate([head, rolled[8:]], axis=0) * cw_ref[CONV_W - 1 - s:CONV_W - s, :]
    act = pre * jax.nn.sigmoid(pre)
    q_scr[...] = act[:, 0:ML_D].astype(BF16)
    k_scr[...] = (act[:, ML_D:2 * ML_D] * (DH_B ** -0.5)).astype(BF16)

    lane = lax.broadcasted_iota(jnp.int32, (1, LANE), 1)
    tok = lane % CHUNK
    jj = lax.broadcasted_iota(jnp.int32, (CHUNK, CHUNK), 0)
    ss = lax.broadcasted_iota(jnp.int32, (CHUNK, CHUNK), 1)
    nt = (((1,), (1,)), ((), ()))
    tn = (((0,), (0,)), ((), ()))
    ct_state = [ct_scr[h] for h in range(H_B)]
    n_state = [n_scr[h:h + 1, :] for h in range(H_B)]
    m_state = [m_scr[h:h + 1, 0:1] for h in range(H_B)]
    for sub in range(tt // ML_SUB):
        r0 = sub * ML_SUB
        gp = g_ref[r0:r0 + ML_SUB, :] + gb_ref[...]
        lsig = jnp.minimum(gp, 0.0) - jnp.log(1.0 + jnp.exp(-jnp.abs(gp)))
        col = jnp.where(lane < H_B, gp, lsig)
        rowl = col.T
        b = rowl[0:8]
        for sft in (1, 2, 4, 8, 16, 32):
            b = b + jnp.where(tok >= sft, pltpu.roll(b, sft, 1), 0.0)
        bcol = jnp.concatenate([b, jnp.zeros((LANE - 8, ML_SUB), F32)], axis=0).T
        for ci in range(ML_SUB // CHUNK):
            c0 = ci * CHUNK
            rows = slice(r0 + c0, r0 + c0 + CHUNK)
            for h in range(H_B):
                hs = slice(h * DH_B, (h + 1) * DH_B)
                b_col = bcol[c0:c0 + CHUNK, H_B + h:H_B + h + 1]
                ig_col = col[c0:c0 + CHUNK, h:h + 1]
                b_row = b[H_B + h:H_B + h + 1, c0:c0 + CHUNK]
                ig_row = rowl[h:h + 1, c0:c0 + CHUNK]
                qh, kh, vh = q_scr[rows, hs], k_scr[rows, hs], v_ref[rows, hs]
                log_d = jnp.where(jj >= ss, b_col - b_row + ig_row, NEG)
                m_loc = jnp.max(log_d, -1, keepdims=True)
                s_loc = lax.dot_general(qh, kh, nt, preferred_element_type=F32) * jnp.exp(log_d - m_loc)
                sv_loc = jnp.dot(s_loc.astype(BF16), vh, preferred_element_type=F32)
                rs_loc = jnp.sum(s_loc, -1, keepdims=True)
                b_last = b_row[:, CHUNK - 1:CHUNK]
                w_max = jnp.max(b_last - b_row + ig_row, -1, keepdims=True)
                w_loc = jnp.exp(b_last - b_col + ig_col - w_max)
                inc_c = lax.dot_general(kh, (w_loc * vh.astype(F32)).astype(BF16), tn, preferred_element_type=F32)
                inc_n = jnp.sum(w_loc * kh.astype(F32), axis=0, keepdims=True)
                m_old, ct, n_row = m_state[h], ct_state[h], n_state[h]
                log_inter = b_col + m_old
                m_out = jnp.maximum(log_inter, m_loc)
                w_inter = jnp.exp(log_inter - m_out)
                w_intra = jnp.exp(m_loc - m_out)
                num = w_inter * jnp.dot(qh, ct.astype(BF16), preferred_element_type=F32) + w_intra * sv_loc
                den = w_inter * jnp.sum(qh.astype(F32) * n_row, -1, keepdims=True) + w_intra * rs_loc
                hid = num / jnp.maximum(jnp.abs(den), jnp.exp(-m_out))
                m_new = jnp.maximum(b_last + m_old, w_max)
                decay = jnp.exp(b_last + m_old - m_new)
                grow = jnp.exp(w_max - m_new)
                ct_state[h] = decay * ct + grow * inc_c
                n_state[h] = decay * n_row + grow * inc_n
                m_state[h] = m_new
                ms = jnp.mean(hid * hid, -1, keepdims=True)
                y = hid * lax.rsqrt(ms + 1e-6) * gn_ref[:, hs] * jax.nn.sigmoid(og_ref[rows, hs])
                y_ref[rows, hs] = y.astype(y_ref.dtype)
    for h in range(H_B):
        ct_scr[h] = ct_state[h]
        n_scr[h:h + 1, :] = n_state[h]
        m_scr[h:h + 1, :] = jnp.broadcast_to(m_state[h], (1, LANE))


def _mlstm(mqk, mv, mo, mg, conv_w, conv_b, gate_b, g_b):
    bsz, seq, _ = mqk.shape
    tt = ML_TT
    gb = jnp.pad(gate_b.reshape(1, 2 * H_B), ((0, 0), (0, LANE - 2 * H_B)))
    nb = ML_NB
    row = lambda n: pl.BlockSpec((nb, tt, n), lambda b, i: (b, i, 0))
    full = lambda r, c: pl.BlockSpec((r, c), lambda b, i: (0, 0))
    tail = pl.BlockSpec((nb, 8, 2 * ML_D), lambda b, i: (b, jnp.maximum(i * (tt // 8) - 1, 0), 0))
    return pl.pallas_call(
        _mlstm_kernel,
        grid=(bsz // nb, seq // tt),
        in_specs=[row(2 * ML_D), tail, row(ML_D), row(ML_D), row(LANE),
                  full(CONV_W, 2 * ML_D), full(1, 2 * ML_D), full(1, LANE), full(1, ML_D)],
        out_specs=row(ML_D),
        out_shape=jax.ShapeDtypeStruct((bsz, seq, ML_D), BF16),
        scratch_shapes=[pltpu.VMEM((nb, H_B, DH_B, DH_B), F32), pltpu.VMEM((nb, 8, DH_B), F32), pltpu.VMEM((nb, 8, LANE), F32),
                        pltpu.VMEM((nb, tt, ML_D), BF16), pltpu.VMEM((nb, tt, ML_D), BF16)],
        compiler_params=_params("arbitrary", "arbitrary"),
        name="mlstm_scan",
    )(mqk, mqk, mv, mo, mg, conv_w, conv_b.reshape(1, 2 * ML_D), gb, g_b.reshape(1, ML_D))


def _mixers(pa, mls, nsa, rope, w_uk, w_uv, kv_norm, conv_w, conv_b, gate_b, cmp_pos, cmp_w1, cmp_w2, grp_norm):
    g_a, g_b, g_c = jnp.split(grp_norm, [H_A * DH_V, H_A * DH_V + H_B * DH_B])
    qc, iq, iw, kc_a, ik, ct_a = _dsa_prep(pa, rope[0], rope[1], w_uk, kv_norm)
    y_a = _dsa_attention(qc, iq, iw, kc_a, ct_a, ik, w_uv, g_a)
    y_b = _mlstm(*mls, conv_w, conv_b, gate_b, g_b)
    nq, ncmp, nks, nvs, nkw, nvw, ng = nsa
    kc, vc = _nsa_compress(ncmp, cmp_pos, cmp_w1, cmp_w2)
    y_c = _nsa_attention(nq, ng, kc, vc, nks, nvs, nkw, nvw, g_c)
    return y_a, y_b, y_c


def _pad_cols(w, n):
    return jnp.pad(w, ((0, 0), (0, n - w.shape[1])))


def kernel(x, c, positions, w_mod, b_mod, w_in, dsa_w_uk, dsa_w_uv, dsa_kv_norm, mlstm_conv_w, mlstm_conv_b, mlstm_gate_b, nsa_cmp_pos, nsa_cmp_w1, nsa_cmp_w2, grp_norm, w_out, ln_g, ln_b, ffn_w_gate, ffn_w_up, ffn_w_down, moe_router, moe_w_gate, moe_w_up, moe_w_down):
    bsz = x.shape[0]
    mod = _modulation(c, w_mod, b_mod).reshape(bsz, DEPTH, N_MOD, 1, D_MODEL)
    rope = _rope_table(positions)
    for l in range(DEPTH):
        sh1, sc1, g1, sh2, sc2, g2 = [mod[:, l, j] for j in range(N_MOD)]
        w = w_in[l].astype(BF16)
        wa = _dsa_weight_layout(w[:, :N_GROUP_A])
        wb = _mlstm_weight_layout(w[:, N_GROUP_A:N_GROUP_A + N_GROUP_B])
        wc, wvt = _nsa_weight_layout(w[:, N_GROUP_A + N_GROUP_B:])
        pa, mqk, mv, mo, mg, *nsa = _input_projection(x, sc1, sh1, wa, wb, wc, wvt)
        ya, yb, yc = _mixers(pa, (mqk, mv, mo, mg), nsa, rope, dsa_w_uk[l], dsa_w_uv[l], dsa_kv_norm[l], mlstm_conv_w[l], mlstm_conv_b[l], mlstm_gate_b[l], nsa_cmp_pos[l], nsa_cmp_w1[l], nsa_cmp_w2[l], grp_norm[l])
        wo = w_out[l].astype(BF16)
        n_a, n_b = H_A * DH_V, H_B * DH_B
        x = _output_projection(ya, yb, yc, x, g1, _pad_head_rows(wo[:n_a], H_A, DH_V), wo[n_a:n_a + n_b], _pad_head_rows(wo[n_a + n_b:], H_C, DH_C), ln_g[l, 0], ln_b[l, 0])
        if l % 2 == 0:
            k = l // 2
            x = _dense_ffn(x, sc2, sh2, g2, ffn_w_gate[k].astype(BF16), ffn_w_up[k].astype(BF16), ffn_w_down[k].astype(BF16), ln_g[l, 1], ln_b[l, 1])
        else:
            k = l // 2
            routing = _moe_router(x, sc2, sh2, moe_router[k])
            x = _moe_ffn(x, sc2, sh2, g2, routing,moe_w_gate[k].astype(BF16), moe_w_up[k].astype(BF16), moe_w_down[k].astype(BF16), ln_g[l, 1], ln_b[l, 1])
    return x
```

```python
import functools

import numpy as np
import jax
import jax.numpy as jnp
from jax import lax
from jax.experimental import pallas as pl
from jax.experimental.pallas import tpu as pltpu

F32 = jnp.float32
BF16 = jnp.bfloat16

D_MODEL = 1024
DEPTH = 2
H_A, DH_NOPE, DH_ROPE, D_C, DH_V, H_I, D_I = 4, 64, 32, 128, 64, 4, 64
K_SEL_MAX = 256
ROPE_THETA = 10000.0
H_B, DH_B, CONV_W, CHUNK = 4, 128, 4, 64
H_C, DH_C, L_CMP, D_STRIDE, CMP_HIDDEN, L_SEL, N_TOP_MAX, WINDOW, Q_BLOCK = 4, 64, 32, 16, 128, 64, 16, 512, 128
D_FF = 2816
N_EXPERTS = 8
D_FF_EXPERT = 3584
N_MOD = 6
ALPHA = (2 * DEPTH) ** 0.25
SPLIT_SIZES = (H_A * DH_NOPE, H_A * DH_ROPE, D_C, DH_ROPE, H_I * D_I, D_I, H_I, H_B * DH_B, H_B * DH_B, H_B * DH_B, H_B, H_B, H_B * DH_B, H_C * DH_C, DH_C, DH_C, DH_C, DH_C, DH_C, DH_C, 3 * H_C)
N_GROUP_A = sum(SPLIT_SIZES[:7])
N_GROUP_B = sum(SPLIT_SIZES[7:13])
N_GROUP_C = sum(SPLIT_SIZES[13:])

LOG2E = 1.4426950408889634
LANE = 128
VMEM_LIMIT = 56 * 1024 * 1024


def _round_up(n, m):
    return (n + m - 1) // m * m


def _params(*sem):
    return pltpu.CompilerParams(dimension_semantics=sem, vmem_limit_bytes=VMEM_LIMIT)


FOLD_ROWS = 64


def _fold_rows(x, op):
    parts = [x[i:i + FOLD_ROWS] for i in range(0, x.shape[0], FOLD_ROWS)]
    while len(parts) > 1:
        parts = [op(parts[i], parts[i + 1]) if i + 1 < len(parts) else parts[i] for i in range(0, len(parts), 2)]
    return parts[0]


def _attend_block(s_buf, b_buf, v_t, m, l, acc_scr, n_head, tq):
    bias = b_buf[...]
    m_new, l_new, ps = [], [], []
    for h in range(n_head):
        hs = slice(h * tq, (h + 1) * tq)
        sh = s_buf[:, hs] + bias
        mh = jnp.maximum(m[:, hs], jnp.max(_fold_rows(sh, jnp.maximum), axis=0, keepdims=True))
        ph = jnp.exp2(sh - mh)
        l_new.append(jnp.exp2(m[:, hs] - mh) * l[:, hs] + jnp.sum(_fold_rows(ph, jnp.add), axis=0, keepdims=True))
        m_new.append(mh)
        ps.append(ph.astype(BF16))
    m_new = jnp.concatenate(m_new, axis=1)
    acc_scr[...] = jnp.exp2(m - m_new) * acc_scr[...] + jnp.dot(v_t, jnp.concatenate(ps, axis=1), preferred_element_type=F32)
    return m_new, jnp.concatenate(l_new, axis=1)


PLANE_KEYS = 256


def _bit_transpose32(x):
    w = [x[8 * i:8 * i + 8, :] for i in range(32)]
    j, m = 16, 0x0000FFFF
    while j:
        k = 0
        while k < 32:
            t = (w[k] ^ lax.shift_right_logical(w[k + j], j)) & m
            w[k] = w[k] ^ t
            w[k + j] = w[k + j] ^ jnp.left_shift(t, j)
            k = (k + j + 1) & ~j
        j >>= 1
        m ^= (m << j) & 0xFFFFFFFF
    return w


def _layer_norm_rows(z, g, b):
    mu = jnp.mean(z, -1, keepdims=True)
    zc = z - mu
    var = jnp.mean(zc * zc, -1, keepdims=True)
    return zc * lax.rsqrt(var + 1e-5) * g + b


def _mod_kernel(c_ref, w_ref, b_ref, o_ref):
    c = c_ref[...]
    a = c * jax.nn.sigmoid(c)
    o_ref[...] = jnp.dot(a, w_ref[...], preferred_element_type=F32, precision=lax.Precision.HIGHEST) + b_ref[...]


def _modulation(c, w_mod, b_mod):
    bsz, d = c.shape
    n = w_mod.shape[1]
    tn = 1024
    return pl.pallas_call(
        _mod_kernel,
        grid=(n // tn,),
        in_specs=[pl.BlockSpec((bsz, d), lambda j: (0, 0)),
                  pl.BlockSpec((d, tn), lambda j: (0, j)),
                  pl.BlockSpec((1, tn), lambda j: (0, j))],
        out_specs=pl.BlockSpec((bsz, tn), lambda j: (0, j)),
        out_shape=jax.ShapeDtypeStruct((bsz, n), F32),
        compiler_params=_params("arbitrary"),
        name="adaln_mod",
    )(c, w_mod, b_mod.reshape(1, n))


NSA_Q0 = 0
NSA_CMP0 = H_C * LANE
NSA_KS0 = NSA_CMP0 + LANE
NSA_KW0 = NSA_KS0 + LANE
NSA_G0 = NSA_KW0 + LANE
NSA_COLS = NSA_G0 + LANE


def _nsa_weight_layout(w):
    d = w.shape[0]
    nq, nkc, nvc, nks, nvs, nkw, nvw, ng = jnp.split(w, np.cumsum(SPLIT_SIZES[13:])[:-1].tolist(), axis=1)
    z = lambda n: jnp.zeros((d, n), w.dtype)
    half = LANE - DH_C
    cols = []
    for h in range(H_C):
        cols += [nq[:, h * DH_C:(h + 1) * DH_C], z(half)]
    cols += [nkc, nvc, nks, z(half), nkw, z(half), ng, z(LANE - 3 * H_C)]
    values_t = jnp.concatenate([nvs, z(half), nvw, z(half)], axis=1).T
    return jnp.concatenate(cols, axis=1), values_t


def _mlstm_weight_layout(w):
    mq, mk, mv, mi, mf, mo = jnp.split(w, np.cumsum(SPLIT_SIZES[7:13])[:-1].tolist(), axis=1)
    return jnp.concatenate([mq, mk, mv, mo, mi, mf, jnp.zeros((w.shape[0], LANE - 2 * H_B), w.dtype)], axis=1)


ML_D = H_B * DH_B


def _inproj_kernel(x_ref, sc_ref, sh_ref, wa_ref, wb_ref, wc_ref, wvt_ref, cos_ref, sin_ref, wuk_ref, kvn_ref,
                   qc_ref, iq_ref, iw_ref, kc_ref, ik_ref, ct_ref, mqk_ref, mv_ref, mo_ref, mg_ref,
                   nq_ref, ncmp_ref, nks_ref, nvs_ref, nkw_ref, nvw_ref, ng_ref):
    u = (x_ref[...] * (1.0 + sc_ref[...]) + sh_ref[...]).astype(BF16)
    _dsa_operands(jnp.dot(u, wa_ref[...], preferred_element_type=F32), cos_ref, sin_ref, wuk_ref, kvn_ref,
                  qc_ref, iq_ref, iw_ref, kc_ref, ik_ref, ct_ref)
    ob = jnp.dot(u, wb_ref[...], preferred_element_type=F32)
    mqk_ref[...] = ob[:, 0:2 * ML_D]
    mv_ref[...] = ob[:, 2 * ML_D:3 * ML_D].astype(BF16)
    mo_ref[...] = ob[:, 3 * ML_D:4 * ML_D]
    mg_ref[...] = ob[:, 4 * ML_D:4 * ML_D + LANE]
    oc = jnp.dot(u, wc_ref[...], preferred_element_type=F32)
    nq_ref[...] = (oc[:, NSA_Q0:NSA_Q0 + H_C * LANE] * (DH_C ** -0.5 * LOG2E)).astype(BF16)
    ncmp_ref[...] = oc[:, NSA_CMP0:NSA_CMP0 + LANE]
    nks_ref[...] = oc[:, NSA_KS0:NSA_KS0 + LANE].astype(BF16)
    nkw_ref[...] = oc[:, NSA_KW0:NSA_KW0 + LANE].astype(BF16)
    ng_ref[...] = oc[:, NSA_G0:NSA_G0 + LANE]
    vt = lax.dot_general(wvt_ref[...], u, (((1,), (1,)), ((), ())), preferred_element_type=F32)
    nvs_ref[...] = vt[0:LANE].astype(BF16)
    nvw_ref[...] = vt[LANE:2 * LANE].astype(BF16)


def _input_projection(x, sc, sh, wa, wb, wc, wvt, cos, sin, w_uk, kv_norm):
    bsz, seq, d = x.shape
    tm = 512
    na, nb, nc = wa.shape[1], wb.shape[1], wc.shape[1]
    wuk = jnp.pad(w_uk, ((0, 0), (0, LANE - DH_NOPE), (0, 0))).astype(BF16)
    once = pl.Buffered(1)
    row = lambda n: pl.BlockSpec((None, tm, n), lambda b, i: (b, i, 0))
    col = pl.BlockSpec((None, LANE, tm), lambda b, i: (b, 0, i))
    vec = pl.BlockSpec((None, 1, d), lambda b, i: (b, 0, 0))
    full = lambda r, c: pl.BlockSpec((r, c), lambda b, i: (0, 0), pipeline_mode=once)
    tok = lambda n, dt: (row(n), jax.ShapeDtypeStruct((bsz, seq, n), dt))
    feat = (col, jax.ShapeDtypeStruct((bsz, LANE, seq), BF16))
    outs = [tok(2 * H_A * LANE, BF16), tok(H_I * LANE, BF16), tok(LANE, F32), tok(2 * LANE, BF16), tok(LANE, BF16), feat,
            tok(2 * ML_D, F32), tok(ML_D, BF16), tok(ML_D, F32), tok(LANE, F32),
            tok(H_C * LANE, BF16), tok(LANE, F32), tok(LANE, BF16), feat, tok(LANE, BF16), feat, tok(LANE, F32)]
    return pl.pallas_call(
        _inproj_kernel,
        grid=(bsz, seq // tm),
        in_specs=[row(d), vec, vec, full(d, na), full(d, nb), full(d, nc), full(2 * LANE, d), row(LANE), row(LANE),
                  pl.BlockSpec((H_A, LANE, D_C), lambda b, i: (0, 0, 0), pipeline_mode=once), full(1, D_C)],
        out_specs=[spec for spec, _ in outs],
        out_shape=[shape for _, shape in outs],
        compiler_params=_params("arbitrary", "arbitrary"),
        name="input_projection",
    )(x, sc, sh, wa, wb, wc, wvt, cos, sin, wuk, kv_norm.reshape(1, D_C))


DSA_QN0 = 0
DSA_QR0 = H_A * LANE
DSA_CKV0 = DSA_QR0 + LANE
DSA_IQ0 = DSA_CKV0 + LANE
DSA_G0 = DSA_IQ0 + H_I * LANE
DSA_COLS = DSA_G0 + LANE
DSA_KR_LANE = D_I
DSA_IW_LANE = D_I + DH_ROPE
DSA_TQ = 128
DSA_KC = 512
INT_MIN = -2 ** 31
KEY_NEG_INF = int(np.int32(np.float32(-np.inf).view(np.int32)) ^ np.int32(0x7FFFFFFF))


def _dsa_weight_layout(w):
    d = w.shape[0]
    qn, qr, ckv, kr, iq, ik, iw = jnp.split(w, np.cumsum(SPLIT_SIZES[:7])[:-1].tolist(), axis=1)
    z = lambda n: jnp.zeros((d, n), w.dtype)
    cols = []
    for h in range(H_A):
        cols += [qn[:, h * DH_NOPE:(h + 1) * DH_NOPE], z(LANE - DH_NOPE)]
    cols += [qr, ckv]
    for h in range(H_I):
        cols += [iq[:, h * D_I:(h + 1) * D_I], z(LANE - D_I)]
    cols += [ik, kr, iw, z(LANE - D_I - DH_ROPE - H_I)]
    return jnp.concatenate(cols, axis=1)


def _rope_table_kernel(pos_ref, freq_ref, cos_ref, sin_ref):
    ang = pos_ref[...].astype(F32) * freq_ref[...]
    lane = lax.broadcasted_iota(jnp.int32, (1, LANE), 1)
    first = (lane % DH_ROPE) < DH_ROPE // 2
    cos_ref[...] = jnp.cos(ang)
    sin_ref[...] = jnp.where(first, -jnp.sin(ang), jnp.sin(ang))


def _rope_table(positions):
    bsz, seq = positions.shape
    tm = 512
    inv_freq = ROPE_THETA ** (-jnp.arange(0, DH_ROPE, 2, dtype=F32) / DH_ROPE)
    freq = jnp.tile(inv_freq, LANE // (DH_ROPE // 2)).reshape(1, LANE)
    out = pl.BlockSpec((None, tm, LANE), lambda b, i: (b, i, 0))
    return pl.pallas_call(
        _rope_table_kernel,
        grid=(bsz, seq // tm),
        in_specs=[pl.BlockSpec((None, tm, 1), lambda b, i: (b, i, 0)), pl.BlockSpec((1, LANE), lambda b, i: (0, 0))],
        out_specs=[out, out],
        out_shape=[jax.ShapeDtypeStruct((bsz, seq, LANE), F32)] * 2,
        compiler_params=_params("arbitrary", "arbitrary"),
        name="rope_table",
    )(positions.reshape(bsz, seq, 1), freq)


def _dsa_operands(pa_ref, cos_ref, sin_ref, wuk_ref, kvn_ref, qc_ref, iq_ref, iw_ref, kc_ref, ik_ref, ct_ref):
    scale = (DH_NOPE + DH_ROPE) ** -0.5 * LOG2E
    cos, sin = cos_ref[...], sin_ref[...]
    lane = lax.broadcasted_iota(jnp.int32, (1, LANE), 1)
    first = (lane % DH_ROPE) < DH_ROPE // 2
    rope_lanes = (lane >= DSA_KR_LANE) & (lane < DSA_KR_LANE + DH_ROPE)

    def rope(v):
        partner = jnp.where(first, pltpu.roll(v, LANE - DH_ROPE // 2, 1), pltpu.roll(v, DH_ROPE // 2, 1))
        return v * cos + partner * sin

    g = pa_ref[:, DSA_G0:DSA_G0 + LANE]
    ckv = pa_ref[:, DSA_CKV0:DSA_CKV0 + LANE]
    ckv_n = ckv * lax.rsqrt(jnp.mean(ckv * ckv, -1, keepdims=True) + 1e-6) * kvn_ref[...]
    kc_ref[:, 0:LANE] = ckv_n.astype(BF16)
    ct_ref[...] = ckv_n.T.astype(BF16)
    kc_ref[:, LANE:2 * LANE] = jnp.where(rope_lanes, rope(g), 0.0).astype(BF16)
    ik_ref[...] = jnp.where(lane < D_I, g, 0.0).astype(BF16)
    iw_ref[...] = g
    iq_ref[...] = pa_ref[:, DSA_IQ0:DSA_IQ0 + H_I * LANE].astype(BF16)
    qr = rope(pa_ref[:, DSA_QR0:DSA_QR0 + LANE]) * scale
    for h in range(H_A):
        qn = pa_ref[:, DSA_QN0 + h * LANE:DSA_QN0 + (h + 1) * LANE].astype(BF16)
        q_abs = jnp.dot(qn, wuk_ref[h], preferred_element_type=F32) * scale
        shift = (DSA_KR_LANE - DH_ROPE * h) % LANE
        qr_h = pltpu.roll(qr, shift, 1) if shift else qr
        qc_ref[:, 2 * h * LANE:(2 * h + 1) * LANE] = q_abs.astype(BF16)
        qc_ref[:, (2 * h + 1) * LANE:(2 * h + 2) * LANE] = jnp.where(rope_lanes, qr_h, 0.0).astype(BF16)


def _dsa_kernel(k_sel, qc_ref, iq_ref, iw_ref, kc_ref, ct_ref, ik_ref, tri_ref, wuv_ref, gn_ref, o_ref,
                key_scr, planes_scr, acc_scr, s0_scr, s1_scr, b0_scr, b1_scr):
    tq, kcs = DSA_TQ, DSA_KC
    t0 = pl.program_id(1) * tq
    n_chunk = (t0 + tq + kcs - 1) // kcs
    nt = (((1,), (1,)), ((), ()))
    qpos = t0 + lax.broadcasted_iota(jnp.int32, (1, tq), 1)
    iw_t = iw_ref[...].T

    def score_chunk(c, carry):
        k0 = pl.multiple_of(c * kcs, kcs)
        ikc = ik_ref[pl.ds(k0, kcs), :]
        sc = jnp.zeros((kcs, tq), F32)
        for h in range(H_I):
            lg = lax.dot_general(ikc, iq_ref[:, h * LANE:(h + 1) * LANE], nt, preferred_element_type=F32)
            sc = sc + jnp.maximum(lg, 0.0) * iw_t[DSA_IW_LANE + h:DSA_IW_LANE + h + 1, :]
        sc = jnp.where(sc == 0.0, 0.0, sc)
        kpos = k0 + lax.broadcasted_iota(jnp.int32, (kcs, 1), 0)
        sc = jnp.where(kpos <= qpos, sc, -jnp.inf)
        bits = pltpu.bitcast(sc, jnp.int32)
        key = jnp.where(bits < 0, bits ^ 0x7FFFFFFF, bits)
        key_scr[pl.ds(k0, kcs), :] = key
        ukey = key ^ INT_MIN
        for blk in range(kcs // PLANE_KEYS):
            words = _bit_transpose32(ukey[blk * PLANE_KEYS:(blk + 1) * PLANE_KEYS])
            row0 = pl.multiple_of((c * (kcs // PLANE_KEYS) + blk) * 8, 8)
            for r in range(32):
                planes_scr[r, pl.ds(row0, 8), :] = words[r]
        return carry

    lax.fori_loop(0, n_chunk, score_chunk, 0)

    n_words = planes_scr.shape[1]
    word_row = lax.broadcasted_iota(jnp.int32, (n_words, tq), 0)
    alive0 = jnp.where(word_row < n_chunk * (kcs // 32), jnp.int32(-1), jnp.int32(0))
    k_int = int(k_sel)

    def bit_pass(i, carry):
        thr_u, above, alive = carry
        ones = alive & planes_scr[i]
        seen_ones = above + jnp.sum(lax.population_count(ones), axis=0, keepdims=True)
        take = seen_ones >= k_int
        alive = jnp.where(take, ones, alive & ~planes_scr[i])
        above = jnp.where(take, above, seen_ones)
        thr_u = jnp.where(take, thr_u | jnp.left_shift(jnp.int32(1), 31 - i), thr_u)
        return thr_u, above, alive

    thr_u, above, _ = lax.fori_loop(0, 32, bit_pass, (jnp.zeros((1, tq), jnp.int32), jnp.zeros((1, tq), jnp.int32), alive0))
    thr = thr_u ^ INT_MIN
    room = (k_int - above).astype(F32)

    qall = jnp.concatenate([qc_ref[:, 2 * h * LANE:(2 * h + 2) * LANE] for h in range(H_A)], axis=0)

    room = jnp.where(thr > KEY_NEG_INF, room, 0.0)
    acc_scr[...] = jnp.zeros_like(acc_scr)

    half = kcs // 2

    def prepare(k0, seen, s_buf, b_buf):
        key = key_scr[pl.ds(k0, half), :]
        tie = jnp.where(key == thr, 1.0, 0.0)
        prefix = jnp.dot(tri_ref[...], tie.astype(BF16), preferred_element_type=F32)
        tie_bias = jnp.where(seen + prefix <= room, jnp.where(key == thr, 0.0, NEG), NEG)
        b_buf[...] = jnp.where(key > thr, 0.0, tie_bias)
        s_buf[...] = lax.dot_general(kc_ref[pl.ds(k0, half), :], qall, nt, preferred_element_type=F32)
        return seen + jnp.sum(_fold_rows(tie, jnp.add), axis=0, keepdims=True)

    def attn_chunk(c, carry):
        m, l, seen = carry
        k0 = pl.multiple_of(c * kcs, kcs)
        k1 = pl.multiple_of(k0 + half, half)
        seen = prepare(k1, seen, s1_scr, b1_scr)
        m, l = _attend_block(s0_scr, b0_scr, ct_ref[:, pl.ds(k0, half)], m, l, acc_scr, H_A, tq)
        k2 = pl.multiple_of(jnp.minimum(c + 1, n_chunk - 1) * kcs, kcs)
        seen = prepare(k2, seen, s0_scr, b0_scr)
        m, l = _attend_block(s1_scr, b1_scr, ct_ref[:, pl.ds(k1, half)], m, l, acc_scr, H_A, tq)
        return m, l, seen

    seen0 = prepare(0, jnp.zeros((1, tq), F32), s0_scr, b0_scr)
    init = (jnp.full((1, H_A * tq), M_FLOOR, F32), jnp.zeros((1, H_A * tq), F32), seen0)
    _, l, _ = lax.fori_loop(0, n_chunk, attn_chunk, init)
    o_lat = (acc_scr[...] / jnp.maximum(l, 1e-30)).astype(BF16)
    for h in range(H_A):
        o = lax.dot_general(o_lat[:, h * tq:(h + 1) * tq], wuv_ref[h], (((0,), (0,)), ((), ())),
                            preferred_element_type=F32)
        ms = jnp.sum(o * o, -1, keepdims=True) * (1.0 / DH_V)
        o_ref[:, h * LANE:(h + 1) * LANE] = (o * lax.rsqrt(ms + 1e-6) * gn_ref[h:h + 1, :]).astype(o_ref.dtype)


def _dsa_attention(qc, iq, iw, kc, ct, ik, w_uv, g_a):
    bsz, seq, _ = qc.shape
    k_sel = float(min(K_SEL_MAX, seq // 4))
    tq, kcs = DSA_TQ, DSA_KC
    half = kcs // 2
    tri = jnp.asarray(np.tril(np.ones((half, half), np.float32)), BF16)
    wuv = jnp.pad(w_uv, ((0, 0), (0, 0), (0, LANE - DH_V))).astype(BF16)
    gn = jnp.pad(g_a.reshape(H_A, DH_V), ((0, 0), (0, LANE - DH_V)))
    row = lambda n: pl.BlockSpec((None, tq, n), lambda b, i: (b, i, 0))
    per_b = lambda n: pl.BlockSpec((None, seq, n), lambda b, i: (b, 0, 0))
    return pl.pallas_call(
        functools.partial(_dsa_kernel, k_sel),
        grid=(bsz, seq // tq),
        in_specs=[row(2 * H_A * LANE), row(H_I * LANE), row(LANE), per_b(2 * LANE),
                  pl.BlockSpec((None, D_C, seq), lambda b, i: (b, 0, 0)), per_b(LANE),
                  pl.BlockSpec((half, half), lambda b, i: (0, 0)), pl.BlockSpec((H_A, D_C, LANE), lambda b, i: (0, 0, 0)),
                  pl.BlockSpec((H_A, LANE), lambda b, i: (0, 0))],
        out_specs=row(H_A * LANE),
        out_shape=jax.ShapeDtypeStruct((bsz, seq, H_A * LANE), BF16),
        scratch_shapes=[pltpu.VMEM((seq, tq), jnp.int32), pltpu.VMEM((32, seq // 32, tq), jnp.int32),
                        pltpu.VMEM((D_C, H_A * tq), F32),
                        pltpu.VMEM((half, H_A * tq), F32), pltpu.VMEM((half, H_A * tq), F32),
                        pltpu.VMEM((half, tq), F32), pltpu.VMEM((half, tq), F32)],
        compiler_params=_params("arbitrary", "arbitrary"),
        name="dsa_attention",
    )(qc, iq, iw, kc, ct, ik, tri, wuv, gn)


def _nsa_compress_kernel(a_ref, pos_ref, w1t_ref, w1b_ref, w2k_ref, w2v_ref, kc_ref, vc_ref):
    a = a_ref[...]
    top = jnp.dot((a + pos_ref[0:1, :]).astype(BF16), w1t_ref[...], preferred_element_type=F32)
    bot = jnp.dot((a + pos_ref[1:2, :]).astype(BF16), w1b_ref[...], preferred_element_type=F32)
    n = a.shape[0]
    pre = top + jnp.concatenate([bot[1:], bot[:1]], axis=0)
    h = (pre * jax.nn.sigmoid(pre)).astype(BF16)
    hid = w2k_ref.shape[0]
    kc_ref[...] = jnp.dot(h[:, :hid], w2k_ref[...], preferred_element_type=F32).astype(BF16)
    vc_ref[...] = lax.dot_general(w2v_ref[...], h[:, hid:], (((1,), (1,)), ((), ())), preferred_element_type=F32).astype(BF16)


def _nsa_compress(ncmp, cmp_pos, cmp_w1, cmp_w2):
    bsz, seq, _ = ncmp.shape
    n_grp = seq // D_STRIDE
    per = L_CMP // D_STRIDE
    width = D_STRIDE * LANE
    a = ncmp.reshape(bsz, n_grp, width)
    w1 = cmp_w1.reshape(2, per, D_STRIDE, DH_C, CMP_HIDDEN)
    zer = jnp.zeros((D_STRIDE, DH_C, CMP_HIDDEN), cmp_w1.dtype)

    def expand(p):
        wk = jnp.concatenate([w1[0, p], zer], axis=1)
        wv = jnp.concatenate([zer, w1[1, p]], axis=1)
        return jnp.concatenate([wk, wv], axis=2).reshape(width, 2 * CMP_HIDDEN).astype(BF16)

    pos = cmp_pos.reshape(2, per, D_STRIDE, DH_C)
    pos = jnp.concatenate([pos[0], pos[1]], axis=-1).reshape(per, width)
    pad_out = ((0, 0), (0, LANE - DH_C))
    w2k = jnp.pad(cmp_w2[0], pad_out).astype(BF16)
    w2v = jnp.pad(cmp_w2[1], pad_out).astype(BF16).T
    full = lambda r, c: pl.BlockSpec((r, c), lambda b: (0, 0))
    return pl.pallas_call(
        _nsa_compress_kernel,
        grid=(bsz,),
        in_specs=[pl.BlockSpec((None, n_grp, width), lambda b: (b, 0, 0)), full(per, width),
                  full(width, 2 * CMP_HIDDEN), full(width, 2 * CMP_HIDDEN), full(CMP_HIDDEN, LANE), full(LANE, CMP_HIDDEN)],
        out_specs=[pl.BlockSpec((None, n_grp, LANE), lambda b: (b, 0, 0)), pl.BlockSpec((None, LANE, n_grp), lambda b: (b, 0, 0))],
        out_shape=[jax.ShapeDtypeStruct((bsz, n_grp, LANE), BF16), jax.ShapeDtypeStruct((bsz, LANE, n_grp), BF16)],
        compiler_params=_params("arbitrary"),
        name="nsa_compress",
    )(a, pos, expand(0), expand(1), w2k, w2v)


NSA_TQ = 128
NSA_KC = 512
NEG = -1e30
M_FLOOR = -1e20


def _softmax_cols(s, bias, n_head, tq):
    out = []
    for h in range(n_head):
        sh = s[:, h * tq:(h + 1) * tq] + bias
        m = jnp.maximum(jnp.max(_fold_rows(sh, jnp.maximum), axis=0, keepdims=True), M_FLOOR)
        e = jnp.exp2(sh - m)
        den = jnp.sum(_fold_rows(e, jnp.add), axis=0, keepdims=True)
        out.append(e * (1.0 / jnp.maximum(den, 1e-30)))
    return out


def _nsa_kernel(q_ref, g_ref, kc_ref, vct_ref, ks_ref, vst_ref, kw_ref, vwt_ref, cover_ref, expand_ref, gn_ref, o_ref,
                acc_scr, s0_scr, s1_scr, b0_scr, b1_scr):
    tq, kc_sz = NSA_TQ, NSA_KC
    t0 = pl.program_id(1) * tq
    nt = (((1,), (1,)), ((), ()))
    q = q_ref[...]
    qa = jnp.concatenate([q[:, h * LANE:(h + 1) * LANE] for h in range(H_C)], axis=0)
    qpos = t0 + lax.broadcasted_iota(jnp.int32, (1, tq), 1)

    n_grp = kc_ref.shape[0]
    s_c = lax.dot_general(kc_ref[...], qa, nt, preferred_element_type=F32)
    n_idx = lax.broadcasted_iota(jnp.int32, (n_grp, 1), 0)
    visible = jnp.where(n_idx < n_grp - 1, n_idx * D_STRIDE + (L_CMP - 1), 2 ** 30)
    p_c = _softmax_cols(s_c, jnp.where(visible <= qpos, 0.0, NEG), H_C, tq)
    o_cmp = jnp.dot(vct_ref[...], jnp.concatenate([p.astype(BF16) for p in p_c], axis=1), preferred_element_type=F32)

    p_sum = p_c[0]
    for h in range(1, H_C):
        p_sum = p_sum + p_c[h]
    hi = p_sum.astype(BF16)
    lo = (p_sum - hi.astype(F32)).astype(BF16)
    cov = cover_ref[...]
    imp_t = jnp.dot(cov, hi, preferred_element_type=F32) + jnp.dot(cov, lo, preferred_element_type=F32)
    n_sel = cov.shape[0]
    jrow = lax.broadcasted_iota(jnp.int32, (n_sel, tq), 0)
    cur = (t0 + lax.broadcasted_iota(jnp.int32, (n_sel, tq), 1)) // L_SEL
    adm = jrow <= cur
    forced = (jrow == 0) | (jrow == cur) | (jrow == cur - 1)
    val = jnp.where(adm & forced, jnp.inf, jnp.where(adm, imp_t, -jnp.inf))
    rank = jnp.zeros((n_sel, tq), F32)
    for jp in range(n_sel):
        r = val[jp:jp + 1, :]
        rank = rank + jnp.where(r == val, jnp.where(jrow > jp, 1.0, 0.0), jnp.where(r > val, 1.0, 0.0))
    sel_t = jnp.where(rank < min(N_TOP_MAX, n_sel), jnp.where(val > -jnp.inf, 1.0, 0.0), 0.0)
    if n_sel < LANE:
        sel_t = jnp.concatenate([sel_t, jnp.zeros((LANE - n_sel, tq), F32)], axis=0)
    sel_t = sel_t.astype(BF16)

    acc_scr[...] = jnp.zeros_like(acc_scr)

    half = kc_sz // 2
    n_chunk = (t0 + tq + kc_sz - 1) // kc_sz

    def prepare(k0, s_buf, b_buf):
        s_buf[...] = lax.dot_general(ks_ref[pl.ds(k0, half), :], qa, nt, preferred_element_type=F32)
        picked = jnp.dot(expand_ref[pl.ds(k0, half), :], sel_t, preferred_element_type=F32)
        kpos = k0 + lax.broadcasted_iota(jnp.int32, (half, 1), 0)
        b_buf[...] = jnp.where(kpos <= qpos, jnp.where(picked > 0.5, 0.0, NEG), NEG)

    def chunk(c, carry):
        m, l = carry
        k0 = pl.multiple_of(c * kc_sz, kc_sz)
        k1 = pl.multiple_of(k0 + half, half)
        prepare(k1, s1_scr, b1_scr)
        m, l = _attend_block(s0_scr, b0_scr, vst_ref[:, pl.ds(k0, half)], m, l, acc_scr, H_C, tq)
        prepare(pl.multiple_of(jnp.minimum(c + 1, n_chunk - 1) * kc_sz, kc_sz), s0_scr, b0_scr)
        return _attend_block(s1_scr, b1_scr, vst_ref[:, pl.ds(k1, half)], m, l, acc_scr, H_C, tq)

    prepare(0, s0_scr, b0_scr)
    init = (jnp.full((1, H_C * tq), M_FLOOR, F32), jnp.zeros((1, H_C * tq), F32))
    _, l_s = lax.fori_loop(0, n_chunk, chunk, init)
    o_slc = acc_scr[...] * (1.0 / jnp.maximum(l_s, 1e-30))

    span = WINDOW + tq
    w0 = pl.multiple_of(jnp.maximum(t0 - WINDOW, 0), tq)
    s_w = lax.dot_general(kw_ref[pl.ds(w0, span), :], qa, nt, preferred_element_type=F32)
    kpos_w = w0 + lax.broadcasted_iota(jnp.int32, (span, 1), 0)
    wbias = jnp.where(kpos_w <= qpos, jnp.where(qpos - kpos_w < WINDOW, 0.0, NEG), NEG)
    p_w = _softmax_cols(s_w, wbias, H_C, tq)
    o_win = jnp.dot(vwt_ref[:, pl.ds(w0, span)], jnp.concatenate([p.astype(BF16) for p in p_w], axis=1), preferred_element_type=F32)

    gates_t = jax.nn.sigmoid(g_ref[...]).T
    for h in range(H_C):
        hs = slice(h * tq, (h + 1) * tq)
        o_t = (gates_t[h:h + 1] * o_cmp[:, hs] + gates_t[H_C + h:H_C + h + 1] * o_slc[:, hs]
               + gates_t[2 * H_C + h:2 * H_C + h + 1] * o_win[:, hs])
        o = o_t.T
        ms = jnp.sum(o * o, -1, keepdims=True) * (1.0 / DH_C)
        o_ref[:, h * LANE:(h + 1) * LANE] = (o * lax.rsqrt(ms + 1e-6) * gn_ref[h:h + 1, :]).astype(o_ref.dtype)


def _nsa_attention(nq, ng, kc, vc_t, nks, nvs_t, nkw, nvw_t, g_c):
    bsz, seq, _ = nq.shape
    n_grp = kc.shape[1]
    n_sel = seq // L_SEL
    grp_start = np.arange(n_grp) * D_STRIDE
    sel_start = np.arange(n_sel) * L_SEL
    cover_t = ((grp_start[None, :] < sel_start[:, None] + L_SEL) & (grp_start[None, :] + L_CMP > sel_start[:, None]))
    cover_t = jnp.asarray(cover_t.astype(np.float32), BF16)
    expand = (np.arange(seq)[:, None] // L_SEL == np.arange(LANE)[None, :]).astype(np.float32)
    expand = jnp.asarray(expand, BF16)
    gn = jnp.pad(g_c.reshape(H_C, DH_C), ((0, 0), (0, LANE - DH_C)))
    tq = NSA_TQ
    row = lambda n: pl.BlockSpec((None, tq, n), lambda b, i: (b, i, 0))
    per_b = lambda r, c: pl.BlockSpec((None, r, c), lambda b, i: (b, 0, 0))
    full = lambda r, c: pl.BlockSpec((r, c), lambda b, i: (0, 0))
    return pl.pallas_call(
        _nsa_kernel,
        grid=(bsz, seq // tq),
        in_specs=[row(H_C * LANE), row(LANE), per_b(n_grp, LANE), per_b(LANE, n_grp), per_b(seq, LANE), per_b(LANE, seq),
                  per_b(seq, LANE), per_b(LANE, seq), full(n_sel, n_grp), full(seq, LANE), full(H_C, LANE)],
        out_specs=row(H_C * LANE),
        out_shape=jax.ShapeDtypeStruct((bsz, seq, H_C * LANE), BF16),
        scratch_shapes=[pltpu.VMEM((LANE, H_C * tq), F32),
                        pltpu.VMEM((NSA_KC // 2, H_C * tq), F32), pltpu.VMEM((NSA_KC // 2, H_C * tq), F32),
                        pltpu.VMEM((NSA_KC // 2, tq), F32), pltpu.VMEM((NSA_KC // 2, tq), F32)],
        compiler_params=_params("arbitrary", "arbitrary"),
        name="nsa_attention",
    )(nq, ng, kc, vc_t, nks, nvs_t, nkw, nvw_t, cover_t, expand, gn)


def _outproj_kernel(ya_ref, yb_ref, yc_ref, x_ref, g_ref, wa_ref, wb_ref, wc_ref, lng_ref, lnb_ref, o_ref):
    y = jnp.dot(ya_ref[...], wa_ref[...], preferred_element_type=F32)
    y += jnp.dot(yb_ref[...], wb_ref[...], preferred_element_type=F32)
    y += jnp.dot(yc_ref[...], wc_ref[...], preferred_element_type=F32)
    z = ALPHA * x_ref[...] + (1.0 + g_ref[...]) * y
    o_ref[...] = _layer_norm_rows(z, lng_ref[...], lnb_ref[...])


def _pad_head_rows(w, n_head, dh):
    d = w.shape[1]
    return jnp.pad(w.reshape(n_head, dh, d), ((0, 0), (0, LANE - dh), (0, 0))).reshape(n_head * LANE, d)


def _output_projection(ya, yb, yc, x, g, wa, wb, wc, ln_g, ln_b):
    bsz, seq, d = x.shape
    tm = 512
    na, nb, nc = ya.shape[-1], yb.shape[-1], yc.shape[-1]
    row = lambda n: pl.BlockSpec((None, tm, n), lambda b, i: (b, i, 0))
    vec = pl.BlockSpec((None, 1, d), lambda b, i: (b, 0, 0))
    full = lambda r, c: pl.BlockSpec((r, c), lambda b, i: (0, 0))
    return pl.pallas_call(
        _outproj_kernel,
        grid=(bsz, seq // tm),
        in_specs=[row(na), row(nb), row(nc), row(d), vec, full(na, d), full(nb, d), full(nc, d), full(1, d), full(1, d)],
        out_specs=row(d),
        out_shape=jax.ShapeDtypeStruct((bsz, seq, d), F32),
        compiler_params=_params("arbitrary", "arbitrary"),
        name="output_projection_ln",
    )(ya, yb, yc, x, g, wa, wb, wc, ln_g.reshape(1, d), ln_b.reshape(1, d))


def _tile_gate_up(wg, wu, tf):
    *lead, d, ff = wg.shape
    split = lambda w: jnp.moveaxis(w.reshape(*lead, d, ff // tf, tf), -2, -3)
    return jnp.concatenate([split(wg), split(wu)], axis=-1)


def _swiglu_chunk(u, wgu_ref, wd_ref):
    tf = wd_ref.shape[0]
    ab = jnp.dot(u, wgu_ref[...], preferred_element_type=F32)
    a, b = ab[:, :tf], ab[:, tf:]
    return jnp.dot((a * jax.nn.sigmoid(a) * b).astype(BF16), wd_ref[...], preferred_element_type=F32)


def _ffn_kernel(x_ref, sc_ref, sh_ref, g_ref, wgu_ref, wd_ref, lng_ref, lnb_ref, o_ref, u_scr, acc_scr):
    f = pl.program_id(2)

    @pl.when(f == 0)
    def _():
        u_scr[...] = (x_ref[...] * (1.0 + sc_ref[...]) + sh_ref[...]).astype(BF16)
        acc_scr[...] = jnp.zeros_like(acc_scr)

    acc_scr[...] += _swiglu_chunk(u_scr[...], wgu_ref, wd_ref)

    @pl.when(f == pl.num_programs(2) - 1)
    def _():
        z = ALPHA * x_ref[...] + (1.0 + g_ref[...]) * acc_scr[...]
        o_ref[...] = _layer_norm_rows(z, lng_ref[...], lnb_ref[...])


def _dense_ffn(x, sc, sh, g, wg, wu, wd, ln_g, ln_b):
    bsz, seq, d = x.shape
    ff = wg.shape[1]
    tm, tf = 1024, 256
    row = pl.BlockSpec((None, tm, d), lambda b, i, f: (b, i, 0))
    vec = pl.BlockSpec((None, 1, d), lambda b, i, f: (b, 0, 0))
    one = pl.BlockSpec((1, d), lambda b, i, f: (0, 0))
    return pl.pallas_call(
        _ffn_kernel,
        grid=(bsz, seq // tm, ff // tf),
        in_specs=[row, vec, vec, vec,
                  pl.BlockSpec((None, d, 2 * tf), lambda b, i, f: (f, 0, 0)),
                  pl.BlockSpec((tf, d), lambda b, i, f: (f, 0)),
                  one, one],
        out_specs=row,
        out_shape=jax.ShapeDtypeStruct((bsz, seq, d), F32),
        scratch_shapes=[pltpu.VMEM((tm, d), BF16), pltpu.VMEM((tm, d), F32)],
        compiler_params=_params("arbitrary", "arbitrary", "arbitrary"),
        name="dense_swiglu_ln",
    )(x, sc, sh, g, _tile_gate_up(wg, wu, tf), wd, ln_g.reshape(1, d), ln_b.reshape(1, d))


def _router_kernel(x_ref, sc_ref, sh_ref, r_ref, lo_ref, up_ref, gate_ref, slot_ref, slot_t_ref, cnt_ref):
    u = x_ref[...] * (1.0 + sc_ref[...]) + sh_ref[...]
    logits = jnp.dot(u, r_ref[...], preferred_element_type=F32, precision=lax.Precision.HIGHEST)
    lane = lax.broadcasted_iota(jnp.int32, logits.shape, 1)
    neg = -jnp.inf
    l1 = jnp.where(lane < N_EXPERTS, logits, neg)
    m1 = jnp.max(l1, -1, keepdims=True)
    i1 = jnp.min(jnp.where(l1 == m1, lane, LANE), -1, keepdims=True)
    l2 = jnp.where(lane == i1, neg, l1)
    m2 = jnp.max(l2, -1, keepdims=True)
    i2 = jnp.min(jnp.where(l2 == m2, lane, LANE), -1, keepdims=True)
    e2 = jnp.exp(m2 - m1)
    w1 = 1.0 / (1.0 + e2)
    w2 = e2 / (1.0 + e2)
    gate_ref[...] = jnp.where(lane == i1, w1, jnp.where(lane == i2, w2, 0.0))
    routed = jnp.where((lane == i1) | (lane == i2), 1.0, 0.0)
    before = jnp.dot(lo_ref[...], routed.astype(BF16), preferred_element_type=F32)
    slot_ref[...] = jnp.where(routed > 0.5, before, -1.0)
    routed_t = routed.T
    before_t = jnp.dot(routed_t.astype(BF16), up_ref[...], preferred_element_type=F32)
    slot_t_ref[...] = jnp.where(routed_t > 0.5, before_t, -1.0)
    cnt_ref[...] = jnp.broadcast_to(jnp.sum(routed, axis=0, keepdims=True), cnt_ref.shape).astype(jnp.int32)


MOE_TM = 1024
MOE_ROWS = 288
MOE_TF = 896
MOE_GROUP = 2


def _moe_router(x, sc, sh, router):
    bsz, seq, d = x.shape
    tm = MOE_TM
    r = jnp.pad(router, ((0, 0), (0, LANE - router.shape[1])))
    upper = np.triu(np.ones((tm, tm), np.float32), 1)
    up, lo = jnp.asarray(upper, BF16), jnp.asarray(upper.T, BF16)
    row = lambda n: pl.BlockSpec((None, tm, n), lambda b, i: (b, i, 0))
    vec = pl.BlockSpec((None, 1, d), lambda b, i: (b, 0, 0))
    full = lambda a, c: pl.BlockSpec((a, c), lambda b, i: (0, 0))
    n_tile = seq // tm
    return pl.pallas_call(
        _router_kernel,
        grid=(bsz, n_tile),
        in_specs=[row(d), vec, vec, full(d, LANE), full(tm, tm), full(tm, tm)],
        out_specs=[row(LANE), row(LANE), pl.BlockSpec((None, None, LANE, tm), lambda b, i: (b, i, 0, 0)),
                   pl.BlockSpec((None, None, 8, LANE), lambda b, i: (b, i, 0, 0))],
        out_shape=[jax.ShapeDtypeStruct((bsz, seq, LANE), F32), jax.ShapeDtypeStruct((bsz, seq, LANE), F32),
                   jax.ShapeDtypeStruct((bsz, n_tile, LANE, tm), F32), jax.ShapeDtypeStruct((bsz, n_tile, 8, LANE), jnp.int32)],
        compiler_params=_params("arbitrary", "arbitrary"),
        name="moe_router",
    )(x, sc, sh, r, lo, up)


def _moe_kernel(cnt_ref, x_ref, sc_ref, sh_ref, g_ref, gate_ref, slot_ref, slot_t_ref, wgu_ref, wd_ref, lng_ref, lnb_ref,
                o_ref, u_scr, xg_scr, acc_scr):
    tm, rows = MOE_TM, MOE_ROWS
    max_pass = xg_scr.shape[0] // MOE_GROUP
    e = pl.program_id(2)
    f = pl.program_id(3)
    first_tile = (pl.program_id(0) * pl.num_programs(1) + pl.program_id(1)) * MOE_GROUP
    n_pass = [(cnt_ref[(first_tile + t) * N_EXPERTS + e] + rows - 1) // rows for t in range(MOE_GROUP)]

    @pl.when((e == 0) & (f == 0))
    def _():
        u_scr[...] = (x_ref[...] * (1.0 + sc_ref[...]) + sh_ref[...]).astype(BF16)
        o_ref[...] = jnp.zeros_like(o_ref)

    @pl.when(f == 0)
    def _():
        for t in range(MOE_GROUP):
            slot_row = slot_t_ref[t, pl.ds(e, 1), :]

            def gather(p, carry, t=t, slot_row=slot_row):
                want = p * rows + lax.broadcasted_iota(jnp.int32, (rows, 1), 0)
                pick = jnp.where(slot_row == want.astype(F32), 1.0, 0.0).astype(BF16)
                xg_scr[t * max_pass + p] = jnp.dot(pick, u_scr[t * tm:(t + 1) * tm, :], preferred_element_type=F32).astype(BF16)
                acc_scr[t * max_pass + p] = jnp.zeros((rows, x_ref.shape[-1]), F32)
                return carry

            lax.fori_loop(0, n_pass[t], gather, 0)

    for t in range(MOE_GROUP):
        def ffn(p, carry, t=t):
            acc_scr[t * max_pass + p] += _swiglu_chunk(xg_scr[t * max_pass + p], wgu_ref, wd_ref)
            return carry

        lax.fori_loop(0, n_pass[t], ffn, 0)

    @pl.when(f == pl.num_programs(3) - 1)
    def _():
        lane = lax.broadcasted_iota(jnp.int32, (tm, LANE), 1)
        for t in range(MOE_GROUP):
            tok = slice(t * tm, (t + 1) * tm)
            slot_col = jnp.sum(jnp.where(lane == e, slot_ref[tok, :], 0.0), axis=1, keepdims=True)
            gate_col = jnp.sum(jnp.where(lane == e, gate_ref[tok, :], 0.0), axis=1, keepdims=True)

            def scatter(p, carry, t=t, tok=tok, slot_col=slot_col, gate_col=gate_col):
                col = lax.broadcasted_iota(jnp.int32, (1, 2 * rows), 1)
                want = p * rows + jnp.where(col >= rows, col - rows, col)
                put = jnp.where(slot_col == want.astype(F32), 1.0, 0.0).astype(BF16)
                y = acc_scr[t * max_pass + p]
                hi = y.astype(BF16)
                lo = (y - hi.astype(F32)).astype(BF16)
                back = jnp.dot(put, jnp.concatenate([hi, lo], axis=0), preferred_element_type=F32)
                o_ref[tok, :] += gate_col * back
                return carry

            lax.fori_loop(0, n_pass[t], scatter, 0)

    @pl.when((e == pl.num_programs(2) - 1) & (f == pl.num_programs(3) - 1))
    def _():
        z = ALPHA * x_ref[...] + (1.0 + g_ref[...]) * o_ref[...]
        o_ref[...] = _layer_norm_rows(z, lng_ref[...], lnb_ref[...])


def _moe_ffn(x, sc, sh, g, routing, wg, wu, wd, ln_g, ln_b):
    gate, slot, slot_t, cnt = routing
    bsz, seq, d = x.shape
    n_e, _, ff = wg.shape
    tm, tf, rows, grp = MOE_TM, MOE_TF, MOE_ROWS, MOE_GROUP
    max_pass = -(-tm // rows)
    counts = cnt[:, :, 0, :n_e].reshape(-1)
    once = pl.Buffered(1)
    row = lambda n: pl.BlockSpec((None, grp * tm, n), lambda b, i, e, f, c: (b, i, 0), pipeline_mode=once)
    vec = pl.BlockSpec((None, 1, d), lambda b, i, e, f, c: (b, 0, 0))
    one = pl.BlockSpec((1, d), lambda b, i, e, f, c: (0, 0))
    grid_spec = pltpu.PrefetchScalarGridSpec(
        num_scalar_prefetch=1,
        grid=(bsz, seq // (grp * tm), n_e, ff // tf),
        in_specs=[row(d), vec, vec, vec, row(LANE), row(LANE),
                  pl.BlockSpec((None, grp, LANE, tm), lambda b, i, e, f, c: (b, i, 0, 0), pipeline_mode=once),
                  pl.BlockSpec((None, None, d, 2 * tf), lambda b, i, e, f, c: (e, f, 0, 0)),
                  pl.BlockSpec((None, tf, d), lambda b, i, e, f, c: (e, f, 0)),
                  one, one],
        out_specs=row(d),
        scratch_shapes=[pltpu.VMEM((grp * tm, d), BF16), pltpu.VMEM((grp * max_pass, rows, d), BF16),
                        pltpu.VMEM((grp * max_pass, rows, d), F32)],
    )
    return pl.pallas_call(
        _moe_kernel,
        grid_spec=grid_spec,
        out_shape=jax.ShapeDtypeStruct((bsz, seq, d), F32),
        compiler_params=_params("arbitrary", "arbitrary", "arbitrary", "arbitrary"),
        name="moe_swiglu_ln",
    )(counts, x, sc, sh, g, gate, slot, slot_t, _tile_gate_up(wg, wu, tf), wd, ln_g.reshape(1, d), ln_b.reshape(1, d))


ML_TT = 256
ML_SUB = 128


ML_NB = 1


def _mlstm_kernel(qk_ref, tail_ref, v_ref, og_ref, g_ref, cw_ref, cb_ref, gb_ref, gn_ref, y_ref,
                  ct_scr, n_scr, m_scr, q_scr, k_scr):
    @pl.when(pl.program_id(1) == 0)
    def _():
        ct_scr[...] = jnp.zeros_like(ct_scr)
        n_scr[...] = jnp.zeros_like(n_scr)
        m_scr[...] = jnp.zeros_like(m_scr)

    for nb in range(ML_NB):
        _mlstm_rows(qk_ref.at[nb], tail_ref.at[nb], v_ref.at[nb], og_ref.at[nb], g_ref.at[nb], cw_ref, cb_ref, gb_ref, gn_ref,
                    y_ref.at[nb], ct_scr.at[nb], n_scr.at[nb], m_scr.at[nb], q_scr.at[nb], k_scr.at[nb])


def _mlstm_rows(qk_ref, tail_ref, v_ref, og_ref, g_ref, cw_ref, cb_ref, gb_ref, gn_ref, y_ref,
                ct_scr, n_scr, m_scr, q_scr, k_scr):
    tt = ML_TT
    step = pl.program_id(1)

    x = qk_ref[...]
    tail = jnp.where(step == 0, 0.0, tail_ref[...])
    row8 = lax.broadcasted_iota(jnp.int32, (8, 1), 0)
    pre = x * cw_ref[CONV_W - 1:CONV_W, :] + cb_ref[...]
    for s in range(1, CONV_W):
        rolled = pltpu.roll(x, s, 0)
        head = jnp.where(row8 < s, pltpu.roll(tail, s, 0), rolled[0:8])
        pre = pre + jnp.concatenate([head, rolled[8:]], axis=0) * cw_ref[CONV_W - 1 - s:CONV_W - s, :]
    act = pre * jax.nn.sigmoid(pre)
    q_scr[...] = act[:, 0:ML_D].astype(BF16)
    k_scr[...] = (act[:, ML_D:2 * ML_D] * (DH_B ** -0.5)).astype(BF16)

    lane = lax.broadcasted_iota(jnp.int32, (1, LANE), 1)
    tok = lane % CHUNK
    jj = lax.broadcasted_iota(jnp.int32, (CHUNK, CHUNK), 0)
    ss = lax.broadcasted_iota(jnp.int32, (CHUNK, CHUNK), 1)
    nt = (((1,), (1,)), ((), ()))
    tn = (((0,), (0,)), ((), ()))
    ct_state = [ct_scr[h] for h in range(H_B)]
    n_state = [n_scr[h:h + 1, :] for h in range(H_B)]
    m_state = [m_scr[h:h + 1, 0:1] for h in range(H_B)]
    for sub in range(tt // ML_SUB):
        r0 = sub * ML_SUB
        gp = g_ref[r0:r0 + ML_SUB, :] + gb_ref[...]
        lsig = jnp.minimum(gp, 0.0) - jnp.log(1.0 + jnp.exp(-jnp.abs(gp)))
        col = jnp.where(lane < H_B, gp, lsig)
        rowl = col.T
        b = rowl[0:8]
        for sft in (1, 2, 4, 8, 16, 32):
            b = b + jnp.where(tok >= sft, pltpu.roll(b, sft, 1), 0.0)
        bcol = jnp.concatenate([b, jnp.zeros((LANE - 8, ML_SUB), F32)], axis=0).T
        for ci in range(ML_SUB // CHUNK):
            c0 = ci * CHUNK
            rows = slice(r0 + c0, r0 + c0 + CHUNK)
            for h in range(H_B):
                hs = slice(h * DH_B, (h + 1) * DH_B)
                b_col = bcol[c0:c0 + CHUNK, H_B + h:H_B + h + 1]
                ig_col = col[c0:c0 + CHUNK, h:h + 1]
                b_row = b[H_B + h:H_B + h + 1, c0:c0 + CHUNK]
                ig_row = rowl[h:h + 1, c0:c0 + CHUNK]
                qh, kh, vh = q_scr[rows, hs], k_scr[rows, hs], v_ref[rows, hs]
                log_d = jnp.where(jj >= ss, b_col - b_row + ig_row, NEG)
                m_loc = jnp.max(log_d, -1, keepdims=True)
                s_loc = lax.dot_general(qh, kh, nt, preferred_element_type=F32) * jnp.exp(log_d - m_loc)
                sv_loc = jnp.dot(s_loc.astype(BF16), vh, preferred_element_type=F32)
                rs_loc = jnp.sum(s_loc, -1, keepdims=True)
                b_last = b_row[:, CHUNK - 1:CHUNK]
                w_max = jnp.max(b_last - b_row + ig_row, -1, keepdims=True)
                w_loc = jnp.exp(b_last - b_col + ig_col - w_max)
                inc_c = lax.dot_general(kh, (w_loc * vh.astype(F32)).astype(BF16), tn, preferred_element_type=F32)
                inc_n = jnp.sum(w_loc * kh.astype(F32), axis=0, keepdims=True)
                m_old, ct, n_row = m_state[h], ct_state[h], n_state[h]
                log_inter = b_col + m_old
                m_out = jnp.maximum(log_inter, m_loc)
                w_inter = jnp.exp(log_inter - m_out)
                w_intra = jnp.exp(m_loc - m_out)
                num = w_inter * jnp.dot(qh, ct.astype(BF16), preferred_element_type=F32) + w_intra * sv_loc
                den = w_inter * jnp.sum(qh.astype(F32) * n_row, -1, keepdims=True) + w_intra * rs_loc
                hid = num / jnp.maximum(jnp.abs(den), jnp.exp(-m_out))
                m_new = jnp.maximum(b_last + m_old, w_max)
                decay = jnp.exp(b_last + m_old - m_new)
                grow = jnp.exp(w_max - m_new)
                ct_state[h] = decay * ct + grow * inc_c
                n_state[h] = decay * n_row + grow * inc_n
                m_state[h] = m_new
                ms = jnp.mean(hid * hid, -1, keepdims=True)
                y = hid * lax.rsqrt(ms + 1e-6) * gn_ref[:, hs] * jax.nn.sigmoid(og_ref[rows, hs])
                y_ref[rows, hs] = y.astype(y_ref.dtype)
    for h in range(H_B):
        ct_scr[h] = ct_state[h]
        n_scr[h:h + 1, :] = n_state[h]
        m_scr[h:h + 1, :] = jnp.broadcast_to(m_state[h], (1, LANE))


def _mlstm(mqk, mv, mo, mg, conv_w, conv_b, gate_b, g_b):
    bsz, seq, _ = mqk.shape
    tt = ML_TT
    gb = jnp.pad(gate_b.reshape(1, 2 * H_B), ((0, 0), (0, LANE - 2 * H_B)))
    nb = ML_NB
    row = lambda n: pl.BlockSpec((nb, tt, n), lambda b, i: (b, i, 0))
    full = lambda r, c: pl.BlockSpec((r, c), lambda b, i: (0, 0))
    tail = pl.BlockSpec((nb, 8, 2 * ML_D), lambda b, i: (b, jnp.maximum(i * (tt // 8) - 1, 0), 0))
    return pl.pallas_call(
        _mlstm_kernel,
        grid=(bsz // nb, seq // tt),
        in_specs=[row(2 * ML_D), tail, row(ML_D), row(ML_D), row(LANE),
                  full(CONV_W, 2 * ML_D), full(1, 2 * ML_D), full(1, LANE), full(1, ML_D)],
        out_specs=row(ML_D),
        out_shape=jax.ShapeDtypeStruct((bsz, seq, ML_D), BF16),
        scratch_shapes=[pltpu.VMEM((nb, H_B, DH_B, DH_B), F32), pltpu.VMEM((nb, 8, DH_B), F32), pltpu.VMEM((nb, 8, LANE), F32),
                        pltpu.VMEM((nb, tt, ML_D), BF16), pltpu.VMEM((nb, tt, ML_D), BF16)],
        compiler_params=_params("arbitrary", "arbitrary"),
        name="mlstm_scan",
    )(mqk, mqk, mv, mo, mg, conv_w, conv_b.reshape(1, 2 * ML_D), gb, g_b.reshape(1, ML_D))


def _mixers(dsa, mls, nsa, w_uv, conv_w, conv_b, gate_b, cmp_pos, cmp_w1, cmp_w2, grp_norm):
    g_a, g_b, g_c = jnp.split(grp_norm, [H_A * DH_V, H_A * DH_V + H_B * DH_B])
    qc, iq, iw, kc_a, ik, ct_a = dsa
    y_a = _dsa_attention(qc, iq, iw, kc_a, ct_a, ik, w_uv, g_a)
    y_b = _mlstm(*mls, conv_w, conv_b, gate_b, g_b)
    nq, ncmp, nks, nvs, nkw, nvw, ng = nsa
    kc, vc = _nsa_compress(ncmp, cmp_pos, cmp_w1, cmp_w2)
    y_c = _nsa_attention(nq, ng, kc, vc, nks, nvs, nkw, nvw, g_c)
    return y_a, y_b, y_c


def _pad_cols(w, n):
    return jnp.pad(w, ((0, 0), (0, n - w.shape[1])))


def kernel(x, c, positions, w_mod, b_mod, w_in, dsa_w_uk, dsa_w_uv, dsa_kv_norm, mlstm_conv_w, mlstm_conv_b, mlstm_gate_b, nsa_cmp_pos, nsa_cmp_w1, nsa_cmp_w2, grp_norm, w_out, ln_g, ln_b, ffn_w_gate, ffn_w_up, ffn_w_down, moe_router, moe_w_gate, moe_w_up, moe_w_down):
    bsz = x.shape[0]
    mod = _modulation(c, w_mod, b_mod).reshape(bsz, DEPTH, N_MOD, 1, D_MODEL)
    rope = _rope_table(positions)
    for l in range(DEPTH):
        sh1, sc1, g1, sh2, sc2, g2 = [mod[:, l, j] for j in range(N_MOD)]
        w = w_in[l].astype(BF16)
        wa = _dsa_weight_layout(w[:, :N_GROUP_A])
        wb = _mlstm_weight_layout(w[:, N_GROUP_A:N_GROUP_A + N_GROUP_B])
        wc, wvt = _nsa_weight_layout(w[:, N_GROUP_A + N_GROUP_B:])
        proj = _input_projection(x, sc1, sh1, wa, wb, wc, wvt, rope[0], rope[1], dsa_w_uk[l], dsa_kv_norm[l])
        dsa, mls, nsa = proj[:6], proj[6:10], proj[10:]
        ya, yb, yc = _mixers(dsa, mls, nsa, dsa_w_uv[l], mlstm_conv_w[l], mlstm_conv_b[l], mlstm_gate_b[l], nsa_cmp_pos[l], nsa_cmp_w1[l], nsa_cmp_w2[l], grp_norm[l])
        wo = w_out[l].astype(BF16)
        n_a, n_b = H_A * DH_V, H_B * DH_B
        x = _output_projection(ya, yb, yc, x, g1, _pad_head_rows(wo[:n_a], H_A, DH_V), wo[n_a:n_a + n_b], _pad_head_rows(wo[n_a + n_b:], H_C, DH_C), ln_g[l, 0], ln_b[l, 0])
        if l % 2 == 0:
            k = l // 2
            x = _dense_ffn(x, sc2, sh2, g2, ffn_w_gate[k].astype(BF16), ffn_w_up[k].astype(BF16), ffn_w_down[k].astype(BF16), ln_g[l, 1], ln_b[l, 1])
        else:
            k = l // 2
            routing = _moe_router(x, sc2, sh2, moe_router[k])
            x = _moe_ffn(x, sc2, sh2, g2, routing,moe_w_gate[k].astype(BF16), moe_w_up[k].astype(BF16), moe_w_down[k].astype(BF16), ln_g[l, 1], ln_b[l, 1])
    return x
```

```python
import functools

import numpy as np
import jax
import jax.numpy as jnp
from jax import lax
from jax.experimental import pallas as pl
from jax.experimental.pallas import tpu as pltpu

F32 = jnp.float32
BF16 = jnp.bfloat16

D_MODEL = 1024
DEPTH = 2
H_A, DH_NOPE, DH_ROPE, D_C, DH_V, H_I, D_I = 4, 64, 32, 128, 64, 4, 64
K_SEL_MAX = 256
ROPE_THETA = 10000.0
H_B, DH_B, CONV_W, CHUNK = 4, 128, 4, 64
H_C, DH_C, L_CMP, D_STRIDE, CMP_HIDDEN, L_SEL, N_TOP_MAX, WINDOW, Q_BLOCK = 4, 64, 32, 16, 128, 64, 16, 512, 128
D_FF = 2816
N_EXPERTS = 8
D_FF_EXPERT = 3584
N_MOD = 6
ALPHA = (2 * DEPTH) ** 0.25
SPLIT_SIZES = (H_A * DH_NOPE, H_A * DH_ROPE, D_C, DH_ROPE, H_I * D_I, D_I, H_I, H_B * DH_B, H_B * DH_B, H_B * DH_B, H_B, H_B, H_B * DH_B, H_C * DH_C, DH_C, DH_C, DH_C, DH_C, DH_C, DH_C, 3 * H_C)
N_GROUP_A = sum(SPLIT_SIZES[:7])
N_GROUP_B = sum(SPLIT_SIZES[7:13])
N_GROUP_C = sum(SPLIT_SIZES[13:])

LOG2E = 1.4426950408889634
LANE = 128
VMEM_LIMIT = 56 * 1024 * 1024


def _round_up(n, m):
    return (n + m - 1) // m * m


def _params(*sem):
    return pltpu.CompilerParams(dimension_semantics=sem, vmem_limit_bytes=VMEM_LIMIT)


FOLD_ROWS = 64


def _fold_rows(x, op):
    parts = [x[i:i + FOLD_ROWS] for i in range(0, x.shape[0], FOLD_ROWS)]
    while len(parts) > 1:
        parts = [op(parts[i], parts[i + 1]) if i + 1 < len(parts) else parts[i] for i in range(0, len(parts), 2)]
    return parts[0]


def _attend_block(s_buf, b_buf, v_t, m, l, acc_scr, n_head, tq):
    bias = b_buf[...]
    m_new, l_new, ps = [], [], []
    for h in range(n_head):
        hs = slice(h * tq, (h + 1) * tq)
        sh = s_buf[:, hs] + bias
        mh = jnp.maximum(m[:, hs], jnp.max(_fold_rows(sh, jnp.maximum), axis=0, keepdims=True))
        ph = jnp.exp2(sh - mh)
        l_new.append(jnp.exp2(m[:, hs] - mh) * l[:, hs] + jnp.sum(_fold_rows(ph, jnp.add), axis=0, keepdims=True))
        m_new.append(mh)
        ps.append(ph.astype(BF16))
    m_new = jnp.concatenate(m_new, axis=1)
    acc_scr[...] = jnp.exp2(m - m_new) * acc_scr[...] + jnp.dot(v_t, jnp.concatenate(ps, axis=1), preferred_element_type=F32)
    return m_new, jnp.concatenate(l_new, axis=1)


PLANE_KEYS = 256


def _bit_transpose32(x):
    w = [x[8 * i:8 * i + 8, :] for i in range(32)]
    j, m = 16, 0x0000FFFF
    while j:
        k = 0
        while k < 32:
            t = (w[k] ^ lax.shift_right_logical(w[k + j], j)) & m
            w[k] = w[k] ^ t
            w[k + j] = w[k + j] ^ jnp.left_shift(t, j)
            k = (k + j + 1) & ~j
        j >>= 1
        m ^= (m << j) & 0xFFFFFFFF
    return w


def _layer_norm_rows(z, g, b):
    mu = jnp.mean(z, -1, keepdims=True)
    zc = z - mu
    var = jnp.mean(zc * zc, -1, keepdims=True)
    return zc * lax.rsqrt(var + 1e-5) * g + b


def _mod_kernel(c_ref, w_ref, b_ref, o_ref):
    c = c_ref[...]
    a = c * jax.nn.sigmoid(c)
    o_ref[...] = jnp.dot(a, w_ref[...], preferred_element_type=F32, precision=lax.Precision.HIGHEST) + b_ref[...]


def _modulation(c, w_mod, b_mod):
    bsz, d = c.shape
    n = w_mod.shape[1]
    tn = 1024
    return pl.pallas_call(
        _mod_kernel,
        grid=(n // tn,),
        in_specs=[pl.BlockSpec((bsz, d), lambda j: (0, 0)),
                  pl.BlockSpec((d, tn), lambda j: (0, j)),
                  pl.BlockSpec((1, tn), lambda j: (0, j))],
        out_specs=pl.BlockSpec((bsz, tn), lambda j: (0, j)),
        out_shape=jax.ShapeDtypeStruct((bsz, n), F32),
        compiler_params=_params("arbitrary"),
        name="adaln_mod",
    )(c, w_mod, b_mod.reshape(1, n))


NSA_Q0 = 0
NSA_CMP0 = H_C * LANE
NSA_KS0 = NSA_CMP0 + LANE
NSA_KW0 = NSA_KS0 + LANE
NSA_G0 = NSA_KW0 + LANE
NSA_COLS = NSA_G0 + LANE


def _nsa_weight_layout(w):
    d = w.shape[0]
    nq, nkc, nvc, nks, nvs, nkw, nvw, ng = jnp.split(w, np.cumsum(SPLIT_SIZES[13:])[:-1].tolist(), axis=1)
    z = lambda n: jnp.zeros((d, n), w.dtype)
    half = LANE - DH_C
    cols = []
    for h in range(H_C):
        cols += [nq[:, h * DH_C:(h + 1) * DH_C], z(half)]
    cols += [nkc, nvc, nks, z(half), nkw, z(half), ng, z(LANE - 3 * H_C)]
    values_t = jnp.concatenate([nvs, z(half), nvw, z(half)], axis=1).T
    return jnp.concatenate(cols, axis=1), values_t


def _mlstm_weight_layout(w):
    mq, mk, mv, mi, mf, mo = jnp.split(w, np.cumsum(SPLIT_SIZES[7:13])[:-1].tolist(), axis=1)
    return jnp.concatenate([mq, mk, mv, mo, mi, mf, jnp.zeros((w.shape[0], LANE - 2 * H_B), w.dtype)], axis=1)


ML_D = H_B * DH_B


def _inproj_kernel(x_ref, sc_ref, sh_ref, wa_ref, wb_ref, wc_ref, wvt_ref, cos_ref, sin_ref, wuk_ref, kvn_ref,
                   qc_ref, iq_ref, iw_ref, kc_ref, ik_ref, ct_ref, mqk_ref, mv_ref, mo_ref, mg_ref,
                   nq_ref, ncmp_ref, nks_ref, nvs_ref, nkw_ref, nvw_ref, ng_ref):
    u = (x_ref[...] * (1.0 + sc_ref[...]) + sh_ref[...]).astype(BF16)
    _dsa_operands(jnp.dot(u, wa_ref[...], preferred_element_type=F32), cos_ref, sin_ref, wuk_ref, kvn_ref,
                  qc_ref, iq_ref, iw_ref, kc_ref, ik_ref, ct_ref)
    ob = jnp.dot(u, wb_ref[...], preferred_element_type=F32)
    mqk_ref[...] = ob[:, 0:2 * ML_D]
    mv_ref[...] = ob[:, 2 * ML_D:3 * ML_D].astype(BF16)
    mo_ref[...] = ob[:, 3 * ML_D:4 * ML_D]
    mg_ref[...] = ob[:, 4 * ML_D:4 * ML_D + LANE]
    oc = jnp.dot(u, wc_ref[...], preferred_element_type=F32)
    nq_ref[...] = (oc[:, NSA_Q0:NSA_Q0 + H_C * LANE] * (DH_C ** -0.5 * LOG2E)).astype(BF16)
    ncmp_ref[...] = oc[:, NSA_CMP0:NSA_CMP0 + LANE]
    nks_ref[...] = oc[:, NSA_KS0:NSA_KS0 + LANE].astype(BF16)
    nkw_ref[...] = oc[:, NSA_KW0:NSA_KW0 + LANE].astype(BF16)
    ng_ref[...] = oc[:, NSA_G0:NSA_G0 + LANE]
    vt = lax.dot_general(wvt_ref[...], u, (((1,), (1,)), ((), ())), preferred_element_type=F32)
    nvs_ref[...] = vt[0:LANE].astype(BF16)
    nvw_ref[...] = vt[LANE:2 * LANE].astype(BF16)


def _input_projection(x, sc, sh, wa, wb, wc, wvt, cos, sin, w_uk, kv_norm):
    bsz, seq, d = x.shape
    tm = 512
    na, nb, nc = wa.shape[1], wb.shape[1], wc.shape[1]
    wuk = jnp.pad(w_uk, ((0, 0), (0, LANE - DH_NOPE), (0, 0))).astype(BF16)
    once = pl.Buffered(1)
    row = lambda n: pl.BlockSpec((None, tm, n), lambda b, i: (b, i, 0))
    col = pl.BlockSpec((None, LANE, tm), lambda b, i: (b, 0, i))
    vec = pl.BlockSpec((None, 1, d), lambda b, i: (b, 0, 0))
    full = lambda r, c: pl.BlockSpec((r, c), lambda b, i: (0, 0), pipeline_mode=once)
    tok = lambda n, dt: (row(n), jax.ShapeDtypeStruct((bsz, seq, n), dt))
    feat = (col, jax.ShapeDtypeStruct((bsz, LANE, seq), BF16))
    outs = [tok(2 * H_A * LANE, BF16), tok(H_I * LANE, BF16), tok(LANE, F32), tok(2 * LANE, BF16), tok(LANE, BF16), feat,
            tok(2 * ML_D, F32), tok(ML_D, BF16), tok(ML_D, F32), tok(LANE, F32),
            tok(H_C * LANE, BF16), tok(LANE, F32), tok(LANE, BF16), feat, tok(LANE, BF16), feat, tok(LANE, F32)]
    return pl.pallas_call(
        _inproj_kernel,
        grid=(bsz, seq // tm),
        in_specs=[row(d), vec, vec, full(d, na), full(d, nb), full(d, nc), full(2 * LANE, d), row(LANE), row(LANE),
                  pl.BlockSpec((H_A, LANE, D_C), lambda b, i: (0, 0, 0), pipeline_mode=once), full(1, D_C)],
        out_specs=[spec for spec, _ in outs],
        out_shape=[shape for _, shape in outs],
        compiler_params=_params("arbitrary", "arbitrary"),
        name="input_projection",
    )(x, sc, sh, wa, wb, wc, wvt, cos, sin, wuk, kv_norm.reshape(1, D_C))


DSA_QN0 = 0
DSA_QR0 = H_A * LANE
DSA_CKV0 = DSA_QR0 + LANE
DSA_IQ0 = DSA_CKV0 + LANE
DSA_G0 = DSA_IQ0 + H_I * LANE
DSA_COLS = DSA_G0 + LANE
DSA_KR_LANE = D_I
DSA_IW_LANE = D_I + DH_ROPE
DSA_TQ = 128
DSA_KC = 512
INT_MIN = -2 ** 31
KEY_NEG_INF = int(np.int32(np.float32(-np.inf).view(np.int32)) ^ np.int32(0x7FFFFFFF))


def _dsa_weight_layout(w):
    d = w.shape[0]
    qn, qr, ckv, kr, iq, ik, iw = jnp.split(w, np.cumsum(SPLIT_SIZES[:7])[:-1].tolist(), axis=1)
    z = lambda n: jnp.zeros((d, n), w.dtype)
    cols = []
    for h in range(H_A):
        cols += [qn[:, h * DH_NOPE:(h + 1) * DH_NOPE], z(LANE - DH_NOPE)]
    cols += [qr, ckv]
    for h in range(H_I):
        cols += [iq[:, h * D_I:(h + 1) * D_I], z(LANE - D_I)]
    cols += [ik, kr, iw, z(LANE - D_I - DH_ROPE - H_I)]
    return jnp.concatenate(cols, axis=1)


def _rope_table_kernel(pos_ref, freq_ref, cos_ref, sin_ref):
    ang = pos_ref[...].astype(F32) * freq_ref[...]
    lane = lax.broadcasted_iota(jnp.int32, (1, LANE), 1)
    first = (lane % DH_ROPE) < DH_ROPE // 2
    cos_ref[...] = jnp.cos(ang)
    sin_ref[...] = jnp.where(first, -jnp.sin(ang), jnp.sin(ang))


def _rope_table(positions):
    bsz, seq = positions.shape
    tm = 512
    inv_freq = ROPE_THETA ** (-jnp.arange(0, DH_ROPE, 2, dtype=F32) / DH_ROPE)
    freq = jnp.tile(inv_freq, LANE // (DH_ROPE // 2)).reshape(1, LANE)
    out = pl.BlockSpec((None, tm, LANE), lambda b, i: (b, i, 0))
    return pl.pallas_call(
        _rope_table_kernel,
        grid=(bsz, seq // tm),
        in_specs=[pl.BlockSpec((None, tm, 1), lambda b, i: (b, i, 0)), pl.BlockSpec((1, LANE), lambda b, i: (0, 0))],
        out_specs=[out, out],
        out_shape=[jax.ShapeDtypeStruct((bsz, seq, LANE), F32)] * 2,
        compiler_params=_params("arbitrary", "arbitrary"),
        name="rope_table",
    )(positions.reshape(bsz, seq, 1), freq)


def _dsa_operands(pa_ref, cos_ref, sin_ref, wuk_ref, kvn_ref, qc_ref, iq_ref, iw_ref, kc_ref, ik_ref, ct_ref):
    scale = (DH_NOPE + DH_ROPE) ** -0.5 * LOG2E
    cos, sin = cos_ref[...], sin_ref[...]
    lane = lax.broadcasted_iota(jnp.int32, (1, LANE), 1)
    first = (lane % DH_ROPE) < DH_ROPE // 2
    rope_lanes = (lane >= DSA_KR_LANE) & (lane < DSA_KR_LANE + DH_ROPE)

    def rope(v):
        partner = jnp.where(first, pltpu.roll(v, LANE - DH_ROPE // 2, 1), pltpu.roll(v, DH_ROPE // 2, 1))
        return v * cos + partner * sin

    g = pa_ref[:, DSA_G0:DSA_G0 + LANE]
    ckv = pa_ref[:, DSA_CKV0:DSA_CKV0 + LANE]
    ckv_n = ckv * lax.rsqrt(jnp.mean(ckv * ckv, -1, keepdims=True) + 1e-6) * kvn_ref[...]
    kc_ref[:, 0:LANE] = ckv_n.astype(BF16)
    ct_ref[...] = ckv_n.T.astype(BF16)
    kc_ref[:, LANE:2 * LANE] = jnp.where(rope_lanes, rope(g), 0.0).astype(BF16)
    ik_ref[...] = jnp.where(lane < D_I, g, 0.0).astype(BF16)
    iw_ref[...] = g
    iq_ref[...] = pa_ref[:, DSA_IQ0:DSA_IQ0 + H_I * LANE].astype(BF16)
    qr = rope(pa_ref[:, DSA_QR0:DSA_QR0 + LANE]) * scale
    for h in range(H_A):
        qn = pa_ref[:, DSA_QN0 + h * LANE:DSA_QN0 + (h + 1) * LANE].astype(BF16)
        q_abs = jnp.dot(qn, wuk_ref[h], preferred_element_type=F32) * scale
        shift = (DSA_KR_LANE - DH_ROPE * h) % LANE
        qr_h = pltpu.roll(qr, shift, 1) if shift else qr
        qc_ref[:, 2 * h * LANE:(2 * h + 1) * LANE] = q_abs.astype(BF16)
        qc_ref[:, (2 * h + 1) * LANE:(2 * h + 2) * LANE] = jnp.where(rope_lanes, qr_h, 0.0).astype(BF16)


def _dsa_kernel(k_sel, qc_ref, iq_ref, iw_ref, kc_ref, ct_ref, ik_ref, tri_ref, wuv_ref, gn_ref, o_ref,
                key_scr, planes_scr, acc_scr, s0_scr, s1_scr, b0_scr, b1_scr):
    tq, kcs = DSA_TQ, DSA_KC
    t0 = pl.program_id(1) * tq
    n_chunk = (t0 + tq + kcs - 1) // kcs
    nt = (((1,), (1,)), ((), ()))
    qpos = t0 + lax.broadcasted_iota(jnp.int32, (1, tq), 1)
    iw_t = iw_ref[...].T

    def score_chunk(c, carry):
        k0 = pl.multiple_of(c * kcs, kcs)
        ikc = ik_ref[pl.ds(k0, kcs), :]
        sc = jnp.zeros((kcs, tq), F32)
        for h in range(H_I):
            lg = lax.dot_general(ikc, iq_ref[:, h * LANE:(h + 1) * LANE], nt, preferred_element_type=F32)
            sc = sc + jnp.maximum(lg, 0.0) * iw_t[DSA_IW_LANE + h:DSA_IW_LANE + h + 1, :]
        sc = jnp.where(sc == 0.0, 0.0, sc)
        kpos = k0 + lax.broadcasted_iota(jnp.int32, (kcs, 1), 0)
        sc = jnp.where(kpos <= qpos, sc, -jnp.inf)
        bits = pltpu.bitcast(sc, jnp.int32)
        key = jnp.where(bits < 0, bits ^ 0x7FFFFFFF, bits)
        key_scr[pl.ds(k0, kcs), :] = key
        ukey = key ^ INT_MIN
        for blk in range(kcs // PLANE_KEYS):
            words = _bit_transpose32(ukey[blk * PLANE_KEYS:(blk + 1) * PLANE_KEYS])
            row0 = pl.multiple_of((c * (kcs // PLANE_KEYS) + blk) * 8, 8)
            for r in range(32):
                planes_scr[r, pl.ds(row0, 8), :] = words[r]
        return carry

    lax.fori_loop(0, n_chunk, score_chunk, 0)

    n_words = planes_scr.shape[1]
    word_row = lax.broadcasted_iota(jnp.int32, (n_words, tq), 0)
    alive0 = jnp.where(word_row < n_chunk * (kcs // 32), jnp.int32(-1), jnp.int32(0))
    k_int = int(k_sel)

    def bit_pass(i, carry):
        thr_u, above, alive = carry
        ones = alive & planes_scr[i]
        seen_ones = above + jnp.sum(lax.population_count(ones), axis=0, keepdims=True)
        take = seen_ones >= k_int
        alive = jnp.where(take, ones, alive & ~planes_scr[i])
        above = jnp.where(take, above, seen_ones)
        thr_u = jnp.where(take, thr_u | jnp.left_shift(jnp.int32(1), 31 - i), thr_u)
        return thr_u, above, alive

    thr_u, above, _ = lax.fori_loop(0, 32, bit_pass, (jnp.zeros((1, tq), jnp.int32), jnp.zeros((1, tq), jnp.int32), alive0))
    thr = thr_u ^ INT_MIN
    room = (k_int - above).astype(F32)

    qall = jnp.concatenate([qc_ref[:, 2 * h * LANE:(2 * h + 2) * LANE] for h in range(H_A)], axis=0)

    room = jnp.where(thr > KEY_NEG_INF, room, 0.0)
    acc_scr[...] = jnp.zeros_like(acc_scr)

    half = kcs // 2

    def prepare(k0, seen, s_buf, b_buf):
        key = key_scr[pl.ds(k0, half), :]
        tie = jnp.where(key == thr, 1.0, 0.0)
        prefix = jnp.dot(tri_ref[...], tie.astype(BF16), preferred_element_type=F32)
        tie_bias = jnp.where(seen + prefix <= room, jnp.where(key == thr, 0.0, NEG), NEG)
        b_buf[...] = jnp.where(key > thr, 0.0, tie_bias)
        s_buf[...] = lax.dot_general(kc_ref[pl.ds(k0, half), :], qall, nt, preferred_element_type=F32)
        return seen + jnp.sum(_fold_rows(tie, jnp.add), axis=0, keepdims=True)

    def attn_chunk(c, carry):
        m, l, seen = carry
        k0 = pl.multiple_of(c * kcs, kcs)
        k1 = pl.multiple_of(k0 + half, half)
        seen = prepare(k1, seen, s1_scr, b1_scr)
        m, l = _attend_block(s0_scr, b0_scr, ct_ref[:, pl.ds(k0, half)], m, l, acc_scr, H_A, tq)
        k2 = pl.multiple_of(jnp.minimum(c + 1, n_chunk - 1) * kcs, kcs)
        seen = prepare(k2, seen, s0_scr, b0_scr)
        m, l = _attend_block(s1_scr, b1_scr, ct_ref[:, pl.ds(k1, half)], m, l, acc_scr, H_A, tq)
        return m, l, seen

    seen0 = prepare(0, jnp.zeros((1, tq), F32), s0_scr, b0_scr)
    init = (jnp.full((1, H_A * tq), M_FLOOR, F32), jnp.zeros((1, H_A * tq), F32), seen0)
    _, l, _ = lax.fori_loop(0, n_chunk, attn_chunk, init)
    o_lat = (acc_scr[...] / jnp.maximum(l, 1e-30)).astype(BF16)
    for h in range(H_A):
        o = lax.dot_general(o_lat[:, h * tq:(h + 1) * tq], wuv_ref[h], (((0,), (0,)), ((), ())),
                            preferred_element_type=F32)
        ms = jnp.sum(o * o, -1, keepdims=True) * (1.0 / DH_V)
        o_ref[:, h * LANE:(h + 1) * LANE] = (o * lax.rsqrt(ms + 1e-6) * gn_ref[h:h + 1, :]).astype(o_ref.dtype)


def _dsa_attention(qc, iq, iw, kc, ct, ik, w_uv, g_a):
    bsz, seq, _ = qc.shape
    k_sel = float(min(K_SEL_MAX, seq // 4))
    tq, kcs = DSA_TQ, DSA_KC
    half = kcs // 2
    tri = jnp.asarray(np.tril(np.ones((half, half), np.float32)), BF16)
    wuv = jnp.pad(w_uv, ((0, 0), (0, 0), (0, LANE - DH_V))).astype(BF16)
    gn = jnp.pad(g_a.reshape(H_A, DH_V), ((0, 0), (0, LANE - DH_V)))
    row = lambda n: pl.BlockSpec((None, tq, n), lambda b, i: (b, i, 0))
    per_b = lambda n: pl.BlockSpec((None, seq, n), lambda b, i: (b, 0, 0))
    return pl.pallas_call(
        functools.partial(_dsa_kernel, k_sel),
        grid=(bsz, seq // tq),
        in_specs=[row(2 * H_A * LANE), row(H_I * LANE), row(LANE), per_b(2 * LANE),
                  pl.BlockSpec((None, D_C, seq), lambda b, i: (b, 0, 0)), per_b(LANE),
                  pl.BlockSpec((half, half), lambda b, i: (0, 0)), pl.BlockSpec((H_A, D_C, LANE), lambda b, i: (0, 0, 0)),
                  pl.BlockSpec((H_A, LANE), lambda b, i: (0, 0))],
        out_specs=row(H_A * LANE),
        out_shape=jax.ShapeDtypeStruct((bsz, seq, H_A * LANE), BF16),
        scratch_shapes=[pltpu.VMEM((seq, tq), jnp.int32), pltpu.VMEM((32, seq // 32, tq), jnp.int32),
                        pltpu.VMEM((D_C, H_A * tq), F32),
                        pltpu.VMEM((half, H_A * tq), F32), pltpu.VMEM((half, H_A * tq), F32),
                        pltpu.VMEM((half, tq), F32), pltpu.VMEM((half, tq), F32)],
        compiler_params=_params("arbitrary", "arbitrary"),
        name="dsa_attention",
    )(qc, iq, iw, kc, ct, ik, tri, wuv, gn)


def _nsa_compress_kernel(a_ref, pos_ref, w1t_ref, w1b_ref, w2k_ref, w2v_ref, kc_ref, vc_ref):
    a = a_ref[...]
    top = jnp.dot((a + pos_ref[0:1, :]).astype(BF16), w1t_ref[...], preferred_element_type=F32)
    bot = jnp.dot((a + pos_ref[1:2, :]).astype(BF16), w1b_ref[...], preferred_element_type=F32)
    n = a.shape[0]
    pre = top + jnp.concatenate([bot[1:], bot[:1]], axis=0)
    h = (pre * jax.nn.sigmoid(pre)).astype(BF16)
    hid = w2k_ref.shape[0]
    kc_ref[...] = jnp.dot(h[:, :hid], w2k_ref[...], preferred_element_type=F32).astype(BF16)
    vc_ref[...] = lax.dot_general(w2v_ref[...], h[:, hid:], (((1,), (1,)), ((), ())), preferred_element_type=F32).astype(BF16)


def _nsa_compress(ncmp, cmp_pos, cmp_w1, cmp_w2):
    bsz, seq, _ = ncmp.shape
    n_grp = seq // D_STRIDE
    per = L_CMP // D_STRIDE
    width = D_STRIDE * LANE
    a = ncmp.reshape(bsz, n_grp, width)
    w1 = cmp_w1.reshape(2, per, D_STRIDE, DH_C, CMP_HIDDEN)
    zer = jnp.zeros((D_STRIDE, DH_C, CMP_HIDDEN), cmp_w1.dtype)

    def expand(p):
        wk = jnp.concatenate([w1[0, p], zer], axis=1)
        wv = jnp.concatenate([zer, w1[1, p]], axis=1)
        return jnp.concatenate([wk, wv], axis=2).reshape(width, 2 * CMP_HIDDEN).astype(BF16)

    pos = cmp_pos.reshape(2, per, D_STRIDE, DH_C)
    pos = jnp.concatenate([pos[0], pos[1]], axis=-1).reshape(per, width)
    pad_out = ((0, 0), (0, LANE - DH_C))
    w2k = jnp.pad(cmp_w2[0], pad_out).astype(BF16)
    w2v = jnp.pad(cmp_w2[1], pad_out).astype(BF16).T
    full = lambda r, c: pl.BlockSpec((r, c), lambda b: (0, 0))
    return pl.pallas_call(
        _nsa_compress_kernel,
        grid=(bsz,),
        in_specs=[pl.BlockSpec((None, n_grp, width), lambda b: (b, 0, 0)), full(per, width),
                  full(width, 2 * CMP_HIDDEN), full(width, 2 * CMP_HIDDEN), full(CMP_HIDDEN, LANE), full(LANE, CMP_HIDDEN)],
        out_specs=[pl.BlockSpec((None, n_grp, LANE), lambda b: (b, 0, 0)), pl.BlockSpec((None, LANE, n_grp), lambda b: (b, 0, 0))],
        out_shape=[jax.ShapeDtypeStruct((bsz, n_grp, LANE), BF16), jax.ShapeDtypeStruct((bsz, LANE, n_grp), BF16)],
        compiler_params=_params("arbitrary"),
        name="nsa_compress",
    )(a, pos, expand(0), expand(1), w2k, w2v)


NSA_TQ = 128
NSA_KC = 512
NEG = -1e30
M_FLOOR = -1e20


def _softmax_cols(s, bias, n_head, tq):
    out = []
    for h in range(n_head):
        sh = s[:, h * tq:(h + 1) * tq] + bias
        m = jnp.maximum(jnp.max(_fold_rows(sh, jnp.maximum), axis=0, keepdims=True), M_FLOOR)
        e = jnp.exp2(sh - m)
        den = jnp.sum(_fold_rows(e, jnp.add), axis=0, keepdims=True)
        out.append(e * (1.0 / jnp.maximum(den, 1e-30)))
    return out


def _nsa_kernel(q_ref, g_ref, kc_ref, vct_ref, ks_ref, vst_ref, kw_ref, vwt_ref, cover_ref, expand_ref, gn_ref, o_ref,
                acc_scr, s0_scr, s1_scr, b0_scr, b1_scr):
    tq, kc_sz = NSA_TQ, NSA_KC
    t0 = pl.program_id(1) * tq
    nt = (((1,), (1,)), ((), ()))
    q = q_ref[...]
    qa = jnp.concatenate([q[:, h * LANE:(h + 1) * LANE] for h in range(H_C)], axis=0)
    qpos = t0 + lax.broadcasted_iota(jnp.int32, (1, tq), 1)

    n_grp = kc_ref.shape[0]
    s_c = lax.dot_general(kc_ref[...], qa, nt, preferred_element_type=F32)
    n_idx = lax.broadcasted_iota(jnp.int32, (n_grp, 1), 0)
    visible = jnp.where(n_idx < n_grp - 1, n_idx * D_STRIDE + (L_CMP - 1), 2 ** 30)
    p_c = _softmax_cols(s_c, jnp.where(visible <= qpos, 0.0, NEG), H_C, tq)
    o_cmp = jnp.dot(vct_ref[...], jnp.concatenate([p.astype(BF16) for p in p_c], axis=1), preferred_element_type=F32)

    p_sum = p_c[0]
    for h in range(1, H_C):
        p_sum = p_sum + p_c[h]
    hi = p_sum.astype(BF16)
    lo = (p_sum - hi.astype(F32)).astype(BF16)
    cov = cover_ref[...]
    imp_t = jnp.dot(cov, hi, preferred_element_type=F32) + jnp.dot(cov, lo, preferred_element_type=F32)
    n_sel = cov.shape[0]
    jrow = lax.broadcasted_iota(jnp.int32, (n_sel, tq), 0)
    cur = (t0 + lax.broadcasted_iota(jnp.int32, (n_sel, tq), 1)) // L_SEL
    adm = jrow <= cur
    forced = (jrow == 0) | (jrow == cur) | (jrow == cur - 1)
    val = jnp.where(adm & forced, jnp.inf, jnp.where(adm, imp_t, -jnp.inf))
    rank = jnp.zeros((n_sel, tq), F32)
    for jp in range(n_sel):
        r = val[jp:jp + 1, :]
        rank = rank + jnp.where(r == val, jnp.where(jrow > jp, 1.0, 0.0), jnp.where(r > val, 1.0, 0.0))
    sel_t = jnp.where(rank < min(N_TOP_MAX, n_sel), jnp.where(val > -jnp.inf, 1.0, 0.0), 0.0)
    if n_sel < LANE:
        sel_t = jnp.concatenate([sel_t, jnp.zeros((LANE - n_sel, tq), F32)], axis=0)
    sel_t = sel_t.astype(BF16)

    acc_scr[...] = jnp.zeros_like(acc_scr)

    half = kc_sz // 2
    n_chunk = (t0 + tq + kc_sz - 1) // kc_sz

    def prepare(k0, s_buf, b_buf):
        s_buf[...] = lax.dot_general(ks_ref[pl.ds(k0, half), :], qa, nt, preferred_element_type=F32)
        picked = jnp.dot(expand_ref[pl.ds(k0, half), :], sel_t, preferred_element_type=F32)
        kpos = k0 + lax.broadcasted_iota(jnp.int32, (half, 1), 0)
        b_buf[...] = jnp.where(kpos <= qpos, jnp.where(picked > 0.5, 0.0, NEG), NEG)

    def chunk(c, carry):
        m, l = carry
        k0 = pl.multiple_of(c * kc_sz, kc_sz)
        k1 = pl.multiple_of(k0 + half, half)
        prepare(k1, s1_scr, b1_scr)
        m, l = _attend_block(s0_scr, b0_scr, vst_ref[:, pl.ds(k0, half)], m, l, acc_scr, H_C, tq)
        prepare(pl.multiple_of(jnp.minimum(c + 1, n_chunk - 1) * kc_sz, kc_sz), s0_scr, b0_scr)
        return _attend_block(s1_scr, b1_scr, vst_ref[:, pl.ds(k1, half)], m, l, acc_scr, H_C, tq)

    prepare(0, s0_scr, b0_scr)
    init = (jnp.full((1, H_C * tq), M_FLOOR, F32), jnp.zeros((1, H_C * tq), F32))
    _, l_s = lax.fori_loop(0, n_chunk, chunk, init)
    o_slc = acc_scr[...] * (1.0 / jnp.maximum(l_s, 1e-30))

    span = WINDOW + tq
    w0 = pl.multiple_of(jnp.maximum(t0 - WINDOW, 0), tq)
    s_w = lax.dot_general(kw_ref[pl.ds(w0, span), :], qa, nt, preferred_element_type=F32)
    kpos_w = w0 + lax.broadcasted_iota(jnp.int32, (span, 1), 0)
    wbias = jnp.where(kpos_w <= qpos, jnp.where(qpos - kpos_w < WINDOW, 0.0, NEG), NEG)
    p_w = _softmax_cols(s_w, wbias, H_C, tq)
    o_win = jnp.dot(vwt_ref[:, pl.ds(w0, span)], jnp.concatenate([p.astype(BF16) for p in p_w], axis=1), preferred_element_type=F32)

    gates_t = jax.nn.sigmoid(g_ref[...]).T
    for h in range(H_C):
        hs = slice(h * tq, (h + 1) * tq)
        o_t = (gates_t[h:h + 1] * o_cmp[:, hs] + gates_t[H_C + h:H_C + h + 1] * o_slc[:, hs]
               + gates_t[2 * H_C + h:2 * H_C + h + 1] * o_win[:, hs])
        o = o_t.T
        ms = jnp.sum(o * o, -1, keepdims=True) * (1.0 / DH_C)
        o_ref[:, h * LANE:(h + 1) * LANE] = (o * lax.rsqrt(ms + 1e-6) * gn_ref[h:h + 1, :]).astype(o_ref.dtype)


def _nsa_attention(nq, ng, kc, vc_t, nks, nvs_t, nkw, nvw_t, g_c):
    bsz, seq, _ = nq.shape
    n_grp = kc.shape[1]
    n_sel = seq // L_SEL
    grp_start = np.arange(n_grp) * D_STRIDE
    sel_start = np.arange(n_sel) * L_SEL
    cover_t = ((grp_start[None, :] < sel_start[:, None] + L_SEL) & (grp_start[None, :] + L_CMP > sel_start[:, None]))
    cover_t = jnp.asarray(cover_t.astype(np.float32), BF16)
    expand = (np.arange(seq)[:, None] // L_SEL == np.arange(LANE)[None, :]).astype(np.float32)
    expand = jnp.asarray(expand, BF16)
    gn = jnp.pad(g_c.reshape(H_C, DH_C), ((0, 0), (0, LANE - DH_C)))
    tq = NSA_TQ
    row = lambda n: pl.BlockSpec((None, tq, n), lambda b, i: (b, i, 0))
    per_b = lambda r, c: pl.BlockSpec((None, r, c), lambda b, i: (b, 0, 0))
    full = lambda r, c: pl.BlockSpec((r, c), lambda b, i: (0, 0))
    return pl.pallas_call(
        _nsa_kernel,
        grid=(bsz, seq // tq),
        in_specs=[row(H_C * LANE), row(LANE), per_b(n_grp, LANE), per_b(LANE, n_grp), per_b(seq, LANE), per_b(LANE, seq),
                  per_b(seq, LANE), per_b(LANE, seq), full(n_sel, n_grp), full(seq, LANE), full(H_C, LANE)],
        out_specs=row(H_C * LANE),
        out_shape=jax.ShapeDtypeStruct((bsz, seq, H_C * LANE), BF16),
        scratch_shapes=[pltpu.VMEM((LANE, H_C * tq), F32),
                        pltpu.VMEM((NSA_KC // 2, H_C * tq), F32), pltpu.VMEM((NSA_KC // 2, H_C * tq), F32),
                        pltpu.VMEM((NSA_KC // 2, tq), F32), pltpu.VMEM((NSA_KC // 2, tq), F32)],
        compiler_params=_params("arbitrary", "arbitrary"),
        name="nsa_attention",
    )(nq, ng, kc, vc_t, nks, nvs_t, nkw, nvw_t, cover_t, expand, gn)


def _outproj_kernel(ya_ref, yb_ref, yc_ref, x_ref, g_ref, wa_ref, wb_ref, wc_ref, lng_ref, lnb_ref, o_ref):
    y = jnp.dot(ya_ref[...], wa_ref[...], preferred_element_type=F32)
    y += jnp.dot(yb_ref[...], wb_ref[...], preferred_element_type=F32)
    y += jnp.dot(yc_ref[...], wc_ref[...], preferred_element_type=F32)
    z = ALPHA * x_ref[...] + (1.0 + g_ref[...]) * y
    o_ref[...] = _layer_norm_rows(z, lng_ref[...], lnb_ref[...])


def _pad_head_rows(w, n_head, dh):
    d = w.shape[1]
    return jnp.pad(w.reshape(n_head, dh, d), ((0, 0), (0, LANE - dh), (0, 0))).reshape(n_head * LANE, d)


def _output_projection(ya, yb, yc, x, g, wa, wb, wc, ln_g, ln_b):
    bsz, seq, d = x.shape
    tm = 512
    na, nb, nc = ya.shape[-1], yb.shape[-1], yc.shape[-1]
    row = lambda n: pl.BlockSpec((None, tm, n), lambda b, i: (b, i, 0))
    vec = pl.BlockSpec((None, 1, d), lambda b, i: (b, 0, 0))
    full = lambda r, c: pl.BlockSpec((r, c), lambda b, i: (0, 0))
    return pl.pallas_call(
        _outproj_kernel,
        grid=(bsz, seq // tm),
        in_specs=[row(na), row(nb), row(nc), row(d), vec, full(na, d), full(nb, d), full(nc, d), full(1, d), full(1, d)],
        out_specs=row(d),
        out_shape=jax.ShapeDtypeStruct((bsz, seq, d), F32),
        compiler_params=_params("arbitrary", "arbitrary"),
        name="output_projection_ln",
    )(ya, yb, yc, x, g, wa, wb, wc, ln_g.reshape(1, d), ln_b.reshape(1, d))


def _tile_gate_up(wg, wu, tf):
    *lead, d, ff = wg.shape
    split = lambda w: jnp.moveaxis(w.reshape(*lead, d, ff // tf, tf), -2, -3)
    return jnp.concatenate([split(wg), split(wu)], axis=-1)


def _swiglu_chunk(u, wgu_ref, wd_ref):
    tf = wd_ref.shape[0]
    ab = jnp.dot(u, wgu_ref[...], preferred_element_type=F32)
    a, b = ab[:, :tf], ab[:, tf:]
    return jnp.dot((a * jax.nn.sigmoid(a) * b).astype(BF16), wd_ref[...], preferred_element_type=F32)


def _ffn_kernel(x_ref, sc_ref, sh_ref, g_ref, wgu_ref, wd_ref, lng_ref, lnb_ref, o_ref, u_scr, acc_scr):
    f = pl.program_id(2)

    @pl.when(f == 0)
    def _():
        u_scr[...] = (x_ref[...] * (1.0 + sc_ref[...]) + sh_ref[...]).astype(BF16)
        acc_scr[...] = jnp.zeros_like(acc_scr)

    acc_scr[...] += _swiglu_chunk(u_scr[...], wgu_ref, wd_ref)

    @pl.when(f == pl.num_programs(2) - 1)
    def _():
        z = ALPHA * x_ref[...] + (1.0 + g_ref[...]) * acc_scr[...]
        o_ref[...] = _layer_norm_rows(z, lng_ref[...], lnb_ref[...])


def _dense_ffn(x, sc, sh, g, wg, wu, wd, ln_g, ln_b):
    bsz, seq, d = x.shape
    ff = wg.shape[1]
    tm, tf = 1024, 256
    row = pl.BlockSpec((None, tm, d), lambda b, i, f: (b, i, 0))
    vec = pl.BlockSpec((None, 1, d), lambda b, i, f: (b, 0, 0))
    one = pl.BlockSpec((1, d), lambda b, i, f: (0, 0))
    return pl.pallas_call(
        _ffn_kernel,
        grid=(bsz, seq // tm, ff // tf),
        in_specs=[row, vec, vec, vec,
                  pl.BlockSpec((None, d, 2 * tf), lambda b, i, f: (f, 0, 0)),
                  pl.BlockSpec((tf, d), lambda b, i, f: (f, 0)),
                  one, one],
        out_specs=row,
        out_shape=jax.ShapeDtypeStruct((bsz, seq, d), F32),
        scratch_shapes=[pltpu.VMEM((tm, d), BF16), pltpu.VMEM((tm, d), F32)],
        compiler_params=_params("arbitrary", "arbitrary", "arbitrary"),
        name="dense_swiglu_ln",
    )(x, sc, sh, g, _tile_gate_up(wg, wu, tf), wd, ln_g.reshape(1, d), ln_b.reshape(1, d))


def _router_kernel(x_ref, sc_ref, sh_ref, r_ref, lo_ref, up_ref, gate_ref, slot_ref, slot_t_ref, cnt_ref):
    u = x_ref[...] * (1.0 + sc_ref[...]) + sh_ref[...]
    logits = jnp.dot(u, r_ref[...], preferred_element_type=F32, precision=lax.Precision.HIGHEST)
    lane = lax.broadcasted_iota(jnp.int32, logits.shape, 1)
    neg = -jnp.inf
    l1 = jnp.where(lane < N_EXPERTS, logits, neg)
    m1 = jnp.max(l1, -1, keepdims=True)
    i1 = jnp.min(jnp.where(l1 == m1, lane, LANE), -1, keepdims=True)
    l2 = jnp.where(lane == i1, neg, l1)
    m2 = jnp.max(l2, -1, keepdims=True)
    i2 = jnp.min(jnp.where(l2 == m2, lane, LANE), -1, keepdims=True)
    e2 = jnp.exp(m2 - m1)
    w1 = 1.0 / (1.0 + e2)
    w2 = e2 / (1.0 + e2)
    gate_ref[...] = jnp.where(lane == i1, w1, jnp.where(lane == i2, w2, 0.0))
    routed = jnp.where((lane == i1) | (lane == i2), 1.0, 0.0)
    before = jnp.dot(lo_ref[...], routed.astype(BF16), preferred_element_type=F32)
    slot_ref[...] = jnp.where(routed > 0.5, before, -1.0)
    routed_t = routed.T
    before_t = jnp.dot(routed_t.astype(BF16), up_ref[...], preferred_element_type=F32)
    slot_t_ref[...] = jnp.where(routed_t > 0.5, before_t, -1.0)
    cnt_ref[...] = jnp.broadcast_to(jnp.sum(routed, axis=0, keepdims=True), cnt_ref.shape).astype(jnp.int32)


MOE_TM = 1024
MOE_ROWS = 288
MOE_TF = 896
MOE_GROUP = 2


def _moe_router(x, sc, sh, router):
    bsz, seq, d = x.shape
    tm = MOE_TM
    r = jnp.pad(router, ((0, 0), (0, LANE - router.shape[1])))
    upper = np.triu(np.ones((tm, tm), np.float32), 1)
    up, lo = jnp.asarray(upper, BF16), jnp.asarray(upper.T, BF16)
    row = lambda n: pl.BlockSpec((None, tm, n), lambda b, i: (b, i, 0))
    vec = pl.BlockSpec((None, 1, d), lambda b, i: (b, 0, 0))
    full = lambda a, c: pl.BlockSpec((a, c), lambda b, i: (0, 0))
    n_tile = seq // tm
    return pl.pallas_call(
        _router_kernel,
        grid=(bsz, n_tile),
        in_specs=[row(d), vec, vec, full(d, LANE), full(tm, tm), full(tm, tm)],
        out_specs=[row(LANE), row(LANE), pl.BlockSpec((None, None, LANE, tm), lambda b, i: (b, i, 0, 0)),
                   pl.BlockSpec((None, None, 8, LANE), lambda b, i: (b, i, 0, 0))],
        out_shape=[jax.ShapeDtypeStruct((bsz, seq, LANE), F32), jax.ShapeDtypeStruct((bsz, seq, LANE), F32),
                   jax.ShapeDtypeStruct((bsz, n_tile, LANE, tm), F32), jax.ShapeDtypeStruct((bsz, n_tile, 8, LANE), jnp.int32)],
        compiler_params=_params("arbitrary", "arbitrary"),
        name="moe_router",
    )(x, sc, sh, r, lo, up)


def _moe_kernel(cnt_ref, x_ref, sc_ref, sh_ref, g_ref, gate_ref, slot_ref, slot_t_ref, wgu_ref, wd_ref, lng_ref, lnb_ref,
                o_ref, u_scr, xg_scr, acc_scr):
    tm, rows = MOE_TM, MOE_ROWS
    max_pass = xg_scr.shape[0] // MOE_GROUP
    e = pl.program_id(2)
    f = pl.program_id(3)
    first_tile = (pl.program_id(0) * pl.num_programs(1) + pl.program_id(1)) * MOE_GROUP
    n_pass = [(cnt_ref[(first_tile + t) * N_EXPERTS + e] + rows - 1) // rows for t in range(MOE_GROUP)]

    @pl.when((e == 0) & (f == 0))
    def _():
        u_scr[...] = (x_ref[...] * (1.0 + sc_ref[...]) + sh_ref[...]).astype(BF16)
        o_ref[...] = jnp.zeros_like(o_ref)

    @pl.when(f == 0)
    def _():
        for t in range(MOE_GROUP):
            slot_row = slot_t_ref[t, pl.ds(e, 1), :]

            def gather(p, carry, t=t, slot_row=slot_row):
                want = p * rows + lax.broadcasted_iota(jnp.int32, (rows, 1), 0)
                pick = jnp.where(slot_row == want.astype(F32), 1.0, 0.0).astype(BF16)
                xg_scr[t * max_pass + p] = jnp.dot(pick, u_scr[t * tm:(t + 1) * tm, :], preferred_element_type=F32).astype(BF16)
                acc_scr[t * max_pass + p] = jnp.zeros((rows, x_ref.shape[-1]), F32)
                return carry

            lax.fori_loop(0, n_pass[t], gather, 0)

    for t in range(MOE_GROUP):
        def ffn(p, carry, t=t):
            acc_scr[t * max_pass + p] += _swiglu_chunk(xg_scr[t * max_pass + p], wgu_ref, wd_ref)
            return carry

        lax.fori_loop(0, n_pass[t], ffn, 0)

    @pl.when(f == pl.num_programs(3) - 1)
    def _():
        lane = lax.broadcasted_iota(jnp.int32, (tm, LANE), 1)
        for t in range(MOE_GROUP):
            tok = slice(t * tm, (t + 1) * tm)
            slot_col = jnp.sum(jnp.where(lane == e, slot_ref[tok, :], 0.0), axis=1, keepdims=True)
            gate_col = jnp.sum(jnp.where(lane == e, gate_ref[tok, :], 0.0), axis=1, keepdims=True)

            def scatter(p, carry, t=t, tok=tok, slot_col=slot_col, gate_col=gate_col):
                col = lax.broadcasted_iota(jnp.int32, (1, 2 * rows), 1)
                want = p * rows + jnp.where(col >= rows, col - rows, col)
                put = jnp.where(slot_col == want.astype(F32), 1.0, 0.0).astype(BF16)
                y = acc_scr[t * max_pass + p]
                hi = y.astype(BF16)
                lo = (y - hi.astype(F32)).astype(BF16)
                back = jnp.dot(put, jnp.concatenate([hi, lo], axis=0), preferred_element_type=F32)
                o_ref[tok, :] += gate_col * back
                return carry

            lax.fori_loop(0, n_pass[t], scatter, 0)

    @pl.when((e == pl.num_programs(2) - 1) & (f == pl.num_programs(3) - 1))
    def _():
        z = ALPHA * x_ref[...] + (1.0 + g_ref[...]) * o_ref[...]
        o_ref[...] = _layer_norm_rows(z, lng_ref[...], lnb_ref[...])


def _moe_ffn(x, sc, sh, g, routing, wg, wu, wd, ln_g, ln_b):
    gate, slot, slot_t, cnt = routing
    bsz, seq, d = x.shape
    n_e, _, ff = wg.shape
    tm, tf, rows, grp = MOE_TM, MOE_TF, MOE_ROWS, MOE_GROUP
    max_pass = -(-tm // rows)
    counts = cnt[:, :, 0, :n_e].reshape(-1)
    once = pl.Buffered(1)
    row = lambda n: pl.BlockSpec((None, grp * tm, n), lambda b, i, e, f, c: (b, i, 0), pipeline_mode=once)
    vec = pl.BlockSpec((None, 1, d), lambda b, i, e, f, c: (b, 0, 0))
    one = pl.BlockSpec((1, d), lambda b, i, e, f, c: (0, 0))
    grid_spec = pltpu.PrefetchScalarGridSpec(
        num_scalar_prefetch=1,
        grid=(bsz, seq // (grp * tm), n_e, ff // tf),
        in_specs=[row(d), vec, vec, vec, row(LANE), row(LANE),
                  pl.BlockSpec((None, grp, LANE, tm), lambda b, i, e, f, c: (b, i, 0, 0), pipeline_mode=once),
                  pl.BlockSpec((None, None, d, 2 * tf), lambda b, i, e, f, c: (e, f, 0, 0)),
                  pl.BlockSpec((None, tf, d), lambda b, i, e, f, c: (e, f, 0)),
                  one, one],
        out_specs=row(d),
        scratch_shapes=[pltpu.VMEM((grp * tm, d), BF16), pltpu.VMEM((grp * max_pass, rows, d), BF16),
                        pltpu.VMEM((grp * max_pass, rows, d), F32)],
    )
    return pl.pallas_call(
        _moe_kernel,
        grid_spec=grid_spec,
        out_shape=jax.ShapeDtypeStruct((bsz, seq, d), F32),
        compiler_params=_params("arbitrary", "arbitrary", "arbitrary", "arbitrary"),
        name="moe_swiglu_ln",
    )(counts, x, sc, sh, g, gate, slot, slot_t, _tile_gate_up(wg, wu, tf), wd, ln_g.reshape(1, d), ln_b.reshape(1, d))


ML_TT = 256
ML_SUB = 128
ML_CHUNK = 128


ML_NB = 1


def _mlstm_kernel(qk_ref, tail_ref, v_ref, og_ref, g_ref, cw_ref, cb_ref, gb_ref, gn_ref, y_ref,
                  ct_scr, n_scr, m_scr, q_scr, k_scr):
    @pl.when(pl.program_id(1) == 0)
    def _():
        ct_scr[...] = jnp.zeros_like(ct_scr)
        n_scr[...] = jnp.zeros_like(n_scr)
        m_scr[...] = jnp.zeros_like(m_scr)

    for nb in range(ML_NB):
        _mlstm_rows(qk_ref.at[nb], tail_ref.at[nb], v_ref.at[nb], og_ref.at[nb], g_ref.at[nb], cw_ref, cb_ref, gb_ref, gn_ref,
                    y_ref.at[nb], ct_scr.at[nb], n_scr.at[nb], m_scr.at[nb], q_scr.at[nb], k_scr.at[nb])


def _mlstm_rows(qk_ref, tail_ref, v_ref, og_ref, g_ref, cw_ref, cb_ref, gb_ref, gn_ref, y_ref,
                ct_scr, n_scr, m_scr, q_scr, k_scr):
    tt = ML_TT
    step = pl.program_id(1)

    x = qk_ref[...]
    tail = jnp.where(step == 0, 0.0, tail_ref[...])
    row8 = lax.broadcasted_iota(jnp.int32, (8, 1), 0)
    pre = x * cw_ref[CONV_W - 1:CONV_W, :] + cb_ref[...]
    for s in range(1, CONV_W):
        rolled = pltpu.roll(x, s, 0)
        head = jnp.where(row8 < s, pltpu.roll(tail, s, 0), rolled[0:8])
        pre = pre + jnp.concatenate([head, rolled[8:]], axis=0) * cw_ref[CONV_W - 1 - s:CONV_W - s, :]
    act = pre * jax.nn.sigmoid(pre)
    q_scr[...] = act[:, 0:ML_D].astype(BF16)
    k_scr[...] = (act[:, ML_D:2 * ML_D] * (DH_B ** -0.5)).astype(BF16)

    lane = lax.broadcasted_iota(jnp.int32, (1, LANE), 1)
    tok = lane % ML_CHUNK
    jj = lax.broadcasted_iota(jnp.int32, (ML_CHUNK, ML_CHUNK), 0)
    ss = lax.broadcasted_iota(jnp.int32, (ML_CHUNK, ML_CHUNK), 1)
    nt = (((1,), (1,)), ((), ()))
    tn = (((0,), (0,)), ((), ()))
    ct_state = [ct_scr[h] for h in range(H_B)]
    n_state = [n_scr[h:h + 1, :] for h in range(H_B)]
    m_state = [m_scr[h:h + 1, 0:1] for h in range(H_B)]
    for sub in range(tt // ML_SUB):
        r0 = sub * ML_SUB
        gp = g_ref[r0:r0 + ML_SUB, :] + gb_ref[...]
        lsig = jnp.minimum(gp, 0.0) - jnp.log(1.0 + jnp.exp(-jnp.abs(gp)))
        col = jnp.where(lane < H_B, gp, lsig)
        rowl = col.T
        b = rowl[0:8]
        sft = 1
        while sft < ML_CHUNK:
            b = b + jnp.where(tok >= sft, pltpu.roll(b, sft, 1), 0.0)
            sft *= 2
        bcol = jnp.concatenate([b, jnp.zeros((LANE - 8, ML_SUB), F32)], axis=0).T
        for ci in range(ML_SUB // ML_CHUNK):
            c0 = ci * ML_CHUNK
            rows = slice(r0 + c0, r0 + c0 + ML_CHUNK)
            for h in range(H_B):
                hs = slice(h * DH_B, (h + 1) * DH_B)
                b_col = bcol[c0:c0 + ML_CHUNK, H_B + h:H_B + h + 1]
                ig_col = col[c0:c0 + ML_CHUNK, h:h + 1]
                b_row = b[H_B + h:H_B + h + 1, c0:c0 + ML_CHUNK]
                ig_row = rowl[h:h + 1, c0:c0 + ML_CHUNK]
                qh, kh, vh = q_scr[rows, hs], k_scr[rows, hs], v_ref[rows, hs]
                log_d = jnp.where(jj >= ss, b_col - b_row + ig_row, NEG)
                m_loc = jnp.max(log_d, -1, keepdims=True)
                s_loc = lax.dot_general(qh, kh, nt, preferred_element_type=F32) * jnp.exp(log_d - m_loc)
                sv_loc = jnp.dot(s_loc.astype(BF16), vh, preferred_element_type=F32)
                rs_loc = jnp.sum(s_loc, -1, keepdims=True)
                b_last = b_row[:, ML_CHUNK - 1:ML_CHUNK]
                w_max = jnp.max(b_last - b_row + ig_row, -1, keepdims=True)
                w_loc = jnp.exp(b_last - b_col + ig_col - w_max)
                inc_c = lax.dot_general(kh, (w_loc * vh.astype(F32)).astype(BF16), tn, preferred_element_type=F32)
                inc_n = jnp.sum(w_loc * kh.astype(F32), axis=0, keepdims=True)
                m_old, ct, n_row = m_state[h], ct_state[h], n_state[h]
                log_inter = b_col + m_old
                m_out = jnp.maximum(log_inter, m_loc)
                w_inter = jnp.exp(log_inter - m_out)
                w_intra = jnp.exp(m_loc - m_out)
                num = w_inter * jnp.dot(qh, ct.astype(BF16), preferred_element_type=F32) + w_intra * sv_loc
                den = w_inter * jnp.sum(qh.astype(F32) * n_row, -1, keepdims=True) + w_intra * rs_loc
                hid = num / jnp.maximum(jnp.abs(den), jnp.exp(-m_out))
                m_new = jnp.maximum(b_last + m_old, w_max)
                decay = jnp.exp(b_last + m_old - m_new)
                grow = jnp.exp(w_max - m_new)
                ct_state[h] = decay * ct + grow * inc_c
                n_state[h] = decay * n_row + grow * inc_n
                m_state[h] = m_new
                ms = jnp.mean(hid * hid, -1, keepdims=True)
                y = hid * lax.rsqrt(ms + 1e-6) * gn_ref[:, hs] * jax.nn.sigmoid(og_ref[rows, hs])
                y_ref[rows, hs] = y.astype(y_ref.dtype)
    for h in range(H_B):
        ct_scr[h] = ct_state[h]
        n_scr[h:h + 1, :] = n_state[h]
        m_scr[h:h + 1, :] = jnp.broadcast_to(m_state[h], (1, LANE))


def _mlstm(mqk, mv, mo, mg, conv_w, conv_b, gate_b, g_b):
    bsz, seq, _ = mqk.shape
    tt = ML_TT
    gb = jnp.pad(gate_b.reshape(1, 2 * H_B), ((0, 0), (0, LANE - 2 * H_B)))
    nb = ML_NB
    row = lambda n: pl.BlockSpec((nb, tt, n), lambda b, i: (b, i, 0))
    full = lambda r, c: pl.BlockSpec((r, c), lambda b, i: (0, 0))
    tail = pl.BlockSpec((nb, 8, 2 * ML_D), lambda b, i: (b, jnp.maximum(i * (tt // 8) - 1, 0), 0))
    return pl.pallas_call(
        _mlstm_kernel,
        grid=(bsz // nb, seq // tt),
        in_specs=[row(2 * ML_D), tail, row(ML_D), row(ML_D), row(LANE),
                  full(CONV_W, 2 * ML_D), full(1, 2 * ML_D), full(1, LANE), full(1, ML_D)],
        out_specs=row(ML_D),
        out_shape=jax.ShapeDtypeStruct((bsz, seq, ML_D), BF16),
        scratch_shapes=[pltpu.VMEM((nb, H_B, DH_B, DH_B), F32), pltpu.VMEM((nb, 8, DH_B), F32), pltpu.VMEM((nb, 8, LANE), F32),
                        pltpu.VMEM((nb, tt, ML_D), BF16), pltpu.VMEM((nb, tt, ML_D), BF16)],
        compiler_params=_params("arbitrary", "arbitrary"),
        name="mlstm_scan",
    )(mqk, mqk, mv, mo, mg, conv_w, conv_b.reshape(1, 2 * ML_D), gb, g_b.reshape(1, ML_D))


def _mixers(dsa, mls, nsa, w_uv, conv_w, conv_b, gate_b, cmp_pos, cmp_w1, cmp_w2, grp_norm):
    g_a, g_b, g_c = jnp.split(grp_norm, [H_A * DH_V, H_A * DH_V + H_B * DH_B])
    qc, iq, iw, kc_a, ik, ct_a = dsa
    y_a = _dsa_attention(qc, iq, iw, kc_a, ct_a, ik, w_uv, g_a)
    y_b = _mlstm(*mls, conv_w, conv_b, gate_b, g_b)
    nq, ncmp, nks, nvs, nkw, nvw, ng = nsa
    kc, vc = _nsa_compress(ncmp, cmp_pos, cmp_w1, cmp_w2)
    y_c = _nsa_attention(nq, ng, kc, vc, nks, nvs, nkw, nvw, g_c)
    return y_a, y_b, y_c


def _pad_cols(w, n):
    return jnp.pad(w, ((0, 0), (0, n - w.shape[1])))


def kernel(x, c, positions, w_mod, b_mod, w_in, dsa_w_uk, dsa_w_uv, dsa_kv_norm, mlstm_conv_w, mlstm_conv_b, mlstm_gate_b, nsa_cmp_pos, nsa_cmp_w1, nsa_cmp_w2, grp_norm, w_out, ln_g, ln_b, ffn_w_gate, ffn_w_up, ffn_w_down, moe_router, moe_w_gate, moe_w_up, moe_w_down):
    bsz = x.shape[0]
    mod = _modulation(c, w_mod, b_mod).reshape(bsz, DEPTH, N_MOD, 1, D_MODEL)
    rope = _rope_table(positions)
    for l in range(DEPTH):
        sh1, sc1, g1, sh2, sc2, g2 = [mod[:, l, j] for j in range(N_MOD)]
        w = w_in[l].astype(BF16)
        wa = _dsa_weight_layout(w[:, :N_GROUP_A])
        wb = _mlstm_weight_layout(w[:, N_GROUP_A:N_GROUP_A + N_GROUP_B])
        wc, wvt = _nsa_weight_layout(w[:, N_GROUP_A + N_GROUP_B:])
        proj = _input_projection(x, sc1, sh1, wa, wb, wc, wvt, rope[0], rope[1], dsa_w_uk[l], dsa_kv_norm[l])
        dsa, mls, nsa = proj[:6], proj[6:10], proj[10:]
        ya, yb, yc = _mixers(dsa, mls, nsa, dsa_w_uv[l], mlstm_conv_w[l], mlstm_conv_b[l], mlstm_gate_b[l], nsa_cmp_pos[l], nsa_cmp_w1[l], nsa_cmp_w2[l], grp_norm[l])
        wo = w_out[l].astype(BF16)
        n_a, n_b = H_A * DH_V, H_B * DH_B
        x = _output_projection(ya, yb, yc, x, g1, _pad_head_rows(wo[:n_a], H_A, DH_V), wo[n_a:n_a + n_b], _pad_head_rows(wo[n_a + n_b:], H_C, DH_C), ln_g[l, 0], ln_b[l, 0])
        if l % 2 == 0:
            k = l // 2
            x = _dense_ffn(x, sc2, sh2, g2, ffn_w_gate[k].astype(BF16), ffn_w_up[k].astype(BF16), ffn_w_down[k].astype(BF16), ln_g[l, 1], ln_b[l, 1])
        else:
            k = l // 2
            routing = _moe_router(x, sc2, sh2, moe_router[k])
            x = _moe_ffn(x, sc2, sh2, g2, routing,moe_w_gate[k].astype(BF16), moe_w_up[k].astype(BF16), moe_w_down[k].astype(BF16), ln_g[l, 1], ln_b[l, 1])
    return x
```

```python
import functools

import numpy as np
import jax
import jax.numpy as jnp
from jax import lax
from jax.experimental import pallas as pl
from jax.experimental.pallas import tpu as pltpu

F32 = jnp.float32
BF16 = jnp.bfloat16

D_MODEL = 1024
DEPTH = 2
H_A, DH_NOPE, DH_ROPE, D_C, DH_V, H_I, D_I = 4, 64, 32, 128, 64, 4, 64
K_SEL_MAX = 256
ROPE_THETA = 10000.0
H_B, DH_B, CONV_W, CHUNK = 4, 128, 4, 64
H_C, DH_C, L_CMP, D_STRIDE, CMP_HIDDEN, L_SEL, N_TOP_MAX, WINDOW, Q_BLOCK = 4, 64, 32, 16, 128, 64, 16, 512, 128
D_FF = 2816
N_EXPERTS = 8
D_FF_EXPERT = 3584
N_MOD = 6
ALPHA = (2 * DEPTH) ** 0.25
SPLIT_SIZES = (H_A * DH_NOPE, H_A * DH_ROPE, D_C, DH_ROPE, H_I * D_I, D_I, H_I, H_B * DH_B, H_B * DH_B, H_B * DH_B, H_B, H_B, H_B * DH_B, H_C * DH_C, DH_C, DH_C, DH_C, DH_C, DH_C, DH_C, 3 * H_C)
N_GROUP_A = sum(SPLIT_SIZES[:7])
N_GROUP_B = sum(SPLIT_SIZES[7:13])
N_GROUP_C = sum(SPLIT_SIZES[13:])

LOG2E = 1.4426950408889634
LANE = 128
VMEM_LIMIT = 56 * 1024 * 1024


def _round_up(n, m):
    return (n + m - 1) // m * m


def _params(*sem):
    return pltpu.CompilerParams(dimension_semantics=sem, vmem_limit_bytes=VMEM_LIMIT)


FOLD_ROWS = 64


def _fold_rows(x, op):
    parts = [x[i:i + FOLD_ROWS] for i in range(0, x.shape[0], FOLD_ROWS)]
    while len(parts) > 1:
        parts = [op(parts[i], parts[i + 1]) if i + 1 < len(parts) else parts[i] for i in range(0, len(parts), 2)]
    return parts[0]


def _attend_block(s_buf, b_buf, v_t, m, l, acc_scr, n_head, tq):
    bias = b_buf[...]
    m_new, l_new, ps = [], [], []
    for h in range(n_head):
        hs = slice(h * tq, (h + 1) * tq)
        sh = s_buf[:, hs] + bias
        mh = jnp.maximum(m[:, hs], jnp.max(_fold_rows(sh, jnp.maximum), axis=0, keepdims=True))
        ph = jnp.exp2(sh - mh)
        l_new.append(jnp.exp2(m[:, hs] - mh) * l[:, hs] + jnp.sum(_fold_rows(ph, jnp.add), axis=0, keepdims=True))
        m_new.append(mh)
        ps.append(ph.astype(BF16))
    m_new = jnp.concatenate(m_new, axis=1)
    acc_scr[...] = jnp.exp2(m - m_new) * acc_scr[...] + jnp.dot(v_t, jnp.concatenate(ps, axis=1), preferred_element_type=F32)
    return m_new, jnp.concatenate(l_new, axis=1)


PLANE_KEYS = 256


def _bit_transpose32(x):
    w = [x[8 * i:8 * i + 8, :] for i in range(32)]
    j, m = 16, 0x0000FFFF
    while j:
        k = 0
        while k < 32:
            t = (w[k] ^ lax.shift_right_logical(w[k + j], j)) & m
            w[k] = w[k] ^ t
            w[k + j] = w[k + j] ^ jnp.left_shift(t, j)
            k = (k + j + 1) & ~j
        j >>= 1
        m ^= (m << j) & 0xFFFFFFFF
    return w


def _layer_norm_rows(z, g, b):
    mu = jnp.mean(z, -1, keepdims=True)
    zc = z - mu
    var = jnp.mean(zc * zc, -1, keepdims=True)
    return zc * lax.rsqrt(var + 1e-5) * g + b


def _mod_kernel(c_ref, w_ref, b_ref, o_ref):
    c = c_ref[...]
    a = c * jax.nn.sigmoid(c)
    o_ref[...] = jnp.dot(a, w_ref[...], preferred_element_type=F32, precision=lax.Precision.HIGHEST) + b_ref[...]


def _modulation(c, w_mod, b_mod):
    bsz, d = c.shape
    n = w_mod.shape[1]
    tn = 1024
    return pl.pallas_call(
        _mod_kernel,
        grid=(n // tn,),
        in_specs=[pl.BlockSpec((bsz, d), lambda j: (0, 0)),
                  pl.BlockSpec((d, tn), lambda j: (0, j)),
                  pl.BlockSpec((1, tn), lambda j: (0, j))],
        out_specs=pl.BlockSpec((bsz, tn), lambda j: (0, j)),
        out_shape=jax.ShapeDtypeStruct((bsz, n), F32),
        compiler_params=_params("arbitrary"),
        name="adaln_mod",
    )(c, w_mod, b_mod.reshape(1, n))


NSA_Q0 = 0
NSA_CMP0 = H_C * LANE
NSA_KS0 = NSA_CMP0 + LANE
NSA_KW0 = NSA_KS0 + LANE
NSA_G0 = NSA_KW0 + LANE
NSA_COLS = NSA_G0 + LANE


def _nsa_weight_layout(w):
    d = w.shape[0]
    nq, nkc, nvc, nks, nvs, nkw, nvw, ng = jnp.split(w, np.cumsum(SPLIT_SIZES[13:])[:-1].tolist(), axis=1)
    z = lambda n: jnp.zeros((d, n), w.dtype)
    half = LANE - DH_C
    cols = []
    for h in range(H_C):
        cols += [nq[:, h * DH_C:(h + 1) * DH_C], z(half)]
    cols += [nkc, nvc, nks, z(half), nkw, z(half), ng, z(LANE - 3 * H_C)]
    values_t = jnp.concatenate([nvs, z(half), nvw, z(half)], axis=1).T
    return jnp.concatenate(cols, axis=1), values_t


def _mlstm_weight_layout(w):
    mq, mk, mv, mi, mf, mo = jnp.split(w, np.cumsum(SPLIT_SIZES[7:13])[:-1].tolist(), axis=1)
    return jnp.concatenate([mq, mk, mv, mo, mi, mf, jnp.zeros((w.shape[0], LANE - 2 * H_B), w.dtype)], axis=1)


ML_D = H_B * DH_B


def _inproj_kernel(x_ref, sc_ref, sh_ref, wa_ref, wb_ref, wc_ref, wvt_ref, cos_ref, sin_ref, wuk_ref, kvn_ref,
                   qc_ref, iq_ref, iw_ref, kc_ref, ik_ref, ct_ref, mqk_ref, mv_ref, mo_ref, mg_ref,
                   nq_ref, ncmp_ref, nks_ref, nvs_ref, nkw_ref, nvw_ref, ng_ref):
    u = (x_ref[...] * (1.0 + sc_ref[...]) + sh_ref[...]).astype(BF16)
    _dsa_operands(jnp.dot(u, wa_ref[...], preferred_element_type=F32), cos_ref, sin_ref, wuk_ref, kvn_ref,
                  qc_ref, iq_ref, iw_ref, kc_ref, ik_ref, ct_ref)
    ob = jnp.dot(u, wb_ref[...], preferred_element_type=F32)
    mqk_ref[...] = ob[:, 0:2 * ML_D]
    mv_ref[...] = ob[:, 2 * ML_D:3 * ML_D].astype(BF16)
    mo_ref[...] = ob[:, 3 * ML_D:4 * ML_D]
    mg_ref[...] = ob[:, 4 * ML_D:4 * ML_D + LANE]
    oc = jnp.dot(u, wc_ref[...], preferred_element_type=F32)
    nq_ref[...] = (oc[:, NSA_Q0:NSA_Q0 + H_C * LANE] * (DH_C ** -0.5 * LOG2E)).astype(BF16)
    ncmp_ref[...] = oc[:, NSA_CMP0:NSA_CMP0 + LANE]
    nks_ref[...] = oc[:, NSA_KS0:NSA_KS0 + LANE].astype(BF16)
    nkw_ref[...] = oc[:, NSA_KW0:NSA_KW0 + LANE].astype(BF16)
    ng_ref[...] = oc[:, NSA_G0:NSA_G0 + LANE]
    vt = lax.dot_general(wvt_ref[...], u, (((1,), (1,)), ((), ())), preferred_element_type=F32)
    nvs_ref[...] = vt[0:LANE].astype(BF16)
    nvw_ref[...] = vt[LANE:2 * LANE].astype(BF16)


def _input_projection(x, sc, sh, wa, wb, wc, wvt, cos, sin, w_uk, kv_norm):
    bsz, seq, d = x.shape
    tm = 512
    na, nb, nc = wa.shape[1], wb.shape[1], wc.shape[1]
    wuk = jnp.pad(w_uk, ((0, 0), (0, LANE - DH_NOPE), (0, 0))).astype(BF16)
    once = pl.Buffered(1)
    row = lambda n: pl.BlockSpec((None, tm, n), lambda b, i: (b, i, 0))
    col = pl.BlockSpec((None, LANE, tm), lambda b, i: (b, 0, i))
    vec = pl.BlockSpec((None, 1, d), lambda b, i: (b, 0, 0))
    full = lambda r, c: pl.BlockSpec((r, c), lambda b, i: (0, 0), pipeline_mode=once)
    tok = lambda n, dt: (row(n), jax.ShapeDtypeStruct((bsz, seq, n), dt))
    feat = (col, jax.ShapeDtypeStruct((bsz, LANE, seq), BF16))
    outs = [tok(2 * H_A * LANE, BF16), tok(H_I * LANE, BF16), tok(LANE, F32), tok(2 * LANE, BF16), tok(LANE, BF16), feat,
            tok(2 * ML_D, F32), tok(ML_D, BF16), tok(ML_D, F32), tok(LANE, F32),
            tok(H_C * LANE, BF16), tok(LANE, F32), tok(LANE, BF16), feat, tok(LANE, BF16), feat, tok(LANE, F32)]
    return pl.pallas_call(
        _inproj_kernel,
        grid=(bsz, seq // tm),
        in_specs=[row(d), vec, vec, full(d, na), full(d, nb), full(d, nc), full(2 * LANE, d), row(LANE), row(LANE),
                  pl.BlockSpec((H_A, LANE, D_C), lambda b, i: (0, 0, 0), pipeline_mode=once), full(1, D_C)],
        out_specs=[spec for spec, _ in outs],
        out_shape=[shape for _, shape in outs],
        compiler_params=_params("arbitrary", "arbitrary"),
        name="input_projection",
    )(x, sc, sh, wa, wb, wc, wvt, cos, sin, wuk, kv_norm.reshape(1, D_C))


DSA_QN0 = 0
DSA_QR0 = H_A * LANE
DSA_CKV0 = DSA_QR0 + LANE
DSA_IQ0 = DSA_CKV0 + LANE
DSA_G0 = DSA_IQ0 + H_I * LANE
DSA_COLS = DSA_G0 + LANE
DSA_KR_LANE = D_I
DSA_IW_LANE = D_I + DH_ROPE
DSA_TQ = 128
DSA_KC = 512
INT_MIN = -2 ** 31
KEY_NEG_INF = int(np.int32(np.float32(-np.inf).view(np.int32)) ^ np.int32(0x7FFFFFFF))


def _dsa_weight_layout(w):
    d = w.shape[0]
    qn, qr, ckv, kr, iq, ik, iw = jnp.split(w, np.cumsum(SPLIT_SIZES[:7])[:-1].tolist(), axis=1)
    z = lambda n: jnp.zeros((d, n), w.dtype)
    cols = []
    for h in range(H_A):
        cols += [qn[:, h * DH_NOPE:(h + 1) * DH_NOPE], z(LANE - DH_NOPE)]
    cols += [qr, ckv]
    for h in range(H_I):
        cols += [iq[:, h * D_I:(h + 1) * D_I], z(LANE - D_I)]
    cols += [ik, kr, iw, z(LANE - D_I - DH_ROPE - H_I)]
    return jnp.concatenate(cols, axis=1)


def _rope_table_kernel(pos_ref, freq_ref, cos_ref, sin_ref):
    ang = pos_ref[...].astype(F32) * freq_ref[...]
    lane = lax.broadcasted_iota(jnp.int32, (1, LANE), 1)
    first = (lane % DH_ROPE) < DH_ROPE // 2
    cos_ref[...] = jnp.cos(ang)
    sin_ref[...] = jnp.where(first, -jnp.sin(ang), jnp.sin(ang))


def _rope_table(positions):
    bsz, seq = positions.shape
    tm = 512
    inv_freq = ROPE_THETA ** (-jnp.arange(0, DH_ROPE, 2, dtype=F32) / DH_ROPE)
    freq = jnp.tile(inv_freq, LANE // (DH_ROPE // 2)).reshape(1, LANE)
    out = pl.BlockSpec((None, tm, LANE), lambda b, i: (b, i, 0))
    return pl.pallas_call(
        _rope_table_kernel,
        grid=(bsz, seq // tm),
        in_specs=[pl.BlockSpec((None, tm, 1), lambda b, i: (b, i, 0)), pl.BlockSpec((1, LANE), lambda b, i: (0, 0))],
        out_specs=[out, out],
        out_shape=[jax.ShapeDtypeStruct((bsz, seq, LANE), F32)] * 2,
        compiler_params=_params("arbitrary", "arbitrary"),
        name="rope_table",
    )(positions.reshape(bsz, seq, 1), freq)


def _dsa_operands(pa_ref, cos_ref, sin_ref, wuk_ref, kvn_ref, qc_ref, iq_ref, iw_ref, kc_ref, ik_ref, ct_ref):
    scale = (DH_NOPE + DH_ROPE) ** -0.5 * LOG2E
    cos, sin = cos_ref[...], sin_ref[...]
    lane = lax.broadcasted_iota(jnp.int32, (1, LANE), 1)
    first = (lane % DH_ROPE) < DH_ROPE // 2
    rope_lanes = (lane >= DSA_KR_LANE) & (lane < DSA_KR_LANE + DH_ROPE)

    def rope(v):
        partner = jnp.where(first, pltpu.roll(v, LANE - DH_ROPE // 2, 1), pltpu.roll(v, DH_ROPE // 2, 1))
        return v * cos + partner * sin

    g = pa_ref[:, DSA_G0:DSA_G0 + LANE]
    ckv = pa_ref[:, DSA_CKV0:DSA_CKV0 + LANE]
    ckv_n = ckv * lax.rsqrt(jnp.mean(ckv * ckv, -1, keepdims=True) + 1e-6) * kvn_ref[...]
    kc_ref[:, 0:LANE] = ckv_n.astype(BF16)
    ct_ref[...] = ckv_n.T.astype(BF16)
    kc_ref[:, LANE:2 * LANE] = jnp.where(rope_lanes, rope(g), 0.0).astype(BF16)
    ik_ref[...] = jnp.where(lane < D_I, g, 0.0).astype(BF16)
    iw_ref[...] = g
    iq_ref[...] = pa_ref[:, DSA_IQ0:DSA_IQ0 + H_I * LANE].astype(BF16)
    qr = rope(pa_ref[:, DSA_QR0:DSA_QR0 + LANE]) * scale
    for h in range(H_A):
        qn = pa_ref[:, DSA_QN0 + h * LANE:DSA_QN0 + (h + 1) * LANE].astype(BF16)
        q_abs = jnp.dot(qn, wuk_ref[h], preferred_element_type=F32) * scale
        shift = (DSA_KR_LANE - DH_ROPE * h) % LANE
        qr_h = pltpu.roll(qr, shift, 1) if shift else qr
        qc_ref[:, 2 * h * LANE:(2 * h + 1) * LANE] = q_abs.astype(BF16)
        qc_ref[:, (2 * h + 1) * LANE:(2 * h + 2) * LANE] = jnp.where(rope_lanes, qr_h, 0.0).astype(BF16)


def _dsa_kernel(k_sel, qc_ref, iq_ref, iw_ref, kc_ref, ct_ref, ik_ref, tri_ref, wuv_ref, gn_ref, o_ref,
                key_scr, planes_scr, acc_scr, s0_scr, s1_scr, b0_scr, b1_scr):
    tq, kcs = DSA_TQ, DSA_KC
    t0 = pl.program_id(1) * tq
    n_chunk = (t0 + tq + kcs - 1) // kcs
    nt = (((1,), (1,)), ((), ()))
    qpos = t0 + lax.broadcasted_iota(jnp.int32, (1, tq), 1)
    iw_t = iw_ref[...].T

    def score_chunk(c, carry):
        k0 = pl.multiple_of(c * kcs, kcs)
        ikc = ik_ref[pl.ds(k0, kcs), :]
        sc = jnp.zeros((kcs, tq), F32)
        for h in range(H_I):
            lg = lax.dot_general(ikc, iq_ref[:, h * LANE:(h + 1) * LANE], nt, preferred_element_type=F32)
            sc = sc + jnp.maximum(lg, 0.0) * iw_t[DSA_IW_LANE + h:DSA_IW_LANE + h + 1, :]
        sc = jnp.where(sc == 0.0, 0.0, sc)
        kpos = k0 + lax.broadcasted_iota(jnp.int32, (kcs, 1), 0)
        sc = jnp.where(kpos <= qpos, sc, -jnp.inf)
        bits = pltpu.bitcast(sc, jnp.int32)
        key = jnp.where(bits < 0, bits ^ 0x7FFFFFFF, bits)
        key_scr[pl.ds(k0, kcs), :] = key
        ukey = key ^ INT_MIN
        for blk in range(kcs // PLANE_KEYS):
            words = _bit_transpose32(ukey[blk * PLANE_KEYS:(blk + 1) * PLANE_KEYS])
            row0 = pl.multiple_of((c * (kcs // PLANE_KEYS) + blk) * 8, 8)
            for r in range(32):
                planes_scr[r, pl.ds(row0, 8), :] = words[r]
        return carry

    lax.fori_loop(0, n_chunk, score_chunk, 0)

    n_words = planes_scr.shape[1]
    word_row = lax.broadcasted_iota(jnp.int32, (n_words, tq), 0)
    alive0 = jnp.where(word_row < n_chunk * (kcs // 32), jnp.int32(-1), jnp.int32(0))
    k_int = int(k_sel)

    def bit_pass(i, carry):
        thr_u, above, alive = carry
        ones = alive & planes_scr[i]
        seen_ones = above + jnp.sum(lax.population_count(ones), axis=0, keepdims=True)
        take = seen_ones >= k_int
        alive = jnp.where(take, ones, alive & ~planes_scr[i])
        above = jnp.where(take, above, seen_ones)
        thr_u = jnp.where(take, thr_u | jnp.left_shift(jnp.int32(1), 31 - i), thr_u)
        return thr_u, above, alive

    thr_u, above, _ = lax.fori_loop(0, 32, bit_pass, (jnp.zeros((1, tq), jnp.int32), jnp.zeros((1, tq), jnp.int32), alive0))
    thr = thr_u ^ INT_MIN
    room = (k_int - above).astype(F32)

    qall = jnp.concatenate([qc_ref[:, 2 * h * LANE:(2 * h + 2) * LANE] for h in range(H_A)], axis=0)

    room = jnp.where(thr > KEY_NEG_INF, room, 0.0)
    acc_scr[...] = jnp.zeros_like(acc_scr)

    half = kcs // 2

    def prepare(k0, seen, s_buf, b_buf):
        key = key_scr[pl.ds(k0, half), :]
        tie = jnp.where(key == thr, 1.0, 0.0)
        prefix = jnp.dot(tri_ref[...], tie.astype(BF16), preferred_element_type=F32)
        tie_bias = jnp.where(seen + prefix <= room, jnp.where(key == thr, 0.0, NEG), NEG)
        b_buf[...] = jnp.where(key > thr, 0.0, tie_bias)
        s_buf[...] = lax.dot_general(kc_ref[pl.ds(k0, half), :], qall, nt, preferred_element_type=F32)
        return seen + jnp.sum(_fold_rows(tie, jnp.add), axis=0, keepdims=True)

    def attn_chunk(c, carry):
        m, l, seen = carry
        k0 = pl.multiple_of(c * kcs, kcs)
        k1 = pl.multiple_of(k0 + half, half)
        seen = prepare(k1, seen, s1_scr, b1_scr)
        m, l = _attend_block(s0_scr, b0_scr, ct_ref[:, pl.ds(k0, half)], m, l, acc_scr, H_A, tq)
        k2 = pl.multiple_of(jnp.minimum(c + 1, n_chunk - 1) * kcs, kcs)
        seen = prepare(k2, seen, s0_scr, b0_scr)
        m, l = _attend_block(s1_scr, b1_scr, ct_ref[:, pl.ds(k1, half)], m, l, acc_scr, H_A, tq)
        return m, l, seen

    seen0 = prepare(0, jnp.zeros((1, tq), F32), s0_scr, b0_scr)
    init = (jnp.full((1, H_A * tq), M_FLOOR, F32), jnp.zeros((1, H_A * tq), F32), seen0)
    _, l, _ = lax.fori_loop(0, n_chunk, attn_chunk, init)
    o_lat = (acc_scr[...] / jnp.maximum(l, 1e-30)).astype(BF16)
    for h in range(H_A):
        o = lax.dot_general(o_lat[:, h * tq:(h + 1) * tq], wuv_ref[h], (((0,), (0,)), ((), ())),
                            preferred_element_type=F32)
        ms = jnp.sum(o * o, -1, keepdims=True) * (1.0 / DH_V)
        o_ref[:, h * LANE:(h + 1) * LANE] = (o * lax.rsqrt(ms + 1e-6) * gn_ref[h:h + 1, :]).astype(o_ref.dtype)


def _dsa_attention(qc, iq, iw, kc, ct, ik, w_uv, g_a):
    bsz, seq, _ = qc.shape
    k_sel = float(min(K_SEL_MAX, seq // 4))
    tq, kcs = DSA_TQ, DSA_KC
    half = kcs // 2
    tri = jnp.asarray(np.tril(np.ones((half, half), np.float32)), BF16)
    wuv = jnp.pad(w_uv, ((0, 0), (0, 0), (0, LANE - DH_V))).astype(BF16)
    gn = jnp.pad(g_a.reshape(H_A, DH_V), ((0, 0), (0, LANE - DH_V)))
    row = lambda n: pl.BlockSpec((None, tq, n), lambda b, i: (b, i, 0))
    per_b = lambda n: pl.BlockSpec((None, seq, n), lambda b, i: (b, 0, 0))
    return pl.pallas_call(
        functools.partial(_dsa_kernel, k_sel),
        grid=(bsz, seq // tq),
        in_specs=[row(2 * H_A * LANE), row(H_I * LANE), row(LANE), per_b(2 * LANE),
                  pl.BlockSpec((None, D_C, seq), lambda b, i: (b, 0, 0)), per_b(LANE),
                  pl.BlockSpec((half, half), lambda b, i: (0, 0)), pl.BlockSpec((H_A, D_C, LANE), lambda b, i: (0, 0, 0)),
                  pl.BlockSpec((H_A, LANE), lambda b, i: (0, 0))],
        out_specs=row(H_A * LANE),
        out_shape=jax.ShapeDtypeStruct((bsz, seq, H_A * LANE), BF16),
        scratch_shapes=[pltpu.VMEM((seq, tq), jnp.int32), pltpu.VMEM((32, seq // 32, tq), jnp.int32),
                        pltpu.VMEM((D_C, H_A * tq), F32),
                        pltpu.VMEM((half, H_A * tq), F32), pltpu.VMEM((half, H_A * tq), F32),
                        pltpu.VMEM((half, tq), F32), pltpu.VMEM((half, tq), F32)],
        compiler_params=_params("arbitrary", "arbitrary"),
        name="dsa_attention",
    )(qc, iq, iw, kc, ct, ik, tri, wuv, gn)


def _nsa_compress_kernel(a_ref, pos_ref, w1t_ref, w1b_ref, w2k_ref, w2v_ref, kc_ref, vc_ref):
    a = a_ref[...]
    top = jnp.dot((a + pos_ref[0:1, :]).astype(BF16), w1t_ref[...], preferred_element_type=F32)
    bot = jnp.dot((a + pos_ref[1:2, :]).astype(BF16), w1b_ref[...], preferred_element_type=F32)
    n = a.shape[0]
    pre = top + jnp.concatenate([bot[1:], bot[:1]], axis=0)
    h = (pre * jax.nn.sigmoid(pre)).astype(BF16)
    hid = w2k_ref.shape[0]
    kc_ref[...] = jnp.dot(h[:, :hid], w2k_ref[...], preferred_element_type=F32).astype(BF16)
    vc_ref[...] = lax.dot_general(w2v_ref[...], h[:, hid:], (((1,), (1,)), ((), ())), preferred_element_type=F32).astype(BF16)


def _nsa_compress(ncmp, cmp_pos, cmp_w1, cmp_w2):
    bsz, seq, _ = ncmp.shape
    n_grp = seq // D_STRIDE
    per = L_CMP // D_STRIDE
    width = D_STRIDE * LANE
    a = ncmp.reshape(bsz, n_grp, width)
    w1 = cmp_w1.reshape(2, per, D_STRIDE, DH_C, CMP_HIDDEN)
    zer = jnp.zeros((D_STRIDE, DH_C, CMP_HIDDEN), cmp_w1.dtype)

    def expand(p):
        wk = jnp.concatenate([w1[0, p], zer], axis=1)
        wv = jnp.concatenate([zer, w1[1, p]], axis=1)
        return jnp.concatenate([wk, wv], axis=2).reshape(width, 2 * CMP_HIDDEN).astype(BF16)

    pos = cmp_pos.reshape(2, per, D_STRIDE, DH_C)
    pos = jnp.concatenate([pos[0], pos[1]], axis=-1).reshape(per, width)
    pad_out = ((0, 0), (0, LANE - DH_C))
    w2k = jnp.pad(cmp_w2[0], pad_out).astype(BF16)
    w2v = jnp.pad(cmp_w2[1], pad_out).astype(BF16).T
    full = lambda r, c: pl.BlockSpec((r, c), lambda b: (0, 0))
    return pl.pallas_call(
        _nsa_compress_kernel,
        grid=(bsz,),
        in_specs=[pl.BlockSpec((None, n_grp, width), lambda b: (b, 0, 0)), full(per, width),
                  full(width, 2 * CMP_HIDDEN), full(width, 2 * CMP_HIDDEN), full(CMP_HIDDEN, LANE), full(LANE, CMP_HIDDEN)],
        out_specs=[pl.BlockSpec((None, n_grp, LANE), lambda b: (b, 0, 0)), pl.BlockSpec((None, LANE, n_grp), lambda b: (b, 0, 0))],
        out_shape=[jax.ShapeDtypeStruct((bsz, n_grp, LANE), BF16), jax.ShapeDtypeStruct((bsz, LANE, n_grp), BF16)],
        compiler_params=_params("arbitrary"),
        name="nsa_compress",
    )(a, pos, expand(0), expand(1), w2k, w2v)


NSA_TQ = 128
NSA_KC = 512
NEG = -1e30
M_FLOOR = -1e20


def _softmax_cols(s, bias, n_head, tq):
    out = []
    for h in range(n_head):
        sh = s[:, h * tq:(h + 1) * tq] + bias
        m = jnp.maximum(jnp.max(_fold_rows(sh, jnp.maximum), axis=0, keepdims=True), M_FLOOR)
        e = jnp.exp2(sh - m)
        den = jnp.sum(_fold_rows(e, jnp.add), axis=0, keepdims=True)
        out.append(e * (1.0 / jnp.maximum(den, 1e-30)))
    return out


def _nsa_kernel(q_ref, g_ref, kc_ref, vct_ref, ks_ref, vst_ref, kw_ref, vwt_ref, cover_ref, expand_ref, gn_ref, o_ref,
                acc_scr, s0_scr, s1_scr, b0_scr, b1_scr):
    tq, kc_sz = NSA_TQ, NSA_KC
    t0 = pl.program_id(1) * tq
    nt = (((1,), (1,)), ((), ()))
    q = q_ref[...]
    qa = jnp.concatenate([q[:, h * LANE:(h + 1) * LANE] for h in range(H_C)], axis=0)
    qpos = t0 + lax.broadcasted_iota(jnp.int32, (1, tq), 1)

    n_grp = kc_ref.shape[0]
    s_c = lax.dot_general(kc_ref[...], qa, nt, preferred_element_type=F32)
    n_idx = lax.broadcasted_iota(jnp.int32, (n_grp, 1), 0)
    visible = jnp.where(n_idx < n_grp - 1, n_idx * D_STRIDE + (L_CMP - 1), 2 ** 30)
    p_c = _softmax_cols(s_c, jnp.where(visible <= qpos, 0.0, NEG), H_C, tq)
    o_cmp = jnp.dot(vct_ref[...], jnp.concatenate([p.astype(BF16) for p in p_c], axis=1), preferred_element_type=F32)

    p_sum = p_c[0]
    for h in range(1, H_C):
        p_sum = p_sum + p_c[h]
    hi = p_sum.astype(BF16)
    lo = (p_sum - hi.astype(F32)).astype(BF16)
    cov = cover_ref[...]
    imp_t = jnp.dot(cov, hi, preferred_element_type=F32) + jnp.dot(cov, lo, preferred_element_type=F32)
    n_sel = cov.shape[0]
    jrow = lax.broadcasted_iota(jnp.int32, (n_sel, tq), 0)
    cur = (t0 + lax.broadcasted_iota(jnp.int32, (n_sel, tq), 1)) // L_SEL
    adm = jrow <= cur
    forced = (jrow == 0) | (jrow == cur) | (jrow == cur - 1)
    val = jnp.where(adm & forced, jnp.inf, jnp.where(adm, imp_t, -jnp.inf))
    rank = jnp.zeros((n_sel, tq), F32)
    for jp in range(n_sel):
        r = val[jp:jp + 1, :]
        rank = rank + jnp.where(r == val, jnp.where(jrow > jp, 1.0, 0.0), jnp.where(r > val, 1.0, 0.0))
    sel_t = jnp.where(rank < min(N_TOP_MAX, n_sel), jnp.where(val > -jnp.inf, 1.0, 0.0), 0.0)
    if n_sel < LANE:
        sel_t = jnp.concatenate([sel_t, jnp.zeros((LANE - n_sel, tq), F32)], axis=0)
    sel_t = sel_t.astype(BF16)

    acc_scr[...] = jnp.zeros_like(acc_scr)

    half = kc_sz // 2
    n_chunk = (t0 + tq + kc_sz - 1) // kc_sz

    def prepare(k0, s_buf, b_buf):
        s_buf[...] = lax.dot_general(ks_ref[pl.ds(k0, half), :], qa, nt, preferred_element_type=F32)
        picked = jnp.dot(expand_ref[pl.ds(k0, half), :], sel_t, preferred_element_type=F32)
        kpos = k0 + lax.broadcasted_iota(jnp.int32, (half, 1), 0)
        b_buf[...] = jnp.where(kpos <= qpos, jnp.where(picked > 0.5, 0.0, NEG), NEG)

    def chunk(c, carry):
        m, l = carry
        k0 = pl.multiple_of(c * kc_sz, kc_sz)
        k1 = pl.multiple_of(k0 + half, half)
        prepare(k1, s1_scr, b1_scr)
        m, l = _attend_block(s0_scr, b0_scr, vst_ref[:, pl.ds(k0, half)], m, l, acc_scr, H_C, tq)
        prepare(pl.multiple_of(jnp.minimum(c + 1, n_chunk - 1) * kc_sz, kc_sz), s0_scr, b0_scr)
        return _attend_block(s1_scr, b1_scr, vst_ref[:, pl.ds(k1, half)], m, l, acc_scr, H_C, tq)

    prepare(0, s0_scr, b0_scr)
    init = (jnp.full((1, H_C * tq), M_FLOOR, F32), jnp.zeros((1, H_C * tq), F32))
    _, l_s = lax.fori_loop(0, n_chunk, chunk, init)
    o_slc = acc_scr[...] * (1.0 / jnp.maximum(l_s, 1e-30))

    span = WINDOW + tq
    w0 = pl.multiple_of(jnp.maximum(t0 - WINDOW, 0), tq)
    s_w = lax.dot_general(kw_ref[pl.ds(w0, span), :], qa, nt, preferred_element_type=F32)
    kpos_w = w0 + lax.broadcasted_iota(jnp.int32, (span, 1), 0)
    wbias = jnp.where(kpos_w <= qpos, jnp.where(qpos - kpos_w < WINDOW, 0.0, NEG), NEG)
    p_w = _softmax_cols(s_w, wbias, H_C, tq)
    o_win = jnp.dot(vwt_ref[:, pl.ds(w0, span)], jnp.concatenate([p.astype(BF16) for p in p_w], axis=1), preferred_element_type=F32)

    gates_t = jax.nn.sigmoid(g_ref[...]).T
    for h in range(H_C):
        hs = slice(h * tq, (h + 1) * tq)
        o_t = (gates_t[h:h + 1] * o_cmp[:, hs] + gates_t[H_C + h:H_C + h + 1] * o_slc[:, hs]
               + gates_t[2 * H_C + h:2 * H_C + h + 1] * o_win[:, hs])
        o = o_t.T
        ms = jnp.sum(o * o, -1, keepdims=True) * (1.0 / DH_C)
        o_ref[:, h * LANE:(h + 1) * LANE] = (o * lax.rsqrt(ms + 1e-6) * gn_ref[h:h + 1, :]).astype(o_ref.dtype)


def _nsa_attention(nq, ng, kc, vc_t, nks, nvs_t, nkw, nvw_t, g_c):
    bsz, seq, _ = nq.shape
    n_grp = kc.shape[1]
    n_sel = seq // L_SEL
    grp_start = np.arange(n_grp) * D_STRIDE
    sel_start = np.arange(n_sel) * L_SEL
    cover_t = ((grp_start[None, :] < sel_start[:, None] + L_SEL) & (grp_start[None, :] + L_CMP > sel_start[:, None]))
    cover_t = jnp.asarray(cover_t.astype(np.float32), BF16)
    expand = (np.arange(seq)[:, None] // L_SEL == np.arange(LANE)[None, :]).astype(np.float32)
    expand = jnp.asarray(expand, BF16)
    gn = jnp.pad(g_c.reshape(H_C, DH_C), ((0, 0), (0, LANE - DH_C)))
    tq = NSA_TQ
    row = lambda n: pl.BlockSpec((None, tq, n), lambda b, i: (b, i, 0))
    per_b = lambda r, c: pl.BlockSpec((None, r, c), lambda b, i: (b, 0, 0))
    full = lambda r, c: pl.BlockSpec((r, c), lambda b, i: (0, 0))
    return pl.pallas_call(
        _nsa_kernel,
        grid=(bsz, seq // tq),
        in_specs=[row(H_C * LANE), row(LANE), per_b(n_grp, LANE), per_b(LANE, n_grp), per_b(seq, LANE), per_b(LANE, seq),
                  per_b(seq, LANE), per_b(LANE, seq), full(n_sel, n_grp), full(seq, LANE), full(H_C, LANE)],
        out_specs=row(H_C * LANE),
        out_shape=jax.ShapeDtypeStruct((bsz, seq, H_C * LANE), BF16),
        scratch_shapes=[pltpu.VMEM((LANE, H_C * tq), F32),
                        pltpu.VMEM((NSA_KC // 2, H_C * tq), F32), pltpu.VMEM((NSA_KC // 2, H_C * tq), F32),
                        pltpu.VMEM((NSA_KC // 2, tq), F32), pltpu.VMEM((NSA_KC // 2, tq), F32)],
        compiler_params=_params("arbitrary", "arbitrary"),
        name="nsa_attention",
    )(nq, ng, kc, vc_t, nks, nvs_t, nkw, nvw_t, cover_t, expand, gn)


def _outproj_kernel(ya_ref, yb_ref, yc_ref, x_ref, g_ref, wa_ref, wb_ref, wc_ref, lng_ref, lnb_ref, o_ref):
    y = jnp.dot(ya_ref[...], wa_ref[...], preferred_element_type=F32)
    y += jnp.dot(yb_ref[...], wb_ref[...], preferred_element_type=F32)
    y += jnp.dot(yc_ref[...], wc_ref[...], preferred_element_type=F32)
    z = ALPHA * x_ref[...] + (1.0 + g_ref[...]) * y
    o_ref[...] = _layer_norm_rows(z, lng_ref[...], lnb_ref[...])


def _pad_head_rows(w, n_head, dh):
    d = w.shape[1]
    return jnp.pad(w.reshape(n_head, dh, d), ((0, 0), (0, LANE - dh), (0, 0))).reshape(n_head * LANE, d)


def _output_projection(ya, yb, yc, x, g, wa, wb, wc, ln_g, ln_b):
    bsz, seq, d = x.shape
    tm = 512
    na, nb, nc = ya.shape[-1], yb.shape[-1], yc.shape[-1]
    row = lambda n: pl.BlockSpec((None, tm, n), lambda b, i: (b, i, 0))
    vec = pl.BlockSpec((None, 1, d), lambda b, i: (b, 0, 0))
    full = lambda r, c: pl.BlockSpec((r, c), lambda b, i: (0, 0))
    return pl.pallas_call(
        _outproj_kernel,
        grid=(bsz, seq // tm),
        in_specs=[row(na), row(nb), row(nc), row(d), vec, full(na, d), full(nb, d), full(nc, d), full(1, d), full(1, d)],
        out_specs=row(d),
        out_shape=jax.ShapeDtypeStruct((bsz, seq, d), F32),
        compiler_params=_params("arbitrary", "arbitrary"),
        name="output_projection_ln",
    )(ya, yb, yc, x, g, wa, wb, wc, ln_g.reshape(1, d), ln_b.reshape(1, d))


def _tile_gate_up(wg, wu, tf):
    *lead, d, ff = wg.shape
    split = lambda w: jnp.moveaxis(w.reshape(*lead, d, ff // tf, tf), -2, -3)
    return jnp.concatenate([split(wg), split(wu)], axis=-1)


def _swiglu_chunk(u, wgu_ref, wd_ref):
    tf = wd_ref.shape[0]
    ab = jnp.dot(u, wgu_ref[...], preferred_element_type=F32)
    a, b = ab[:, :tf], ab[:, tf:]
    return jnp.dot((a * jax.nn.sigmoid(a) * b).astype(BF16), wd_ref[...], preferred_element_type=F32)


def _ffn_kernel(x_ref, sc_ref, sh_ref, g_ref, wgu_ref, wd_ref, lng_ref, lnb_ref, o_ref, u_scr, acc_scr):
    f = pl.program_id(2)

    @pl.when(f == 0)
    def _():
        u_scr[...] = (x_ref[...] * (1.0 + sc_ref[...]) + sh_ref[...]).astype(BF16)
        acc_scr[...] = jnp.zeros_like(acc_scr)

    acc_scr[...] += _swiglu_chunk(u_scr[...], wgu_ref, wd_ref)

    @pl.when(f == pl.num_programs(2) - 1)
    def _():
        z = ALPHA * x_ref[...] + (1.0 + g_ref[...]) * acc_scr[...]
        o_ref[...] = _layer_norm_rows(z, lng_ref[...], lnb_ref[...])


def _dense_ffn(x, sc, sh, g, wg, wu, wd, ln_g, ln_b):
    bsz, seq, d = x.shape
    ff = wg.shape[1]
    tm, tf = 1024, 256
    row = pl.BlockSpec((None, tm, d), lambda b, i, f: (b, i, 0))
    vec = pl.BlockSpec((None, 1, d), lambda b, i, f: (b, 0, 0))
    one = pl.BlockSpec((1, d), lambda b, i, f: (0, 0))
    return pl.pallas_call(
        _ffn_kernel,
        grid=(bsz, seq // tm, ff // tf),
        in_specs=[row, vec, vec, vec,
                  pl.BlockSpec((None, d, 2 * tf), lambda b, i, f: (f, 0, 0)),
                  pl.BlockSpec((tf, d), lambda b, i, f: (f, 0)),
                  one, one],
        out_specs=row,
        out_shape=jax.ShapeDtypeStruct((bsz, seq, d), F32),
        scratch_shapes=[pltpu.VMEM((tm, d), BF16), pltpu.VMEM((tm, d), F32)],
        compiler_params=_params("arbitrary", "arbitrary", "arbitrary"),
        name="dense_swiglu_ln",
    )(x, sc, sh, g, _tile_gate_up(wg, wu, tf), wd, ln_g.reshape(1, d), ln_b.reshape(1, d))


def _router_kernel(x_ref, sc_ref, sh_ref, r_ref, lo_ref, up_ref, gate_ref, slot_ref, slot_t_ref, cnt_ref):
    u = x_ref[...] * (1.0 + sc_ref[...]) + sh_ref[...]
    logits = jnp.dot(u, r_ref[...], preferred_element_type=F32, precision=lax.Precision.HIGHEST)
    lane = lax.broadcasted_iota(jnp.int32, logits.shape, 1)
    neg = -jnp.inf
    l1 = jnp.where(lane < N_EXPERTS, logits, neg)
    m1 = jnp.max(l1, -1, keepdims=True)
    i1 = jnp.min(jnp.where(l1 == m1, lane, LANE), -1, keepdims=True)
    l2 = jnp.where(lane == i1, neg, l1)
    m2 = jnp.max(l2, -1, keepdims=True)
    i2 = jnp.min(jnp.where(l2 == m2, lane, LANE), -1, keepdims=True)
    e2 = jnp.exp(m2 - m1)
    w1 = 1.0 / (1.0 + e2)
    w2 = e2 / (1.0 + e2)
    gate_ref[...] = jnp.where(lane == i1, w1, jnp.where(lane == i2, w2, 0.0))
    routed = jnp.where((lane == i1) | (lane == i2), 1.0, 0.0)
    before = jnp.dot(lo_ref[...], routed.astype(BF16), preferred_element_type=F32)
    slot_ref[...] = jnp.where(routed > 0.5, before, -1.0)
    routed_t = routed.T
    before_t = jnp.dot(routed_t.astype(BF16), up_ref[...], preferred_element_type=F32)
    slot_t_ref[...] = jnp.where(routed_t > 0.5, before_t, -1.0)
    cnt_ref[...] = jnp.broadcast_to(jnp.sum(routed, axis=0, keepdims=True), cnt_ref.shape).astype(jnp.int32)


MOE_TM = 1024
MOE_ROWS = 288
MOE_TF = 896
MOE_GROUP = 2


def _moe_router(x, sc, sh, router):
    bsz, seq, d = x.shape
    tm = MOE_TM
    r = jnp.pad(router, ((0, 0), (0, LANE - router.shape[1])))
    upper = np.triu(np.ones((tm, tm), np.float32), 1)
    up, lo = jnp.asarray(upper, BF16), jnp.asarray(upper.T, BF16)
    row = lambda n: pl.BlockSpec((None, tm, n), lambda b, i: (b, i, 0))
    vec = pl.BlockSpec((None, 1, d), lambda b, i: (b, 0, 0))
    full = lambda a, c: pl.BlockSpec((a, c), lambda b, i: (0, 0))
    n_tile = seq // tm
    return pl.pallas_call(
        _router_kernel,
        grid=(bsz, n_tile),
        in_specs=[row(d), vec, vec, full(d, LANE), full(tm, tm), full(tm, tm)],
        out_specs=[row(LANE), row(LANE), pl.BlockSpec((None, None, LANE, tm), lambda b, i: (b, i, 0, 0)),
                   pl.BlockSpec((None, None, 8, LANE), lambda b, i: (b, i, 0, 0))],
        out_shape=[jax.ShapeDtypeStruct((bsz, seq, LANE), F32), jax.ShapeDtypeStruct((bsz, seq, LANE), F32),
                   jax.ShapeDtypeStruct((bsz, n_tile, LANE, tm), F32), jax.ShapeDtypeStruct((bsz, n_tile, 8, LANE), jnp.int32)],
        compiler_params=_params("arbitrary", "arbitrary"),
        name="moe_router",
    )(x, sc, sh, r, lo, up)


def _moe_kernel(cnt_ref, x_ref, sc_ref, sh_ref, g_ref, gate_ref, slot_ref, slot_t_ref, wgu_ref, wd_ref, lng_ref, lnb_ref,
                o_ref, u_scr, xg_scr, acc_scr):
    tm, rows = MOE_TM, MOE_ROWS
    max_pass = xg_scr.shape[0] // MOE_GROUP
    e = pl.program_id(2)
    f = pl.program_id(3)
    first_tile = (pl.program_id(0) * pl.num_programs(1) + pl.program_id(1)) * MOE_GROUP
    n_pass = [(cnt_ref[(first_tile + t) * N_EXPERTS + e] + rows - 1) // rows for t in range(MOE_GROUP)]

    @pl.when((e == 0) & (f == 0))
    def _():
        u_scr[...] = (x_ref[...] * (1.0 + sc_ref[...]) + sh_ref[...]).astype(BF16)
        o_ref[...] = jnp.zeros_like(o_ref)

    @pl.when(f == 0)
    def _():
        for t in range(MOE_GROUP):
            slot_row = slot_t_ref[t, pl.ds(e, 1), :]

            def gather(p, carry, t=t, slot_row=slot_row):
                want = p * rows + lax.broadcasted_iota(jnp.int32, (rows, 1), 0)
                pick = jnp.where(slot_row == want.astype(F32), 1.0, 0.0).astype(BF16)
                xg_scr[t * max_pass + p] = jnp.dot(pick, u_scr[t * tm:(t + 1) * tm, :], preferred_element_type=F32).astype(BF16)
                acc_scr[t * max_pass + p] = jnp.zeros((rows, x_ref.shape[-1]), F32)
                return carry

            lax.fori_loop(0, n_pass[t], gather, 0)

    for t in range(MOE_GROUP):
        def ffn(p, carry, t=t):
            acc_scr[t * max_pass + p] += _swiglu_chunk(xg_scr[t * max_pass + p], wgu_ref, wd_ref)
            return carry

        lax.fori_loop(0, n_pass[t], ffn, 0)

    @pl.when(f == pl.num_programs(3) - 1)
    def _():
        lane = lax.broadcasted_iota(jnp.int32, (tm, LANE), 1)
        for t in range(MOE_GROUP):
            tok = slice(t * tm, (t + 1) * tm)
            slot_col = jnp.sum(jnp.where(lane == e, slot_ref[tok, :], 0.0), axis=1, keepdims=True)
            gate_col = jnp.sum(jnp.where(lane == e, gate_ref[tok, :], 0.0), axis=1, keepdims=True)

            def scatter(p, carry, t=t, tok=tok, slot_col=slot_col, gate_col=gate_col):
                col = lax.broadcasted_iota(jnp.int32, (1, 2 * rows), 1)
                want = p * rows + jnp.where(col >= rows, col - rows, col)
                put = jnp.where(slot_col == want.astype(F32), 1.0, 0.0).astype(BF16)
                y = acc_scr[t * max_pass + p]
                hi = y.astype(BF16)
                lo = (y - hi.astype(F32)).astype(BF16)
                back = jnp.dot(put, jnp.concatenate([hi, lo], axis=0), preferred_element_type=F32)
                o_ref[tok, :] += gate_col * back
                return carry

            lax.fori_loop(0, n_pass[t], scatter, 0)

    @pl.when((e == pl.num_programs(2) - 1) & (f == pl.num_programs(3) - 1))
    def _():
        z = ALPHA * x_ref[...] + (1.0 + g_ref[...]) * o_ref[...]
        o_ref[...] = _layer_norm_rows(z, lng_ref[...], lnb_ref[...])


def _moe_ffn(x, sc, sh, g, routing, wg, wu, wd, ln_g, ln_b):
    gate, slot, slot_t, cnt = routing
    bsz, seq, d = x.shape
    n_e, _, ff = wg.shape
    tm, tf, rows, grp = MOE_TM, MOE_TF, MOE_ROWS, MOE_GROUP
    max_pass = -(-tm // rows)
    counts = cnt[:, :, 0, :n_e].reshape(-1)
    once = pl.Buffered(1)
    row = lambda n: pl.BlockSpec((None, grp * tm, n), lambda b, i, e, f, c: (b, i, 0), pipeline_mode=once)
    vec = pl.BlockSpec((None, 1, d), lambda b, i, e, f, c: (b, 0, 0))
    one = pl.BlockSpec((1, d), lambda b, i, e, f, c: (0, 0))
    grid_spec = pltpu.PrefetchScalarGridSpec(
        num_scalar_prefetch=1,
        grid=(bsz, seq // (grp * tm), n_e, ff // tf),
        in_specs=[row(d), vec, vec, vec, row(LANE), row(LANE),
                  pl.BlockSpec((None, grp, LANE, tm), lambda b, i, e, f, c: (b, i, 0, 0), pipeline_mode=once),
                  pl.BlockSpec((None, None, d, 2 * tf), lambda b, i, e, f, c: (e, f, 0, 0)),
                  pl.BlockSpec((None, tf, d), lambda b, i, e, f, c: (e, f, 0)),
                  one, one],
        out_specs=row(d),
        scratch_shapes=[pltpu.VMEM((grp * tm, d), BF16), pltpu.VMEM((grp * max_pass, rows, d), BF16),
                        pltpu.VMEM((grp * max_pass, rows, d), F32)],
    )
    return pl.pallas_call(
        _moe_kernel,
        grid_spec=grid_spec,
        out_shape=jax.ShapeDtypeStruct((bsz, seq, d), F32),
        compiler_params=_params("arbitrary", "arbitrary", "arbitrary", "arbitrary"),
        name="moe_swiglu_ln",
    )(counts, x, sc, sh, g, gate, slot, slot_t, _tile_gate_up(wg, wu, tf), wd, ln_g.reshape(1, d), ln_b.reshape(1, d))


ML_TT = 256
ML_SUB = 256
ML_CHUNK = 256


ML_NB = 1


def _mlstm_kernel(qk_ref, tail_ref, v_ref, og_ref, g_ref, cw_ref, cb_ref, gb_ref, gn_ref, y_ref,
                  ct_scr, n_scr, m_scr, q_scr, k_scr):
    @pl.when(pl.program_id(1) == 0)
    def _():
        ct_scr[...] = jnp.zeros_like(ct_scr)
        n_scr[...] = jnp.zeros_like(n_scr)
        m_scr[...] = jnp.zeros_like(m_scr)

    for nb in range(ML_NB):
        _mlstm_rows(qk_ref.at[nb], tail_ref.at[nb], v_ref.at[nb], og_ref.at[nb], g_ref.at[nb], cw_ref, cb_ref, gb_ref, gn_ref,
                    y_ref.at[nb], ct_scr.at[nb], n_scr.at[nb], m_scr.at[nb], q_scr.at[nb], k_scr.at[nb])


def _mlstm_rows(qk_ref, tail_ref, v_ref, og_ref, g_ref, cw_ref, cb_ref, gb_ref, gn_ref, y_ref,
                ct_scr, n_scr, m_scr, q_scr, k_scr):
    tt = ML_TT
    step = pl.program_id(1)

    x = qk_ref[...]
    tail = jnp.where(step == 0, 0.0, tail_ref[...])
    row8 = lax.broadcasted_iota(jnp.int32, (8, 1), 0)
    pre = x * cw_ref[CONV_W - 1:CONV_W, :] + cb_ref[...]
    for s in range(1, CONV_W):
        rolled = pltpu.roll(x, s, 0)
        head = jnp.where(row8 < s, pltpu.roll(tail, s, 0), rolled[0:8])
        pre = pre + jnp.concatenate([head, rolled[8:]], axis=0) * cw_ref[CONV_W - 1 - s:CONV_W - s, :]
    act = pre * jax.nn.sigmoid(pre)
    q_scr[...] = act[:, 0:ML_D].astype(BF16)
    k_scr[...] = (act[:, ML_D:2 * ML_D] * (DH_B ** -0.5)).astype(BF16)

    lane = lax.broadcasted_iota(jnp.int32, (1, LANE), 1)
    tok = lax.broadcasted_iota(jnp.int32, (1, ML_SUB), 1) % ML_CHUNK
    jj = lax.broadcasted_iota(jnp.int32, (ML_CHUNK, ML_CHUNK), 0)
    ss = lax.broadcasted_iota(jnp.int32, (ML_CHUNK, ML_CHUNK), 1)
    nt = (((1,), (1,)), ((), ()))
    tn = (((0,), (0,)), ((), ()))
    ct_state = [ct_scr[h] for h in range(H_B)]
    n_state = [n_scr[h:h + 1, :] for h in range(H_B)]
    m_state = [m_scr[h:h + 1, 0:1] for h in range(H_B)]
    for sub in range(tt // ML_SUB):
        r0 = sub * ML_SUB
        gp = g_ref[r0:r0 + ML_SUB, :] + gb_ref[...]
        lsig = jnp.minimum(gp, 0.0) - jnp.log(1.0 + jnp.exp(-jnp.abs(gp)))
        col = jnp.where(lane < H_B, gp, lsig)
        rowl = col.T
        b = rowl[0:8]
        sft = 1
        while sft < ML_CHUNK:
            b = b + jnp.where(tok >= sft, pltpu.roll(b, sft, 1), 0.0)
            sft *= 2
        bcol = jnp.concatenate([b, jnp.zeros((LANE - 8, ML_SUB), F32)], axis=0).T
        for ci in range(ML_SUB // ML_CHUNK):
            c0 = ci * ML_CHUNK
            rows = slice(r0 + c0, r0 + c0 + ML_CHUNK)
            for h in range(H_B):
                hs = slice(h * DH_B, (h + 1) * DH_B)
                b_col = bcol[c0:c0 + ML_CHUNK, H_B + h:H_B + h + 1]
                ig_col = col[c0:c0 + ML_CHUNK, h:h + 1]
                b_row = b[H_B + h:H_B + h + 1, c0:c0 + ML_CHUNK]
                ig_row = rowl[h:h + 1, c0:c0 + ML_CHUNK]
                qh, kh, vh = q_scr[rows, hs], k_scr[rows, hs], v_ref[rows, hs]
                log_d = jnp.where(jj >= ss, b_col - b_row + ig_row, NEG)
                m_loc = jnp.max(log_d, -1, keepdims=True)
                s_loc = lax.dot_general(qh, kh, nt, preferred_element_type=F32) * jnp.exp(log_d - m_loc)
                sv_loc = jnp.dot(s_loc.astype(BF16), vh, preferred_element_type=F32)
                rs_loc = jnp.sum(s_loc, -1, keepdims=True)
                b_last = b_row[:, ML_CHUNK - 1:ML_CHUNK]
                w_max = jnp.max(b_last - b_row + ig_row, -1, keepdims=True)
                w_loc = jnp.exp(b_last - b_col + ig_col - w_max)
                inc_c = lax.dot_general(kh, (w_loc * vh.astype(F32)).astype(BF16), tn, preferred_element_type=F32)
                inc_n = jnp.sum(w_loc * kh.astype(F32), axis=0, keepdims=True)
                m_old, ct, n_row = m_state[h], ct_state[h], n_state[h]
                log_inter = b_col + m_old
                m_out = jnp.maximum(log_inter, m_loc)
                w_inter = jnp.exp(log_inter - m_out)
                w_intra = jnp.exp(m_loc - m_out)
                num = w_inter * jnp.dot(qh, ct.astype(BF16), preferred_element_type=F32) + w_intra * sv_loc
                den = w_inter * jnp.sum(qh.astype(F32) * n_row, -1, keepdims=True) + w_intra * rs_loc
                hid = num / jnp.maximum(jnp.abs(den), jnp.exp(-m_out))
                m_new = jnp.maximum(b_last + m_old, w_max)
                decay = jnp.exp(b_last + m_old - m_new)
                grow = jnp.exp(w_max - m_new)
                ct_state[h] = decay * ct + grow * inc_c
                n_state[h] = decay * n_row + grow * inc_n
                m_state[h] = m_new
                ms = jnp.mean(hid * hid, -1, keepdims=True)
                y = hid * lax.rsqrt(ms + 1e-6) * gn_ref[:, hs] * jax.nn.sigmoid(og_ref[rows, hs])
                y_ref[rows, hs] = y.astype(y_ref.dtype)
    for h in range(H_B):
        ct_scr[h] = ct_state[h]
        n_scr[h:h + 1, :] = n_state[h]
        m_scr[h:h + 1, :] = jnp.broadcast_to(m_state[h], (1, LANE))


def _mlstm(mqk, mv, mo, mg, conv_w, conv_b, gate_b, g_b):
    bsz, seq, _ = mqk.shape
    tt = ML_TT
    gb = jnp.pad(gate_b.reshape(1, 2 * H_B), ((0, 0), (0, LANE - 2 * H_B)))
    nb = ML_NB
    row = lambda n: pl.BlockSpec((nb, tt, n), lambda b, i: (b, i, 0))
    full = lambda r, c: pl.BlockSpec((r, c), lambda b, i: (0, 0))
    tail = pl.BlockSpec((nb, 8, 2 * ML_D), lambda b, i: (b, jnp.maximum(i * (tt // 8) - 1, 0), 0))
    return pl.pallas_call(
        _mlstm_kernel,
        grid=(bsz // nb, seq // tt),
        in_specs=[row(2 * ML_D), tail, row(ML_D), row(ML_D), row(LANE),
                  full(CONV_W, 2 * ML_D), full(1, 2 * ML_D), full(1, LANE), full(1, ML_D)],
        out_specs=row(ML_D),
        out_shape=jax.ShapeDtypeStruct((bsz, seq, ML_D), BF16),
        scratch_shapes=[pltpu.VMEM((nb, H_B, DH_B, DH_B), F32), pltpu.VMEM((nb, 8, DH_B), F32), pltpu.VMEM((nb, 8, LANE), F32),
                        pltpu.VMEM((nb, tt, ML_D), BF16), pltpu.VMEM((nb, tt, ML_D), BF16)],
        compiler_params=_params("arbitrary", "arbitrary"),
        name="mlstm_scan",
    )(mqk, mqk, mv, mo, mg, conv_w, conv_b.reshape(1, 2 * ML_D), gb, g_b.reshape(1, ML_D))


def _mixers(dsa, mls, nsa, w_uv, conv_w, conv_b, gate_b, cmp_pos, cmp_w1, cmp_w2, grp_norm):
    g_a, g_b, g_c = jnp.split(grp_norm, [H_A * DH_V, H_A * DH_V + H_B * DH_B])
    qc, iq, iw, kc_a, ik, ct_a = dsa
    y_a = _dsa_attention(qc, iq, iw, kc_a, ct_a, ik, w_uv, g_a)
    y_b = _mlstm(*mls, conv_w, conv_b, gate_b, g_b)
    nq, ncmp, nks, nvs, nkw, nvw, ng = nsa
    kc, vc = _nsa_compress(ncmp, cmp_pos, cmp_w1, cmp_w2)
    y_c = _nsa_attention(nq, ng, kc, vc, nks, nvs, nkw, nvw, g_c)
    return y_a, y_b, y_c


def _pad_cols(w, n):
    return jnp.pad(w, ((0, 0), (0, n - w.shape[1])))


def kernel(x, c, positions, w_mod, b_mod, w_in, dsa_w_uk, dsa_w_uv, dsa_kv_norm, mlstm_conv_w, mlstm_conv_b, mlstm_gate_b, nsa_cmp_pos, nsa_cmp_w1, nsa_cmp_w2, grp_norm, w_out, ln_g, ln_b, ffn_w_gate, ffn_w_up, ffn_w_down, moe_router, moe_w_gate, moe_w_up, moe_w_down):
    bsz, seq, d_model = x.shape
    assert d_model == D_MODEL and bsz % ML_NB == 0, (bsz, d_model)
    assert all(seq % t == 0 for t in (MOE_GROUP * MOE_TM, ML_TT, DSA_KC, NSA_KC)) and seq >= WINDOW + NSA_TQ, seq
    mod = _modulation(c, w_mod, b_mod).reshape(bsz, DEPTH, N_MOD, 1, D_MODEL)
    rope = _rope_table(positions)
    for l in range(DEPTH):
        sh1, sc1, g1, sh2, sc2, g2 = [mod[:, l, j] for j in range(N_MOD)]
        w = w_in[l].astype(BF16)
        wa = _dsa_weight_layout(w[:, :N_GROUP_A])
        wb = _mlstm_weight_layout(w[:, N_GROUP_A:N_GROUP_A + N_GROUP_B])
        wc, wvt = _nsa_weight_layout(w[:, N_GROUP_A + N_GROUP_B:])
        proj = _input_projection(x, sc1, sh1, wa, wb, wc, wvt, rope[0], rope[1], dsa_w_uk[l], dsa_kv_norm[l])
        dsa, mls, nsa = proj[:6], proj[6:10], proj[10:]
        ya, yb, yc = _mixers(dsa, mls, nsa, dsa_w_uv[l], mlstm_conv_w[l], mlstm_conv_b[l], mlstm_gate_b[l], nsa_cmp_pos[l], nsa_cmp_w1[l], nsa_cmp_w2[l], grp_norm[l])
        wo = w_out[l].astype(BF16)
        n_a, n_b = H_A * DH_V, H_B * DH_B
        x = _output_projection(ya, yb, yc, x, g1, _pad_head_rows(wo[:n_a], H_A, DH_V), wo[n_a:n_a + n_b], _pad_head_rows(wo[n_a + n_b:], H_C, DH_C), ln_g[l, 0], ln_b[l, 0])
        if l % 2 == 0:
            k = l // 2
            x = _dense_ffn(x, sc2, sh2, g2, ffn_w_gate[k].astype(BF16), ffn_w_up[k].astype(BF16), ffn_w_down[k].astype(BF16), ln_g[l, 1], ln_b[l, 1])
        else:
            k = l // 2
            routing = _moe_router(x, sc2, sh2, moe_router[k])
            x = _moe_ffn(x, sc2, sh2, g2, routing,moe_w_gate[k].astype(BF16), moe_w_up[k].astype(BF16), moe_w_down[k].astype(BF16), ln_g[l, 1], ln_b[l, 1])
    return x
```
